```python
import math
import jax
import jax.numpy as jnp
from jax import lax
import numpy as np


D_MODEL = 1024
BATCH = 8
SEQ = 8192
DEPTH = 2

GRID_W = 64
CTX_LEN = 256
ROPE_BASE = 10000.0
NORM_EPS = 1e-6
Q_BLOCK = 128

MLA_HEADS = 8
MLA_Q_LORA = 256
MLA_KV_LORA = 128
MLA_NOPE = 64
MLA_ROPE = 32
MLA_V = 64
MLA_SCALE = (MLA_NOPE + MLA_ROPE) ** -0.5

GQA_HEADS = 8
GQA_KV_HEADS = 2
GQA_DIM = 64
GQA_SCALE = GQA_DIM ** -0.5

SSD_HEADS = 8
SSD_HEAD_DIM = 64
SSD_INNER = SSD_HEADS * SSD_HEAD_DIM
SSD_GROUPS = 2
SSD_STATE = 128
SSD_CONV = 3
SSD_CHUNK = 128
SSD_XBC = SSD_INNER + 2 * SSD_GROUPS * SSD_STATE

DIFF_HEADS = 4
DIFF_DIM = 64
DIFF_V = 2 * DIFF_DIM
DIFF_SCALE = DIFF_DIM ** -0.5

N_EXPERTS = 64
TOP_K = 8
N_GROUPS = 8
TOPK_GROUPS = 4
EXPERT_FF = 256
SHARED_FF = 256
ROUTED_SCALE = 2.5
MOE_BLOCK = 128

ATT_SPLITS = (MLA_Q_LORA, MLA_KV_LORA, MLA_ROPE, GQA_HEADS * GQA_DIM, GQA_KV_HEADS * GQA_DIM, GQA_KV_HEADS * GQA_DIM)
REC_SPLITS = (SSD_INNER, SSD_XBC, SSD_HEADS, SSD_HEADS, 2 * DIFF_HEADS * DIFF_DIM, 2 * DIFF_HEADS * DIFF_DIM, DIFF_HEADS * DIFF_V)
ATT_IN = sum(ATT_SPLITS)
REC_IN = sum(REC_SPLITS)
MIX_WIDTH = MLA_HEADS * MLA_V + GQA_HEADS * GQA_DIM
N_EVEN = (DEPTH + 1) // 2
N_ODD = DEPTH // 2

kernel_name = 'hybrid_dit_mla_gqa_ssd_diffattn_moe'


def rms_norm(x, g):
    xf = x.astype(jnp.float32)
    y = xf * lax.rsqrt(jnp.mean(xf * xf, axis=-1, keepdims=True) + NORM_EPS)
    return (y * g.astype(jnp.float32)).astype(x.dtype)


def _split(p, sizes):
    return jnp.split(p, [int(s) for s in np.cumsum(sizes)[:-1]], axis=-1)


def axial_rope_tables(n_tokens, dim):
    rows = n_tokens // GRID_W
    row = jnp.repeat(jnp.arange(rows, dtype=jnp.float32), GRID_W)
    col = jnp.tile(jnp.arange(GRID_W, dtype=jnp.float32), rows)
    half = dim // 2
    inv = ROPE_BASE ** (-(jnp.arange(half // 2, dtype=jnp.float32) * 2.0 / half))
    ang_r = row[:, None] * inv
    ang_c = col[:, None] * inv
    ang = jnp.concatenate([ang_r, ang_r, ang_c, ang_c], axis=-1)
    return jnp.cos(ang), jnp.sin(ang)


def apply_rope(x, cos, sin):
    half = x.shape[-1] // 2
    def rot(u):
        u1, u2 = jnp.split(u, 2, axis=-1)
        return jnp.concatenate([-u2, u1], axis=-1)
    rotated = jnp.concatenate([rot(x[..., :half]), rot(x[..., half:])], axis=-1)
    cos = cos[None, :, None, :].astype(x.dtype)
    sin = sin[None, :, None, :].astype(x.dtype)
    return x * cos + rotated * sin


def block_attention(q, k, v, scale):
    b, nq, h, dk = q.shape
    hk, dv = k.shape[2], v.shape[-1]
    g = h // hk
    nb = nq // Q_BLOCK
    qb = jnp.moveaxis(q.reshape(b, nb, Q_BLOCK, hk, g, dk), 1, 0)
    def one_block(qi):
        s = jnp.einsum('bqkgd,bskd->bkgqs', qi, k).astype(jnp.float32) * scale
        p = jax.nn.softmax(s, axis=-1).astype(v.dtype)
        return jnp.einsum('bkgqs,bskd->bqkgd', p, v)
    o = lax.map(one_block, qb)
    return jnp.moveaxis(o, 0, 1).reshape(b, nq, h, dv)


def block_diff_attention(q, k, v, lam, scale):
    b, nq, h, _, d = q.shape
    dv = v.shape[-1]
    nb = nq // Q_BLOCK
    qb = jnp.moveaxis(q.reshape(b, nb, Q_BLOCK, h, 2, d), 1, 0)
    def one_block(qi):
        s = jnp.einsum('bqhmd,bshmd->bmhqs', qi, k).astype(jnp.float32) * scale
        p = jax.nn.softmax(s, axis=-1)
        w = (p[:, 0] - lam * p[:, 1]).astype(v.dtype)
        return jnp.einsum('bhqs,bshe->bqhe', w, v)
    o = lax.map(one_block, qb)
    return jnp.moveaxis(o, 0, 1).reshape(b, nq, h, dv)


def attn_project(hh, w_in, q_norm, w_q_up, kv_norm, w_kv_up, gqa_qn, gqa_kn, rope_mla, rope_hd):
    b, n, _ = hh.shape
    q_lat, kv_lat, k_pe, gq, gk, gv = _split(hh @ w_in, ATT_SPLITS)
    q = (rms_norm(q_lat, q_norm) @ w_q_up).reshape(b, n, MLA_HEADS, MLA_NOPE + MLA_ROPE)
    kv = (rms_norm(kv_lat, kv_norm) @ w_kv_up).reshape(b, n, MLA_HEADS, MLA_NOPE + MLA_V)
    q_nope, q_pe = q[..., :MLA_NOPE], q[..., MLA_NOPE:]
    k_nope, v_mla = kv[..., :MLA_NOPE], kv[..., MLA_NOPE:]
    k_pe = k_pe[:, :, None, :]
    gq = rms_norm(gq.reshape(b, n, GQA_HEADS, GQA_DIM), gqa_qn)
    gk = rms_norm(gk.reshape(b, n, GQA_KV_HEADS, GQA_DIM), gqa_kn)
    gv = gv.reshape(b, n, GQA_KV_HEADS, GQA_DIM)
    if rope_mla is not None:
        q_pe = apply_rope(q_pe, *rope_mla)
        k_pe = apply_rope(k_pe, *rope_mla)
        gq = apply_rope(gq, *rope_hd)
        gk = apply_rope(gk, *rope_hd)
    q_mla = jnp.concatenate([q_nope, q_pe], axis=-1)
    k_mla = jnp.concatenate([k_nope, jnp.broadcast_to(k_pe, (b, n, MLA_HEADS, MLA_ROPE))], axis=-1)
    return (q_mla, k_mla, v_mla), (gq, gk, gv)


def attn_mixer(u, uc, w_in, q_norm, w_q_up, kv_norm, w_kv_up, gqa_qn, gqa_kn, w_out, rope_mla, rope_hd, need_ctx):
    (qa, ka, va), (qb, kb, vb) = attn_project(u, w_in, q_norm, w_q_up, kv_norm, w_kv_up, gqa_qn, gqa_kn, rope_mla, rope_hd)
    (qac, kac, vac), (qbc, kbc, vbc) = attn_project(uc, w_in, q_norm, w_q_up, kv_norm, w_kv_up, gqa_qn, gqa_kn, None, None)
    cat = lambda a, a_c: jnp.concatenate([a, a_c], axis=1)
    def merge(oa, ob):
        return jnp.concatenate([oa.reshape(*oa.shape[:2], -1), ob.reshape(*ob.shape[:2], -1)], axis=-1) @ w_out
    out = merge(block_attention(qa, cat(ka, kac), cat(va, vac), MLA_SCALE),
                block_attention(qb, cat(kb, kbc), cat(vb, vbc), GQA_SCALE))
    out_c = None
    if need_ctx:
        out_c = merge(block_attention(qac, kac, vac, MLA_SCALE), block_attention(qbc, kbc, vbc, GQA_SCALE))
    return out, out_c


def centred_dwconv(u, w, bias):
    k = w.shape[0]
    out = lax.conv_general_dilated(u, w[:, None, :], window_strides=(1,), padding=[(k // 2, k // 2)],
                                   dimension_numbers=('NWC', 'WIO', 'NWC'), feature_group_count=u.shape[-1])
    return out + bias


def ssd_scan(x, dt, a, bm, cm, init_state, want_y):
    b, l, h, p = x.shape
    g, n = bm.shape[2], bm.shape[3]
    hg = h // g
    nc = l // SSD_CHUNK
    xs = (x * dt[..., None].astype(x.dtype)).reshape(b, nc, SSD_CHUNK, g, hg, p)
    adt = (dt * a).reshape(b, nc, SSD_CHUNK, g, hg)
    a_cs = jnp.cumsum(jnp.moveaxis(adt, 2, -1), axis=-1)
    bc = bm.reshape(b, nc, SSD_CHUNK, g, n)
    cc = cm.reshape(b, nc, SSD_CHUNK, g, n)
    decay_to_end = jnp.exp(a_cs[..., -1:] - a_cs)
    chunk_states = jnp.einsum('bcsgn,bcghs,bcsghp->bcghpn', bc, decay_to_end, xs)
    chunk_decay = jnp.exp(a_cs[..., -1])
    def step(state, inp):
        dec, st = inp
        return state * dec[..., None, None] + st, state
    final, prev = lax.scan(step, init_state, (jnp.moveaxis(chunk_decay, 1, 0), jnp.moveaxis(chunk_states, 1, 0)))
    if not want_y:
        return None, final
    prev = jnp.moveaxis(prev, 0, 1)
    seg = a_cs[..., :, None] - a_cs[..., None, :]
    lower = jnp.tril(jnp.ones((SSD_CHUNK, SSD_CHUNK), dtype=bool))
    lmat = jnp.exp(jnp.where(lower, seg, -jnp.inf))
    cb = jnp.einsum('bclgn,bcsgn->bcgls', cc, bc)
    y_diag = jnp.einsum('bcgls,bcghls,bcsghp->bclghp', cb, lmat, xs)
    y_off = jnp.einsum('bclgn,bcghpn,bcghl->bclghp', cc, prev, jnp.exp(a_cs))
    y = (y_diag + y_off).reshape(b, l, h, p)
    return y.astype(x.dtype), final


def ssd_direction(lat, ctxs, dt_bias, a_log, reverse, need_ctx):
    a = -jnp.exp(a_log.astype(jnp.float32))
    def prep(xs, bm, cm, dt_raw):
        dt = jax.nn.softplus(dt_raw.astype(jnp.float32) + dt_bias.astype(jnp.float32))
        seq = (xs, bm, cm, dt)
        if reverse:
            seq = tuple(jnp.flip(t, axis=1) for t in seq)
        return seq
    xl, bl, cl, dtl = prep(*lat)
    xc, bcx, ccx, dtc = prep(*ctxs)
    init = jnp.zeros((xl.shape[0], SSD_GROUPS, SSD_HEADS // SSD_GROUPS, SSD_HEAD_DIM, SSD_STATE), jnp.float32)
    yc, state_c = ssd_scan(xc, dtc, a, bcx, ccx, init, need_ctx)
    yl, _ = ssd_scan(xl, dtl, a, bl, cl, state_c, True)
    if reverse:
        yl = jnp.flip(yl, axis=1)
        yc = jnp.flip(yc, axis=1) if yc is not None else None
    return yl, yc


def rec_project(hh, w_in, conv_w, conv_b, rope_hd):
    b, n, _ = hh.shape
    z, xbc, dt_f, dt_b, dq, dk, dv = _split(hh @ w_in, REC_SPLITS)
    xbc = jax.nn.silu(centred_dwconv(xbc, conv_w, conv_b))
    xs, bm, cm = _split(xbc, (SSD_INNER, SSD_GROUPS * SSD_STATE, SSD_GROUPS * SSD_STATE))
    xs = xs.reshape(b, n, SSD_HEADS, SSD_HEAD_DIM)
    bm = bm.reshape(b, n, SSD_GROUPS, SSD_STATE)
    cm = cm.reshape(b, n, SSD_GROUPS, SSD_STATE)
    dq = dq.reshape(b, n, 2 * DIFF_HEADS, DIFF_DIM)
    dk = dk.reshape(b, n, 2 * DIFF_HEADS, DIFF_DIM)
    if rope_hd is not None:
        dq = apply_rope(dq, *rope_hd)
        dk = apply_rope(dk, *rope_hd)
    dq = dq.reshape(b, n, DIFF_HEADS, 2, DIFF_DIM)
    dk = dk.reshape(b, n, DIFF_HEADS, 2, DIFF_DIM)
    dv = dv.reshape(b, n, DIFF_HEADS, DIFF_V)
    return (z, xs, bm, cm, dt_f, dt_b), (dq, dk, dv)


def rec_mixer(u, uc, w_in, conv_w, conv_b, dt_bias_f, dt_bias_b, a_log_f, a_log_b, d_skip, ssd_norm,
              lq1, lk1, lq2, lk2, subln, w_out, lambda_init, rope_hd, need_ctx):
    (z, xs, bm, cm, dtf, dtb), (dq, dk, dv) = rec_project(u, w_in, conv_w, conv_b, rope_hd)
    (zc, xsc, bmc, cmc, dtfc, dtbc), (dqc, dkc, dvc) = rec_project(uc, w_in, conv_w, conv_b, None)
    yf, yfc = ssd_direction((xs, bm, cm, dtf), (xsc, bmc, cmc, dtfc), dt_bias_f, a_log_f, False, need_ctx)
    yb, ybc = ssd_direction((xs, bm, cm, dtb), (xsc, bmc, cmc, dtbc), dt_bias_b, a_log_b, True, need_ctx)
    f32 = jnp.float32
    lam = (jnp.exp(jnp.sum(lq1.astype(f32) * lk1.astype(f32))) - jnp.exp(jnp.sum(lq2.astype(f32) * lk2.astype(f32)))
           + lambda_init)
    def ssd_out(y_f, y_b, x_in, zz):
        b, n = zz.shape[:2]
        y = (y_f + y_b + x_in * d_skip[:, None]).reshape(b, n, SSD_INNER) * jax.nn.silu(zz)
        y = rms_norm(y.reshape(b, n, SSD_GROUPS, -1), ssd_norm.reshape(SSD_GROUPS, -1))
        return y.reshape(b, n, SSD_INNER)
    def diff_out(o):
        b, n = o.shape[:2]
        return (rms_norm(o, subln) * (1.0 - lambda_init)).reshape(b, n, DIFF_HEADS * DIFF_V)
    cat = lambda a, a_c: jnp.concatenate([a, a_c], axis=1)
    o_d = block_diff_attention(dq, cat(dk, dkc), cat(dv, dvc), lam, DIFF_SCALE)
    out = jnp.concatenate([ssd_out(yf, yb, xs, z), diff_out(o_d)], axis=-1) @ w_out
    out_c = None
    if need_ctx:
        o_dc = block_diff_attention(dqc, dkc, dvc, lam, DIFF_SCALE)
        out_c = jnp.concatenate([ssd_out(yfc, ybc, xsc, zc), diff_out(o_dc)], axis=-1) @ w_out
    return out, out_c


def moe_ffn(t, router_w, router_b, w_gate, w_up, w_down, sh_gate, sh_up, sh_down):
    n_tok, d = t.shape
    scores = jax.nn.sigmoid((t @ router_w).astype(jnp.float32))
    sel = scores + router_b.astype(jnp.float32)
    grp_score = jnp.sum(lax.top_k(sel.reshape(n_tok, N_GROUPS, -1), 2)[0], axis=-1)
    _, grp_idx = lax.top_k(grp_score, TOPK_GROUPS)
    grp_mask = jnp.sum(jax.nn.one_hot(grp_idx, N_GROUPS, dtype=jnp.float32), axis=1) > 0
    masked = jnp.where(jnp.repeat(grp_mask, N_EXPERTS // N_GROUPS, axis=-1), sel, -jnp.inf)
    _, top_idx = lax.top_k(masked, TOP_K)
    top_w = jnp.take_along_axis(scores, top_idx, axis=-1)
    top_w = top_w / jnp.sum(top_w, axis=-1, keepdims=True) * ROUTED_SCALE
    n_asg = n_tok * TOP_K
    flat_e = top_idx.reshape(-1)
    flat_tok = jnp.arange(n_asg, dtype=jnp.int32) // TOP_K
    flat_w = top_w.reshape(-1)
    order = jnp.argsort(flat_e)
    e_sorted = flat_e[order]
    counts = jnp.bincount(flat_e, length=N_EXPERTS)
    padded = (counts + MOE_BLOCK - 1) // MOE_BLOCK * MOE_BLOCK
    pad_end = jnp.cumsum(padded)
    pad_start = pad_end - padded
    start = jnp.cumsum(counts) - counts
    dest = pad_start[e_sorted] + jnp.arange(n_asg, dtype=jnp.int32) - start[e_sorted]
    n_blocks = -(-n_asg // MOE_BLOCK) + N_EXPERTS
    slot_tok = jnp.zeros((n_blocks * MOE_BLOCK,), jnp.int32).at[dest].set(flat_tok[order])
    slot_w = jnp.zeros((n_blocks * MOE_BLOCK,), jnp.float32).at[dest].set(flat_w[order])
    block_exp = jnp.minimum(jnp.searchsorted(pad_end, jnp.arange(n_blocks) * MOE_BLOCK, side='right'), N_EXPERTS - 1)
    shared = (jax.nn.silu(t @ sh_gate) * (t @ sh_up)) @ sh_down
    def body(acc, blk):
        e, tok, w = blk
        xb = t[tok]
        hid = jax.nn.silu(xb @ w_gate[e]) * (xb @ w_up[e])
        yb = (hid @ w_down[e]) * w[:, None].astype(t.dtype)
        return acc.at[tok].add(yb), None
    out, _ = lax.scan(body, shared, (block_exp, slot_tok.reshape(n_blocks, MOE_BLOCK), slot_w.reshape(n_blocks, MOE_BLOCK)))
    return out


def setup_inputs(seed: int = 0) -> dict:
    key = jax.random.key(seed)
    ks = iter(jax.random.split(key, 64))
    f32 = jnp.float32
    def nrm(shape, fan_in, scale=1.0):
        return jax.random.normal(next(ks), shape, f32) * (scale * fan_in ** -0.5)
    def gain(shape):
        return 1.0 + 0.02 * jax.random.normal(next(ks), shape, f32)
    def small(shape, s):
        return s * jax.random.normal(next(ks), shape, f32)
    def dt_bias(shape):
        u = jax.random.uniform(next(ks), shape, f32, minval=math.log(1e-3), maxval=math.log(1e-1))
        dt = jnp.exp(u)
        return dt + jnp.log(-jnp.expm1(-dt))
    def a_log(shape):
        return jnp.log(jax.random.uniform(next(ks), shape, f32, minval=1.0, maxval=16.0))
    D = D_MODEL
    return {
        'x': jax.random.normal(next(ks), (BATCH, SEQ, D), f32),
        'c': jax.random.normal(next(ks), (BATCH, D), f32),
        'ctx': jax.random.normal(next(ks), (BATCH, CTX_LEN, D), f32),
        'c_ctx': jax.random.normal(next(ks), (D,), f32),
        'ada_w': nrm((DEPTH, D, 6 * D), D, 0.5),
        'ada_b': small((DEPTH, 6 * D), 0.02),
        'norm_mix_pre': gain((DEPTH, D)),
        'norm_mix_post': gain((DEPTH, D)),
        'norm_ffn_pre': gain((DEPTH, D)),
        'norm_ffn_post': gain((DEPTH, D)),
        'mix_w_out': nrm((DEPTH, MIX_WIDTH, D), MIX_WIDTH),
        'att_w_in': nrm((N_EVEN, D, ATT_IN), D),
        'mla_q_norm': gain((N_EVEN, MLA_Q_LORA)),
        'mla_w_q_up': nrm((N_EVEN, MLA_Q_LORA, MLA_HEADS * (MLA_NOPE + MLA_ROPE)), MLA_Q_LORA),
        'mla_kv_norm': gain((N_EVEN, MLA_KV_LORA)),
        'mla_w_kv_up': nrm((N_EVEN, MLA_KV_LORA, MLA_HEADS * (MLA_NOPE + MLA_V)), MLA_KV_LORA),
        'gqa_q_norm': gain((N_EVEN, GQA_DIM)),
        'gqa_k_norm': gain((N_EVEN, GQA_DIM)),
        'rec_w_in': nrm((N_ODD, D, REC_IN), D),
        'ssd_conv_w': nrm((N_ODD, SSD_CONV, SSD_XBC), SSD_CONV),
        'ssd_conv_b': small((N_ODD, SSD_XBC), 0.02),
        'ssd_dt_bias_f': dt_bias((N_ODD, SSD_HEADS)),
        'ssd_dt_bias_b': dt_bias((N_ODD, SSD_HEADS)),
        'ssd_a_log_f': a_log((N_ODD, SSD_HEADS)),
        'ssd_a_log_b': a_log((N_ODD, SSD_HEADS)),
        'ssd_d': gain((N_ODD, SSD_HEADS)),
        'ssd_norm': gain((N_ODD, SSD_INNER)),
        'diff_lambda_q1': small((N_ODD, DIFF_DIM), 0.1),
        'diff_lambda_k1': small((N_ODD, DIFF_DIM), 0.1),
        'diff_lambda_q2': small((N_ODD, DIFF_DIM), 0.1),
        'diff_lambda_k2': small((N_ODD, DIFF_DIM), 0.1),
        'diff_subln': gain((N_ODD, DIFF_V)),
        'router_w': nrm((DEPTH, D, N_EXPERTS), D),
        'router_b': small((DEPTH, N_EXPERTS), 0.01),
        'exp_w_gate': nrm((DEPTH, N_EXPERTS, D, EXPERT_FF), D),
        'exp_w_up': nrm((DEPTH, N_EXPERTS, D, EXPERT_FF), D),
        'exp_w_down': nrm((DEPTH, N_EXPERTS, EXPERT_FF, D), EXPERT_FF),
        'sh_w_gate': nrm((DEPTH, D, SHARED_FF), D),
        'sh_w_up': nrm((DEPTH, D, SHARED_FF), D),
        'sh_w_down': nrm((DEPTH, SHARED_FF, D), SHARED_FF),
    }


def reference(x, c, ctx, c_ctx, ada_w, ada_b, norm_mix_pre, norm_mix_post, norm_ffn_pre, norm_ffn_post, mix_w_out,
              att_w_in, mla_q_norm, mla_w_q_up, mla_kv_norm, mla_w_kv_up, gqa_q_norm, gqa_k_norm,
              rec_w_in, ssd_conv_w, ssd_conv_b, ssd_dt_bias_f, ssd_dt_bias_b, ssd_a_log_f, ssd_a_log_b, ssd_d, ssd_norm,
              diff_lambda_q1, diff_lambda_k1, diff_lambda_q2, diff_lambda_k2, diff_subln,
              router_w, router_b, exp_w_gate, exp_w_up, exp_w_down, sh_w_gate, sh_w_up, sh_w_down):
    b, n, d = x.shape
    n_ctx = ctx.shape[1]
    rope_mla = axial_rope_tables(n, MLA_ROPE)
    rope_hd = axial_rope_tables(n, GQA_DIM)
    silu_c = jax.nn.silu(c)
    silu_cc = jax.nn.silu(c_ctx)
    h, hc = x, ctx
    for i in range(DEPTH):
        last = i == DEPTH - 1
        j = i // 2
        shift1, scale1, gate1, shift2, scale2, gate2 = jnp.split((silu_c @ ada_w[i] + ada_b[i])[:, None, :], 6, axis=-1)
        shift1c, scale1c, gate1c, shift2c, scale2c, gate2c = jnp.split((silu_cc @ ada_w[i] + ada_b[i])[None, None, :], 6, axis=-1)
        u = rms_norm(h, norm_mix_pre[i]) * (1 + scale1) + shift1
        uc = rms_norm(hc, norm_mix_pre[i]) * (1 + scale1c) + shift1c
        if i % 2 == 0:
            o, oc = attn_mixer(u, uc, att_w_in[j], mla_q_norm[j], mla_w_q_up[j], mla_kv_norm[j], mla_w_kv_up[j],
                               gqa_q_norm[j], gqa_k_norm[j], mix_w_out[i], rope_mla, rope_hd, not last)
        else:
            lambda_init = 0.8 - 0.6 * math.exp(-0.3 * i)
            o, oc = rec_mixer(u, uc, rec_w_in[j], ssd_conv_w[j], ssd_conv_b[j], ssd_dt_bias_f[j], ssd_dt_bias_b[j],
                              ssd_a_log_f[j], ssd_a_log_b[j], ssd_d[j], ssd_norm[j],
                              diff_lambda_q1[j], diff_lambda_k1[j], diff_lambda_q2[j], diff_lambda_k2[j], diff_subln[j],
                              mix_w_out[i], lambda_init, rope_hd, not last)
        h = h + gate1 * rms_norm(o, norm_mix_post[i])
        v = rms_norm(h, norm_ffn_pre[i]) * (1 + scale2) + shift2
        if last:
            f = moe_ffn(v.reshape(b * n, d), router_w[i], router_b[i], exp_w_gate[i], exp_w_up[i], exp_w_down[i],
                        sh_w_gate[i], sh_w_up[i], sh_w_down[i]).reshape(b, n, d)
        else:
            hc = hc + gate1c * rms_norm(oc, norm_mix_post[i])
            vc = rms_norm(hc, norm_ffn_pre[i]) * (1 + scale2c) + shift2c
            f_all = moe_ffn(jnp.concatenate([v.reshape(b * n, d), vc.reshape(b * n_ctx, d)], axis=0),
                            router_w[i], router_b[i], exp_w_gate[i], exp_w_up[i], exp_w_down[i],
                            sh_w_gate[i], sh_w_up[i], sh_w_down[i])
            f = f_all[:b * n].reshape(b, n, d)
            hc = hc + gate2c * rms_norm(f_all[b * n:].reshape(b, n_ctx, d), norm_ffn_post[i])
        h = h + gate2 * rms_norm(f, norm_ffn_post[i])
    return h
```

```python
import functools
import math

import numpy as np
import jax
import jax.numpy as jnp
from jax import lax
from jax.experimental import pallas as pl
from jax.experimental.pallas import tpu as pltpu

F32 = jnp.float32
BF16 = jnp.bfloat16
LOG2E = 1.4426950408889634

GRID_W = 64
ROPE_BASE = 10000.0
NORM_EPS = 1e-6

MLA_HEADS, MLA_Q_LORA, MLA_KV_LORA, MLA_NOPE, MLA_ROPE, MLA_V = 8, 256, 128, 64, 32, 64
MLA_SCALE = (MLA_NOPE + MLA_ROPE) ** -0.5
GQA_HEADS, GQA_KV_HEADS, GQA_DIM = 8, 2, 64
GQA_SCALE = GQA_DIM ** -0.5
SSD_HEADS, SSD_HEAD_DIM, SSD_GROUPS, SSD_STATE, SSD_CONV, SSD_CHUNK = 8, 64, 2, 128, 3, 128
SSD_INNER = SSD_HEADS * SSD_HEAD_DIM
SSD_XBC = SSD_INNER + 2 * SSD_GROUPS * SSD_STATE
DIFF_HEADS, DIFF_DIM = 4, 64
DIFF_V = 2 * DIFF_DIM
DIFF_SCALE = DIFF_DIM ** -0.5
N_EXPERTS, TOP_K, N_GROUPS, TOPK_GROUPS, EXPERT_FF, SHARED_FF = 64, 8, 8, 4, 256, 256
ROUTED_SCALE = 2.5

LANES = 128
VMEM_LIMIT_BYTES = 56 * 1024 * 1024

ROW_BLOCK = 256
ATTN_Q_TILE = 256
ATTN_KV_CHUNK = 512
MOE_TOKENS = 768
MOE_ROWS = 128
MOE_PAIR = 2


def _cparams(sem):
    return pltpu.CompilerParams(dimension_semantics=sem, vmem_limit_bytes=VMEM_LIMIT_BYTES)


def _rms(x, axis):
    return x * lax.rsqrt(jnp.mean(x * x, axis=axis, keepdims=True) + NORM_EPS)


def _silu(x):
    return x * jax.nn.sigmoid(x)


def _dot(a, b):
    return jnp.dot(a, b, preferred_element_type=F32)


def _dot_nt(a, b):
    return lax.dot_general(a, b, (((1,), (1,)), ((), ())), preferred_element_type=F32)


def _dot_tn(a, b):
    return lax.dot_general(a, b, (((0,), (0,)), ((), ())), preferred_element_type=F32)


def _ada_kernel(c_ref, w_ref, b_ref, o_ref):
    s = _silu(c_ref[...])
    o_ref[...] = jnp.dot(s, w_ref[...], preferred_element_type=F32, precision=lax.Precision.HIGHEST) + b_ref[...]


def _ada_call(c_all, ada_w, ada_b):
    depth, d, six_d = ada_w.shape
    rows = c_all.shape[0]
    cols = six_d // 4
    return pl.pallas_call(
        _ada_kernel,
        grid=(depth, six_d // cols),
        in_specs=[
            pl.BlockSpec((rows, d), lambda i, j: (0, 0)),
            pl.BlockSpec((None, d, cols), lambda i, j: (i, 0, j)),
            pl.BlockSpec((None, 1, cols), lambda i, j: (i, 0, j)),
        ],
        out_specs=pl.BlockSpec((None, rows, cols), lambda i, j: (i, 0, j)),
        out_shape=jax.ShapeDtypeStruct((depth, rows, six_d), F32),
        compiler_params=_cparams(("arbitrary", "arbitrary")),
        name="ada_mod",
    )(c_all, ada_w, ada_b.reshape(depth, 1, six_d))


def _attn_pre_kernel(h_ref, mod_ref, gpre_ref, wfm_ref, wtm_ref, qn_ref, kvn_col_ref, kvn_row_ref, wq_ref, wk_ref,
                     wv_ref, gq_g_ref, gq_gr_ref, gk_g_ref, gk_gr_ref,
                     cqm_ref, sqm_ref, cqh_ref, sqh_ref, ckm_ref, skm_ref, ckh_ref, skh_ref,
                     qm_ref, km_ref, vm_ref, qg_ref, kg_ref, vg_ref):
    h = h_ref[...]
    u = _rms(h, -1) * gpre_ref[...]
    u = u * (1.0 + mod_ref[1:2, :]) + mod_ref[0:1, :]
    ub = u.astype(BF16)
    zt = _dot_nt(wfm_ref[...], ub)
    zk = _dot(ub, wtm_ref[...])

    qn = (_rms(zt[0:256], 0) * qn_ref[...]).astype(BF16)
    qt = _dot(wq_ref[...], qn)
    cq, sq = cqm_ref[...], sqm_ref[...]
    qs = MLA_SCALE * LOG2E
    zero32 = jnp.zeros((32, h.shape[0]), BF16)
    for hd in range(MLA_HEADS):
        pe = qt[512 + 32 * hd:544 + 32 * hd] * cq + qt[768 + 32 * hd:800 + 32 * hd] * sq
        qm_ref[hd, 0:64, :] = (qt[64 * hd:64 * hd + 64] * qs).astype(BF16)
        qm_ref[hd, 64:96, :] = (pe * qs).astype(BF16)
        qm_ref[hd, 96:128, :] = zero32

    kvn_t = (_rms(zt[256:384], 0) * kvn_col_ref[...]).astype(BF16)
    vt = _dot(wv_ref[...], kvn_t)
    for hd in range(MLA_HEADS):
        vm_ref[hd] = vt[64 * hd:64 * hd + 64].astype(BF16)
    kvn = (_rms(zk[:, 0:128], -1) * kvn_row_ref[...]).astype(BF16)
    kn = _dot(kvn, wk_ref[...])
    kpe = zk[:, 128:256] * ckm_ref[...] + zk[:, 256:384] * skm_ref[...]
    for hd in range(MLA_HEADS):
        km_ref[hd] = (kn[:, 128 * hd:128 * hd + 128] + kpe).astype(BF16)

    cqh, sqh = cqh_ref[...], sqh_ref[...]
    gs = GQA_SCALE * LOG2E
    zero64 = jnp.zeros((64, h.shape[0]), BF16)
    grp = GQA_HEADS // GQA_KV_HEADS
    for hd in range(GQA_HEADS):
        raw = zt[384 + 64 * hd:448 + 64 * hd]
        rot = zt[896 + 64 * hd:960 + 64 * hd]
        r = lax.rsqrt(jnp.mean(raw * raw, axis=0, keepdims=True) + NORM_EPS)
        q = (raw * r * gq_g_ref[...]) * cqh + (rot * r * gq_gr_ref[...]) * sqh
        q = (q * gs).astype(BF16)
        if hd // grp == 0:
            qg_ref[hd, 0:64, :] = q
            qg_ref[hd, 64:128, :] = zero64
        else:
            qg_ref[hd, 0:64, :] = zero64
            qg_ref[hd, 64:128, :] = q
    for kvh in range(GQA_KV_HEADS):
        vg_ref[kvh] = zt[1408 + 64 * kvh:1472 + 64 * kvh].astype(BF16)

    gk, gkr = zk[:, 384:512], zk[:, 512:640]
    lane = lax.broadcasted_iota(jnp.int32, gk.shape, 1)
    lo = lane < 64
    sq0 = jnp.sum(jnp.where(lo, gk * gk, 0.0), axis=-1, keepdims=True)
    sq1 = jnp.sum(jnp.where(lo, 0.0, gk * gk), axis=-1, keepdims=True)
    r = jnp.where(lo, lax.rsqrt(sq0 / GQA_DIM + NORM_EPS), lax.rsqrt(sq1 / GQA_DIM + NORM_EPS))
    kg_ref[...] = ((gk * r * gk_g_ref[...]) * ckh_ref[...] + (gkr * r * gk_gr_ref[...]) * skh_ref[...]).astype(BF16)


def _rot_map(dim):
    q = dim // 4
    j = np.arange(dim)
    even = (j // q) % 2 == 0
    return np.where(even, j + q, j - q), np.where(even, -1.0, 1.0).astype(np.float32)


def _rope_tables(n_tokens, dim, pad_rows):
    rows = n_tokens // GRID_W
    row = jnp.repeat(jnp.arange(rows, dtype=F32), GRID_W)
    col = jnp.tile(jnp.arange(GRID_W, dtype=F32), rows)
    half = dim // 2
    inv = ROPE_BASE ** (-(jnp.arange(half // 2, dtype=F32) * 2.0 / half))
    ang_r = row[:, None] * inv
    ang_c = col[:, None] * inv
    ang = jnp.concatenate([ang_r, ang_r, ang_c, ang_c], axis=-1)
    cos = jnp.concatenate([jnp.cos(ang), jnp.ones((pad_rows, dim), F32)], axis=0)
    sin = jnp.concatenate([jnp.sin(ang), jnp.zeros((pad_rows, dim), F32)], axis=0)
    return cos, sin


class _Geom:
    def __init__(self, b, n, c):
        assert c == ROW_BLOCK and n % ROW_BLOCK == 0 and n % GRID_W == 0
        self.b, self.n, self.c = b, n, c
        self.t = b * n + b * c
        self.lat_blocks = b * n // ROW_BLOCK
        self.blocks_per_seq = n // ROW_BLOCK
        self.n_blocks = self.t // ROW_BLOCK
        assert self.t % MOE_TOKENS == 0

    def mod_row(self, i):
        return jnp.where(i < self.lat_blocks, i // self.blocks_per_seq, self.b)

    def pos_block(self, i):
        return jnp.where(i < self.lat_blocks, i % self.blocks_per_seq, self.blocks_per_seq)


def _full(shape):
    nd = len(shape)
    return pl.BlockSpec(shape, lambda *_: (0,) * nd)


def _attn_pre_call(g, h, mod, gpre, w):
    t, d = h.shape
    rb = ROW_BLOCK
    row = lambda i: (i, 0)
    tm_tab = pl.BlockSpec((rb, LANES), lambda i: (g.pos_block(i), 0))
    fm32 = pl.BlockSpec((32, rb), lambda i: (0, g.pos_block(i)))
    fm64 = pl.BlockSpec((64, rb), lambda i: (0, g.pos_block(i)))
    in_specs = [
        pl.BlockSpec((rb, d), row),
        pl.BlockSpec((None, 6, d), lambda i: (g.mod_row(i), 0, 0)),
        _full(gpre.shape), _full(w["wfm"].shape), _full(w["wtm"].shape), _full(w["qn"].shape),
        _full(w["kvn_col"].shape), _full(w["kvn_row"].shape), _full(w["wq"].shape), _full(w["wk"].shape),
        _full(w["wv"].shape), _full(w["gq_g"].shape), _full(w["gq_gr"].shape), _full(w["gk_g"].shape),
        _full(w["gk_gr"].shape),
        fm32, fm32, fm64, fm64, tm_tab, tm_tab, tm_tab, tm_tab,
    ]
    out_shape = [
        jax.ShapeDtypeStruct((MLA_HEADS, 128, t), BF16),
        jax.ShapeDtypeStruct((MLA_HEADS, t, 128), BF16),
        jax.ShapeDtypeStruct((MLA_HEADS, MLA_V, t), BF16),
        jax.ShapeDtypeStruct((GQA_HEADS, 128, t), BF16),
        jax.ShapeDtypeStruct((t, 128), BF16),
        jax.ShapeDtypeStruct((GQA_KV_HEADS, GQA_DIM, t), BF16),
    ]
    out_specs = [
        pl.BlockSpec((MLA_HEADS, 128, rb), lambda i: (0, 0, i)),
        pl.BlockSpec((MLA_HEADS, rb, 128), lambda i: (0, i, 0)),
        pl.BlockSpec((MLA_HEADS, MLA_V, rb), lambda i: (0, 0, i)),
        pl.BlockSpec((GQA_HEADS, 128, rb), lambda i: (0, 0, i)),
        pl.BlockSpec((rb, 128), row),
        pl.BlockSpec((GQA_KV_HEADS, GQA_DIM, rb), lambda i: (0, 0, i)),
    ]
    return pl.pallas_call(
        _attn_pre_kernel, grid=(g.n_blocks,), in_specs=in_specs, out_specs=out_specs, out_shape=out_shape,
        compiler_params=_cparams(("parallel",)), name="attn_pre",
    )(h, mod, gpre, w["wfm"], w["wtm"], w["qn"], w["kvn_col"], w["kvn_row"], w["wq"], w["wk"], w["wv"],
      w["gq_g"], w["gq_gr"], w["gk_g"], w["gk_gr"],
      w["cqm"], w["sqm"], w["cqh"], w["sqh"], w["ckm"], w["skm"], w["ckh"], w["skh"])


def _attn_pre_weights(g, att_w_in, q_norm, w_q_up, kv_norm, w_kv_up, gqa_qn, gqa_kn):
    d = att_w_in.shape[0]
    o = np.cumsum([0, MLA_Q_LORA, MLA_KV_LORA, MLA_ROPE, GQA_HEADS * GQA_DIM, GQA_KV_HEADS * GQA_DIM,
                   GQA_KV_HEADS * GQA_DIM])
    w_qlat, w_kvlat, w_kpe, w_gq, w_gk, w_gv = (att_w_in[:, o[i]:o[i + 1]] for i in range(6))
    src32, sgn32 = _rot_map(MLA_ROPE)
    src64, sgn64 = _rot_map(GQA_DIM)

    def rot_heads(wcols, heads, dim, src, sgn):
        wh = wcols.reshape(d, heads, dim)
        return (wh[:, :, src] * sgn).reshape(d, heads * dim)

    wfm = jnp.concatenate([w_qlat, w_kvlat, w_gq, rot_heads(w_gq, GQA_HEADS, GQA_DIM, src64, sgn64), w_gv], axis=1)
    zpad = lambda x, lo, hi: jnp.pad(x, ((0, 0), (lo, hi)))
    wtm = jnp.concatenate([
        w_kvlat, zpad(w_kpe, 64, 32), zpad(w_kpe[:, src32] * sgn32, 64, 32),
        w_gk, rot_heads(w_gk, GQA_KV_HEADS, GQA_DIM, src64, sgn64)], axis=1)
    wq = w_q_up.reshape(MLA_Q_LORA, MLA_HEADS, MLA_NOPE + MLA_ROPE)
    wq_pe = wq[:, :, MLA_NOPE:]
    wq_all = jnp.concatenate([
        wq[:, :, :MLA_NOPE].reshape(MLA_Q_LORA, -1), wq_pe.reshape(MLA_Q_LORA, -1),
        (wq_pe[:, :, src32] * sgn32).reshape(MLA_Q_LORA, -1)], axis=1)
    wkv = w_kv_up.reshape(MLA_KV_LORA, MLA_HEADS, MLA_NOPE + MLA_V)
    wk = jnp.pad(wkv[:, :, :MLA_NOPE], ((0, 0), (0, 0), (0, 128 - MLA_NOPE))).reshape(MLA_KV_LORA, -1)
    wv = wkv[:, :, MLA_NOPE:].reshape(MLA_KV_LORA, -1)
    cos_m, sin_m = _rope_tables(g.n, MLA_ROPE, ROW_BLOCK)
    cos_h, sin_h = _rope_tables(g.n, GQA_DIM, ROW_BLOCK)
    two = lambda x: jnp.concatenate([x, x], axis=1)
    return dict(
        wfm=wfm.T.astype(BF16), wtm=wtm.astype(BF16),
        qn=q_norm.reshape(-1, 1), kvn_col=kv_norm.reshape(-1, 1), kvn_row=kv_norm.reshape(1, -1),
        wq=wq_all.T.astype(BF16), wk=wk.astype(BF16), wv=wv.T.astype(BF16),
        gq_g=gqa_qn.reshape(-1, 1), gq_gr=gqa_qn[src64].reshape(-1, 1),
        gk_g=two(gqa_kn.reshape(1, -1)), gk_gr=two(gqa_kn[src64].reshape(1, -1)),
        cqm=cos_m.T, sqm=sin_m.T, cqh=cos_h.T, sqh=sin_h.T,
        ckm=zpad(cos_m, 64, 32), skm=zpad(sin_m, 64, 32), ckh=two(cos_h), skh=two(sin_h),
    )


def _softmax_step(carry, qt, k_c, vt_c):
    m, l, acc = carry
    st = _dot(k_c, qt)
    m_new = jnp.maximum(m, jnp.max(st, axis=0, keepdims=True))
    alpha = jnp.exp2(m - m_new)
    p = jnp.exp2(st - m_new)
    l = alpha * l + jnp.sum(p, axis=0, keepdims=True)
    acc = alpha * acc + _dot(vt_c, p.astype(BF16))
    return m_new, l, acc


def _attend(qt, klat_ref, kctx_ref, vlat_ref, vctx_ref, n_lat_chunks, dv):
    tq = qt.shape[1]
    kc = ATTN_KV_CHUNK

    def body(c, carry):
        off = pl.multiple_of(c * kc, kc)
        return _softmax_step(carry, qt, klat_ref[pl.ds(off, kc), :], vlat_ref[:, pl.ds(off, kc)])

    init = (jnp.full((1, tq), -jnp.inf, F32), jnp.zeros((1, tq), F32), jnp.zeros((dv, tq), F32))
    carry = lax.fori_loop(0, n_lat_chunks, body, init)
    _, l, acc = _softmax_step(carry, qt, kctx_ref[...], vctx_ref[...])
    return acc / l


def _attn_kernel(qt_ref, klat_ref, kctx_ref, vlat_ref, vctx_ref, o_ref, *, nq, n_lat_chunks, dv):
    qi = pl.program_id(2)
    chunks = jnp.where(qi < nq, n_lat_chunks, 0)
    o_ref[...] = _attend(qt_ref[...], klat_ref, kctx_ref, vlat_ref, vctx_ref, chunks, dv).astype(o_ref.dtype)


def _attn_call(g, qt, k, vt, *, kv_heads, shared_k, name):
    heads = qt.shape[0]
    dv = vt.shape[1]
    grp = heads // kv_heads
    tq = ATTN_Q_TILE
    nq = g.n // tq
    ctx_blk = g.b * g.n // g.c
    q_idx = lambda b, h, qi: (h, 0, jnp.where(qi < nq, b * nq + qi, g.b * nq + b))
    if shared_k:
        klat = pl.BlockSpec((g.n, 128), lambda b, h, qi: (b, 0))
        kctx = pl.BlockSpec((g.c, 128), lambda b, h, qi: (ctx_blk + b, 0))
    else:
        klat = pl.BlockSpec((None, g.n, 128), lambda b, h, qi: (h // grp, b, 0))
        kctx = pl.BlockSpec((None, g.c, 128), lambda b, h, qi: (h // grp, ctx_blk + b, 0))
    return pl.pallas_call(
        functools.partial(_attn_kernel, nq=nq, n_lat_chunks=g.n // ATTN_KV_CHUNK, dv=dv),
        grid=(g.b, heads, nq + 1),
        in_specs=[
            pl.BlockSpec((None, 128, tq), q_idx), klat, kctx,
            pl.BlockSpec((None, dv, g.n), lambda b, h, qi: (h // grp, 0, b)),
            pl.BlockSpec((None, dv, g.c), lambda b, h, qi: (h // grp, 0, ctx_blk + b)),
        ],
        out_specs=pl.BlockSpec((dv, tq), lambda b, h, qi: (h, jnp.where(qi < nq, b * nq + qi, g.b * nq + b))),
        out_shape=jax.ShapeDtypeStruct((heads * dv, g.t), BF16),
        compiler_params=_cparams(("parallel", "parallel", "arbitrary")),
        name=name,
    )(qt, k, k, vt, vt)


def _split_bf16(x):
    hi = x.astype(BF16)
    return hi, (x - hi.astype(F32)).astype(BF16)


def _post_tail(o, h_ref, mod_ref, gpost_ref, gffn_ref, rw_ref, hn_ref, v_ref, lg_ref):
    hn = h_ref[...] + mod_ref[2:3, :] * (_rms(o, -1) * gpost_ref[...])
    hn_ref[...] = hn
    v = (_rms(hn, -1) * gffn_ref[...]) * (1.0 + mod_ref[4:5, :]) + mod_ref[3:4, :]
    v_hi, v_lo = _split_bf16(v)
    v_ref[...] = v_hi
    w_hi, w_lo = _split_bf16(rw_ref[...])
    lg_ref[...] = _dot_nt(w_hi, v_hi) + (_dot_nt(w_hi, v_lo) + _dot_nt(w_lo, v_hi))


def _attn_post_kernel(oa_ref, ob_ref, wa_ref, wb_ref, h_ref, mod_ref, gpost_ref, gffn_ref, rw_ref,
                      hn_ref, v_ref, lg_ref):
    o = _dot_tn(oa_ref[...], wa_ref[...]) + _dot_tn(ob_ref[...], wb_ref[...])
    _post_tail(o, h_ref, mod_ref, gpost_ref, gffn_ref, rw_ref, hn_ref, v_ref, lg_ref)


def _post_specs(g, d):
    rb = ROW_BLOCK
    ins = [
        pl.BlockSpec((rb, d), lambda i: (i, 0)),
        pl.BlockSpec((None, 6, d), lambda i: (g.mod_row(i), 0, 0)),
        _full((1, d)), _full((1, d)), _full((N_EXPERTS, d)),
    ]
    outs = [pl.BlockSpec((rb, d), lambda i: (i, 0)), pl.BlockSpec((rb, d), lambda i: (i, 0)),
            pl.BlockSpec((N_EXPERTS, rb), lambda i: (0, i))]
    shapes = [jax.ShapeDtypeStruct((g.t, d), F32), jax.ShapeDtypeStruct((g.t, d), BF16),
              jax.ShapeDtypeStruct((N_EXPERTS, g.t), F32)]
    return ins, outs, shapes


def _attn_post_call(g, oa, ob, w_out, h, mod, gpost, gffn, router_w):
    d = h.shape[1]
    rb = ROW_BLOCK
    half = oa.shape[0]
    ins, outs, shapes = _post_specs(g, d)
    wo = w_out.astype(BF16)
    return pl.pallas_call(
        _attn_post_kernel, grid=(g.n_blocks,),
        in_specs=[pl.BlockSpec((half, rb), lambda i: (0, i)), pl.BlockSpec((half, rb), lambda i: (0, i)),
                  _full((half, d)), _full((half, d))] + ins,
        out_specs=outs, out_shape=shapes,
        compiler_params=_cparams(("parallel",)), name="attn_post",
    )(oa, ob, wo[:half], wo[half:], h, mod, gpost.reshape(1, d), gffn.reshape(1, d), router_w.T)


def _router_kernel(lg_ref, bias_ref, tri_ref, gate_ref, rank_ref, cnt_ref):
    tb = lg_ref.shape[1]
    per = N_EXPERTS // N_GROUPS
    shp = (N_GROUPS, per, tb)
    scores = jax.nn.sigmoid(lg_ref[...])
    s3 = scores.reshape(shp)
    sel = (scores + bias_ref[...]).reshape(shp)
    sub = lax.broadcasted_iota(jnp.int32, shp, 1)
    grp = lax.broadcasted_iota(jnp.int32, shp, 0)
    neg = -jnp.inf
    m1 = jnp.max(sel, axis=1, keepdims=True)
    i1 = jnp.min(jnp.where(sel == m1, sub, per), axis=1, keepdims=True)
    m2 = jnp.max(jnp.where(sub == i1, neg, sel), axis=1, keepdims=True)
    cur = jnp.broadcast_to(m1 + m2, shp)
    gmask = jnp.zeros(shp, F32)
    for _ in range(TOPK_GROUPS):
        gm = jnp.max(cur, axis=0, keepdims=True)
        gi = jnp.min(jnp.where(cur == gm, grp, N_GROUPS), axis=0, keepdims=True)
        pick = grp == gi
        gmask = jnp.where(pick, 1.0, gmask)
        cur = jnp.where(pick, neg, cur)
    masked = jnp.where(gmask > 0.0, sel, neg)
    eidx = grp * per + sub
    chosen = jnp.zeros(shp, F32)
    for _ in range(TOP_K):
        mx = jnp.max(jnp.max(masked, axis=1, keepdims=True), axis=0, keepdims=True)
        cand = jnp.where(masked == mx, eidx, N_EXPERTS)
        ei = jnp.min(jnp.min(cand, axis=1, keepdims=True), axis=0, keepdims=True)
        pick = eidx == ei
        chosen = jnp.where(pick, 1.0, chosen)
        masked = jnp.where(pick, neg, masked)
    top_w = jnp.where(chosen > 0.0, s3, 0.0)
    denom = jnp.sum(jnp.sum(top_w, axis=1, keepdims=True), axis=0, keepdims=True)
    gate_ref[...] = (top_w / denom * ROUTED_SCALE).reshape(N_EXPERTS, tb)
    ch2 = chosen.reshape(N_EXPERTS, tb)
    chb = ch2.astype(BF16)
    before = _dot(chb, tri_ref[...])
    rank_ref[...] = jnp.where(ch2 > 0.0, before, -1.0).astype(jnp.int32)
    cnt_ref[...] = _dot(chb, jnp.ones((tb, LANES), BF16))


def _router_call(g, logits_t, router_b):
    tb = MOE_TOKENS
    nsb = g.t // tb
    tri = (np.arange(tb)[:, None] < np.arange(tb)[None, :]).astype(np.float32)
    blk = pl.BlockSpec((N_EXPERTS, tb), lambda s: (0, s))
    gates, rank, cnt = pl.pallas_call(
        _router_kernel, grid=(nsb,),
        in_specs=[blk, _full((N_EXPERTS, 1)), _full((tb, tb))],
        out_specs=[blk, blk, pl.BlockSpec((None, N_EXPERTS, LANES), lambda s: (s, 0, 0))],
        out_shape=[jax.ShapeDtypeStruct((N_EXPERTS, g.t), F32), jax.ShapeDtypeStruct((N_EXPERTS, g.t), jnp.int32),
                   jax.ShapeDtypeStruct((nsb, N_EXPERTS, LANES), F32)],
        compiler_params=_cparams(("parallel",)), name="router",
    )(logits_t, router_b.reshape(-1, 1), jnp.asarray(tri, BF16))
    return gates, rank, cnt[:, :, 0].astype(jnp.int32).reshape(-1)


def _moe_kernel(cnt_ref, v_ref, gate_ref, rank_ref, wg_ref, wu_ref, wd_ref, sg_ref, su_ref, sd_ref, o_ref):
    s, j = pl.program_id(0), pl.program_id(1)
    tb = v_ref.shape[0]
    rows = MOE_ROWS

    @pl.when(j == 0)
    def _():
        vb = v_ref[...]
        hid = (_silu(_dot(vb, sg_ref[...])) * _dot(vb, su_ref[...])).astype(BF16)
        o_ref[...] = _dot(hid, sd_ref[...])

    base = s * N_EXPERTS + j * MOE_PAIR
    n_max = cnt_ref[base]
    for e in range(1, MOE_PAIR):
        n_max = jnp.maximum(n_max, cnt_ref[base + e])
    passes = (n_max + rows - 1) // rows
    row_id = lax.broadcasted_iota(jnp.int32, (rows, tb), 0)

    def body(c, carry):
        vb = v_ref[...]
        sel, outs = [], []
        for e in range(MOE_PAIR):
            ex = j * MOE_PAIR + e
            rk = rank_ref[pl.ds(ex, 1), :] - c * rows
            hit = row_id == rk
            onehot = jnp.where(hit, 1.0, 0.0).astype(BF16)
            w_row = jnp.sum(jnp.where(hit, gate_ref[pl.ds(ex, 1), :], 0.0), axis=1, keepdims=True)
            xs = _dot(onehot, vb).astype(BF16)
            hid = (_silu(_dot(xs, wg_ref[e])) * _dot(xs, wu_ref[e])).astype(BF16)
            y = _dot(hid, wd_ref[e])
            sel.append(onehot)
            outs.append((y * w_row).astype(BF16))
        o_ref[...] += _dot_tn(jnp.concatenate(sel, axis=0), jnp.concatenate(outs, axis=0))
        return carry

    lax.fori_loop(0, passes, body, 0)


def _moe_call(g, v, gates, rank, counts, wg, wu, wd, sg, su, sd):
    t, d = v.shape
    tb = MOE_TOKENS
    ff = wg.shape[2]
    grid_spec = pltpu.PrefetchScalarGridSpec(
        num_scalar_prefetch=1,
        grid=(t // tb, N_EXPERTS // MOE_PAIR),
        in_specs=[
            pl.BlockSpec((tb, d), lambda s, j, c: (s, 0)),
            pl.BlockSpec((N_EXPERTS, tb), lambda s, j, c: (0, s)),
            pl.BlockSpec((N_EXPERTS, tb), lambda s, j, c: (0, s)),
            pl.BlockSpec((MOE_PAIR, d, ff), lambda s, j, c: (j, 0, 0)),
            pl.BlockSpec((MOE_PAIR, d, ff), lambda s, j, c: (j, 0, 0)),
            pl.BlockSpec((MOE_PAIR, ff, d), lambda s, j, c: (j, 0, 0)),
            pl.BlockSpec((d, sg.shape[1]), lambda s, j, c: (0, 0)),
            pl.BlockSpec((d, su.shape[1]), lambda s, j, c: (0, 0)),
            pl.BlockSpec((sd.shape[0], d), lambda s, j, c: (0, 0)),
        ],
        out_specs=pl.BlockSpec((tb, d), lambda s, j, c: (s, 0)),
    )
    return pl.pallas_call(
        _moe_kernel, grid_spec=grid_spec, out_shape=jax.ShapeDtypeStruct((t, d), F32),
        compiler_params=_cparams(("parallel", "arbitrary")), name="moe",
    )(counts, v, gates, rank, wg.astype(BF16), wu.astype(BF16), wd.astype(BF16),
      sg.astype(BF16), su.astype(BF16), sd.astype(BF16))


def _rec_pre_kernel(h_ref, mod_ref, gpre_ref, wtm_ref, wfm_ref, cqh_ref, sqh_ref, ckh_ref, skh_ref,
                    z_ref, xbc_ref, dt_ref, kd_ref, qd_ref, vd_ref):
    h = h_ref[...]
    u = _rms(h, -1) * gpre_ref[...]
    u = u * (1.0 + mod_ref[1:2, :]) + mod_ref[0:1, :]
    ub = u.astype(BF16)
    zk = _dot(ub, wtm_ref[...])
    zt = _dot_nt(wfm_ref[...], ub)
    z_ref[...] = zk[:, 0:512].astype(z_ref.dtype)
    xbc_ref[...] = zk[:, 512:1536]
    dt_ref[...] = zk[:, 1536:1664]
    ck, sk = ckh_ref[...], skh_ref[...]
    for j in range(DIFF_HEADS):
        lo = 1664 + 128 * j
        kd_ref[j] = (zk[:, lo:lo + 128] * ck + zk[:, lo + 512:lo + 640] * sk).astype(BF16)
    cq, sq = cqh_ref[...], sqh_ref[...]
    qs = DIFF_SCALE * LOG2E
    zero64 = jnp.zeros((64, h.shape[0]), BF16)
    for hd in range(2 * DIFF_HEADS):
        q = ((zt[64 * hd:64 * hd + 64] * cq + zt[512 + 64 * hd:576 + 64 * hd] * sq) * qs).astype(BF16)
        if hd % 2 == 0:
            qd_ref[hd, 0:64, :] = q
            qd_ref[hd, 64:128, :] = zero64
        else:
            qd_ref[hd, 0:64, :] = zero64
            qd_ref[hd, 64:128, :] = q
    for j in range(DIFF_HEADS):
        vd_ref[j] = zt[1024 + 128 * j:1152 + 128 * j].astype(BF16)


def _rec_pre_weights(g, rec_w_in):
    d = rec_w_in.shape[0]
    o = np.cumsum([0, SSD_INNER, SSD_XBC, SSD_HEADS, SSD_HEADS, 2 * DIFF_HEADS * DIFF_DIM, 2 * DIFF_HEADS * DIFF_DIM,
                   DIFF_HEADS * DIFF_V])
    w_z, w_xbc, w_dtf, w_dtb, w_dq, w_dk, w_dv = (rec_w_in[:, o[i]:o[i + 1]] for i in range(7))
    src64, sgn64 = _rot_map(DIFF_DIM)

    def rot_heads(wcols):
        wh = wcols.reshape(d, 2 * DIFF_HEADS, DIFF_DIM)
        return (wh[:, :, src64] * sgn64).reshape(d, -1)

    w_dt = jnp.pad(jnp.concatenate([w_dtf, w_dtb], axis=1), ((0, 0), (0, LANES - 2 * SSD_HEADS)))
    wtm = jnp.concatenate([w_z, w_xbc, w_dt, w_dk, rot_heads(w_dk)], axis=1)
    wfm = jnp.concatenate([w_dq, rot_heads(w_dq), w_dv], axis=1)
    cos_h, sin_h = _rope_tables(g.n, DIFF_DIM, ROW_BLOCK)
    two = lambda x: jnp.concatenate([x, x], axis=1)
    return dict(wtm=wtm.astype(BF16), wfm=wfm.T.astype(BF16), cqh=cos_h.T, sqh=sin_h.T, ckh=two(cos_h), skh=two(sin_h))


def _rec_pre_call(g, h, mod, gpre, w):
    t, d = h.shape
    rb = ROW_BLOCK
    row = lambda i: (i, 0)
    tm_tab = pl.BlockSpec((rb, LANES), lambda i: (g.pos_block(i), 0))
    fm64 = pl.BlockSpec((64, rb), lambda i: (0, g.pos_block(i)))
    nd = 2 * DIFF_HEADS
    return pl.pallas_call(
        _rec_pre_kernel, grid=(g.n_blocks,),
        in_specs=[pl.BlockSpec((rb, d), row), pl.BlockSpec((None, 6, d), lambda i: (g.mod_row(i), 0, 0)),
                  _full(gpre.shape), _full(w["wtm"].shape), _full(w["wfm"].shape), fm64, fm64, tm_tab, tm_tab],
        out_specs=[
            pl.BlockSpec((rb, SSD_INNER), row), pl.BlockSpec((rb, SSD_XBC), row), pl.BlockSpec((rb, LANES), row),
            pl.BlockSpec((DIFF_HEADS, rb, 128), lambda i: (0, i, 0)),
            pl.BlockSpec((nd, 128, rb), lambda i: (0, 0, i)),
            pl.BlockSpec((DIFF_HEADS, DIFF_V, rb), lambda i: (0, 0, i)),
        ],
        out_shape=[
            jax.ShapeDtypeStruct((t, SSD_INNER), BF16), jax.ShapeDtypeStruct((t, SSD_XBC), F32),
            jax.ShapeDtypeStruct((t, LANES), F32), jax.ShapeDtypeStruct((DIFF_HEADS, t, 128), BF16),
            jax.ShapeDtypeStruct((nd, 128, t), BF16), jax.ShapeDtypeStruct((DIFF_HEADS, DIFF_V, t), BF16),
        ],
        compiler_params=_cparams(("parallel",)), name="rec_pre",
    )(h, mod, gpre, w["wtm"], w["wfm"], w["cqh"], w["sqh"], w["ckh"], w["skh"])


def _conv_kernel(x_ref, prev_ref, next_ref, w_ref, b_ref, o_ref, *, seq_blocks, lat_blocks):
    i = pl.program_id(0)
    rb = x_ref.shape[0]
    pos = i % seq_blocks
    is_ctx = i >= lat_blocks
    first = jnp.logical_or(is_ctx, pos == 0)
    last = jnp.logical_or(is_ctx, pos == seq_blocks - 1)
    x = x_ref[...]
    prev_row = jnp.where(first, 0.0, prev_ref[7:8, :])
    next_row = jnp.where(last, 0.0, next_ref[0:1, :])
    rid = lax.broadcasted_iota(jnp.int32, x.shape, 0)
    x_prev = jnp.where(rid == 0, prev_row, pltpu.roll(x, 1, axis=0))
    x_next = jnp.where(rid == rb - 1, next_row, pltpu.roll(x, rb - 1, axis=0))
    y = w_ref[0:1, :] * x_prev + w_ref[1:2, :] * x + w_ref[2:3, :] * x_next + b_ref[...]
    o_ref[...] = _silu(y)


def _conv_call(g, xbc, conv_w, conv_b):
    t, ch = xbc.shape
    rb = ROW_BLOCK
    halo = 8
    per = rb // halo
    last_halo = t // halo - 1
    return pl.pallas_call(
        functools.partial(_conv_kernel, seq_blocks=g.blocks_per_seq, lat_blocks=g.lat_blocks),
        grid=(g.n_blocks,),
        in_specs=[
            pl.BlockSpec((rb, ch), lambda i: (i, 0)),
            pl.BlockSpec((halo, ch), lambda i: (jnp.maximum(i * per - 1, 0), 0)),
            pl.BlockSpec((halo, ch), lambda i: (jnp.minimum((i + 1) * per, last_halo), 0)),
            _full((SSD_CONV, ch)), _full((1, ch)),
        ],
        out_specs=pl.BlockSpec((rb, ch), lambda i: (i, 0)),
        out_shape=jax.ShapeDtypeStruct((t, ch), F32),
        compiler_params=_cparams(("parallel",)), name="ssd_conv",
    )(xbc, xbc, xbc, conv_w, conv_b.reshape(1, ch))


def _ssd_kernel(x_ref, dt_ref, bias_ref, alog_ref, tri_ref, trit_ref, eye_ref, y_ref, state_ref, *, lane_off, reverse):
    k = pl.program_id(1)
    ln = SSD_CHUNK
    hi = lax.Precision.HIGHEST

    @pl.when(k == 0)
    def _():
        state_ref[...] = jnp.zeros_like(state_ref)

    x = x_ref[:, 0:SSD_INNER]
    raw = dt_ref[...] + bias_ref[...]
    dt = jnp.maximum(raw, 0.0) + jnp.log1p(jnp.exp(-jnp.abs(raw)))
    adt = dt * (-jnp.exp(alog_ref[...]))
    acs = jnp.dot(tri_ref[...], adt, preferred_element_type=F32, precision=hi)
    acs_row = lax.dot_general(adt, trit_ref[...], (((0,), (0,)), ((), ())), preferred_element_type=F32, precision=hi)
    dt_row = lax.dot_general(dt, eye_ref[...], (((0,), (0,)), ((), ())), preferred_element_type=F32, precision=hi)
    tot = acs[0:1, :] if reverse else acs[ln - 1:ln, :]
    w_all = jnp.exp(tot - acs) * dt
    ea_all = jnp.exp(acs)
    etot = jnp.exp(tot)
    li = lax.broadcasted_iota(jnp.int32, (ln, ln), 0)
    si = lax.broadcasted_iota(jnp.int32, (ln, ln), 1)
    keep = (si >= li) if reverse else (si <= li)
    lane = lax.broadcasted_iota(jnp.int32, (ln, LANES), 1)
    left = lane < SSD_HEAD_DIM
    per_group = SSD_HEADS // SSD_GROUPS
    for gi in range(SSD_GROUPS):
        bm = x_ref[:, SSD_INNER + SSD_STATE * gi:SSD_INNER + SSD_STATE * (gi + 1)].astype(BF16)
        cm = x_ref[:, SSD_INNER + SSD_STATE * (SSD_GROUPS + gi):SSD_INNER + SSD_STATE * (SSD_GROUPS + gi + 1)]
        cm = cm.astype(BF16)
        cb = _dot_nt(cm, bm)
        for pr in range(per_group // 2):
            h0 = gi * per_group + 2 * pr
            xp = x[:, SSD_HEAD_DIM * h0:SSD_HEAD_DIM * (h0 + 2)]
            mats = []
            for hd in (h0, h0 + 1):
                c = lane_off + hd
                seg = acs[:, c:c + 1] - acs_row[c:c + 1, :]
                lmat = jnp.exp(jnp.where(keep, seg, -jnp.inf))
                mats.append((cb * lmat * dt_row[c:c + 1, :]).astype(BF16))
            xb = xp.astype(BF16)
            zero = jnp.zeros_like(xb)
            rhs = jnp.concatenate([jnp.where(left, xb, zero), jnp.where(left, zero, xb)], axis=0)
            y_diag = _dot(jnp.concatenate(mats, axis=1), rhs)
            st = state_ref[h0 // 2]
            c0 = lane_off + h0
            ea = jnp.where(left, ea_all[:, c0:c0 + 1], ea_all[:, c0 + 1:c0 + 2])
            y_off = _dot(cm, st.astype(BF16)) * ea
            y_ref[:, SSD_HEAD_DIM * h0:SSD_HEAD_DIM * (h0 + 2)] = y_diag + y_off
            wcol = jnp.where(left, w_all[:, c0:c0 + 1], w_all[:, c0 + 1:c0 + 2])
            cs = _dot_tn(bm, (xp * wcol).astype(BF16))
            dec = jnp.where(left[0:1, :], etot[:, c0:c0 + 1], etot[:, c0 + 1:c0 + 2])
            state_ref[h0 // 2] = st * dec + cs


def _ssd_call(g, xbc_act, dt, dt_bias, a_log, *, reverse):
    t = xbc_act.shape[0]
    ln = SSD_CHUNK
    cc = g.c // ln
    nl = g.n // ln
    ctx0 = g.b * g.n // ln

    def chunk(b, k):
        if reverse:
            return jnp.where(k < cc, ctx0 + b * cc + (cc - 1 - k), b * nl + (nl - 1 - (k - cc)))
        return jnp.where(k < cc, ctx0 + b * cc + k, b * nl + (k - cc))

    idx = np.arange(ln)
    lower = (idx[:, None] >= idx[None, :]).astype(np.float32)
    tri = lower.T if reverse else lower
    row = lambda b, k: (chunk(b, k), 0)
    off = SSD_HEADS if reverse else 0
    lanes = lambda p: jnp.pad(p.reshape(1, -1), ((0, 0), (off, LANES - off - SSD_HEADS)))
    return pl.pallas_call(
        functools.partial(_ssd_kernel, lane_off=SSD_HEADS if reverse else 0, reverse=reverse),
        grid=(g.b, cc + nl),
        in_specs=[pl.BlockSpec((ln, SSD_XBC), row), pl.BlockSpec((ln, LANES), row),
                  _full((1, LANES)), _full((1, LANES)), _full((ln, ln)), _full((ln, ln)), _full((ln, ln))],
        out_specs=pl.BlockSpec((ln, SSD_INNER), row),
        out_shape=jax.ShapeDtypeStruct((t, SSD_INNER), F32),
        scratch_shapes=[pltpu.VMEM((SSD_HEADS // 2, SSD_STATE, 2 * SSD_HEAD_DIM), F32)],
        compiler_params=_cparams(("parallel", "arbitrary")),
        name="ssd_bwd" if reverse else "ssd_fwd",
    )(xbc_act, dt, lanes(dt_bias), lanes(a_log), jnp.asarray(tri), jnp.asarray(tri.T), jnp.eye(ln, dtype=F32))


def _diff_attn_kernel(qt_ref, klat_ref, kctx_ref, vlat_ref, vctx_ref, lq1_ref, lk1_ref, lq2_ref, lk2_ref, sub_ref,
                      o_ref, *, nq, n_lat_chunks, lambda_init):
    qi = pl.program_id(2)
    tq = qt_ref.shape[2]
    chunks = jnp.where(qi < nq, n_lat_chunks, 0)
    qt = jnp.concatenate([qt_ref[0], qt_ref[1]], axis=1)
    o = _attend(qt, klat_ref, kctx_ref, vlat_ref, vctx_ref, chunks, DIFF_V)
    lam = (jnp.exp(jnp.sum(lq1_ref[...] * lk1_ref[...], axis=1, keepdims=True))
           - jnp.exp(jnp.sum(lq2_ref[...] * lk2_ref[...], axis=1, keepdims=True)) + lambda_init)
    od = o[:, 0:tq] - lam * o[:, tq:2 * tq]
    o_ref[...] = ((_rms(od, 0) * sub_ref[...]) * (1.0 - lambda_init)).astype(o_ref.dtype)


def _diff_attn_call(g, qd, kd, vd, lq1, lk1, lq2, lk2, subln, lambda_init):
    tq = ATTN_Q_TILE
    nq = g.n // tq
    ctx_blk = g.b * g.n // g.c
    qblk = lambda b, j, qi: jnp.where(qi < nq, b * nq + qi, g.b * nq + b)
    vec = lambda a: a.reshape(1, -1)
    return pl.pallas_call(
        functools.partial(_diff_attn_kernel, nq=nq, n_lat_chunks=g.n // ATTN_KV_CHUNK, lambda_init=lambda_init),
        grid=(g.b, DIFF_HEADS, nq + 1),
        in_specs=[
            pl.BlockSpec((2, 128, tq), lambda b, j, qi: (j, 0, qblk(b, j, qi))),
            pl.BlockSpec((None, g.n, 128), lambda b, j, qi: (j, b, 0)),
            pl.BlockSpec((None, g.c, 128), lambda b, j, qi: (j, ctx_blk + b, 0)),
            pl.BlockSpec((None, DIFF_V, g.n), lambda b, j, qi: (j, 0, b)),
            pl.BlockSpec((None, DIFF_V, g.c), lambda b, j, qi: (j, 0, ctx_blk + b)),
            _full((1, DIFF_DIM)), _full((1, DIFF_DIM)), _full((1, DIFF_DIM)), _full((1, DIFF_DIM)),
            _full((DIFF_V, 1)),
        ],
        out_specs=pl.BlockSpec((DIFF_V, tq), lambda b, j, qi: (j, qblk(b, j, qi))),
        out_shape=jax.ShapeDtypeStruct((DIFF_HEADS * DIFF_V, g.t), BF16),
        compiler_params=_cparams(("parallel", "parallel", "arbitrary")),
        name="attn_diff",
    )(qd, kd, kd, vd, vd, vec(lq1), vec(lk1), vec(lq2), vec(lk2), subln.reshape(-1, 1))


def _rec_post_kernel(yf_ref, yb_ref, xs_ref, z_ref, od_ref, dskip_ref, gn_ref, wa_ref, wb_ref,
                     h_ref, mod_ref, gpost_ref, gffn_ref, rw_ref, hn_ref, v_ref, lg_ref):
    y = (yf_ref[...] + yb_ref[...] + xs_ref[...] * dskip_ref[...]) * _silu(z_ref[...].astype(F32))
    half = SSD_INNER // SSD_GROUPS
    o = _dot_tn(od_ref[...], wb_ref[...])
    for gi in range(SSD_GROUPS):
        lo = half * gi
        yn = (_rms(y[:, lo:lo + half], -1) * gn_ref[:, lo:lo + half]).astype(BF16)
        o = o + _dot(yn, wa_ref[lo:lo + half, :])
    _post_tail(o, h_ref, mod_ref, gpost_ref, gffn_ref, rw_ref, hn_ref, v_ref, lg_ref)


def _rec_post_call(g, yf, yb, xbc_act, zs, od, d_skip, ssd_norm, w_out, h, mod, gpost, gffn, router_w):
    d = h.shape[1]
    rb = ROW_BLOCK
    ins, outs, shapes = _post_specs(g, d)
    wo = w_out.astype(BF16)
    row512 = pl.BlockSpec((rb, SSD_INNER), lambda i: (i, 0))
    return pl.pallas_call(
        _rec_post_kernel, grid=(g.n_blocks,),
        in_specs=[row512, row512, row512, row512, pl.BlockSpec((SSD_INNER, rb), lambda i: (0, i)),
                  _full((1, SSD_INNER)), _full((1, SSD_INNER)), _full((SSD_INNER, d)), _full((SSD_INNER, d))] + ins,
        out_specs=outs, out_shape=shapes,
        compiler_params=_cparams(("parallel",)), name="rec_post",
    )(yf, yb, xbc_act, zs, od, jnp.repeat(d_skip, SSD_HEAD_DIM).reshape(1, -1), ssd_norm.reshape(1, -1),
      wo[:SSD_INNER], wo[SSD_INNER:], h, mod, gpost.reshape(1, d), gffn.reshape(1, d), router_w.T)


def _ffn_res_kernel(h_ref, f_ref, mod_ref, g_ref, o_ref):
    o_ref[...] = h_ref[...] + mod_ref[5:6, :] * (_rms(f_ref[...], -1) * g_ref[...])


def _ffn_res_call(g, h, f, mod, gain, n_rows):
    d = h.shape[1]
    rb = ROW_BLOCK
    row = pl.BlockSpec((rb, d), lambda i: (i, 0))
    return pl.pallas_call(
        _ffn_res_kernel, grid=(n_rows // rb,),
        in_specs=[row, row, pl.BlockSpec((None, 6, d), lambda i: (g.mod_row(i), 0, 0)), _full((1, d))],
        out_specs=row, out_shape=jax.ShapeDtypeStruct((n_rows, d), F32),
        compiler_params=_cparams(("parallel",)), name="ffn_res",
    )(h, f, mod, gain.reshape(1, d))


def kernel(x, c, ctx, c_ctx, ada_w, ada_b, norm_mix_pre, norm_mix_post, norm_ffn_pre, norm_ffn_post, mix_w_out, att_w_in, mla_q_norm, mla_w_q_up, mla_kv_norm, mla_w_kv_up, gqa_q_norm, gqa_k_norm, rec_w_in, ssd_conv_w, ssd_conv_b, ssd_dt_bias_f, ssd_dt_bias_b, ssd_a_log_f, ssd_a_log_b, ssd_d, ssd_norm, diff_lambda_q1, diff_lambda_k1, diff_lambda_q2, diff_lambda_k2, diff_subln, router_w, router_b, exp_w_gate, exp_w_up, exp_w_down, sh_w_gate, sh_w_up, sh_w_down):
    b, n, d = x.shape
    n_ctx = ctx.shape[1]
    depth = ada_w.shape[0]
    g = _Geom(b, n, n_ctx)
    mod_rows = -(-(b + 1) // 8) * 8
    c_all = jnp.concatenate([c, c_ctx[None, :], jnp.zeros((mod_rows - b - 1, d), F32)], axis=0)
    mods = _ada_call(c_all, ada_w, ada_b).reshape(depth, mod_rows, 6, d)
    h = jnp.concatenate([x.reshape(b * n, d), ctx.reshape(b * n_ctx, d)], axis=0)
    for i in range(depth):
        last = i == depth - 1
        jdx = i // 2
        mod = mods[i]
        if i % 2 == 0:
            w = _attn_pre_weights(g, att_w_in[jdx], mla_q_norm[jdx], mla_w_q_up[jdx], mla_kv_norm[jdx],
                                  mla_w_kv_up[jdx], gqa_q_norm[jdx], gqa_k_norm[jdx])
            qm, km, vm, qg, kg, vg = _attn_pre_call(g, h, mod, norm_mix_pre[i].reshape(1, d), w)
            oa = _attn_call(g, qm, km, vm, kv_heads=MLA_HEADS, shared_k=False, name="attn_mla")
            ob = _attn_call(g, qg, kg, vg, kv_heads=GQA_KV_HEADS, shared_k=True, name="attn_gqa")
            h, v, logits_t = _attn_post_call(g, oa, ob, mix_w_out[i], h, mod, norm_mix_post[i], norm_ffn_pre[i],
                                             router_w[i])
        else:
            lambda_init = 0.8 - 0.6 * math.exp(-0.3 * i)
            w = _rec_pre_weights(g, rec_w_in[jdx])
            zs, xbc, dt, kd, qd, vd = _rec_pre_call(g, h, mod, norm_mix_pre[i].reshape(1, d), w)
            xbc_act = _conv_call(g, xbc, ssd_conv_w[jdx], ssd_conv_b[jdx])
            yf = _ssd_call(g, xbc_act, dt, ssd_dt_bias_f[jdx], ssd_a_log_f[jdx], reverse=False)
            yb = _ssd_call(g, xbc_act, dt, ssd_dt_bias_b[jdx], ssd_a_log_b[jdx], reverse=True)
            od = _diff_attn_call(g, qd, kd, vd, diff_lambda_q1[jdx], diff_lambda_k1[jdx], diff_lambda_q2[jdx],
                                 diff_lambda_k2[jdx], diff_subln[jdx], lambda_init)
            h, v, logits_t = _rec_post_call(g, yf, yb, xbc_act, zs, od, ssd_d[jdx], ssd_norm[jdx], mix_w_out[i], h,
                                            mod, norm_mix_post[i], norm_ffn_pre[i], router_w[i])
        gates, rank, counts = _router_call(g, logits_t, router_b[i])
        f = _moe_call(g, v, gates, rank, counts, exp_w_gate[i], exp_w_up[i], exp_w_down[i],
                      sh_w_gate[i], sh_w_up[i], sh_w_down[i])
        h = _ffn_res_call(g, h, f, mod, norm_ffn_post[i], b * n if last else g.t)
    return h.reshape(b, n, d)
```

```python
import functools
import math

import numpy as np
import jax
import jax.numpy as jnp
from jax import lax
from jax.experimental import pallas as pl
from jax.experimental.pallas import tpu as pltpu

F32 = jnp.float32
BF16 = jnp.bfloat16
LOG2E = 1.4426950408889634

GRID_W = 64
ROPE_BASE = 10000.0
NORM_EPS = 1e-6

MLA_HEADS, MLA_Q_LORA, MLA_KV_LORA, MLA_NOPE, MLA_ROPE, MLA_V = 8, 256, 128, 64, 32, 64
MLA_SCALE = (MLA_NOPE + MLA_ROPE) ** -0.5
GQA_HEADS, GQA_KV_HEADS, GQA_DIM = 8, 2, 64
GQA_SCALE = GQA_DIM ** -0.5
SSD_HEADS, SSD_HEAD_DIM, SSD_GROUPS, SSD_STATE, SSD_CONV, SSD_CHUNK = 8, 64, 2, 128, 3, 128
SSD_INNER = SSD_HEADS * SSD_HEAD_DIM
SSD_XBC = SSD_INNER + 2 * SSD_GROUPS * SSD_STATE
DIFF_HEADS, DIFF_DIM = 4, 64
DIFF_V = 2 * DIFF_DIM
DIFF_SCALE = DIFF_DIM ** -0.5
N_EXPERTS, TOP_K, N_GROUPS, TOPK_GROUPS, EXPERT_FF, SHARED_FF = 64, 8, 8, 4, 256, 256
ROUTED_SCALE = 2.5

LANES = 128
VMEM_LIMIT_BYTES = 56 * 1024 * 1024

ROW_BLOCK = 256
ATTN_Q_TILE = 512
DIFF_Q_TILE = 256
ATTN_KV_CHUNK = 512
ATTN_EXP_ROWS = 2048
MOE_TOKENS = 768
MOE_ROWS = 128
MOE_PAIR = 4


def _cparams(sem):
    return pltpu.CompilerParams(dimension_semantics=sem, vmem_limit_bytes=VMEM_LIMIT_BYTES)


def _rms(x, axis):
    return x * lax.rsqrt(jnp.mean(x * x, axis=axis, keepdims=True) + NORM_EPS)


def _silu(x):
    return x * jax.nn.sigmoid(x)


def _dot(a, b):
    return jnp.dot(a, b, preferred_element_type=F32)


def _dot_nt(a, b):
    return lax.dot_general(a, b, (((1,), (1,)), ((), ())), preferred_element_type=F32)


def _dot_tn(a, b):
    return lax.dot_general(a, b, (((0,), (0,)), ((), ())), preferred_element_type=F32)


def _ada_kernel(c_ref, w_ref, b_ref, o_ref):
    s = _silu(c_ref[...])
    o_ref[...] = jnp.dot(s, w_ref[...], preferred_element_type=F32, precision=lax.Precision.HIGHEST) + b_ref[...]


def _ada_call(c_all, ada_w, ada_b):
    depth, d, six_d = ada_w.shape
    rows = c_all.shape[0]
    cols = six_d // 4
    return pl.pallas_call(
        _ada_kernel,
        grid=(depth, six_d // cols),
        in_specs=[
            pl.BlockSpec((rows, d), lambda i, j: (0, 0)),
            pl.BlockSpec((None, d, cols), lambda i, j: (i, 0, j)),
            pl.BlockSpec((None, 1, cols), lambda i, j: (i, 0, j)),
        ],
        out_specs=pl.BlockSpec((None, rows, cols), lambda i, j: (i, 0, j)),
        out_shape=jax.ShapeDtypeStruct((depth, rows, six_d), F32),
        compiler_params=_cparams(("arbitrary", "arbitrary")),
        name="ada_mod",
    )(c_all, ada_w, ada_b.reshape(depth, 1, six_d))


def _attn_pre_kernel(h_ref, mod_ref, gpre_ref, wfm_ref, wtm_ref, qn_ref, kvn_col_ref, kvn_row_ref, wq_ref, wk_ref,
                     wv_ref, gq_g_ref, gq_gr_ref, gk_g_ref, gk_gr_ref,
                     cqm_ref, sqm_ref, cqh_ref, sqh_ref, ckm_ref, skm_ref, ckh_ref, skh_ref,
                     qm_ref, km_ref, vm_ref, qg_ref, kg_ref, vg_ref):
    h = h_ref[...]
    u = _rms(h, -1) * gpre_ref[...]
    u = u * (1.0 + mod_ref[1:2, :]) + mod_ref[0:1, :]
    ub = u.astype(BF16)
    zt = _dot_nt(wfm_ref[...], ub)
    zk = _dot(ub, wtm_ref[...])

    qn = (_rms(zt[0:256], 0) * qn_ref[...]).astype(BF16)
    qt = _dot(wq_ref[...], qn)
    cq, sq = cqm_ref[...], sqm_ref[...]
    qs = MLA_SCALE * LOG2E
    zero32 = jnp.zeros((32, h.shape[0]), BF16)
    for hd in range(MLA_HEADS):
        pe = qt[512 + 32 * hd:544 + 32 * hd] * cq + qt[768 + 32 * hd:800 + 32 * hd] * sq
        qm_ref[hd, 0:64, :] = (qt[64 * hd:64 * hd + 64] * qs).astype(BF16)
        qm_ref[hd, 64:96, :] = (pe * qs).astype(BF16)
        qm_ref[hd, 96:128, :] = zero32

    kvn_t = (_rms(zt[256:384], 0) * kvn_col_ref[...]).astype(BF16)
    vt = _dot(wv_ref[...], kvn_t)
    for hd in range(MLA_HEADS):
        vm_ref[hd] = vt[64 * hd:64 * hd + 64].astype(BF16)
    kvn = (_rms(zk[:, 0:128], -1) * kvn_row_ref[...]).astype(BF16)
    kn = _dot(kvn, wk_ref[...])
    kpe = zk[:, 128:256] * ckm_ref[...] + zk[:, 256:384] * skm_ref[...]
    for hd in range(MLA_HEADS):
        km_ref[hd] = (kn[:, 128 * hd:128 * hd + 128] + kpe).astype(BF16)

    cqh, sqh = cqh_ref[...], sqh_ref[...]
    gs = GQA_SCALE * LOG2E
    zero64 = jnp.zeros((64, h.shape[0]), BF16)
    grp = GQA_HEADS // GQA_KV_HEADS
    for hd in range(GQA_HEADS):
        raw = zt[384 + 64 * hd:448 + 64 * hd]
        rot = zt[896 + 64 * hd:960 + 64 * hd]
        r = lax.rsqrt(jnp.mean(raw * raw, axis=0, keepdims=True) + NORM_EPS)
        q = (raw * r * gq_g_ref[...]) * cqh + (rot * r * gq_gr_ref[...]) * sqh
        q = (q * gs).astype(BF16)
        if hd // grp == 0:
            qg_ref[hd, 0:64, :] = q
            qg_ref[hd, 64:128, :] = zero64
        else:
            qg_ref[hd, 0:64, :] = zero64
            qg_ref[hd, 64:128, :] = q
    for kvh in range(GQA_KV_HEADS):
        vg_ref[kvh] = zt[1408 + 64 * kvh:1472 + 64 * kvh].astype(BF16)

    gk, gkr = zk[:, 384:512], zk[:, 512:640]
    lane = lax.broadcasted_iota(jnp.int32, gk.shape, 1)
    lo = lane < 64
    sq0 = jnp.sum(jnp.where(lo, gk * gk, 0.0), axis=-1, keepdims=True)
    sq1 = jnp.sum(jnp.where(lo, 0.0, gk * gk), axis=-1, keepdims=True)
    r = jnp.where(lo, lax.rsqrt(sq0 / GQA_DIM + NORM_EPS), lax.rsqrt(sq1 / GQA_DIM + NORM_EPS))
    kg_ref[...] = ((gk * r * gk_g_ref[...]) * ckh_ref[...] + (gkr * r * gk_gr_ref[...]) * skh_ref[...]).astype(BF16)


def _rot_map(dim):
    q = dim // 4
    j = np.arange(dim)
    even = (j // q) % 2 == 0
    return np.where(even, j + q, j - q), np.where(even, -1.0, 1.0).astype(np.float32)


def _rope_tables(n_tokens, dim, pad_rows):
    rows = n_tokens // GRID_W
    row = jnp.repeat(jnp.arange(rows, dtype=F32), GRID_W)
    col = jnp.tile(jnp.arange(GRID_W, dtype=F32), rows)
    half = dim // 2
    inv = ROPE_BASE ** (-(jnp.arange(half // 2, dtype=F32) * 2.0 / half))
    ang_r = row[:, None] * inv
    ang_c = col[:, None] * inv
    ang = jnp.concatenate([ang_r, ang_r, ang_c, ang_c], axis=-1)
    cos = jnp.concatenate([jnp.cos(ang), jnp.ones((pad_rows, dim), F32)], axis=0)
    sin = jnp.concatenate([jnp.sin(ang), jnp.zeros((pad_rows, dim), F32)], axis=0)
    return cos, sin


class _Geom:
    def __init__(self, b, n, c):
        assert c == ROW_BLOCK and n % ROW_BLOCK == 0 and n % GRID_W == 0
        self.b, self.n, self.c = b, n, c
        self.t = b * n + b * c
        self.lat_blocks = b * n // ROW_BLOCK
        self.blocks_per_seq = n // ROW_BLOCK
        self.n_blocks = self.t // ROW_BLOCK
        assert self.t % MOE_TOKENS == 0

    def mod_row(self, i):
        return jnp.where(i < self.lat_blocks, i // self.blocks_per_seq, self.b)

    def pos_block(self, i):
        return jnp.where(i < self.lat_blocks, i % self.blocks_per_seq, self.blocks_per_seq)


def _full(shape):
    nd = len(shape)
    return pl.BlockSpec(shape, lambda *_: (0,) * nd)


def _attn_pre_call(g, h, mod, gpre, w):
    t, d = h.shape
    rb = ROW_BLOCK
    row = lambda i: (i, 0)
    tm_tab = pl.BlockSpec((rb, LANES), lambda i: (g.pos_block(i), 0))
    fm32 = pl.BlockSpec((32, rb), lambda i: (0, g.pos_block(i)))
    fm64 = pl.BlockSpec((64, rb), lambda i: (0, g.pos_block(i)))
    in_specs = [
        pl.BlockSpec((rb, d), row),
        pl.BlockSpec((None, 6, d), lambda i: (g.mod_row(i), 0, 0)),
        _full(gpre.shape), _full(w["wfm"].shape), _full(w["wtm"].shape), _full(w["qn"].shape),
        _full(w["kvn_col"].shape), _full(w["kvn_row"].shape), _full(w["wq"].shape), _full(w["wk"].shape),
        _full(w["wv"].shape), _full(w["gq_g"].shape), _full(w["gq_gr"].shape), _full(w["gk_g"].shape),
        _full(w["gk_gr"].shape),
        fm32, fm32, fm64, fm64, tm_tab, tm_tab, tm_tab, tm_tab,
    ]
    out_shape = [
        jax.ShapeDtypeStruct((MLA_HEADS, 128, t), BF16),
        jax.ShapeDtypeStruct((MLA_HEADS, t, 128), BF16),
        jax.ShapeDtypeStruct((MLA_HEADS, MLA_V, t), BF16),
        jax.ShapeDtypeStruct((GQA_HEADS, 128, t), BF16),
        jax.ShapeDtypeStruct((t, 128), BF16),
        jax.ShapeDtypeStruct((GQA_KV_HEADS, GQA_DIM, t), BF16),
    ]
    out_specs = [
        pl.BlockSpec((MLA_HEADS, 128, rb), lambda i: (0, 0, i)),
        pl.BlockSpec((MLA_HEADS, rb, 128), lambda i: (0, i, 0)),
        pl.BlockSpec((MLA_HEADS, MLA_V, rb), lambda i: (0, 0, i)),
        pl.BlockSpec((GQA_HEADS, 128, rb), lambda i: (0, 0, i)),
        pl.BlockSpec((rb, 128), row),
        pl.BlockSpec((GQA_KV_HEADS, GQA_DIM, rb), lambda i: (0, 0, i)),
    ]
    return pl.pallas_call(
        _attn_pre_kernel, grid=(g.n_blocks,), in_specs=in_specs, out_specs=out_specs, out_shape=out_shape,
        compiler_params=_cparams(("parallel",)), name="attn_pre",
    )(h, mod, gpre, w["wfm"], w["wtm"], w["qn"], w["kvn_col"], w["kvn_row"], w["wq"], w["wk"], w["wv"],
      w["gq_g"], w["gq_gr"], w["gk_g"], w["gk_gr"],
      w["cqm"], w["sqm"], w["cqh"], w["sqh"], w["ckm"], w["skm"], w["ckh"], w["skh"])


def _attn_pre_weights(g, att_w_in, q_norm, w_q_up, kv_norm, w_kv_up, gqa_qn, gqa_kn):
    d = att_w_in.shape[0]
    o = np.cumsum([0, MLA_Q_LORA, MLA_KV_LORA, MLA_ROPE, GQA_HEADS * GQA_DIM, GQA_KV_HEADS * GQA_DIM,
                   GQA_KV_HEADS * GQA_DIM])
    w_qlat, w_kvlat, w_kpe, w_gq, w_gk, w_gv = (att_w_in[:, o[i]:o[i + 1]] for i in range(6))
    src32, sgn32 = _rot_map(MLA_ROPE)
    src64, sgn64 = _rot_map(GQA_DIM)

    def rot_heads(wcols, heads, dim, src, sgn):
        wh = wcols.reshape(d, heads, dim)
        return (wh[:, :, src] * sgn).reshape(d, heads * dim)

    wfm = jnp.concatenate([w_qlat, w_kvlat, w_gq, rot_heads(w_gq, GQA_HEADS, GQA_DIM, src64, sgn64), w_gv], axis=1)
    zpad = lambda x, lo, hi: jnp.pad(x, ((0, 0), (lo, hi)))
    wtm = jnp.concatenate([
        w_kvlat, zpad(w_kpe, 64, 32), zpad(w_kpe[:, src32] * sgn32, 64, 32),
        w_gk, rot_heads(w_gk, GQA_KV_HEADS, GQA_DIM, src64, sgn64)], axis=1)
    wq = w_q_up.reshape(MLA_Q_LORA, MLA_HEADS, MLA_NOPE + MLA_ROPE)
    wq_pe = wq[:, :, MLA_NOPE:]
    wq_all = jnp.concatenate([
        wq[:, :, :MLA_NOPE].reshape(MLA_Q_LORA, -1), wq_pe.reshape(MLA_Q_LORA, -1),
        (wq_pe[:, :, src32] * sgn32).reshape(MLA_Q_LORA, -1)], axis=1)
    wkv = w_kv_up.reshape(MLA_KV_LORA, MLA_HEADS, MLA_NOPE + MLA_V)
    wk = jnp.pad(wkv[:, :, :MLA_NOPE], ((0, 0), (0, 0), (0, 128 - MLA_NOPE))).reshape(MLA_KV_LORA, -1)
    wv = wkv[:, :, MLA_NOPE:].reshape(MLA_KV_LORA, -1)
    cos_m, sin_m = _rope_tables(g.n, MLA_ROPE, ROW_BLOCK)
    cos_h, sin_h = _rope_tables(g.n, GQA_DIM, ROW_BLOCK)
    two = lambda x: jnp.concatenate([x, x], axis=1)
    return dict(
        wfm=wfm.T.astype(BF16), wtm=wtm.astype(BF16),
        qn=q_norm.reshape(-1, 1), kvn_col=kv_norm.reshape(-1, 1), kvn_row=kv_norm.reshape(1, -1),
        wq=wq_all.T.astype(BF16), wk=wk.astype(BF16), wv=wv.T.astype(BF16),
        gq_g=gqa_qn.reshape(-1, 1), gq_gr=gqa_qn[src64].reshape(-1, 1),
        gk_g=two(gqa_kn.reshape(1, -1)), gk_gr=two(gqa_kn[src64].reshape(1, -1)),
        cqm=cos_m.T, sqm=sin_m.T, cqh=cos_h.T, sqh=sin_h.T,
        ckm=zpad(cos_m, 64, 32), skm=zpad(sin_m, 64, 32), ckh=two(cos_h), skh=two(sin_h),
    )


def _fold8(x, op):
    r, w = x.shape
    return op(x.reshape(r // 8, 8, w), axis=0)


def _attend(qt, segments, s_ref, dv):
    tq = qt.shape[1]
    pieces, off = [], 0
    for k_ref, vt_ref in segments:
        n = k_ref.shape[0]
        for lo in range(0, n, ATTN_KV_CHUNK):
            rows = min(ATTN_KV_CHUNK, n - lo)
            pieces.append((off, lo, rows, k_ref, vt_ref))
            off += rows
    m8 = None
    for so, lo, rows, k_ref, _ in pieces:
        s = _dot(k_ref[lo:lo + rows, :], qt)
        s_ref[so:so + rows, :] = s
        part = _fold8(s, jnp.max)
        m8 = part if m8 is None else jnp.maximum(m8, part)
    m = jnp.max(m8, axis=0, keepdims=True)
    acc = jnp.zeros((dv + 16, tq), F32)
    off = 0
    for _, vt_ref in segments:
        n = vt_ref.shape[1]
        step = min(n, ATTN_EXP_ROWS)

        def body(i, acc, off=off, vt_ref=vt_ref, step=step):
            base = i * step if isinstance(i, int) else pl.multiple_of(i * step, step)
            for lo in range(0, step, ATTN_KV_CHUNK):
                rows = min(ATTN_KV_CHUNK, step - lo)
                p = jnp.exp2((s_ref[pl.ds(off + base + lo, rows), :] - m).astype(BF16))
                vt1 = jnp.concatenate([vt_ref[:, pl.ds(base + lo, rows)], jnp.ones((16, rows), BF16)], axis=0)
                acc = acc + _dot(vt1, p)
            return acc

        acc = body(0, acc) if n == step else lax.fori_loop(0, n // step, body, acc)
        off += n
    return acc[0:dv] / acc[dv:dv + 1]


def _attn_kernel(qt_ref, *refs, dv):
    o_ref, s_ref = refs[-2:]
    kv = refs[:-2]
    half = len(kv) // 2
    segs = [(kv[i], kv[half + i]) for i in range(half)]
    o_ref[...] = _attend(qt_ref[...], segs, s_ref, dv).astype(o_ref.dtype)


def _attn_call(g, qt, k, vt, *, kv_heads, shared_k, name):
    heads = qt.shape[0]
    dv = vt.shape[1]
    grp = heads // kv_heads
    tq = ATTN_Q_TILE
    nq = g.n // tq
    ctx_blk = g.b * g.n // g.c
    if shared_k:
        klat = pl.BlockSpec((g.n, 128), lambda b, h, *_: (b, 0))
        kctx = pl.BlockSpec((g.c, 128), lambda b, h, *_: (ctx_blk + b, 0))
    else:
        klat = pl.BlockSpec((None, g.n, 128), lambda b, h, *_: (h // grp, b, 0))
        kctx = pl.BlockSpec((None, g.c, 128), lambda b, h, *_: (h // grp, ctx_blk + b, 0))
    vlat = pl.BlockSpec((None, dv, g.n), lambda b, h, *_: (h // grp, 0, b))
    vctx = pl.BlockSpec((None, dv, g.c), lambda b, h, *_: (h // grp, 0, ctx_blk + b))
    lat = pl.pallas_call(
        functools.partial(_attn_kernel, dv=dv),
        grid=(g.b, heads, nq),
        scratch_shapes=[pltpu.VMEM((g.n + g.c, tq), F32)],
        in_specs=[pl.BlockSpec((None, 128, tq), lambda b, h, qi: (h, 0, b * nq + qi)), klat, kctx, vlat, vctx],
        out_specs=pl.BlockSpec((dv, tq), lambda b, h, qi: (h, b * nq + qi)),
        out_shape=jax.ShapeDtypeStruct((heads * dv, g.b * g.n), BF16),
        compiler_params=_cparams(("parallel", "parallel", "arbitrary")),
        name=name,
    )(qt, k, k, vt, vt)
    ctx = pl.pallas_call(
        functools.partial(_attn_kernel, dv=dv),
        grid=(g.b, heads),
        scratch_shapes=[pltpu.VMEM((g.c, g.c), F32)],
        in_specs=[pl.BlockSpec((None, 128, g.c), lambda b, h: (h, 0, ctx_blk + b)), kctx, vctx],
        out_specs=pl.BlockSpec((dv, g.c), lambda b, h: (h, b)),
        out_shape=jax.ShapeDtypeStruct((heads * dv, g.b * g.c), BF16),
        compiler_params=_cparams(("parallel", "parallel")),
        name=name + "_ctx",
    )(qt, k, vt)
    return jnp.concatenate([lat, ctx], axis=1)


def _split_bf16(x):
    hi = x.astype(BF16)
    return hi, (x - hi.astype(F32)).astype(BF16)


def _post_tail(o, h_ref, mod_ref, gpost_ref, gffn_ref, rw_ref, hn_ref, v_ref, lg_ref):
    hn = h_ref[...] + mod_ref[2:3, :] * (_rms(o, -1) * gpost_ref[...])
    hn_ref[...] = hn
    v = (_rms(hn, -1) * gffn_ref[...]) * (1.0 + mod_ref[4:5, :]) + mod_ref[3:4, :]
    v_hi, v_lo = _split_bf16(v)
    v_ref[...] = v_hi
    w_hi, w_lo = _split_bf16(rw_ref[...])
    lg_ref[...] = _dot_nt(w_hi, v_hi) + (_dot_nt(w_hi, v_lo) + _dot_nt(w_lo, v_hi))


def _attn_post_kernel(oa_ref, ob_ref, wa_ref, wb_ref, h_ref, mod_ref, gpost_ref, gffn_ref, rw_ref,
                      hn_ref, v_ref, lg_ref):
    o = _dot_tn(oa_ref[...], wa_ref[...]) + _dot_tn(ob_ref[...], wb_ref[...])
    _post_tail(o, h_ref, mod_ref, gpost_ref, gffn_ref, rw_ref, hn_ref, v_ref, lg_ref)


def _post_specs(g, d):
    rb = ROW_BLOCK
    ins = [
        pl.BlockSpec((rb, d), lambda i: (i, 0)),
        pl.BlockSpec((None, 6, d), lambda i: (g.mod_row(i), 0, 0)),
        _full((1, d)), _full((1, d)), _full((N_EXPERTS, d)),
    ]
    outs = [pl.BlockSpec((rb, d), lambda i: (i, 0)), pl.BlockSpec((rb, d), lambda i: (i, 0)),
            pl.BlockSpec((N_EXPERTS, rb), lambda i: (0, i))]
    shapes = [jax.ShapeDtypeStruct((g.t, d), F32), jax.ShapeDtypeStruct((g.t, d), BF16),
              jax.ShapeDtypeStruct((N_EXPERTS, g.t), F32)]
    return ins, outs, shapes


def _attn_post_call(g, oa, ob, w_out, h, mod, gpost, gffn, router_w):
    d = h.shape[1]
    rb = ROW_BLOCK
    half = oa.shape[0]
    ins, outs, shapes = _post_specs(g, d)
    wo = w_out.astype(BF16)
    return pl.pallas_call(
        _attn_post_kernel, grid=(g.n_blocks,),
        in_specs=[pl.BlockSpec((half, rb), lambda i: (0, i)), pl.BlockSpec((half, rb), lambda i: (0, i)),
                  _full((half, d)), _full((half, d))] + ins,
        out_specs=outs, out_shape=shapes,
        compiler_params=_cparams(("parallel",)), name="attn_post",
    )(oa, ob, wo[:half], wo[half:], h, mod, gpost.reshape(1, d), gffn.reshape(1, d), router_w.T)


def _router_kernel(lg_ref, bias_ref, tri_ref, gate_ref, rank_ref, cnt_ref):
    tb = lg_ref.shape[1]
    per = N_EXPERTS // N_GROUPS
    shp = (N_GROUPS, per, tb)
    scores = jax.nn.sigmoid(lg_ref[...])
    s3 = scores.reshape(shp)
    sel = (scores + bias_ref[...]).reshape(shp)
    sub = lax.broadcasted_iota(jnp.int32, shp, 1)
    grp = lax.broadcasted_iota(jnp.int32, shp, 0)
    neg = -jnp.inf
    m1 = jnp.max(sel, axis=1, keepdims=True)
    i1 = jnp.min(jnp.where(sel == m1, sub, per), axis=1, keepdims=True)
    m2 = jnp.max(jnp.where(sub == i1, neg, sel), axis=1, keepdims=True)
    cur = jnp.broadcast_to(m1 + m2, shp)
    gmask = jnp.zeros(shp, F32)
    for _ in range(TOPK_GROUPS):
        gm = jnp.max(cur, axis=0, keepdims=True)
        gi = jnp.min(jnp.where(cur == gm, grp, N_GROUPS), axis=0, keepdims=True)
        pick = grp == gi
        gmask = jnp.where(pick, 1.0, gmask)
        cur = jnp.where(pick, neg, cur)
    masked = jnp.where(gmask > 0.0, sel, neg)
    eidx = grp * per + sub
    chosen = jnp.zeros(shp, F32)
    for _ in range(TOP_K):
        mx = jnp.max(jnp.max(masked, axis=1, keepdims=True), axis=0, keepdims=True)
        cand = jnp.where(masked == mx, eidx, N_EXPERTS)
        ei = jnp.min(jnp.min(cand, axis=1, keepdims=True), axis=0, keepdims=True)
        pick = eidx == ei
        chosen = jnp.where(pick, 1.0, chosen)
        masked = jnp.where(pick, neg, masked)
    top_w = jnp.where(chosen > 0.0, s3, 0.0)
    denom = jnp.sum(jnp.sum(top_w, axis=1, keepdims=True), axis=0, keepdims=True)
    gate_ref[...] = (top_w / denom * ROUTED_SCALE).reshape(N_EXPERTS, tb)
    ch2 = chosen.reshape(N_EXPERTS, tb)
    chb = ch2.astype(BF16)
    before = _dot(chb, tri_ref[...])
    rank_ref[...] = jnp.where(ch2 > 0.0, before, -1.0).astype(jnp.int32)
    cnt_ref[...] = _dot(chb, jnp.ones((tb, LANES), BF16))


def _router_call(g, logits_t, router_b):
    tb = MOE_TOKENS
    nsb = g.t // tb
    tri = (np.arange(tb)[:, None] < np.arange(tb)[None, :]).astype(np.float32)
    blk = pl.BlockSpec((N_EXPERTS, tb), lambda s: (0, s))
    gates, rank, cnt = pl.pallas_call(
        _router_kernel, grid=(nsb,),
        in_specs=[blk, _full((N_EXPERTS, 1)), _full((tb, tb))],
        out_specs=[blk, blk, pl.BlockSpec((None, N_EXPERTS, LANES), lambda s: (s, 0, 0))],
        out_shape=[jax.ShapeDtypeStruct((N_EXPERTS, g.t), F32), jax.ShapeDtypeStruct((N_EXPERTS, g.t), jnp.int32),
                   jax.ShapeDtypeStruct((nsb, N_EXPERTS, LANES), F32)],
        compiler_params=_cparams(("parallel",)), name="router",
    )(logits_t, router_b.reshape(-1, 1), jnp.asarray(tri, BF16))
    return gates, rank, cnt[:, :, 0].astype(jnp.int32).reshape(-1)


def _moe_kernel(cnt_ref, v_ref, gate_ref, rank_ref, wg_ref, wu_ref, wd_ref, sg_ref, su_ref, sd_ref, o_ref):
    s, j = pl.program_id(0), pl.program_id(1)
    tb = v_ref.shape[0]
    rows = MOE_ROWS

    @pl.when(j == 0)
    def _():
        vb = v_ref[...]
        hid = (_silu(_dot(vb, sg_ref[...])) * _dot(vb, su_ref[...])).astype(BF16)
        o_ref[...] = _dot(hid, sd_ref[...])

    base = s * N_EXPERTS + j * MOE_PAIR
    n_max = cnt_ref[base]
    for e in range(1, MOE_PAIR):
        n_max = jnp.maximum(n_max, cnt_ref[base + e])
    passes = (n_max + rows - 1) // rows
    row_id = lax.broadcasted_iota(jnp.int32, (rows, tb), 0)

    def body(c, carry):
        vb = v_ref[...]
        sel, outs = [], []
        for e in range(MOE_PAIR):
            ex = j * MOE_PAIR + e
            rk = rank_ref[pl.ds(ex, 1), :] - c * rows
            hit = row_id == rk
            onehot = jnp.where(hit, 1.0, 0.0).astype(BF16)
            w_row = jnp.sum(jnp.where(hit, gate_ref[pl.ds(ex, 1), :], 0.0), axis=1, keepdims=True)
            xs = _dot(onehot, vb).astype(BF16)
            hid = (_silu(_dot(xs, wg_ref[e])) * _dot(xs, wu_ref[e])).astype(BF16)
            y = _dot(hid, wd_ref[e])
            sel.append(onehot)
            outs.append((y * w_row).astype(BF16))
        o_ref[...] += _dot_tn(jnp.concatenate(sel, axis=0), jnp.concatenate(outs, axis=0))
        return carry

    lax.fori_loop(0, passes, body, 0)


def _moe_call(g, v, gates, rank, counts, wg, wu, wd, sg, su, sd):
    t, d = v.shape
    tb = MOE_TOKENS
    ff = wg.shape[2]
    grid_spec = pltpu.PrefetchScalarGridSpec(
        num_scalar_prefetch=1,
        grid=(t // tb, N_EXPERTS // MOE_PAIR),
        in_specs=[
            pl.BlockSpec((tb, d), lambda s, j, c: (s, 0)),
            pl.BlockSpec((N_EXPERTS, tb), lambda s, j, c: (0, s)),
            pl.BlockSpec((N_EXPERTS, tb), lambda s, j, c: (0, s)),
            pl.BlockSpec((MOE_PAIR, d, ff), lambda s, j, c: (j, 0, 0)),
            pl.BlockSpec((MOE_PAIR, d, ff), lambda s, j, c: (j, 0, 0)),
            pl.BlockSpec((MOE_PAIR, ff, d), lambda s, j, c: (j, 0, 0)),
            pl.BlockSpec((d, sg.shape[1]), lambda s, j, c: (0, 0)),
            pl.BlockSpec((d, su.shape[1]), lambda s, j, c: (0, 0)),
            pl.BlockSpec((sd.shape[0], d), lambda s, j, c: (0, 0)),
        ],
        out_specs=pl.BlockSpec((tb, d), lambda s, j, c: (s, 0)),
    )
    return pl.pallas_call(
        _moe_kernel, grid_spec=grid_spec, out_shape=jax.ShapeDtypeStruct((t, d), F32),
        compiler_params=_cparams(("parallel", "arbitrary")), name="moe",
    )(counts, v, gates, rank, wg.astype(BF16), wu.astype(BF16), wd.astype(BF16),
      sg.astype(BF16), su.astype(BF16), sd.astype(BF16))


def _rec_pre_kernel(h_ref, mod_ref, gpre_ref, wtm_ref, wfm_ref, cqh_ref, sqh_ref, ckh_ref, skh_ref,
                    z_ref, xbc_ref, dt_ref, kd_ref, qd_ref, vd_ref):
    h = h_ref[...]
    u = _rms(h, -1) * gpre_ref[...]
    u = u * (1.0 + mod_ref[1:2, :]) + mod_ref[0:1, :]
    ub = u.astype(BF16)
    zk = _dot(ub, wtm_ref[...])
    zt = _dot_nt(wfm_ref[...], ub)
    z_ref[...] = zk[:, 0:512].astype(z_ref.dtype)
    xbc_ref[...] = zk[:, 512:1536]
    dt_ref[...] = zk[:, 1536:1664]
    ck, sk = ckh_ref[...], skh_ref[...]
    for j in range(DIFF_HEADS):
        lo = 1664 + 128 * j
        kd_ref[j] = (zk[:, lo:lo + 128] * ck + zk[:, lo + 512:lo + 640] * sk).astype(BF16)
    cq, sq = cqh_ref[...], sqh_ref[...]
    qs = DIFF_SCALE * LOG2E
    zero64 = jnp.zeros((64, h.shape[0]), BF16)
    for hd in range(2 * DIFF_HEADS):
        q = ((zt[64 * hd:64 * hd + 64] * cq + zt[512 + 64 * hd:576 + 64 * hd] * sq) * qs).astype(BF16)
        if hd % 2 == 0:
            qd_ref[hd, 0:64, :] = q
            qd_ref[hd, 64:128, :] = zero64
        else:
            qd_ref[hd, 0:64, :] = zero64
            qd_ref[hd, 64:128, :] = q
    for j in range(DIFF_HEADS):
        vd_ref[j] = zt[1024 + 128 * j:1152 + 128 * j].astype(BF16)


def _rec_pre_weights(g, rec_w_in):
    d = rec_w_in.shape[0]
    o = np.cumsum([0, SSD_INNER, SSD_XBC, SSD_HEADS, SSD_HEADS, 2 * DIFF_HEADS * DIFF_DIM, 2 * DIFF_HEADS * DIFF_DIM,
                   DIFF_HEADS * DIFF_V])
    w_z, w_xbc, w_dtf, w_dtb, w_dq, w_dk, w_dv = (rec_w_in[:, o[i]:o[i + 1]] for i in range(7))
    src64, sgn64 = _rot_map(DIFF_DIM)

    def rot_heads(wcols):
        wh = wcols.reshape(d, 2 * DIFF_HEADS, DIFF_DIM)
        return (wh[:, :, src64] * sgn64).reshape(d, -1)

    w_dt = jnp.pad(jnp.concatenate([w_dtf, w_dtb], axis=1), ((0, 0), (0, LANES - 2 * SSD_HEADS)))
    wtm = jnp.concatenate([w_z, w_xbc, w_dt, w_dk, rot_heads(w_dk)], axis=1)
    wfm = jnp.concatenate([w_dq, rot_heads(w_dq), w_dv], axis=1)
    cos_h, sin_h = _rope_tables(g.n, DIFF_DIM, ROW_BLOCK)
    two = lambda x: jnp.concatenate([x, x], axis=1)
    return dict(wtm=wtm.astype(BF16), wfm=wfm.T.astype(BF16), cqh=cos_h.T, sqh=sin_h.T, ckh=two(cos_h), skh=two(sin_h))


def _rec_pre_call(g, h, mod, gpre, w):
    t, d = h.shape
    rb = ROW_BLOCK
    row = lambda i: (i, 0)
    tm_tab = pl.BlockSpec((rb, LANES), lambda i: (g.pos_block(i), 0))
    fm64 = pl.BlockSpec((64, rb), lambda i: (0, g.pos_block(i)))
    nd = 2 * DIFF_HEADS
    return pl.pallas_call(
        _rec_pre_kernel, grid=(g.n_blocks,),
        in_specs=[pl.BlockSpec((rb, d), row), pl.BlockSpec((None, 6, d), lambda i: (g.mod_row(i), 0, 0)),
                  _full(gpre.shape), _full(w["wtm"].shape), _full(w["wfm"].shape), fm64, fm64, tm_tab, tm_tab],
        out_specs=[
            pl.BlockSpec((rb, SSD_INNER), row), pl.BlockSpec((rb, SSD_XBC), row), pl.BlockSpec((rb, LANES), row),
            pl.BlockSpec((DIFF_HEADS, rb, 128), lambda i: (0, i, 0)),
            pl.BlockSpec((nd, 128, rb), lambda i: (0, 0, i)),
            pl.BlockSpec((DIFF_HEADS, DIFF_V, rb), lambda i: (0, 0, i)),
        ],
        out_shape=[
            jax.ShapeDtypeStruct((t, SSD_INNER), BF16), jax.ShapeDtypeStruct((t, SSD_XBC), F32),
            jax.ShapeDtypeStruct((t, LANES), F32), jax.ShapeDtypeStruct((DIFF_HEADS, t, 128), BF16),
            jax.ShapeDtypeStruct((nd, 128, t), BF16), jax.ShapeDtypeStruct((DIFF_HEADS, DIFF_V, t), BF16),
        ],
        compiler_params=_cparams(("parallel",)), name="rec_pre",
    )(h, mod, gpre, w["wtm"], w["wfm"], w["cqh"], w["sqh"], w["ckh"], w["skh"])


def _conv_kernel(x_ref, prev_ref, next_ref, w_ref, b_ref, o_ref, *, seq_blocks, lat_blocks):
    i = pl.program_id(0)
    rb = x_ref.shape[0]
    pos = i % seq_blocks
    is_ctx = i >= lat_blocks
    first = jnp.logical_or(is_ctx, pos == 0)
    last = jnp.logical_or(is_ctx, pos == seq_blocks - 1)
    x = x_ref[...]
    prev_row = jnp.where(first, 0.0, prev_ref[7:8, :])
    next_row = jnp.where(last, 0.0, next_ref[0:1, :])
    rid = lax.broadcasted_iota(jnp.int32, x.shape, 0)
    x_prev = jnp.where(rid == 0, prev_row, pltpu.roll(x, 1, axis=0))
    x_next = jnp.where(rid == rb - 1, next_row, pltpu.roll(x, rb - 1, axis=0))
    y = w_ref[0:1, :] * x_prev + w_ref[1:2, :] * x + w_ref[2:3, :] * x_next + b_ref[...]
    o_ref[...] = _silu(y)


def _conv_call(g, xbc, conv_w, conv_b):
    t, ch = xbc.shape
    rb = ROW_BLOCK
    halo = 8
    per = rb // halo
    last_halo = t // halo - 1
    return pl.pallas_call(
        functools.partial(_conv_kernel, seq_blocks=g.blocks_per_seq, lat_blocks=g.lat_blocks),
        grid=(g.n_blocks,),
        in_specs=[
            pl.BlockSpec((rb, ch), lambda i: (i, 0)),
            pl.BlockSpec((halo, ch), lambda i: (jnp.maximum(i * per - 1, 0), 0)),
            pl.BlockSpec((halo, ch), lambda i: (jnp.minimum((i + 1) * per, last_halo), 0)),
            _full((SSD_CONV, ch)), _full((1, ch)),
        ],
        out_specs=pl.BlockSpec((rb, ch), lambda i: (i, 0)),
        out_shape=jax.ShapeDtypeStruct((t, ch), F32),
        compiler_params=_cparams(("parallel",)), name="ssd_conv",
    )(xbc, xbc, xbc, conv_w, conv_b.reshape(1, ch))


def _ssd_kernel(x_ref, dt_ref, bias_ref, alog_ref, tri_ref, trit_ref, eye_ref, y_ref, state_ref, *, lane_off, reverse):
    k = pl.program_id(1)
    ln = SSD_CHUNK
    hi = lax.Precision.HIGHEST

    @pl.when(k == 0)
    def _():
        state_ref[...] = jnp.zeros_like(state_ref)

    x = x_ref[:, 0:SSD_INNER]
    raw = dt_ref[...] + bias_ref[...]
    dt = jnp.maximum(raw, 0.0) + jnp.log1p(jnp.exp(-jnp.abs(raw)))
    adt = dt * (-jnp.exp(alog_ref[...]))
    acs = jnp.dot(tri_ref[...], adt, preferred_element_type=F32, precision=hi)
    acs_row = lax.dot_general(adt, trit_ref[...], (((0,), (0,)), ((), ())), preferred_element_type=F32, precision=hi)
    dt_row = lax.dot_general(dt, eye_ref[...], (((0,), (0,)), ((), ())), preferred_element_type=F32, precision=hi)
    tot = acs[0:1, :] if reverse else acs[ln - 1:ln, :]
    w_all = jnp.exp(tot - acs) * dt
    ea_all = jnp.exp(acs)
    etot = jnp.exp(tot)
    li = lax.broadcasted_iota(jnp.int32, (ln, ln), 0)
    si = lax.broadcasted_iota(jnp.int32, (ln, ln), 1)
    keep = (si >= li) if reverse else (si <= li)
    lane = lax.broadcasted_iota(jnp.int32, (ln, LANES), 1)
    left = lane < SSD_HEAD_DIM
    per_group = SSD_HEADS // SSD_GROUPS
    for gi in range(SSD_GROUPS):
        bm = x_ref[:, SSD_INNER + SSD_STATE * gi:SSD_INNER + SSD_STATE * (gi + 1)].astype(BF16)
        cm = x_ref[:, SSD_INNER + SSD_STATE * (SSD_GROUPS + gi):SSD_INNER + SSD_STATE * (SSD_GROUPS + gi + 1)]
        cm = cm.astype(BF16)
        cb = _dot_nt(cm, bm)
        for pr in range(per_group // 2):
            h0 = gi * per_group + 2 * pr
            xp = x[:, SSD_HEAD_DIM * h0:SSD_HEAD_DIM * (h0 + 2)]
            mats = []
            for hd in (h0, h0 + 1):
                c = lane_off + hd
                seg = acs[:, c:c + 1] - acs_row[c:c + 1, :]
                lmat = jnp.exp(jnp.where(keep, seg, -jnp.inf))
                mats.append((cb * lmat * dt_row[c:c + 1, :]).astype(BF16))
            xb = xp.astype(BF16)
            zero = jnp.zeros_like(xb)
            rhs = jnp.concatenate([jnp.where(left, xb, zero), jnp.where(left, zero, xb)], axis=0)
            y_diag = _dot(jnp.concatenate(mats, axis=1), rhs)
            st = state_ref[h0 // 2]
            c0 = lane_off + h0
            ea = jnp.where(left, ea_all[:, c0:c0 + 1], ea_all[:, c0 + 1:c0 + 2])
            y_off = _dot(cm, st.astype(BF16)) * ea
            y_ref[:, SSD_HEAD_DIM * h0:SSD_HEAD_DIM * (h0 + 2)] = y_diag + y_off
            wcol = jnp.where(left, w_all[:, c0:c0 + 1], w_all[:, c0 + 1:c0 + 2])
            cs = _dot_tn(bm, (xp * wcol).astype(BF16))
            dec = jnp.where(left[0:1, :], etot[:, c0:c0 + 1], etot[:, c0 + 1:c0 + 2])
            state_ref[h0 // 2] = st * dec + cs


def _ssd_call(g, xbc_act, dt, dt_bias, a_log, *, reverse):
    t = xbc_act.shape[0]
    ln = SSD_CHUNK
    cc = g.c // ln
    nl = g.n // ln
    ctx0 = g.b * g.n // ln

    def chunk(b, k):
        if reverse:
            return jnp.where(k < cc, ctx0 + b * cc + (cc - 1 - k), b * nl + (nl - 1 - (k - cc)))
        return jnp.where(k < cc, ctx0 + b * cc + k, b * nl + (k - cc))

    idx = np.arange(ln)
    lower = (idx[:, None] >= idx[None, :]).astype(np.float32)
    tri = lower.T if reverse else lower
    row = lambda b, k: (chunk(b, k), 0)
    off = SSD_HEADS if reverse else 0
    lanes = lambda p: jnp.pad(p.reshape(1, -1), ((0, 0), (off, LANES - off - SSD_HEADS)))
    return pl.pallas_call(
        functools.partial(_ssd_kernel, lane_off=SSD_HEADS if reverse else 0, reverse=reverse),
        grid=(g.b, cc + nl),
        in_specs=[pl.BlockSpec((ln, SSD_XBC), row), pl.BlockSpec((ln, LANES), row),
                  _full((1, LANES)), _full((1, LANES)), _full((ln, ln)), _full((ln, ln)), _full((ln, ln))],
        out_specs=pl.BlockSpec((ln, SSD_INNER), row),
        out_shape=jax.ShapeDtypeStruct((t, SSD_INNER), F32),
        scratch_shapes=[pltpu.VMEM((SSD_HEADS // 2, SSD_STATE, 2 * SSD_HEAD_DIM), F32)],
        compiler_params=_cparams(("parallel", "arbitrary")),
        name="ssd_bwd" if reverse else "ssd_fwd",
    )(xbc_act, dt, lanes(dt_bias), lanes(a_log), jnp.asarray(tri), jnp.asarray(tri.T), jnp.eye(ln, dtype=F32))


def _diff_attn_kernel(qt_ref, *refs, lambda_init):
    o_ref, s_ref = refs[-2:]
    lq1_ref, lk1_ref, lq2_ref, lk2_ref, sub_ref = refs[-7:-2]
    kv = refs[:-7]
    half = len(kv) // 2
    segs = [(kv[i], kv[half + i]) for i in range(half)]
    tq = qt_ref.shape[2]
    qt = jnp.concatenate([qt_ref[0], qt_ref[1]], axis=1)
    o = _attend(qt, segs, s_ref, DIFF_V)
    lam = (jnp.exp(jnp.sum(lq1_ref[...] * lk1_ref[...], axis=1, keepdims=True))
           - jnp.exp(jnp.sum(lq2_ref[...] * lk2_ref[...], axis=1, keepdims=True)) + lambda_init)
    od = o[:, 0:tq] - lam * o[:, tq:2 * tq]
    o_ref[...] = ((_rms(od, 0) * sub_ref[...]) * (1.0 - lambda_init)).astype(o_ref.dtype)


def _diff_attn_call(g, qd, kd, vd, lq1, lk1, lq2, lk2, subln, lambda_init):
    tq = DIFF_Q_TILE
    nq = g.n // tq
    ctx_blk = g.b * g.n // g.c
    vec = lambda a: a.reshape(1, -1)
    params = [_full((1, DIFF_DIM)), _full((1, DIFF_DIM)), _full((1, DIFF_DIM)), _full((1, DIFF_DIM)),
              _full((DIFF_V, 1))]
    pvals = (vec(lq1), vec(lk1), vec(lq2), vec(lk2), subln.reshape(-1, 1))
    klat = pl.BlockSpec((None, g.n, 128), lambda b, j, *_: (j, b, 0))
    kctx = pl.BlockSpec((None, g.c, 128), lambda b, j, *_: (j, ctx_blk + b, 0))
    vlat = pl.BlockSpec((None, DIFF_V, g.n), lambda b, j, *_: (j, 0, b))
    vctx = pl.BlockSpec((None, DIFF_V, g.c), lambda b, j, *_: (j, 0, ctx_blk + b))
    kern = functools.partial(_diff_attn_kernel, lambda_init=lambda_init)
    lat = pl.pallas_call(
        kern, grid=(g.b, DIFF_HEADS, nq),
        scratch_shapes=[pltpu.VMEM((g.n + g.c, 2 * tq), F32)],
        in_specs=[pl.BlockSpec((2, 128, tq), lambda b, j, qi: (j, 0, b * nq + qi)), klat, kctx, vlat, vctx] + params,
        out_specs=pl.BlockSpec((DIFF_V, tq), lambda b, j, qi: (j, b * nq + qi)),
        out_shape=jax.ShapeDtypeStruct((DIFF_HEADS * DIFF_V, g.b * g.n), BF16),
        compiler_params=_cparams(("parallel", "parallel", "arbitrary")),
        name="attn_diff",
    )(qd, kd, kd, vd, vd, *pvals)
    ctx = pl.pallas_call(
        kern, grid=(g.b, DIFF_HEADS),
        scratch_shapes=[pltpu.VMEM((g.c, 2 * g.c), F32)],
        in_specs=[pl.BlockSpec((2, 128, g.c), lambda b, j: (j, 0, ctx_blk + b)), kctx, vctx] + params,
        out_specs=pl.BlockSpec((DIFF_V, g.c), lambda b, j: (j, b)),
        out_shape=jax.ShapeDtypeStruct((DIFF_HEADS * DIFF_V, g.b * g.c), BF16),
        compiler_params=_cparams(("parallel", "parallel")),
        name="attn_diff_ctx",
    )(qd, kd, vd, *pvals)
    return jnp.concatenate([lat, ctx], axis=1)


def _rec_post_kernel(yf_ref, yb_ref, xs_ref, z_ref, od_ref, dskip_ref, gn_ref, wa_ref, wb_ref,
                     h_ref, mod_ref, gpost_ref, gffn_ref, rw_ref, hn_ref, v_ref, lg_ref):
    y = (yf_ref[...] + yb_ref[...] + xs_ref[...] * dskip_ref[...]) * _silu(z_ref[...].astype(F32))
    half = SSD_INNER // SSD_GROUPS
    o = _dot_tn(od_ref[...], wb_ref[...])
    for gi in range(SSD_GROUPS):
        lo = half * gi
        yn = (_rms(y[:, lo:lo + half], -1) * gn_ref[:, lo:lo + half]).astype(BF16)
        o = o + _dot(yn, wa_ref[lo:lo + half, :])
    _post_tail(o, h_ref, mod_ref, gpost_ref, gffn_ref, rw_ref, hn_ref, v_ref, lg_ref)


def _rec_post_call(g, yf, yb, xbc_act, zs, od, d_skip, ssd_norm, w_out, h, mod, gpost, gffn, router_w):
    d = h.shape[1]
    rb = ROW_BLOCK
    ins, outs, shapes = _post_specs(g, d)
    wo = w_out.astype(BF16)
    row512 = pl.BlockSpec((rb, SSD_INNER), lambda i: (i, 0))
    return pl.pallas_call(
        _rec_post_kernel, grid=(g.n_blocks,),
        in_specs=[row512, row512, row512, row512, pl.BlockSpec((SSD_INNER, rb), lambda i: (0, i)),
                  _full((1, SSD_INNER)), _full((1, SSD_INNER)), _full((SSD_INNER, d)), _full((SSD_INNER, d))] + ins,
        out_specs=outs, out_shape=shapes,
        compiler_params=_cparams(("parallel",)), name="rec_post",
    )(yf, yb, xbc_act, zs, od, jnp.repeat(d_skip, SSD_HEAD_DIM).reshape(1, -1), ssd_norm.reshape(1, -1),
      wo[:SSD_INNER], wo[SSD_INNER:], h, mod, gpost.reshape(1, d), gffn.reshape(1, d), router_w.T)


def _ffn_res_kernel(h_ref, f_ref, mod_ref, g_ref, o_ref):
    o_ref[...] = h_ref[...] + mod_ref[5:6, :] * (_rms(f_ref[...], -1) * g_ref[...])


def _ffn_res_call(g, h, f, mod, gain, n_rows):
    d = h.shape[1]
    rb = ROW_BLOCK
    row = pl.BlockSpec((rb, d), lambda i: (i, 0))
    return pl.pallas_call(
        _ffn_res_kernel, grid=(n_rows // rb,),
        in_specs=[row, row, pl.BlockSpec((None, 6, d), lambda i: (g.mod_row(i), 0, 0)), _full((1, d))],
        out_specs=row, out_shape=jax.ShapeDtypeStruct((n_rows, d), F32),
        compiler_params=_cparams(("parallel",)), name="ffn_res",
    )(h, f, mod, gain.reshape(1, d))


def kernel(x, c, ctx, c_ctx, ada_w, ada_b, norm_mix_pre, norm_mix_post, norm_ffn_pre, norm_ffn_post, mix_w_out, att_w_in, mla_q_norm, mla_w_q_up, mla_kv_norm, mla_w_kv_up, gqa_q_norm, gqa_k_norm, rec_w_in, ssd_conv_w, ssd_conv_b, ssd_dt_bias_f, ssd_dt_bias_b, ssd_a_log_f, ssd_a_log_b, ssd_d, ssd_norm, diff_lambda_q1, diff_lambda_k1, diff_lambda_q2, diff_lambda_k2, diff_subln, router_w, router_b, exp_w_gate, exp_w_up, exp_w_down, sh_w_gate, sh_w_up, sh_w_down):
    b, n, d = x.shape
    n_ctx = ctx.shape[1]
    depth = ada_w.shape[0]
    g = _Geom(b, n, n_ctx)
    mod_rows = -(-(b + 1) // 8) * 8
    c_all = jnp.concatenate([c, c_ctx[None, :], jnp.zeros((mod_rows - b - 1, d), F32)], axis=0)
    mods = _ada_call(c_all, ada_w, ada_b).reshape(depth, mod_rows, 6, d)
    h = jnp.concatenate([x.reshape(b * n, d), ctx.reshape(b * n_ctx, d)], axis=0)
    for i in range(depth):
        last = i == depth - 1
        jdx = i // 2
        mod = mods[i]
        if i % 2 == 0:
            w = _attn_pre_weights(g, att_w_in[jdx], mla_q_norm[jdx], mla_w_q_up[jdx], mla_kv_norm[jdx],
                                  mla_w_kv_up[jdx], gqa_q_norm[jdx], gqa_k_norm[jdx])
            qm, km, vm, qg, kg, vg = _attn_pre_call(g, h, mod, norm_mix_pre[i].reshape(1, d), w)
            oa = _attn_call(g, qm, km, vm, kv_heads=MLA_HEADS, shared_k=False, name="attn_mla")
            ob = _attn_call(g, qg, kg, vg, kv_heads=GQA_KV_HEADS, shared_k=True, name="attn_gqa")
            h, v, logits_t = _attn_post_call(g, oa, ob, mix_w_out[i], h, mod, norm_mix_post[i], norm_ffn_pre[i],
                                             router_w[i])
        else:
            lambda_init = 0.8 - 0.6 * math.exp(-0.3 * i)
            w = _rec_pre_weights(g, rec_w_in[jdx])
            zs, xbc, dt, kd, qd, vd = _rec_pre_call(g, h, mod, norm_mix_pre[i].reshape(1, d), w)
            xbc_act = _conv_call(g, xbc, ssd_conv_w[jdx], ssd_conv_b[jdx])
            yf = _ssd_call(g, xbc_act, dt, ssd_dt_bias_f[jdx], ssd_a_log_f[jdx], reverse=False)
            yb = _ssd_call(g, xbc_act, dt, ssd_dt_bias_b[jdx], ssd_a_log_b[jdx], reverse=True)
            od = _diff_attn_call(g, qd, kd, vd, diff_lambda_q1[jdx], diff_lambda_k1[jdx], diff_lambda_q2[jdx],
                                 diff_lambda_k2[jdx], diff_subln[jdx], lambda_init)
            h, v, logits_t = _rec_post_call(g, yf, yb, xbc_act, zs, od, ssd_d[jdx], ssd_norm[jdx], mix_w_out[i], h,
                                            mod, norm_mix_post[i], norm_ffn_pre[i], router_w[i])
        gates, rank, counts = _router_call(g, logits_t, router_b[i])
        f = _moe_call(g, v, gates, rank, counts, exp_w_gate[i], exp_w_up[i], exp_w_down[i],
                      sh_w_gate[i], sh_w_up[i], sh_w_down[i])
        h = _ffn_res_call(g, h, f, mod, norm_ffn_post[i], b * n if last else g.t)
    return h.reshape(b, n, d)
```

```python
import functools
import math

import numpy as np
import jax
import jax.numpy as jnp
from jax import lax
from jax.experimental import pallas as pl
from jax.experimental.pallas import tpu as pltpu

F32 = jnp.float32
BF16 = jnp.bfloat16
LOG2E = 1.4426950408889634

GRID_W = 64
ROPE_BASE = 10000.0
NORM_EPS = 1e-6

MLA_HEADS, MLA_Q_LORA, MLA_KV_LORA, MLA_NOPE, MLA_ROPE, MLA_V = 8, 256, 128, 64, 32, 64
MLA_SCALE = (MLA_NOPE + MLA_ROPE) ** -0.5
GQA_HEADS, GQA_KV_HEADS, GQA_DIM = 8, 2, 64
GQA_SCALE = GQA_DIM ** -0.5
SSD_HEADS, SSD_HEAD_DIM, SSD_GROUPS, SSD_STATE, SSD_CONV, SSD_CHUNK = 8, 64, 2, 128, 3, 128
SSD_INNER = SSD_HEADS * SSD_HEAD_DIM
SSD_XBC = SSD_INNER + 2 * SSD_GROUPS * SSD_STATE
DIFF_HEADS, DIFF_DIM = 4, 64
DIFF_V = 2 * DIFF_DIM
DIFF_SCALE = DIFF_DIM ** -0.5
N_EXPERTS, TOP_K, N_GROUPS, TOPK_GROUPS, EXPERT_FF, SHARED_FF = 64, 8, 8, 4, 256, 256
ROUTED_SCALE = 2.5

LANES = 128
VMEM_LIMIT_BYTES = 56 * 1024 * 1024

ROW_BLOCK = 256
ATTN_Q_TILE = 512
DIFF_Q_TILE = 256
ATTN_KV_CHUNK = 512
ATTN_EXP_ROWS = 4096
MOE_TOKENS = 768
MOE_ROWS = 128
MOE_PAIR = 4
MOE_SLOTS = 4


def _cparams(sem, flags=None):
    return pltpu.CompilerParams(dimension_semantics=sem, vmem_limit_bytes=VMEM_LIMIT_BYTES, flags=flags)


def _rms(x, axis):
    return x * lax.rsqrt(jnp.mean(x * x, axis=axis, keepdims=True) + NORM_EPS)


def _silu(x):
    return x * jax.nn.sigmoid(x)


def _dot(a, b):
    return jnp.dot(a, b, preferred_element_type=F32)


def _dot_nt(a, b):
    return lax.dot_general(a, b, (((1,), (1,)), ((), ())), preferred_element_type=F32)


def _dot_tn(a, b):
    return lax.dot_general(a, b, (((0,), (0,)), ((), ())), preferred_element_type=F32)


def _ada_kernel(c_ref, w_ref, b_ref, o_ref):
    s = _silu(c_ref[...])
    o_ref[...] = jnp.dot(s, w_ref[...], preferred_element_type=F32, precision=lax.Precision.HIGHEST) + b_ref[...]


def _ada_call(c_all, ada_w, ada_b):
    depth, d, six_d = ada_w.shape
    rows = c_all.shape[0]
    cols = six_d // 4
    return pl.pallas_call(
        _ada_kernel,
        grid=(depth, six_d // cols),
        in_specs=[
            pl.BlockSpec((rows, d), lambda i, j: (0, 0)),
            pl.BlockSpec((None, d, cols), lambda i, j: (i, 0, j)),
            pl.BlockSpec((None, 1, cols), lambda i, j: (i, 0, j)),
        ],
        out_specs=pl.BlockSpec((None, rows, cols), lambda i, j: (i, 0, j)),
        out_shape=jax.ShapeDtypeStruct((depth, rows, six_d), F32),
        compiler_params=_cparams(("arbitrary", "arbitrary")),
        name="ada_mod",
    )(c_all, ada_w, ada_b.reshape(depth, 1, six_d))


def _attn_pre_kernel(h_ref, mod_ref, gpre_ref, wfm_ref, wtm_ref, qn_ref, kvn_col_ref, kvn_row_ref, wq_ref, wk_ref,
                     wv_ref, gq_g_ref, gq_gr_ref, gk_g_ref, gk_gr_ref,
                     cqm_ref, sqm_ref, cqh_ref, sqh_ref, ckm_ref, skm_ref, ckh_ref, skh_ref,
                     qm_ref, km_ref, vm_ref, qg_ref, kg_ref, vg_ref):
    h = h_ref[...]
    u = _rms(h, -1) * gpre_ref[...]
    u = u * (1.0 + mod_ref[1:2, :]) + mod_ref[0:1, :]
    ub = u.astype(BF16)
    zt = _dot_nt(wfm_ref[...], ub)
    zk = _dot(ub, wtm_ref[...])

    qn = (_rms(zt[0:256], 0) * qn_ref[...]).astype(BF16)
    qt = _dot(wq_ref[...], qn)
    cq, sq = cqm_ref[...], sqm_ref[...]
    qs = MLA_SCALE * LOG2E
    zero32 = jnp.zeros((32, h.shape[0]), BF16)
    for hd in range(MLA_HEADS):
        pe = qt[512 + 32 * hd:544 + 32 * hd] * cq + qt[768 + 32 * hd:800 + 32 * hd] * sq
        qm_ref[hd, 0:64, :] = (qt[64 * hd:64 * hd + 64] * qs).astype(BF16)
        qm_ref[hd, 64:96, :] = (pe * qs).astype(BF16)
        qm_ref[hd, 96:128, :] = zero32

    kvn_t = (_rms(zt[256:384], 0) * kvn_col_ref[...]).astype(BF16)
    vt = _dot(wv_ref[...], kvn_t)
    for hd in range(MLA_HEADS):
        vm_ref[hd] = vt[64 * hd:64 * hd + 64].astype(BF16)
    kvn = (_rms(zk[:, 0:128], -1) * kvn_row_ref[...]).astype(BF16)
    kn = _dot(kvn, wk_ref[...])
    kpe = zk[:, 128:256] * ckm_ref[...] + zk[:, 256:384] * skm_ref[...]
    for hd in range(MLA_HEADS):
        km_ref[hd] = (kn[:, 128 * hd:128 * hd + 128] + kpe).astype(BF16)

    cqh, sqh = cqh_ref[...], sqh_ref[...]
    gs = GQA_SCALE * LOG2E
    zero64 = jnp.zeros((64, h.shape[0]), BF16)
    grp = GQA_HEADS // GQA_KV_HEADS
    for hd in range(GQA_HEADS):
        raw = zt[384 + 64 * hd:448 + 64 * hd]
        rot = zt[896 + 64 * hd:960 + 64 * hd]
        r = lax.rsqrt(jnp.mean(raw * raw, axis=0, keepdims=True) + NORM_EPS)
        q = (raw * r * gq_g_ref[...]) * cqh + (rot * r * gq_gr_ref[...]) * sqh
        q = (q * gs).astype(BF16)
        if hd // grp == 0:
            qg_ref[hd, 0:64, :] = q
            qg_ref[hd, 64:128, :] = zero64
        else:
            qg_ref[hd, 0:64, :] = zero64
            qg_ref[hd, 64:128, :] = q
    for kvh in range(GQA_KV_HEADS):
        vg_ref[kvh] = zt[1408 + 64 * kvh:1472 + 64 * kvh].astype(BF16)

    gk, gkr = zk[:, 384:512], zk[:, 512:640]
    lane = lax.broadcasted_iota(jnp.int32, gk.shape, 1)
    lo = lane < 64
    sq0 = jnp.sum(jnp.where(lo, gk * gk, 0.0), axis=-1, keepdims=True)
    sq1 = jnp.sum(jnp.where(lo, 0.0, gk * gk), axis=-1, keepdims=True)
    r = jnp.where(lo, lax.rsqrt(sq0 / GQA_DIM + NORM_EPS), lax.rsqrt(sq1 / GQA_DIM + NORM_EPS))
    kg_ref[...] = ((gk * r * gk_g_ref[...]) * ckh_ref[...] + (gkr * r * gk_gr_ref[...]) * skh_ref[...]).astype(BF16)


def _rot_map(dim):
    q = dim // 4
    j = np.arange(dim)
    even = (j // q) % 2 == 0
    return np.where(even, j + q, j - q), np.where(even, -1.0, 1.0).astype(np.float32)


def _rope_tables(n_tokens, dim, pad_rows):
    rows = n_tokens // GRID_W
    row = jnp.repeat(jnp.arange(rows, dtype=F32), GRID_W)
    col = jnp.tile(jnp.arange(GRID_W, dtype=F32), rows)
    half = dim // 2
    inv = ROPE_BASE ** (-(jnp.arange(half // 2, dtype=F32) * 2.0 / half))
    ang_r = row[:, None] * inv
    ang_c = col[:, None] * inv
    ang = jnp.concatenate([ang_r, ang_r, ang_c, ang_c], axis=-1)
    cos = jnp.concatenate([jnp.cos(ang), jnp.ones((pad_rows, dim), F32)], axis=0)
    sin = jnp.concatenate([jnp.sin(ang), jnp.zeros((pad_rows, dim), F32)], axis=0)
    return cos, sin


class _Geom:
    def __init__(self, b, n, c):
        assert c == ROW_BLOCK and n % ROW_BLOCK == 0 and n % GRID_W == 0
        self.b, self.n, self.c = b, n, c
        self.t = b * n + b * c
        self.lat_blocks = b * n // ROW_BLOCK
        self.blocks_per_seq = n // ROW_BLOCK
        self.n_blocks = self.t // ROW_BLOCK
        assert self.t % MOE_TOKENS == 0

    def mod_row(self, i):
        return jnp.where(i < self.lat_blocks, i // self.blocks_per_seq, self.b)

    def pos_block(self, i):
        return jnp.where(i < self.lat_blocks, i % self.blocks_per_seq, self.blocks_per_seq)


def _full(shape):
    nd = len(shape)
    return pl.BlockSpec(shape, lambda *_: (0,) * nd)


def _attn_pre_call(g, h, mod, gpre, w):
    t, d = h.shape
    rb = ROW_BLOCK
    row = lambda i: (i, 0)
    tm_tab = pl.BlockSpec((rb, LANES), lambda i: (g.pos_block(i), 0))
    fm32 = pl.BlockSpec((32, rb), lambda i: (0, g.pos_block(i)))
    fm64 = pl.BlockSpec((64, rb), lambda i: (0, g.pos_block(i)))
    in_specs = [
        pl.BlockSpec((rb, d), row),
        pl.BlockSpec((None, 6, d), lambda i: (g.mod_row(i), 0, 0)),
        _full(gpre.shape), _full(w["wfm"].shape), _full(w["wtm"].shape), _full(w["qn"].shape),
        _full(w["kvn_col"].shape), _full(w["kvn_row"].shape), _full(w["wq"].shape), _full(w["wk"].shape),
        _full(w["wv"].shape), _full(w["gq_g"].shape), _full(w["gq_gr"].shape), _full(w["gk_g"].shape),
        _full(w["gk_gr"].shape),
        fm32, fm32, fm64, fm64, tm_tab, tm_tab, tm_tab, tm_tab,
    ]
    out_shape = [
        jax.ShapeDtypeStruct((MLA_HEADS, 128, t), BF16),
        jax.ShapeDtypeStruct((MLA_HEADS, t, 128), BF16),
        jax.ShapeDtypeStruct((MLA_HEADS, MLA_V, t), BF16),
        jax.ShapeDtypeStruct((GQA_HEADS, 128, t), BF16),
        jax.ShapeDtypeStruct((t, 128), BF16),
        jax.ShapeDtypeStruct((GQA_KV_HEADS, GQA_DIM, t), BF16),
    ]
    out_specs = [
        pl.BlockSpec((MLA_HEADS, 128, rb), lambda i: (0, 0, i)),
        pl.BlockSpec((MLA_HEADS, rb, 128), lambda i: (0, i, 0)),
        pl.BlockSpec((MLA_HEADS, MLA_V, rb), lambda i: (0, 0, i)),
        pl.BlockSpec((GQA_HEADS, 128, rb), lambda i: (0, 0, i)),
        pl.BlockSpec((rb, 128), row),
        pl.BlockSpec((GQA_KV_HEADS, GQA_DIM, rb), lambda i: (0, 0, i)),
    ]
    return pl.pallas_call(
        _attn_pre_kernel, grid=(g.n_blocks,), in_specs=in_specs, out_specs=out_specs, out_shape=out_shape,
        compiler_params=_cparams(("parallel",)), name="attn_pre",
    )(h, mod, gpre, w["wfm"], w["wtm"], w["qn"], w["kvn_col"], w["kvn_row"], w["wq"], w["wk"], w["wv"],
      w["gq_g"], w["gq_gr"], w["gk_g"], w["gk_gr"],
      w["cqm"], w["sqm"], w["cqh"], w["sqh"], w["ckm"], w["skm"], w["ckh"], w["skh"])


def _attn_pre_weights(g, att_w_in, q_norm, w_q_up, kv_norm, w_kv_up, gqa_qn, gqa_kn):
    d = att_w_in.shape[0]
    o = np.cumsum([0, MLA_Q_LORA, MLA_KV_LORA, MLA_ROPE, GQA_HEADS * GQA_DIM, GQA_KV_HEADS * GQA_DIM,
                   GQA_KV_HEADS * GQA_DIM])
    w_qlat, w_kvlat, w_kpe, w_gq, w_gk, w_gv = (att_w_in[:, o[i]:o[i + 1]] for i in range(6))
    src32, sgn32 = _rot_map(MLA_ROPE)
    src64, sgn64 = _rot_map(GQA_DIM)

    def rot_heads(wcols, heads, dim, src, sgn):
        wh = wcols.reshape(d, heads, dim)
        return (wh[:, :, src] * sgn).reshape(d, heads * dim)

    wfm = jnp.concatenate([w_qlat, w_kvlat, w_gq, rot_heads(w_gq, GQA_HEADS, GQA_DIM, src64, sgn64), w_gv], axis=1)
    zpad = lambda x, lo, hi: jnp.pad(x, ((0, 0), (lo, hi)))
    wtm = jnp.concatenate([
        w_kvlat, zpad(w_kpe, 64, 32), zpad(w_kpe[:, src32] * sgn32, 64, 32),
        w_gk, rot_heads(w_gk, GQA_KV_HEADS, GQA_DIM, src64, sgn64)], axis=1)
    wq = w_q_up.reshape(MLA_Q_LORA, MLA_HEADS, MLA_NOPE + MLA_ROPE)
    wq_pe = wq[:, :, MLA_NOPE:]
    wq_all = jnp.concatenate([
        wq[:, :, :MLA_NOPE].reshape(MLA_Q_LORA, -1), wq_pe.reshape(MLA_Q_LORA, -1),
        (wq_pe[:, :, src32] * sgn32).reshape(MLA_Q_LORA, -1)], axis=1)
    wkv = w_kv_up.reshape(MLA_KV_LORA, MLA_HEADS, MLA_NOPE + MLA_V)
    wk = jnp.pad(wkv[:, :, :MLA_NOPE], ((0, 0), (0, 0), (0, 128 - MLA_NOPE))).reshape(MLA_KV_LORA, -1)
    wv = wkv[:, :, MLA_NOPE:].reshape(MLA_KV_LORA, -1)
    cos_m, sin_m = _rope_tables(g.n, MLA_ROPE, ROW_BLOCK)
    cos_h, sin_h = _rope_tables(g.n, GQA_DIM, ROW_BLOCK)
    two = lambda x: jnp.concatenate([x, x], axis=1)
    return dict(
        wfm=wfm.T.astype(BF16), wtm=wtm.astype(BF16),
        qn=q_norm.reshape(-1, 1), kvn_col=kv_norm.reshape(-1, 1), kvn_row=kv_norm.reshape(1, -1),
        wq=wq_all.T.astype(BF16), wk=wk.astype(BF16), wv=wv.T.astype(BF16),
        gq_g=gqa_qn.reshape(-1, 1), gq_gr=gqa_qn[src64].reshape(-1, 1),
        gk_g=two(gqa_kn.reshape(1, -1)), gk_gr=two(gqa_kn[src64].reshape(1, -1)),
        cqm=cos_m.T, sqm=sin_m.T, cqh=cos_h.T, sqh=sin_h.T,
        ckm=zpad(cos_m, 64, 32), skm=zpad(sin_m, 64, 32), ckh=two(cos_h), skh=two(sin_h),
    )


def _fold8(x, op):
    r, w = x.shape
    return op(x.reshape(r // 8, 8, w), axis=0)


def _attend(qt, segments, s_ref, dv):
    tq = qt.shape[1]
    pieces, off = [], 0
    for k_ref, vt_ref in segments:
        n = k_ref.shape[0]
        for lo in range(0, n, ATTN_KV_CHUNK):
            rows = min(ATTN_KV_CHUNK, n - lo)
            pieces.append((off, lo, rows, k_ref, vt_ref))
            off += rows
    m8 = None
    for so, lo, rows, k_ref, _ in pieces:
        s = _dot(k_ref[lo:lo + rows, :], qt)
        s_ref[so:so + rows, :] = s
        part = _fold8(s, jnp.max)
        m8 = part if m8 is None else jnp.maximum(m8, part)
    m = jnp.max(m8, axis=0, keepdims=True)
    acc = jnp.zeros((dv + 16, tq), F32)
    off = 0
    for _, vt_ref in segments:
        n = vt_ref.shape[1]
        step = min(n, ATTN_EXP_ROWS)

        def body(i, acc, off=off, vt_ref=vt_ref, step=step):
            base = i * step if isinstance(i, int) else pl.multiple_of(i * step, step)
            for lo in range(0, step, ATTN_KV_CHUNK):
                rows = min(ATTN_KV_CHUNK, step - lo)
                p = jnp.exp2((s_ref[pl.ds(off + base + lo, rows), :] - m).astype(BF16))
                vt1 = jnp.concatenate([vt_ref[:, pl.ds(base + lo, rows)], jnp.ones((16, rows), BF16)], axis=0)
                acc = acc + _dot(vt1, p)
            return acc

        acc = body(0, acc) if n == step else lax.fori_loop(0, n // step, body, acc)
        off += n
    return acc[0:dv] / acc[dv:dv + 1]


def _attn_kernel(qt_ref, *refs, dv):
    o_ref, s_ref = refs[-2:]
    kv = refs[:-2]
    half = len(kv) // 2
    segs = [(kv[i], kv[half + i]) for i in range(half)]
    o_ref[...] = _attend(qt_ref[...], segs, s_ref, dv).astype(o_ref.dtype)


def _attn_call(g, qt, k, vt, *, kv_heads, shared_k, name):
    heads = qt.shape[0]
    dv = vt.shape[1]
    grp = heads // kv_heads
    tq = ATTN_Q_TILE
    nq = g.n // tq
    ctx_blk = g.b * g.n // g.c
    if shared_k:
        klat = pl.BlockSpec((g.n, 128), lambda b, h, *_: (b, 0))
        kctx = pl.BlockSpec((g.c, 128), lambda b, h, *_: (ctx_blk + b, 0))
    else:
        klat = pl.BlockSpec((None, g.n, 128), lambda b, h, *_: (h // grp, b, 0))
        kctx = pl.BlockSpec((None, g.c, 128), lambda b, h, *_: (h // grp, ctx_blk + b, 0))
    vlat = pl.BlockSpec((None, dv, g.n), lambda b, h, *_: (h // grp, 0, b))
    vctx = pl.BlockSpec((None, dv, g.c), lambda b, h, *_: (h // grp, 0, ctx_blk + b))
    lat = pl.pallas_call(
        functools.partial(_attn_kernel, dv=dv),
        grid=(g.b, heads, nq),
        scratch_shapes=[pltpu.VMEM((g.n + g.c, tq), F32)],
        in_specs=[pl.BlockSpec((None, 128, tq), lambda b, h, qi: (h, 0, b * nq + qi)), klat, kctx, vlat, vctx],
        out_specs=pl.BlockSpec((dv, tq), lambda b, h, qi: (h, b * nq + qi)),
        out_shape=jax.ShapeDtypeStruct((heads * dv, g.b * g.n), BF16),
        compiler_params=_cparams(("parallel", "parallel", "arbitrary")),
        name=name,
    )(qt, k, k, vt, vt)
    ctx = pl.pallas_call(
        functools.partial(_attn_kernel, dv=dv),
        grid=(g.b, heads),
        scratch_shapes=[pltpu.VMEM((g.c, g.c), F32)],
        in_specs=[pl.BlockSpec((None, 128, g.c), lambda b, h: (h, 0, ctx_blk + b)), kctx, vctx],
        out_specs=pl.BlockSpec((dv, g.c), lambda b, h: (h, b)),
        out_shape=jax.ShapeDtypeStruct((heads * dv, g.b * g.c), BF16),
        compiler_params=_cparams(("parallel", "parallel")),
        name=name + "_ctx",
    )(qt, k, vt)
    return jnp.concatenate([lat, ctx], axis=1)


def _split_bf16(x):
    hi = x.astype(BF16)
    return hi, (x - hi.astype(F32)).astype(BF16)


def _post_tail(o, h_ref, mod_ref, gpost_ref, gffn_ref, rw_ref, hn_ref, v_ref, lg_ref):
    hn = h_ref[...] + mod_ref[2:3, :] * (_rms(o, -1) * gpost_ref[...])
    hn_ref[...] = hn
    v = (_rms(hn, -1) * gffn_ref[...]) * (1.0 + mod_ref[4:5, :]) + mod_ref[3:4, :]
    v_hi, v_lo = _split_bf16(v)
    v_ref[...] = v_hi
    w_hi, w_lo = _split_bf16(rw_ref[...])
    lg_ref[...] = _dot_nt(w_hi, v_hi) + (_dot_nt(w_hi, v_lo) + _dot_nt(w_lo, v_hi))


def _attn_post_kernel(oa_ref, ob_ref, wa_ref, wb_ref, h_ref, mod_ref, gpost_ref, gffn_ref, rw_ref,
                      hn_ref, v_ref, lg_ref):
    o = _dot_tn(oa_ref[...], wa_ref[...]) + _dot_tn(ob_ref[...], wb_ref[...])
    _post_tail(o, h_ref, mod_ref, gpost_ref, gffn_ref, rw_ref, hn_ref, v_ref, lg_ref)


def _post_specs(g, d):
    rb = ROW_BLOCK
    ins = [
        pl.BlockSpec((rb, d), lambda i: (i, 0)),
        pl.BlockSpec((None, 6, d), lambda i: (g.mod_row(i), 0, 0)),
        _full((1, d)), _full((1, d)), _full((N_EXPERTS, d)),
    ]
    outs = [pl.BlockSpec((rb, d), lambda i: (i, 0)), pl.BlockSpec((rb, d), lambda i: (i, 0)),
            pl.BlockSpec((N_EXPERTS, rb), lambda i: (0, i))]
    shapes = [jax.ShapeDtypeStruct((g.t, d), F32), jax.ShapeDtypeStruct((g.t, d), BF16),
              jax.ShapeDtypeStruct((N_EXPERTS, g.t), F32)]
    return ins, outs, shapes


def _attn_post_call(g, oa, ob, w_out, h, mod, gpost, gffn, router_w):
    d = h.shape[1]
    rb = ROW_BLOCK
    half = oa.shape[0]
    ins, outs, shapes = _post_specs(g, d)
    wo = w_out.astype(BF16)
    return pl.pallas_call(
        _attn_post_kernel, grid=(g.n_blocks,),
        in_specs=[pl.BlockSpec((half, rb), lambda i: (0, i)), pl.BlockSpec((half, rb), lambda i: (0, i)),
                  _full((half, d)), _full((half, d))] + ins,
        out_specs=outs, out_shape=shapes,
        compiler_params=_cparams(("parallel",)), name="attn_post",
    )(oa, ob, wo[:half], wo[half:], h, mod, gpost.reshape(1, d), gffn.reshape(1, d), router_w.T)


def _router_kernel(lg_ref, bias_ref, tri_ref, gate_ref, rank_ref, cnt_ref):
    tb = lg_ref.shape[1]
    per = N_EXPERTS // N_GROUPS
    shp = (N_GROUPS, per, tb)
    scores = jax.nn.sigmoid(lg_ref[...])
    s3 = scores.reshape(shp)
    sel = (scores + bias_ref[...]).reshape(shp)
    sub = lax.broadcasted_iota(jnp.int32, shp, 1)
    grp = lax.broadcasted_iota(jnp.int32, shp, 0)
    neg = -jnp.inf
    m1 = jnp.max(sel, axis=1, keepdims=True)
    i1 = jnp.min(jnp.where(sel == m1, sub, per), axis=1, keepdims=True)
    m2 = jnp.max(jnp.where(sub == i1, neg, sel), axis=1, keepdims=True)
    cur = jnp.broadcast_to(m1 + m2, shp)
    gmask = jnp.zeros(shp, F32)
    for _ in range(TOPK_GROUPS):
        gm = jnp.max(cur, axis=0, keepdims=True)
        gi = jnp.min(jnp.where(cur == gm, grp, N_GROUPS), axis=0, keepdims=True)
        pick = grp == gi
        gmask = jnp.where(pick, 1.0, gmask)
        cur = jnp.where(pick, neg, cur)
    masked = jnp.where(gmask > 0.0, sel, neg)
    eidx = grp * per + sub
    chosen = jnp.zeros(shp, F32)
    for _ in range(TOP_K):
        mx = jnp.max(jnp.max(masked, axis=1, keepdims=True), axis=0, keepdims=True)
        cand = jnp.where(masked == mx, eidx, N_EXPERTS)
        ei = jnp.min(jnp.min(cand, axis=1, keepdims=True), axis=0, keepdims=True)
        pick = eidx == ei
        chosen = jnp.where(pick, 1.0, chosen)
        masked = jnp.where(pick, neg, masked)
    top_w = jnp.where(chosen > 0.0, s3, 0.0)
    denom = jnp.sum(jnp.sum(top_w, axis=1, keepdims=True), axis=0, keepdims=True)
    gate_ref[...] = (top_w / denom * ROUTED_SCALE).reshape(N_EXPERTS, tb)
    ch2 = chosen.reshape(N_EXPERTS, tb)
    chb = ch2.astype(BF16)
    before = _dot(chb, tri_ref[...])
    rank_ref[...] = jnp.where(ch2 > 0.0, before, -1.0).astype(jnp.int32)
    cnt_ref[...] = _dot(chb, jnp.ones((tb, LANES), BF16))


def _router_call(g, logits_t, router_b):
    tb = MOE_TOKENS
    nsb = g.t // tb
    tri = (np.arange(tb)[:, None] < np.arange(tb)[None, :]).astype(np.float32)
    blk = pl.BlockSpec((N_EXPERTS, tb), lambda s: (0, s))
    gates, rank, cnt = pl.pallas_call(
        _router_kernel, grid=(nsb,),
        in_specs=[blk, _full((N_EXPERTS, 1)), _full((tb, tb))],
        out_specs=[blk, blk, pl.BlockSpec((None, N_EXPERTS, LANES), lambda s: (s, 0, 0))],
        out_shape=[jax.ShapeDtypeStruct((N_EXPERTS, g.t), F32), jax.ShapeDtypeStruct((N_EXPERTS, g.t), jnp.int32),
                   jax.ShapeDtypeStruct((nsb, N_EXPERTS, LANES), F32)],
        compiler_params=_cparams(("parallel",)), name="router",
    )(logits_t, router_b.reshape(-1, 1), jnp.asarray(tri, BF16))
    return gates, rank, cnt[:, :, 0].astype(jnp.int32).reshape(-1)


def _moe_kernel(cnt_ref, v_ref, gate_ref, rank_ref, wg_ref, wu_ref, wd_ref, sg_ref, su_ref, sd_ref, o_ref,
                psel_ref, ysel_ref, slot_ref):
    s, j = pl.program_id(0), pl.program_id(1)
    tb = v_ref.shape[0]
    rows = MOE_ROWS

    @pl.when(j == 0)
    def _():
        vb = v_ref[...]
        hid = (_silu(_dot(vb, sg_ref[...])) * _dot(vb, su_ref[...])).astype(BF16)
        o_ref[...] = _dot(hid, sd_ref[...])
        psel_ref[...] = jnp.zeros_like(psel_ref)
        ysel_ref[...] = jnp.zeros_like(ysel_ref)
        slot_ref[0] = 0

    row_id = lax.broadcasted_iota(jnp.int32, (rows, tb), 0)

    def flush(n_slots):
        stage_row = lax.broadcasted_iota(jnp.int32, psel_ref.shape, 0)
        sel = jnp.where(stage_row < n_slots * rows, psel_ref[...], jnp.zeros_like(psel_ref))
        o_ref[...] += _dot_tn(sel, ysel_ref[...])

    slot = slot_ref[0]
    for e in range(MOE_PAIR):
        ex = j * MOE_PAIR + e
        n_e = cnt_ref[s * N_EXPERTS + ex]

        def body(c, slot, e=e, ex=ex):
            rk = rank_ref[pl.ds(ex, 1), :] - c * rows
            hit = row_id == rk
            onehot = jnp.where(hit, 1.0, 0.0).astype(BF16)
            w_row = jnp.sum(jnp.where(hit, gate_ref[pl.ds(ex, 1), :], 0.0), axis=1, keepdims=True)
            xs = _dot(onehot, v_ref[...]).astype(BF16)
            hid = (_silu(_dot(xs, wg_ref[e])) * _dot(xs, wu_ref[e])).astype(BF16)
            y = _dot(hid, wd_ref[e])
            at = pl.multiple_of(slot * rows, rows)
            psel_ref[pl.ds(at, rows), :] = onehot
            ysel_ref[pl.ds(at, rows), :] = (y * w_row).astype(BF16)

            @pl.when(slot == MOE_SLOTS - 1)
            def _():
                flush(MOE_SLOTS)

            return jnp.where(slot == MOE_SLOTS - 1, 0, slot + 1)

        slot = lax.fori_loop(0, (n_e + rows - 1) // rows, body, slot)
    slot_ref[0] = slot

    @pl.when(jnp.logical_and(j == pl.num_programs(1) - 1, slot > 0))
    def _():
        flush(slot)


def _moe_call(g, v, gates, rank, counts, wg, wu, wd, sg, su, sd):
    t, d = v.shape
    tb = MOE_TOKENS
    ff = wg.shape[2]
    grid_spec = pltpu.PrefetchScalarGridSpec(
        num_scalar_prefetch=1,
        grid=(t // tb, N_EXPERTS // MOE_PAIR),
        in_specs=[
            pl.BlockSpec((tb, d), lambda s, j, c: (s, 0)),
            pl.BlockSpec((N_EXPERTS, tb), lambda s, j, c: (0, s)),
            pl.BlockSpec((N_EXPERTS, tb), lambda s, j, c: (0, s)),
            pl.BlockSpec((MOE_PAIR, d, ff), lambda s, j, c: (j, 0, 0)),
            pl.BlockSpec((MOE_PAIR, d, ff), lambda s, j, c: (j, 0, 0)),
            pl.BlockSpec((MOE_PAIR, ff, d), lambda s, j, c: (j, 0, 0)),
            pl.BlockSpec((d, sg.shape[1]), lambda s, j, c: (0, 0)),
            pl.BlockSpec((d, su.shape[1]), lambda s, j, c: (0, 0)),
            pl.BlockSpec((sd.shape[0], d), lambda s, j, c: (0, 0)),
        ],
        out_specs=pl.BlockSpec((tb, d), lambda s, j, c: (s, 0)),
        scratch_shapes=[pltpu.VMEM((MOE_SLOTS * MOE_ROWS, tb), BF16), pltpu.VMEM((MOE_SLOTS * MOE_ROWS, d), BF16),
                        pltpu.SMEM((1,), jnp.int32)],
    )
    return pl.pallas_call(
        _moe_kernel, grid_spec=grid_spec, out_shape=jax.ShapeDtypeStruct((t, d), F32),
        compiler_params=_cparams(("parallel", "arbitrary")), name="moe",
    )(counts, v, gates, rank, wg.astype(BF16), wu.astype(BF16), wd.astype(BF16),
      sg.astype(BF16), su.astype(BF16), sd.astype(BF16))


def _rec_pre_kernel(h_ref, mod_ref, gpre_ref, wtm_ref, wfm_ref, cqh_ref, sqh_ref, ckh_ref, skh_ref,
                    z_ref, xbc_ref, dt_ref, kd_ref, qd_ref, vd_ref):
    h = h_ref[...]
    u = _rms(h, -1) * gpre_ref[...]
    u = u * (1.0 + mod_ref[1:2, :]) + mod_ref[0:1, :]
    ub = u.astype(BF16)
    zk = _dot(ub, wtm_ref[...])
    zt = _dot_nt(wfm_ref[...], ub)
    z_ref[...] = zk[:, 0:512].astype(z_ref.dtype)
    xbc_ref[...] = zk[:, 512:1536]
    dt_ref[...] = zk[:, 1536:1664]
    ck, sk = ckh_ref[...], skh_ref[...]
    for j in range(DIFF_HEADS):
        lo = 1664 + 128 * j
        kd_ref[j] = (zk[:, lo:lo + 128] * ck + zk[:, lo + 512:lo + 640] * sk).astype(BF16)
    cq, sq = cqh_ref[...], sqh_ref[...]
    qs = DIFF_SCALE * LOG2E
    zero64 = jnp.zeros((64, h.shape[0]), BF16)
    for hd in range(2 * DIFF_HEADS):
        q = ((zt[64 * hd:64 * hd + 64] * cq + zt[512 + 64 * hd:576 + 64 * hd] * sq) * qs).astype(BF16)
        if hd % 2 == 0:
            qd_ref[hd, 0:64, :] = q
            qd_ref[hd, 64:128, :] = zero64
        else:
            qd_ref[hd, 0:64, :] = zero64
            qd_ref[hd, 64:128, :] = q
    for j in range(DIFF_HEADS):
        vd_ref[j] = zt[1024 + 128 * j:1152 + 128 * j].astype(BF16)


def _rec_pre_weights(g, rec_w_in):
    d = rec_w_in.shape[0]
    o = np.cumsum([0, SSD_INNER, SSD_XBC, SSD_HEADS, SSD_HEADS, 2 * DIFF_HEADS * DIFF_DIM, 2 * DIFF_HEADS * DIFF_DIM,
                   DIFF_HEADS * DIFF_V])
    w_z, w_xbc, w_dtf, w_dtb, w_dq, w_dk, w_dv = (rec_w_in[:, o[i]:o[i + 1]] for i in range(7))
    src64, sgn64 = _rot_map(DIFF_DIM)

    def rot_heads(wcols):
        wh = wcols.reshape(d, 2 * DIFF_HEADS, DIFF_DIM)
        return (wh[:, :, src64] * sgn64).reshape(d, -1)

    w_dt = jnp.pad(jnp.concatenate([w_dtf, w_dtb], axis=1), ((0, 0), (0, LANES - 2 * SSD_HEADS)))
    wtm = jnp.concatenate([w_z, w_xbc, w_dt, w_dk, rot_heads(w_dk)], axis=1)
    wfm = jnp.concatenate([w_dq, rot_heads(w_dq), w_dv], axis=1)
    cos_h, sin_h = _rope_tables(g.n, DIFF_DIM, ROW_BLOCK)
    two = lambda x: jnp.concatenate([x, x], axis=1)
    return dict(wtm=wtm.astype(BF16), wfm=wfm.T.astype(BF16), cqh=cos_h.T, sqh=sin_h.T, ckh=two(cos_h), skh=two(sin_h))


def _rec_pre_call(g, h, mod, gpre, w):
    t, d = h.shape
    rb = ROW_BLOCK
    row = lambda i: (i, 0)
    tm_tab = pl.BlockSpec((rb, LANES), lambda i: (g.pos_block(i), 0))
    fm64 = pl.BlockSpec((64, rb), lambda i: (0, g.pos_block(i)))
    nd = 2 * DIFF_HEADS
    return pl.pallas_call(
        _rec_pre_kernel, grid=(g.n_blocks,),
        in_specs=[pl.BlockSpec((rb, d), row), pl.BlockSpec((None, 6, d), lambda i: (g.mod_row(i), 0, 0)),
                  _full(gpre.shape), _full(w["wtm"].shape), _full(w["wfm"].shape), fm64, fm64, tm_tab, tm_tab],
        out_specs=[
            pl.BlockSpec((rb, SSD_INNER), row), pl.BlockSpec((rb, SSD_XBC), row), pl.BlockSpec((rb, LANES), row),
            pl.BlockSpec((DIFF_HEADS, rb, 128), lambda i: (0, i, 0)),
            pl.BlockSpec((nd, 128, rb), lambda i: (0, 0, i)),
            pl.BlockSpec((DIFF_HEADS, DIFF_V, rb), lambda i: (0, 0, i)),
        ],
        out_shape=[
            jax.ShapeDtypeStruct((t, SSD_INNER), BF16), jax.ShapeDtypeStruct((t, SSD_XBC), F32),
            jax.ShapeDtypeStruct((t, LANES), F32), jax.ShapeDtypeStruct((DIFF_HEADS, t, 128), BF16),
            jax.ShapeDtypeStruct((nd, 128, t), BF16), jax.ShapeDtypeStruct((DIFF_HEADS, DIFF_V, t), BF16),
        ],
        compiler_params=_cparams(("parallel",)), name="rec_pre",
    )(h, mod, gpre, w["wtm"], w["wfm"], w["cqh"], w["sqh"], w["ckh"], w["skh"])


def _conv_kernel(x_ref, prev_ref, next_ref, w_ref, b_ref, o_ref, *, seq_blocks, lat_blocks):
    i = pl.program_id(0)
    rb = x_ref.shape[0]
    pos = i % seq_blocks
    is_ctx = i >= lat_blocks
    first = jnp.logical_or(is_ctx, pos == 0)
    last = jnp.logical_or(is_ctx, pos == seq_blocks - 1)
    x = x_ref[...]
    prev_row = jnp.where(first, 0.0, prev_ref[7:8, :])
    next_row = jnp.where(last, 0.0, next_ref[0:1, :])
    rid = lax.broadcasted_iota(jnp.int32, x.shape, 0)
    x_prev = jnp.where(rid == 0, prev_row, pltpu.roll(x, 1, axis=0))
    x_next = jnp.where(rid == rb - 1, next_row, pltpu.roll(x, rb - 1, axis=0))
    y = w_ref[0:1, :] * x_prev + w_ref[1:2, :] * x + w_ref[2:3, :] * x_next + b_ref[...]
    o_ref[...] = _silu(y)


def _conv_call(g, xbc, conv_w, conv_b):
    t, ch = xbc.shape
    rb = ROW_BLOCK
    halo = 8
    per = rb // halo
    last_halo = t // halo - 1
    return pl.pallas_call(
        functools.partial(_conv_kernel, seq_blocks=g.blocks_per_seq, lat_blocks=g.lat_blocks),
        grid=(g.n_blocks,),
        in_specs=[
            pl.BlockSpec((rb, ch), lambda i: (i, 0)),
            pl.BlockSpec((halo, ch), lambda i: (jnp.maximum(i * per - 1, 0), 0)),
            pl.BlockSpec((halo, ch), lambda i: (jnp.minimum((i + 1) * per, last_halo), 0)),
            _full((SSD_CONV, ch)), _full((1, ch)),
        ],
        out_specs=pl.BlockSpec((rb, ch), lambda i: (i, 0)),
        out_shape=jax.ShapeDtypeStruct((t, ch), F32),
        compiler_params=_cparams(("parallel",)), name="ssd_conv",
    )(xbc, xbc, xbc, conv_w, conv_b.reshape(1, ch))


def _ssd_kernel(x_ref, dt_ref, bias_ref, alog_ref, tri_ref, trit_ref, eye_ref, y_ref, state_ref, *, lane_off, reverse):
    k = pl.program_id(1)
    ln = SSD_CHUNK
    hi = lax.Precision.HIGHEST

    @pl.when(k == 0)
    def _():
        state_ref[...] = jnp.zeros_like(state_ref)

    x = x_ref[:, 0:SSD_INNER]
    raw = dt_ref[...] + bias_ref[...]
    dt = jnp.maximum(raw, 0.0) + jnp.log1p(jnp.exp(-jnp.abs(raw)))
    adt = dt * (-jnp.exp(alog_ref[...]))
    acs = jnp.dot(tri_ref[...], adt, preferred_element_type=F32, precision=hi)
    acs_row = lax.dot_general(adt, trit_ref[...], (((0,), (0,)), ((), ())), preferred_element_type=F32, precision=hi)
    dt_row = lax.dot_general(dt, eye_ref[...], (((0,), (0,)), ((), ())), preferred_element_type=F32, precision=hi)
    tot = acs[0:1, :] if reverse else acs[ln - 1:ln, :]
    w_all = jnp.exp(tot - acs) * dt
    ea_all = jnp.exp(acs)
    etot = jnp.exp(tot)
    li = lax.broadcasted_iota(jnp.int32, (ln, ln), 0)
    si = lax.broadcasted_iota(jnp.int32, (ln, ln), 1)
    keep = (si >= li) if reverse else (si <= li)
    lane = lax.broadcasted_iota(jnp.int32, (ln, LANES), 1)
    left = lane < SSD_HEAD_DIM
    per_group = SSD_HEADS // SSD_GROUPS
    for gi in range(SSD_GROUPS):
        bm = x_ref[:, SSD_INNER + SSD_STATE * gi:SSD_INNER + SSD_STATE * (gi + 1)].astype(BF16)
        cm = x_ref[:, SSD_INNER + SSD_STATE * (SSD_GROUPS + gi):SSD_INNER + SSD_STATE * (SSD_GROUPS + gi + 1)]
        cm = cm.astype(BF16)
        cb = _dot_nt(cm, bm)
        for pr in range(per_group // 2):
            h0 = gi * per_group + 2 * pr
            xp = x[:, SSD_HEAD_DIM * h0:SSD_HEAD_DIM * (h0 + 2)]
            mats = []
            for hd in (h0, h0 + 1):
                c = lane_off + hd
                seg = acs[:, c:c + 1] - acs_row[c:c + 1, :]
                lmat = jnp.exp(jnp.where(keep, seg, -jnp.inf))
                mats.append((cb * lmat * dt_row[c:c + 1, :]).astype(BF16))
            xb = xp.astype(BF16)
            zero = jnp.zeros_like(xb)
            rhs = jnp.concatenate([jnp.where(left, xb, zero), jnp.where(left, zero, xb)], axis=0)
            y_diag = _dot(jnp.concatenate(mats, axis=1), rhs)
            st = state_ref[h0 // 2]
            c0 = lane_off + h0
            ea = jnp.where(left, ea_all[:, c0:c0 + 1], ea_all[:, c0 + 1:c0 + 2])
            y_off = _dot(cm, st.astype(BF16)) * ea
            y_ref[:, SSD_HEAD_DIM * h0:SSD_HEAD_DIM * (h0 + 2)] = y_diag + y_off
            wcol = jnp.where(left, w_all[:, c0:c0 + 1], w_all[:, c0 + 1:c0 + 2])
            cs = _dot_tn(bm, (xp * wcol).astype(BF16))
            dec = jnp.where(left[0:1, :], etot[:, c0:c0 + 1], etot[:, c0 + 1:c0 + 2])
            state_ref[h0 // 2] = st * dec + cs


def _ssd_call(g, xbc_act, dt, dt_bias, a_log, *, reverse):
    t = xbc_act.shape[0]
    ln = SSD_CHUNK
    cc = g.c // ln
    nl = g.n // ln
    ctx0 = g.b * g.n // ln

    def chunk(b, k):
        if reverse:
            return jnp.where(k < cc, ctx0 + b * cc + (cc - 1 - k), b * nl + (nl - 1 - (k - cc)))
        return jnp.where(k < cc, ctx0 + b * cc + k, b * nl + (k - cc))

    idx = np.arange(ln)
    lower = (idx[:, None] >= idx[None, :]).astype(np.float32)
    tri = lower.T if reverse else lower
    row = lambda b, k: (chunk(b, k), 0)
    off = SSD_HEADS if reverse else 0
    lanes = lambda p: jnp.pad(p.reshape(1, -1), ((0, 0), (off, LANES - off - SSD_HEADS)))
    return pl.pallas_call(
        functools.partial(_ssd_kernel, lane_off=SSD_HEADS if reverse else 0, reverse=reverse),
        grid=(g.b, cc + nl),
        in_specs=[pl.BlockSpec((ln, SSD_XBC), row), pl.BlockSpec((ln, LANES), row),
                  _full((1, LANES)), _full((1, LANES)), _full((ln, ln)), _full((ln, ln)), _full((ln, ln))],
        out_specs=pl.BlockSpec((ln, SSD_INNER), row),
        out_shape=jax.ShapeDtypeStruct((t, SSD_INNER), F32),
        scratch_shapes=[pltpu.VMEM((SSD_HEADS // 2, SSD_STATE, 2 * SSD_HEAD_DIM), F32)],
        compiler_params=_cparams(("parallel", "arbitrary")),
        name="ssd_bwd" if reverse else "ssd_fwd",
    )(xbc_act, dt, lanes(dt_bias), lanes(a_log), jnp.asarray(tri), jnp.asarray(tri.T), jnp.eye(ln, dtype=F32))


def _diff_finish(o, tq, lq1_ref, lk1_ref, lq2_ref, lk2_ref, sub_ref, o_ref, lambda_init):
    lam = (jnp.exp(jnp.sum(lq1_ref[...] * lk1_ref[...], axis=1, keepdims=True))
           - jnp.exp(jnp.sum(lq2_ref[...] * lk2_ref[...], axis=1, keepdims=True)) + lambda_init)
    od = o[:, 0:tq] - lam * o[:, tq:2 * tq]
    o_ref[...] = ((_rms(od, 0) * sub_ref[...]) * (1.0 - lambda_init)).astype(o_ref.dtype)


def _diff_attn_kernel(qt_ref, *refs, lambda_init):
    o_ref, s_ref = refs[-2:]
    params = refs[-7:-2]
    kv = refs[:-7]
    half = len(kv) // 2
    segs = [(kv[i], kv[half + i]) for i in range(half)]
    qt = jnp.concatenate([qt_ref[0], qt_ref[1]], axis=1)
    o = _attend(qt, segs, s_ref, DIFF_V)
    _diff_finish(o, qt_ref.shape[2], *params, o_ref, lambda_init)


def _diff_attn_call(g, qd, kd, vd, lq1, lk1, lq2, lk2, subln, lambda_init):
    tq = DIFF_Q_TILE
    nq = g.n // tq
    ctx_blk = g.b * g.n // g.c
    vec = lambda a: a.reshape(1, -1)
    params = [_full((1, DIFF_DIM)), _full((1, DIFF_DIM)), _full((1, DIFF_DIM)), _full((1, DIFF_DIM)),
              _full((DIFF_V, 1))]
    pvals = (vec(lq1), vec(lk1), vec(lq2), vec(lk2), subln.reshape(-1, 1))
    klat = pl.BlockSpec((None, g.n, 128), lambda b, j, *_: (j, b, 0))
    kctx = pl.BlockSpec((None, g.c, 128), lambda b, j, *_: (j, ctx_blk + b, 0))
    vlat = pl.BlockSpec((None, DIFF_V, g.n), lambda b, j, *_: (j, 0, b))
    vctx = pl.BlockSpec((None, DIFF_V, g.c), lambda b, j, *_: (j, 0, ctx_blk + b))
    kern = functools.partial(_diff_attn_kernel, lambda_init=lambda_init)
    lat = pl.pallas_call(
        kern, grid=(g.b, DIFF_HEADS, nq),
        scratch_shapes=[pltpu.VMEM((g.n + g.c, 2 * tq), F32)],
        in_specs=[pl.BlockSpec((2, 128, tq), lambda b, j, qi: (j, 0, b * nq + qi)), klat, kctx, vlat, vctx] + params,
        out_specs=pl.BlockSpec((DIFF_V, tq), lambda b, j, qi: (j, b * nq + qi)),
        out_shape=jax.ShapeDtypeStruct((DIFF_HEADS * DIFF_V, g.b * g.n), BF16),
        compiler_params=_cparams(("parallel", "parallel", "arbitrary")),
        name="attn_diff",
    )(qd, kd, kd, vd, vd, *pvals)
    ctx = pl.pallas_call(
        kern, grid=(g.b, DIFF_HEADS),
        scratch_shapes=[pltpu.VMEM((g.c, 2 * g.c), F32)],
        in_specs=[pl.BlockSpec((2, 128, g.c), lambda b, j: (j, 0, ctx_blk + b)), kctx, vctx] + params,
        out_specs=pl.BlockSpec((DIFF_V, g.c), lambda b, j: (j, b)),
        out_shape=jax.ShapeDtypeStruct((DIFF_HEADS * DIFF_V, g.b * g.c), BF16),
        compiler_params=_cparams(("parallel", "parallel")),
        name="attn_diff_ctx",
    )(qd, kd, vd, *pvals)
    return jnp.concatenate([lat, ctx], axis=1)


def _rec_post_kernel(yf_ref, yb_ref, xs_ref, z_ref, od_ref, dskip_ref, gn_ref, wa_ref, wb_ref,
                     h_ref, mod_ref, gpost_ref, gffn_ref, rw_ref, hn_ref, v_ref, lg_ref):
    y = (yf_ref[...] + yb_ref[...] + xs_ref[...] * dskip_ref[...]) * _silu(z_ref[...].astype(F32))
    half = SSD_INNER // SSD_GROUPS
    o = _dot_tn(od_ref[...], wb_ref[...])
    for gi in range(SSD_GROUPS):
        lo = half * gi
        yn = (_rms(y[:, lo:lo + half], -1) * gn_ref[:, lo:lo + half]).astype(BF16)
        o = o + _dot(yn, wa_ref[lo:lo + half, :])
    _post_tail(o, h_ref, mod_ref, gpost_ref, gffn_ref, rw_ref, hn_ref, v_ref, lg_ref)


def _rec_post_call(g, yf, yb, xbc_act, zs, od, d_skip, ssd_norm, w_out, h, mod, gpost, gffn, router_w):
    d = h.shape[1]
    rb = ROW_BLOCK
    ins, outs, shapes = _post_specs(g, d)
    wo = w_out.astype(BF16)
    row512 = pl.BlockSpec((rb, SSD_INNER), lambda i: (i, 0))
    return pl.pallas_call(
        _rec_post_kernel, grid=(g.n_blocks,),
        in_specs=[row512, row512, row512, row512, pl.BlockSpec((SSD_INNER, rb), lambda i: (0, i)),
                  _full((1, SSD_INNER)), _full((1, SSD_INNER)), _full((SSD_INNER, d)), _full((SSD_INNER, d))] + ins,
        out_specs=outs, out_shape=shapes,
        compiler_params=_cparams(("parallel",)), name="rec_post",
    )(yf, yb, xbc_act, zs, od, jnp.repeat(d_skip, SSD_HEAD_DIM).reshape(1, -1), ssd_norm.reshape(1, -1),
      wo[:SSD_INNER], wo[SSD_INNER:], h, mod, gpost.reshape(1, d), gffn.reshape(1, d), router_w.T)


def _ffn_res_kernel(h_ref, f_ref, mod_ref, g_ref, o_ref):
    o_ref[...] = h_ref[...] + mod_ref[5:6, :] * (_rms(f_ref[...], -1) * g_ref[...])


def _ffn_res_call(g, h, f, mod, gain, n_rows):
    d = h.shape[1]
    rb = ROW_BLOCK
    row = pl.BlockSpec((rb, d), lambda i: (i, 0))
    return pl.pallas_call(
        _ffn_res_kernel, grid=(n_rows // rb,),
        in_specs=[row, row, pl.BlockSpec((None, 6, d), lambda i: (g.mod_row(i), 0, 0)), _full((1, d))],
        out_specs=row, out_shape=jax.ShapeDtypeStruct((n_rows, d), F32),
        compiler_params=_cparams(("parallel",)), name="ffn_res",
    )(h, f, mod, gain.reshape(1, d))


def kernel(x, c, ctx, c_ctx, ada_w, ada_b, norm_mix_pre, norm_mix_post, norm_ffn_pre, norm_ffn_post, mix_w_out, att_w_in, mla_q_norm, mla_w_q_up, mla_kv_norm, mla_w_kv_up, gqa_q_norm, gqa_k_norm, rec_w_in, ssd_conv_w, ssd_conv_b, ssd_dt_bias_f, ssd_dt_bias_b, ssd_a_log_f, ssd_a_log_b, ssd_d, ssd_norm, diff_lambda_q1, diff_lambda_k1, diff_lambda_q2, diff_lambda_k2, diff_subln, router_w, router_b, exp_w_gate, exp_w_up, exp_w_down, sh_w_gate, sh_w_up, sh_w_down):
    b, n, d = x.shape
    n_ctx = ctx.shape[1]
    depth = ada_w.shape[0]
    g = _Geom(b, n, n_ctx)
    mod_rows = -(-(b + 1) // 8) * 8
    c_all = jnp.concatenate([c, c_ctx[None, :], jnp.zeros((mod_rows - b - 1, d), F32)], axis=0)
    mods = _ada_call(c_all, ada_w, ada_b).reshape(depth, mod_rows, 6, d)
    h = jnp.concatenate([x.reshape(b * n, d), ctx.reshape(b * n_ctx, d)], axis=0)
    for i in range(depth):
        last = i == depth - 1
        jdx = i // 2
        mod = mods[i]
        if i % 2 == 0:
            w = _attn_pre_weights(g, att_w_in[jdx], mla_q_norm[jdx], mla_w_q_up[jdx], mla_kv_norm[jdx],
                                  mla_w_kv_up[jdx], gqa_q_norm[jdx], gqa_k_norm[jdx])
            qm, km, vm, qg, kg, vg = _attn_pre_call(g, h, mod, norm_mix_pre[i].reshape(1, d), w)
            oa = _attn_call(g, qm, km, vm, kv_heads=MLA_HEADS, shared_k=False, name="attn_mla")
            ob = _attn_call(g, qg, kg, vg, kv_heads=GQA_KV_HEADS, shared_k=True, name="attn_gqa")
            h, v, logits_t = _attn_post_call(g, oa, ob, mix_w_out[i], h, mod, norm_mix_post[i], norm_ffn_pre[i],
                                             router_w[i])
        else:
            lambda_init = 0.8 - 0.6 * math.exp(-0.3 * i)
            w = _rec_pre_weights(g, rec_w_in[jdx])
            zs, xbc, dt, kd, qd, vd = _rec_pre_call(g, h, mod, norm_mix_pre[i].reshape(1, d), w)
            xbc_act = _conv_call(g, xbc, ssd_conv_w[jdx], ssd_conv_b[jdx])
            yf = _ssd_call(g, xbc_act, dt, ssd_dt_bias_f[jdx], ssd_a_log_f[jdx], reverse=False)
            yb = _ssd_call(g, xbc_act, dt, ssd_dt_bias_b[jdx], ssd_a_log_b[jdx], reverse=True)
            od = _diff_attn_call(g, qd, kd, vd, diff_lambda_q1[jdx], diff_lambda_k1[jdx], diff_lambda_q2[jdx],
                                 diff_lambda_k2[jdx], diff_subln[jdx], lambda_init)
            h, v, logits_t = _rec_post_call(g, yf, yb, xbc_act, zs, od, ssd_d[jdx], ssd_norm[jdx], mix_w_out[i], h,
                                            mod, norm_mix_post[i], norm_ffn_pre[i], router_w[i])
        gates, rank, counts = _router_call(g, logits_t, router_b[i])
        f = _moe_call(g, v, gates, rank, counts, exp_w_gate[i], exp_w_up[i], exp_w_down[i],
                      sh_w_gate[i], sh_w_up[i], sh_w_down[i])
        h = _ffn_res_call(g, h, f, mod, norm_ffn_post[i], b * n if last else g.t)
    return h.reshape(b, n, d)
```

```python
import functools
import math

import numpy as np
import jax
import jax.numpy as jnp
from jax import lax
from jax.experimental import pallas as pl
from jax.experimental.pallas import tpu as pltpu

F32 = jnp.float32
BF16 = jnp.bfloat16
LOG2E = 1.4426950408889634

GRID_W = 64
ROPE_BASE = 10000.0
NORM_EPS = 1e-6

MLA_HEADS, MLA_Q_LORA, MLA_KV_LORA, MLA_NOPE, MLA_ROPE, MLA_V = 8, 256, 128, 64, 32, 64
MLA_SCALE = (MLA_NOPE + MLA_ROPE) ** -0.5
GQA_HEADS, GQA_KV_HEADS, GQA_DIM = 8, 2, 64
GQA_SCALE = GQA_DIM ** -0.5
SSD_HEADS, SSD_HEAD_DIM, SSD_GROUPS, SSD_STATE, SSD_CONV, SSD_CHUNK = 8, 64, 2, 128, 3, 128
SSD_INNER = SSD_HEADS * SSD_HEAD_DIM
SSD_XBC = SSD_INNER + 2 * SSD_GROUPS * SSD_STATE
DIFF_HEADS, DIFF_DIM = 4, 64
DIFF_V = 2 * DIFF_DIM
DIFF_SCALE = DIFF_DIM ** -0.5
N_EXPERTS, TOP_K, N_GROUPS, TOPK_GROUPS, EXPERT_FF, SHARED_FF = 64, 8, 8, 4, 256, 256
ROUTED_SCALE = 2.5

LANES = 128
VMEM_LIMIT_BYTES = 56 * 1024 * 1024

ROW_BLOCK = 512
ATTN_Q_TILE = 512
DIFF_Q_TILE = 256
ATTN_KV_CHUNK = 512
ATTN_EXP_ROWS = 4096
MOE_TOKENS = 768
MOE_ROWS = 128
MOE_PAIR = 8
MOE_SLOTS = 4


def _cparams(sem, flags=None):
    return pltpu.CompilerParams(dimension_semantics=sem, vmem_limit_bytes=VMEM_LIMIT_BYTES, flags=flags)


def _rms(x, axis):
    return x * lax.rsqrt(jnp.mean(x * x, axis=axis, keepdims=True) + NORM_EPS)


def _silu(x):
    return x * jax.nn.sigmoid(x)


def _dot(a, b):
    return jnp.dot(a, b, preferred_element_type=F32)


def _dot_nt(a, b):
    return lax.dot_general(a, b, (((1,), (1,)), ((), ())), preferred_element_type=F32)


def _dot_tn(a, b):
    return lax.dot_general(a, b, (((0,), (0,)), ((), ())), preferred_element_type=F32)


def _ada_kernel(c_ref, w_ref, b_ref, o_ref):
    s = _silu(c_ref[...])
    o_ref[...] = jnp.dot(s, w_ref[...], preferred_element_type=F32, precision=lax.Precision.HIGHEST) + b_ref[...]


def _ada_call(c_all, ada_w, ada_b):
    depth, d, six_d = ada_w.shape
    rows = c_all.shape[0]
    cols = six_d // 4
    return pl.pallas_call(
        _ada_kernel,
        grid=(depth, six_d // cols),
        in_specs=[
            pl.BlockSpec((rows, d), lambda i, j: (0, 0)),
            pl.BlockSpec((None, d, cols), lambda i, j: (i, 0, j)),
            pl.BlockSpec((None, 1, cols), lambda i, j: (i, 0, j)),
        ],
        out_specs=pl.BlockSpec((None, rows, cols), lambda i, j: (i, 0, j)),
        out_shape=jax.ShapeDtypeStruct((depth, rows, six_d), F32),
        compiler_params=_cparams(("arbitrary", "arbitrary")),
        name="ada_mod",
    )(c_all, ada_w, ada_b.reshape(depth, 1, six_d))


def _attn_pre_kernel(h_ref, mod_ref, gpre_ref, wfm_ref, wtm_ref, qn_ref, kvn_col_ref, kvn_row_ref, wq_ref, wk_ref,
                     wv_ref, gq_g_ref, gq_gr_ref, gk_g_ref, gk_gr_ref,
                     cqm_ref, sqm_ref, cqh_ref, sqh_ref, ckm_ref, skm_ref, ckh_ref, skh_ref,
                     qm_ref, km_ref, vm_ref, qg_ref, kg_ref, vg_ref):
    h = h_ref[...]
    u = _rms(h, -1) * gpre_ref[...]
    u = u * (1.0 + mod_ref[1:2, :]) + mod_ref[0:1, :]
    ub = u.astype(BF16)
    zt = _dot_nt(wfm_ref[...], ub)
    zk = _dot(ub, wtm_ref[...])

    qn = (_rms(zt[0:256], 0) * qn_ref[...]).astype(BF16)
    qt = _dot(wq_ref[...], qn)
    cq, sq = cqm_ref[...], sqm_ref[...]
    qs = MLA_SCALE * LOG2E
    zero32 = jnp.zeros((32, h.shape[0]), BF16)
    for hd in range(MLA_HEADS):
        pe = qt[512 + 32 * hd:544 + 32 * hd] * cq + qt[768 + 32 * hd:800 + 32 * hd] * sq
        qm_ref[hd, 0:64, :] = (qt[64 * hd:64 * hd + 64] * qs).astype(BF16)
        qm_ref[hd, 64:96, :] = (pe * qs).astype(BF16)
        qm_ref[hd, 96:128, :] = zero32

    kvn_t = (_rms(zt[256:384], 0) * kvn_col_ref[...]).astype(BF16)
    vt = _dot(wv_ref[...], kvn_t)
    for hd in range(MLA_HEADS):
        vm_ref[hd] = vt[64 * hd:64 * hd + 64].astype(BF16)
    kvn = (_rms(zk[:, 0:128], -1) * kvn_row_ref[...]).astype(BF16)
    kn = _dot(kvn, wk_ref[...])
    kpe = zk[:, 128:256] * ckm_ref[...] + zk[:, 256:384] * skm_ref[...]
    for hd in range(MLA_HEADS):
        km_ref[hd] = (kn[:, 128 * hd:128 * hd + 128] + kpe).astype(BF16)

    cqh, sqh = cqh_ref[...], sqh_ref[...]
    gs = GQA_SCALE * LOG2E
    zero64 = jnp.zeros((64, h.shape[0]), BF16)
    grp = GQA_HEADS // GQA_KV_HEADS
    for hd in range(GQA_HEADS):
        raw = zt[384 + 64 * hd:448 + 64 * hd]
        rot = zt[896 + 64 * hd:960 + 64 * hd]
        r = lax.rsqrt(jnp.mean(raw * raw, axis=0, keepdims=True) + NORM_EPS)
        q = (raw * r * gq_g_ref[...]) * cqh + (rot * r * gq_gr_ref[...]) * sqh
        q = (q * gs).astype(BF16)
        if hd // grp == 0:
            qg_ref[hd, 0:64, :] = q
            qg_ref[hd, 64:128, :] = zero64
        else:
            qg_ref[hd, 0:64, :] = zero64
            qg_ref[hd, 64:128, :] = q
    for kvh in range(GQA_KV_HEADS):
        vg_ref[kvh] = zt[1408 + 64 * kvh:1472 + 64 * kvh].astype(BF16)

    gk, gkr = zk[:, 384:512], zk[:, 512:640]
    lane = lax.broadcasted_iota(jnp.int32, gk.shape, 1)
    lo = lane < 64
    sq0 = jnp.sum(jnp.where(lo, gk * gk, 0.0), axis=-1, keepdims=True)
    sq1 = jnp.sum(jnp.where(lo, 0.0, gk * gk), axis=-1, keepdims=True)
    r = jnp.where(lo, lax.rsqrt(sq0 / GQA_DIM + NORM_EPS), lax.rsqrt(sq1 / GQA_DIM + NORM_EPS))
    kg_ref[...] = ((gk * r * gk_g_ref[...]) * ckh_ref[...] + (gkr * r * gk_gr_ref[...]) * skh_ref[...]).astype(BF16)


def _rot_map(dim):
    q = dim // 4
    j = np.arange(dim)
    even = (j // q) % 2 == 0
    return np.where(even, j + q, j - q), np.where(even, -1.0, 1.0).astype(np.float32)


def _rope_tables(n_tokens, dim, pad_rows):
    rows = n_tokens // GRID_W
    row = jnp.repeat(jnp.arange(rows, dtype=F32), GRID_W)
    col = jnp.tile(jnp.arange(GRID_W, dtype=F32), rows)
    half = dim // 2
    inv = ROPE_BASE ** (-(jnp.arange(half // 2, dtype=F32) * 2.0 / half))
    ang_r = row[:, None] * inv
    ang_c = col[:, None] * inv
    ang = jnp.concatenate([ang_r, ang_r, ang_c, ang_c], axis=-1)
    cos = jnp.concatenate([jnp.cos(ang), jnp.ones((pad_rows, dim), F32)], axis=0)
    sin = jnp.concatenate([jnp.sin(ang), jnp.zeros((pad_rows, dim), F32)], axis=0)
    return cos, sin


class _Geom:
    def __init__(self, b, n, c):
        assert n % ROW_BLOCK == 0 and (b * c) % ROW_BLOCK == 0 and n % c == 0 and n % GRID_W == 0
        self.b, self.n, self.c = b, n, c
        self.t = b * n + b * c
        self.lat_blocks = b * n // ROW_BLOCK
        self.blocks_per_seq = n // ROW_BLOCK
        self.n_blocks = self.t // ROW_BLOCK
        assert self.t % MOE_TOKENS == 0

    def mod_row(self, i):
        return jnp.where(i < self.lat_blocks, i // self.blocks_per_seq, self.b)

    def pos_block(self, i):
        return jnp.where(i < self.lat_blocks, i % self.blocks_per_seq, self.blocks_per_seq)


def _full(shape):
    nd = len(shape)
    return pl.BlockSpec(shape, lambda *_: (0,) * nd)


def _attn_pre_call(g, h, mod, gpre, w):
    t, d = h.shape
    rb = ROW_BLOCK
    row = lambda i: (i, 0)
    tm_tab = pl.BlockSpec((rb, LANES), lambda i: (g.pos_block(i), 0))
    fm32 = pl.BlockSpec((32, rb), lambda i: (0, g.pos_block(i)))
    fm64 = pl.BlockSpec((64, rb), lambda i: (0, g.pos_block(i)))
    in_specs = [
        pl.BlockSpec((rb, d), row),
        pl.BlockSpec((None, 6, d), lambda i: (g.mod_row(i), 0, 0)),
        _full(gpre.shape), _full(w["wfm"].shape), _full(w["wtm"].shape), _full(w["qn"].shape),
        _full(w["kvn_col"].shape), _full(w["kvn_row"].shape), _full(w["wq"].shape), _full(w["wk"].shape),
        _full(w["wv"].shape), _full(w["gq_g"].shape), _full(w["gq_gr"].shape), _full(w["gk_g"].shape),
        _full(w["gk_gr"].shape),
        fm32, fm32, fm64, fm64, tm_tab, tm_tab, tm_tab, tm_tab,
    ]
    out_shape = [
        jax.ShapeDtypeStruct((MLA_HEADS, 128, t), BF16),
        jax.ShapeDtypeStruct((MLA_HEADS, t, 128), BF16),
        jax.ShapeDtypeStruct((MLA_HEADS, MLA_V, t), BF16),
        jax.ShapeDtypeStruct((GQA_HEADS, 128, t), BF16),
        jax.ShapeDtypeStruct((t, 128), BF16),
        jax.ShapeDtypeStruct((GQA_KV_HEADS, GQA_DIM, t), BF16),
    ]
    out_specs = [
        pl.BlockSpec((MLA_HEADS, 128, rb), lambda i: (0, 0, i)),
        pl.BlockSpec((MLA_HEADS, rb, 128), lambda i: (0, i, 0)),
        pl.BlockSpec((MLA_HEADS, MLA_V, rb), lambda i: (0, 0, i)),
        pl.BlockSpec((GQA_HEADS, 128, rb), lambda i: (0, 0, i)),
        pl.BlockSpec((rb, 128), row),
        pl.BlockSpec((GQA_KV_HEADS, GQA_DIM, rb), lambda i: (0, 0, i)),
    ]
    return pl.pallas_call(
        _attn_pre_kernel, grid=(g.n_blocks,), in_specs=in_specs, out_specs=out_specs, out_shape=out_shape,
        compiler_params=_cparams(("parallel",)), name="attn_pre",
    )(h, mod, gpre, w["wfm"], w["wtm"], w["qn"], w["kvn_col"], w["kvn_row"], w["wq"], w["wk"], w["wv"],
      w["gq_g"], w["gq_gr"], w["gk_g"], w["gk_gr"],
      w["cqm"], w["sqm"], w["cqh"], w["sqh"], w["ckm"], w["skm"], w["ckh"], w["skh"])


def _attn_pre_weights(g, att_w_in, q_norm, w_q_up, kv_norm, w_kv_up, gqa_qn, gqa_kn):
    d = att_w_in.shape[0]
    o = np.cumsum([0, MLA_Q_LORA, MLA_KV_LORA, MLA_ROPE, GQA_HEADS * GQA_DIM, GQA_KV_HEADS * GQA_DIM,
                   GQA_KV_HEADS * GQA_DIM])
    w_qlat, w_kvlat, w_kpe, w_gq, w_gk, w_gv = (att_w_in[:, o[i]:o[i + 1]] for i in range(6))
    src32, sgn32 = _rot_map(MLA_ROPE)
    src64, sgn64 = _rot_map(GQA_DIM)

    def rot_heads(wcols, heads, dim, src, sgn):
        wh = wcols.reshape(d, heads, dim)
        return (wh[:, :, src] * sgn).reshape(d, heads * dim)

    wfm = jnp.concatenate([w_qlat, w_kvlat, w_gq, rot_heads(w_gq, GQA_HEADS, GQA_DIM, src64, sgn64), w_gv], axis=1)
    zpad = lambda x, lo, hi: jnp.pad(x, ((0, 0), (lo, hi)))
    wtm = jnp.concatenate([
        w_kvlat, zpad(w_kpe, 64, 32), zpad(w_kpe[:, src32] * sgn32, 64, 32),
        w_gk, rot_heads(w_gk, GQA_KV_HEADS, GQA_DIM, src64, sgn64)], axis=1)
    wq = w_q_up.reshape(MLA_Q_LORA, MLA_HEADS, MLA_NOPE + MLA_ROPE)
    wq_pe = wq[:, :, MLA_NOPE:]
    wq_all = jnp.concatenate([
        wq[:, :, :MLA_NOPE].reshape(MLA_Q_LORA, -1), wq_pe.reshape(MLA_Q_LORA, -1),
        (wq_pe[:, :, src32] * sgn32).reshape(MLA_Q_LORA, -1)], axis=1)
    wkv = w_kv_up.reshape(MLA_KV_LORA, MLA_HEADS, MLA_NOPE + MLA_V)
    wk = jnp.pad(wkv[:, :, :MLA_NOPE], ((0, 0), (0, 0), (0, 128 - MLA_NOPE))).reshape(MLA_KV_LORA, -1)
    wv = wkv[:, :, MLA_NOPE:].reshape(MLA_KV_LORA, -1)
    cos_m, sin_m = _rope_tables(g.n, MLA_ROPE, ROW_BLOCK)
    cos_h, sin_h = _rope_tables(g.n, GQA_DIM, ROW_BLOCK)
    two = lambda x: jnp.concatenate([x, x], axis=1)
    return dict(
        wfm=wfm.T.astype(BF16), wtm=wtm.astype(BF16),
        qn=q_norm.reshape(-1, 1), kvn_col=kv_norm.reshape(-1, 1), kvn_row=kv_norm.reshape(1, -1),
        wq=wq_all.T.astype(BF16), wk=wk.astype(BF16), wv=wv.T.astype(BF16),
        gq_g=gqa_qn.reshape(-1, 1), gq_gr=gqa_qn[src64].reshape(-1, 1),
        gk_g=two(gqa_kn.reshape(1, -1)), gk_gr=two(gqa_kn[src64].reshape(1, -1)),
        cqm=cos_m.T, sqm=sin_m.T, cqh=cos_h.T, sqh=sin_h.T,
        ckm=zpad(cos_m, 64, 32), skm=zpad(sin_m, 64, 32), ckh=two(cos_h), skh=two(sin_h),
    )


def _fold8(x, op):
    r, w = x.shape
    return op(x.reshape(r // 8, 8, w), axis=0)


def _attend(qt, segments, s_ref, dv):
    tq = qt.shape[1]
    pieces, off = [], 0
    for k_ref, vt_ref in segments:
        n = k_ref.shape[0]
        for lo in range(0, n, ATTN_KV_CHUNK):
            rows = min(ATTN_KV_CHUNK, n - lo)
            pieces.append((off, lo, rows, k_ref, vt_ref))
            off += rows
    m8 = None
    for so, lo, rows, k_ref, _ in pieces:
        s = _dot(k_ref[lo:lo + rows, :], qt)
        s_ref[so:so + rows, :] = s
        part = _fold8(s, jnp.max)
        m8 = part if m8 is None else jnp.maximum(m8, part)
    m = jnp.max(m8, axis=0, keepdims=True)
    acc = jnp.zeros((dv + 16, tq), F32)
    off = 0
    for _, vt_ref in segments:
        n = vt_ref.shape[1]
        step = min(n, ATTN_EXP_ROWS)

        def body(i, acc, off=off, vt_ref=vt_ref, step=step):
            base = i * step if isinstance(i, int) else pl.multiple_of(i * step, step)
            for lo in range(0, step, ATTN_KV_CHUNK):
                rows = min(ATTN_KV_CHUNK, step - lo)
                p = jnp.exp2((s_ref[pl.ds(off + base + lo, rows), :] - m).astype(BF16))
                vt1 = jnp.concatenate([vt_ref[:, pl.ds(base + lo, rows)], jnp.ones((16, rows), BF16)], axis=0)
                acc = acc + _dot(vt1, p)
            return acc

        acc = body(0, acc) if n == step else lax.fori_loop(0, n // step, body, acc)
        off += n
    return acc[0:dv] / acc[dv:dv + 1]


def _attn_kernel(qt_ref, *refs, dv):
    o_ref, s_ref = refs[-2:]
    kv = refs[:-2]
    half = len(kv) // 2
    segs = [(kv[i], kv[half + i]) for i in range(half)]
    o_ref[...] = _attend(qt_ref[...], segs, s_ref, dv).astype(o_ref.dtype)


def _attn_call(g, qt, k, vt, *, kv_heads, shared_k, name):
    heads = qt.shape[0]
    dv = vt.shape[1]
    grp = heads // kv_heads
    tq = ATTN_Q_TILE
    nq = g.n // tq
    ctx_blk = g.b * g.n // g.c
    if shared_k:
        klat = pl.BlockSpec((g.n, 128), lambda b, h, *_: (b, 0))
        kctx = pl.BlockSpec((g.c, 128), lambda b, h, *_: (ctx_blk + b, 0))
    else:
        klat = pl.BlockSpec((None, g.n, 128), lambda b, h, *_: (h // grp, b, 0))
        kctx = pl.BlockSpec((None, g.c, 128), lambda b, h, *_: (h // grp, ctx_blk + b, 0))
    vlat = pl.BlockSpec((None, dv, g.n), lambda b, h, *_: (h // grp, 0, b))
    vctx = pl.BlockSpec((None, dv, g.c), lambda b, h, *_: (h // grp, 0, ctx_blk + b))
    lat = pl.pallas_call(
        functools.partial(_attn_kernel, dv=dv),
        grid=(g.b, heads, nq),
        scratch_shapes=[pltpu.VMEM((g.n + g.c, tq), F32)],
        in_specs=[pl.BlockSpec((None, 128, tq), lambda b, h, qi: (h, 0, b * nq + qi)), klat, kctx, vlat, vctx],
        out_specs=pl.BlockSpec((dv, tq), lambda b, h, qi: (h, b * nq + qi)),
        out_shape=jax.ShapeDtypeStruct((heads * dv, g.b * g.n), BF16),
        compiler_params=_cparams(("parallel", "parallel", "arbitrary")),
        name=name,
    )(qt, k, k, vt, vt)
    ctx = pl.pallas_call(
        functools.partial(_attn_kernel, dv=dv),
        grid=(g.b, heads),
        scratch_shapes=[pltpu.VMEM((g.c, g.c), F32)],
        in_specs=[pl.BlockSpec((None, 128, g.c), lambda b, h: (h, 0, ctx_blk + b)), kctx, vctx],
        out_specs=pl.BlockSpec((dv, g.c), lambda b, h: (h, b)),
        out_shape=jax.ShapeDtypeStruct((heads * dv, g.b * g.c), BF16),
        compiler_params=_cparams(("parallel", "parallel")),
        name=name + "_ctx",
    )(qt, k, vt)
    return jnp.concatenate([lat, ctx], axis=1)


def _split_bf16(x):
    hi = x.astype(BF16)
    return hi, (x - hi.astype(F32)).astype(BF16)


def _post_tail(o, h_ref, mod_ref, gpost_ref, gffn_ref, rw_ref, hn_ref, v_ref, lg_ref):
    hn = h_ref[...] + mod_ref[2:3, :] * (_rms(o, -1) * gpost_ref[...])
    hn_ref[...] = hn
    v = (_rms(hn, -1) * gffn_ref[...]) * (1.0 + mod_ref[4:5, :]) + mod_ref[3:4, :]
    v_hi, v_lo = _split_bf16(v)
    v_ref[...] = v_hi
    w_hi, w_lo = _split_bf16(rw_ref[...])
    lg_ref[...] = _dot_nt(w_hi, v_hi) + (_dot_nt(w_hi, v_lo) + _dot_nt(w_lo, v_hi))


def _attn_post_kernel(oa_ref, ob_ref, wa_ref, wb_ref, h_ref, mod_ref, gpost_ref, gffn_ref, rw_ref,
                      hn_ref, v_ref, lg_ref):
    o = _dot_tn(oa_ref[...], wa_ref[...]) + _dot_tn(ob_ref[...], wb_ref[...])
    _post_tail(o, h_ref, mod_ref, gpost_ref, gffn_ref, rw_ref, hn_ref, v_ref, lg_ref)


def _post_specs(g, d):
    rb = ROW_BLOCK
    ins = [
        pl.BlockSpec((rb, d), lambda i: (i, 0)),
        pl.BlockSpec((None, 6, d), lambda i: (g.mod_row(i), 0, 0)),
        _full((1, d)), _full((1, d)), _full((N_EXPERTS, d)),
    ]
    outs = [pl.BlockSpec((rb, d), lambda i: (i, 0)), pl.BlockSpec((rb, d), lambda i: (i, 0)),
            pl.BlockSpec((N_EXPERTS, rb), lambda i: (0, i))]
    shapes = [jax.ShapeDtypeStruct((g.t, d), F32), jax.ShapeDtypeStruct((g.t, d), BF16),
              jax.ShapeDtypeStruct((N_EXPERTS, g.t), F32)]
    return ins, outs, shapes


def _attn_post_call(g, oa, ob, w_out, h, mod, gpost, gffn, router_w):
    d = h.shape[1]
    rb = ROW_BLOCK
    half = oa.shape[0]
    ins, outs, shapes = _post_specs(g, d)
    wo = w_out.astype(BF16)
    return pl.pallas_call(
        _attn_post_kernel, grid=(g.n_blocks,),
        in_specs=[pl.BlockSpec((half, rb), lambda i: (0, i)), pl.BlockSpec((half, rb), lambda i: (0, i)),
                  _full((half, d)), _full((half, d))] + ins,
        out_specs=outs, out_shape=shapes,
        compiler_params=_cparams(("parallel",)), name="attn_post",
    )(oa, ob, wo[:half], wo[half:], h, mod, gpost.reshape(1, d), gffn.reshape(1, d), router_w.T)


def _router_kernel(lg_ref, bias_ref, tri_ref, gate_ref, rank_ref, cnt_ref):
    tb = lg_ref.shape[1]
    per = N_EXPERTS // N_GROUPS
    shp = (N_GROUPS, per, tb)
    scores = jax.nn.sigmoid(lg_ref[...])
    s3 = scores.reshape(shp)
    sel = (scores + bias_ref[...]).reshape(shp)
    sub = lax.broadcasted_iota(jnp.int32, shp, 1)
    grp = lax.broadcasted_iota(jnp.int32, shp, 0)
    neg = -jnp.inf
    m1 = jnp.max(sel, axis=1, keepdims=True)
    i1 = jnp.min(jnp.where(sel == m1, sub, per), axis=1, keepdims=True)
    m2 = jnp.max(jnp.where(sub == i1, neg, sel), axis=1, keepdims=True)
    cur = jnp.broadcast_to(m1 + m2, shp)
    gmask = jnp.zeros(shp, F32)
    for _ in range(TOPK_GROUPS):
        gm = jnp.max(cur, axis=0, keepdims=True)
        gi = jnp.min(jnp.where(cur == gm, grp, N_GROUPS), axis=0, keepdims=True)
        pick = grp == gi
        gmask = jnp.where(pick, 1.0, gmask)
        cur = jnp.where(pick, neg, cur)
    masked = jnp.where(gmask > 0.0, sel, neg)
    eidx = grp * per + sub
    chosen = jnp.zeros(shp, F32)
    for _ in range(TOP_K):
        mx = jnp.max(jnp.max(masked, axis=1, keepdims=True), axis=0, keepdims=True)
        cand = jnp.where(masked == mx, eidx, N_EXPERTS)
        ei = jnp.min(jnp.min(cand, axis=1, keepdims=True), axis=0, keepdims=True)
        pick = eidx == ei
        chosen = jnp.where(pick, 1.0, chosen)
        masked = jnp.where(pick, neg, masked)
    top_w = jnp.where(chosen > 0.0, s3, 0.0)
    denom = jnp.sum(jnp.sum(top_w, axis=1, keepdims=True), axis=0, keepdims=True)
    gate_ref[...] = (top_w / denom * ROUTED_SCALE).reshape(N_EXPERTS, tb)
    ch2 = chosen.reshape(N_EXPERTS, tb)
    chb = ch2.astype(BF16)
    before = _dot(chb, tri_ref[...])
    rank_ref[...] = jnp.where(ch2 > 0.0, before, -1.0).astype(jnp.int32)
    cnt_ref[...] = _dot(chb, jnp.ones((tb, LANES), BF16))


def _router_call(g, logits_t, router_b):
    tb = MOE_TOKENS
    nsb = g.t // tb
    tri = (np.arange(tb)[:, None] < np.arange(tb)[None, :]).astype(np.float32)
    blk = pl.BlockSpec((N_EXPERTS, tb), lambda s: (0, s))
    gates, rank, cnt = pl.pallas_call(
        _router_kernel, grid=(nsb,),
        in_specs=[blk, _full((N_EXPERTS, 1)), _full((tb, tb))],
        out_specs=[blk, blk, pl.BlockSpec((None, N_EXPERTS, LANES), lambda s: (s, 0, 0))],
        out_shape=[jax.ShapeDtypeStruct((N_EXPERTS, g.t), F32), jax.ShapeDtypeStruct((N_EXPERTS, g.t), jnp.int32),
                   jax.ShapeDtypeStruct((nsb, N_EXPERTS, LANES), F32)],
        compiler_params=_cparams(("parallel",)), name="router",
    )(logits_t, router_b.reshape(-1, 1), jnp.asarray(tri, BF16))
    return gates, rank, cnt[:, :, 0].astype(jnp.int32).reshape(-1)


def _moe_kernel(cnt_ref, v_ref, gate_ref, rank_ref, wg_ref, wu_ref, wd_ref, sg_ref, su_ref, sd_ref, o_ref,
                psel_ref, ysel_ref, slot_ref):
    s, j = pl.program_id(0), pl.program_id(1)
    tb = v_ref.shape[0]
    rows = MOE_ROWS

    @pl.when(j == 0)
    def _():
        vb = v_ref[...]
        hid = (_silu(_dot(vb, sg_ref[...])) * _dot(vb, su_ref[...])).astype(BF16)
        o_ref[...] = _dot(hid, sd_ref[...])
        psel_ref[...] = jnp.zeros_like(psel_ref)
        ysel_ref[...] = jnp.zeros_like(ysel_ref)
        slot_ref[0] = 0

    row_id = lax.broadcasted_iota(jnp.int32, (rows, tb), 0)

    def flush(n_slots):
        stage_row = lax.broadcasted_iota(jnp.int32, psel_ref.shape, 0)
        sel = jnp.where(stage_row < n_slots * rows, psel_ref[...], jnp.zeros_like(psel_ref))
        o_ref[...] += _dot_tn(sel, ysel_ref[...])

    ends, total = [], 0
    for e in range(MOE_PAIR):
        total = total + (cnt_ref[s * N_EXPERTS + j * MOE_PAIR + e] + rows - 1) // rows
        ends.append(total)

    def run_item(i, slot):
        e = sum((i >= end).astype(jnp.int32) for end in ends[:-1])
        first = sum(jnp.where(e == k + 1, ends[k], 0) for k in range(MOE_PAIR - 1))
        ex = j * MOE_PAIR + e
        rk = jnp.where(i < total, rank_ref[pl.ds(ex, 1), :] - (i - first) * rows, -1)
        hit = row_id == rk
        onehot = jnp.where(hit, 1.0, 0.0).astype(BF16)
        w_row = jnp.sum(jnp.where(hit, gate_ref[pl.ds(ex, 1), :], 0.0), axis=1, keepdims=True)
        xs = _dot(onehot, v_ref[...]).astype(BF16)
        hid = (_silu(_dot(xs, wg_ref[e])) * _dot(xs, wu_ref[e])).astype(BF16)
        y = _dot(hid, wd_ref[e])
        at = pl.multiple_of(slot * rows, rows)
        psel_ref[pl.ds(at, rows), :] = onehot
        ysel_ref[pl.ds(at, rows), :] = (y * w_row).astype(BF16)

    def body(it, slot):
        run_item(2 * it, slot)
        run_item(2 * it + 1, slot + 1)

        @pl.when(slot == MOE_SLOTS - 2)
        def _():
            flush(MOE_SLOTS)

        return jnp.where(slot == MOE_SLOTS - 2, 0, slot + 2)

    slot = lax.fori_loop(0, (total + 1) // 2, body, slot_ref[0])
    slot_ref[0] = slot

    @pl.when(jnp.logical_and(j == pl.num_programs(1) - 1, slot > 0))
    def _():
        flush(slot)


def _moe_call(g, v, gates, rank, counts, wg, wu, wd, sg, su, sd):
    t, d = v.shape
    tb = MOE_TOKENS
    ff = wg.shape[2]
    grid_spec = pltpu.PrefetchScalarGridSpec(
        num_scalar_prefetch=1,
        grid=(t // tb, N_EXPERTS // MOE_PAIR),
        in_specs=[
            pl.BlockSpec((tb, d), lambda s, j, c: (s, 0)),
            pl.BlockSpec((N_EXPERTS, tb), lambda s, j, c: (0, s)),
            pl.BlockSpec((N_EXPERTS, tb), lambda s, j, c: (0, s)),
            pl.BlockSpec((MOE_PAIR, d, ff), lambda s, j, c: (j, 0, 0)),
            pl.BlockSpec((MOE_PAIR, d, ff), lambda s, j, c: (j, 0, 0)),
            pl.BlockSpec((MOE_PAIR, ff, d), lambda s, j, c: (j, 0, 0)),
            pl.BlockSpec((d, sg.shape[1]), lambda s, j, c: (0, 0)),
            pl.BlockSpec((d, su.shape[1]), lambda s, j, c: (0, 0)),
            pl.BlockSpec((sd.shape[0], d), lambda s, j, c: (0, 0)),
        ],
        out_specs=pl.BlockSpec((tb, d), lambda s, j, c: (s, 0)),
        scratch_shapes=[pltpu.VMEM((MOE_SLOTS * MOE_ROWS, tb), BF16), pltpu.VMEM((MOE_SLOTS * MOE_ROWS, d), BF16),
                        pltpu.SMEM((1,), jnp.int32)],
    )
    return pl.pallas_call(
        _moe_kernel, grid_spec=grid_spec, out_shape=jax.ShapeDtypeStruct((t, d), F32),
        compiler_params=_cparams(("parallel", "arbitrary")), name="moe",
    )(counts, v, gates, rank, wg.astype(BF16), wu.astype(BF16), wd.astype(BF16),
      sg.astype(BF16), su.astype(BF16), sd.astype(BF16))


def _rec_pre_kernel(h_ref, mod_ref, gpre_ref, wtm_ref, wfm_ref, cqh_ref, sqh_ref, ckh_ref, skh_ref,
                    z_ref, xbc_ref, dt_ref, kd_ref, qd_ref, vd_ref):
    h = h_ref[...]
    u = _rms(h, -1) * gpre_ref[...]
    u = u * (1.0 + mod_ref[1:2, :]) + mod_ref[0:1, :]
    ub = u.astype(BF16)
    zk = _dot(ub, wtm_ref[...])
    zt = _dot_nt(wfm_ref[...], ub)
    z_ref[...] = zk[:, 0:512].astype(z_ref.dtype)
    xbc_ref[...] = zk[:, 512:1536]
    dt_ref[...] = zk[:, 1536:1664]
    ck, sk = ckh_ref[...], skh_ref[...]
    for j in range(DIFF_HEADS):
        lo = 1664 + 128 * j
        kd_ref[j] = (zk[:, lo:lo + 128] * ck + zk[:, lo + 512:lo + 640] * sk).astype(BF16)
    cq, sq = cqh_ref[...], sqh_ref[...]
    qs = DIFF_SCALE * LOG2E
    zero64 = jnp.zeros((64, h.shape[0]), BF16)
    for hd in range(2 * DIFF_HEADS):
        q = ((zt[64 * hd:64 * hd + 64] * cq + zt[512 + 64 * hd:576 + 64 * hd] * sq) * qs).astype(BF16)
        if hd % 2 == 0:
            qd_ref[hd, 0:64, :] = q
            qd_ref[hd, 64:128, :] = zero64
        else:
            qd_ref[hd, 0:64, :] = zero64
            qd_ref[hd, 64:128, :] = q
    for j in range(DIFF_HEADS):
        vd_ref[j] = zt[1024 + 128 * j:1152 + 128 * j].astype(BF16)


def _rec_pre_weights(g, rec_w_in):
    d = rec_w_in.shape[0]
    o = np.cumsum([0, SSD_INNER, SSD_XBC, SSD_HEADS, SSD_HEADS, 2 * DIFF_HEADS * DIFF_DIM, 2 * DIFF_HEADS * DIFF_DIM,
                   DIFF_HEADS * DIFF_V])
    w_z, w_xbc, w_dtf, w_dtb, w_dq, w_dk, w_dv = (rec_w_in[:, o[i]:o[i + 1]] for i in range(7))
    src64, sgn64 = _rot_map(DIFF_DIM)

    def rot_heads(wcols):
        wh = wcols.reshape(d, 2 * DIFF_HEADS, DIFF_DIM)
        return (wh[:, :, src64] * sgn64).reshape(d, -1)

    w_dt = jnp.pad(jnp.concatenate([w_dtf, w_dtb], axis=1), ((0, 0), (0, LANES - 2 * SSD_HEADS)))
    wtm = jnp.concatenate([w_z, w_xbc, w_dt, w_dk, rot_heads(w_dk)], axis=1)
    wfm = jnp.concatenate([w_dq, rot_heads(w_dq), w_dv], axis=1)
    cos_h, sin_h = _rope_tables(g.n, DIFF_DIM, ROW_BLOCK)
    two = lambda x: jnp.concatenate([x, x], axis=1)
    return dict(wtm=wtm.astype(BF16), wfm=wfm.T.astype(BF16), cqh=cos_h.T, sqh=sin_h.T, ckh=two(cos_h), skh=two(sin_h))


def _rec_pre_call(g, h, mod, gpre, w):
    t, d = h.shape
    rb = ROW_BLOCK
    row = lambda i: (i, 0)
    tm_tab = pl.BlockSpec((rb, LANES), lambda i: (g.pos_block(i), 0))
    fm64 = pl.BlockSpec((64, rb), lambda i: (0, g.pos_block(i)))
    nd = 2 * DIFF_HEADS
    return pl.pallas_call(
        _rec_pre_kernel, grid=(g.n_blocks,),
        in_specs=[pl.BlockSpec((rb, d), row), pl.BlockSpec((None, 6, d), lambda i: (g.mod_row(i), 0, 0)),
                  _full(gpre.shape), _full(w["wtm"].shape), _full(w["wfm"].shape), fm64, fm64, tm_tab, tm_tab],
        out_specs=[
            pl.BlockSpec((rb, SSD_INNER), row), pl.BlockSpec((rb, SSD_XBC), row), pl.BlockSpec((rb, LANES), row),
            pl.BlockSpec((DIFF_HEADS, rb, 128), lambda i: (0, i, 0)),
            pl.BlockSpec((nd, 128, rb), lambda i: (0, 0, i)),
            pl.BlockSpec((DIFF_HEADS, DIFF_V, rb), lambda i: (0, 0, i)),
        ],
        out_shape=[
            jax.ShapeDtypeStruct((t, SSD_INNER), BF16), jax.ShapeDtypeStruct((t, SSD_XBC), F32),
            jax.ShapeDtypeStruct((t, LANES), F32), jax.ShapeDtypeStruct((DIFF_HEADS, t, 128), BF16),
            jax.ShapeDtypeStruct((nd, 128, t), BF16), jax.ShapeDtypeStruct((DIFF_HEADS, DIFF_V, t), BF16),
        ],
        compiler_params=_cparams(("parallel",)), name="rec_pre",
    )(h, mod, gpre, w["wtm"], w["wfm"], w["cqh"], w["sqh"], w["ckh"], w["skh"])


def _conv_kernel(x_ref, prev_ref, next_ref, w_ref, b_ref, o_ref, *, seq_blocks, lat_blocks):
    i = pl.program_id(0)
    rb = x_ref.shape[0]
    pos = i % seq_blocks
    is_ctx = i >= lat_blocks
    first = jnp.logical_or(is_ctx, pos == 0)
    last = jnp.logical_or(is_ctx, pos == seq_blocks - 1)
    x = x_ref[...]
    prev_row = jnp.where(first, 0.0, prev_ref[7:8, :])
    next_row = jnp.where(last, 0.0, next_ref[0:1, :])
    rid = lax.broadcasted_iota(jnp.int32, x.shape, 0)
    x_prev = jnp.where(rid == 0, prev_row, pltpu.roll(x, 1, axis=0))
    x_next = jnp.where(rid == rb - 1, next_row, pltpu.roll(x, rb - 1, axis=0))
    y = w_ref[0:1, :] * x_prev + w_ref[1:2, :] * x + w_ref[2:3, :] * x_next + b_ref[...]
    o_ref[...] = _silu(y)


def _conv_call(g, xbc, conv_w, conv_b):
    t, ch = xbc.shape
    rb = g.c
    halo = 8
    per = rb // halo
    last_halo = t // halo - 1
    return pl.pallas_call(
        functools.partial(_conv_kernel, seq_blocks=g.n // rb, lat_blocks=g.b * g.n // rb),
        grid=(t // rb,),
        in_specs=[
            pl.BlockSpec((rb, ch), lambda i: (i, 0)),
            pl.BlockSpec((halo, ch), lambda i: (jnp.maximum(i * per - 1, 0), 0)),
            pl.BlockSpec((halo, ch), lambda i: (jnp.minimum((i + 1) * per, last_halo), 0)),
            _full((SSD_CONV, ch)), _full((1, ch)),
        ],
        out_specs=pl.BlockSpec((rb, ch), lambda i: (i, 0)),
        out_shape=jax.ShapeDtypeStruct((t, ch), F32),
        compiler_params=_cparams(("parallel",)), name="ssd_conv",
    )(xbc, xbc, xbc, conv_w, conv_b.reshape(1, ch))


def _ssd_kernel(x_ref, dt_ref, bias_ref, alog_ref, tri_ref, trit_ref, eye_ref, y_ref, state_ref, *, lane_off, reverse):
    k = pl.program_id(1)
    ln = SSD_CHUNK
    hi = lax.Precision.HIGHEST

    @pl.when(k == 0)
    def _():
        state_ref[...] = jnp.zeros_like(state_ref)

    x = x_ref[:, 0:SSD_INNER]
    raw = dt_ref[...] + bias_ref[...]
    dt = jnp.maximum(raw, 0.0) + jnp.log1p(jnp.exp(-jnp.abs(raw)))
    adt = dt * (-jnp.exp(alog_ref[...]))
    acs = jnp.dot(tri_ref[...], adt, preferred_element_type=F32, precision=hi)
    acs_row = lax.dot_general(adt, trit_ref[...], (((0,), (0,)), ((), ())), preferred_element_type=F32, precision=hi)
    dt_row = lax.dot_general(dt, eye_ref[...], (((0,), (0,)), ((), ())), preferred_element_type=F32, precision=hi)
    tot = acs[0:1, :] if reverse else acs[ln - 1:ln, :]
    w_all = jnp.exp(tot - acs) * dt
    ea_all = jnp.exp(acs)
    etot = jnp.exp(tot)
    li = lax.broadcasted_iota(jnp.int32, (ln, ln), 0)
    si = lax.broadcasted_iota(jnp.int32, (ln, ln), 1)
    keep = (si >= li) if reverse else (si <= li)
    lane = lax.broadcasted_iota(jnp.int32, (ln, LANES), 1)
    left = lane < SSD_HEAD_DIM
    per_group = SSD_HEADS // SSD_GROUPS
    for gi in range(SSD_GROUPS):
        bm = x_ref[:, SSD_INNER + SSD_STATE * gi:SSD_INNER + SSD_STATE * (gi + 1)].astype(BF16)
        cm = x_ref[:, SSD_INNER + SSD_STATE * (SSD_GROUPS + gi):SSD_INNER + SSD_STATE * (SSD_GROUPS + gi + 1)]
        cm = cm.astype(BF16)
        cb = _dot_nt(cm, bm)
        for pr in range(per_group // 2):
            h0 = gi * per_group + 2 * pr
            xp = x[:, SSD_HEAD_DIM * h0:SSD_HEAD_DIM * (h0 + 2)]
            mats = []
            for hd in (h0, h0 + 1):
                c = lane_off + hd
                seg = acs[:, c:c + 1] - acs_row[c:c + 1, :]
                lmat = jnp.exp(jnp.where(keep, seg, -jnp.inf))
                mats.append((cb * lmat * dt_row[c:c + 1, :]).astype(BF16))
            xb = xp.astype(BF16)
            zero = jnp.zeros_like(xb)
            rhs = jnp.concatenate([jnp.where(left, xb, zero), jnp.where(left, zero, xb)], axis=0)
            y_diag = _dot(jnp.concatenate(mats, axis=1), rhs)
            st = state_ref[h0 // 2]
            c0 = lane_off + h0
            ea = jnp.where(left, ea_all[:, c0:c0 + 1], ea_all[:, c0 + 1:c0 + 2])
            y_off = _dot(cm, st.astype(BF16)) * ea
            y_ref[:, SSD_HEAD_DIM * h0:SSD_HEAD_DIM * (h0 + 2)] = y_diag + y_off
            wcol = jnp.where(left, w_all[:, c0:c0 + 1], w_all[:, c0 + 1:c0 + 2])
            cs = _dot_tn(bm, (xp * wcol).astype(BF16))
            dec = jnp.where(left[0:1, :], etot[:, c0:c0 + 1], etot[:, c0 + 1:c0 + 2])
            state_ref[h0 // 2] = st * dec + cs


def _ssd_call(g, xbc_act, dt, dt_bias, a_log, *, reverse):
    t = xbc_act.shape[0]
    ln = SSD_CHUNK
    cc = g.c // ln
    nl = g.n // ln
    ctx0 = g.b * g.n // ln

    def chunk(b, k):
        if reverse:
            return jnp.where(k < cc, ctx0 + b * cc + (cc - 1 - k), b * nl + (nl - 1 - (k - cc)))
        return jnp.where(k < cc, ctx0 + b * cc + k, b * nl + (k - cc))

    idx = np.arange(ln)
    lower = (idx[:, None] >= idx[None, :]).astype(np.float32)
    tri = lower.T if reverse else lower
    row = lambda b, k: (chunk(b, k), 0)
    off = SSD_HEADS if reverse else 0
    lanes = lambda p: jnp.pad(p.reshape(1, -1), ((0, 0), (off, LANES - off - SSD_HEADS)))
    return pl.pallas_call(
        functools.partial(_ssd_kernel, lane_off=SSD_HEADS if reverse else 0, reverse=reverse),
        grid=(g.b, cc + nl),
        in_specs=[pl.BlockSpec((ln, SSD_XBC), row), pl.BlockSpec((ln, LANES), row),
                  _full((1, LANES)), _full((1, LANES)), _full((ln, ln)), _full((ln, ln)), _full((ln, ln))],
        out_specs=pl.BlockSpec((ln, SSD_INNER), row),
        out_shape=jax.ShapeDtypeStruct((t, SSD_INNER), F32),
        scratch_shapes=[pltpu.VMEM((SSD_HEADS // 2, SSD_STATE, 2 * SSD_HEAD_DIM), F32)],
        compiler_params=_cparams(("parallel", "arbitrary")),
        name="ssd_bwd" if reverse else "ssd_fwd",
    )(xbc_act, dt, lanes(dt_bias), lanes(a_log), jnp.asarray(tri), jnp.asarray(tri.T), jnp.eye(ln, dtype=F32))


def _diff_finish(o, tq, lq1_ref, lk1_ref, lq2_ref, lk2_ref, sub_ref, o_ref, lambda_init):
    lam = (jnp.exp(jnp.sum(lq1_ref[...] * lk1_ref[...], axis=1, keepdims=True))
           - jnp.exp(jnp.sum(lq2_ref[...] * lk2_ref[...], axis=1, keepdims=True)) + lambda_init)
    od = o[:, 0:tq] - lam * o[:, tq:2 * tq]
    o_ref[...] = ((_rms(od, 0) * sub_ref[...]) * (1.0 - lambda_init)).astype(o_ref.dtype)


def _diff_attn_kernel(qt_ref, *refs, lambda_init):
    o_ref, s_ref = refs[-2:]
    params = refs[-7:-2]
    kv = refs[:-7]
    half = len(kv) // 2
    segs = [(kv[i], kv[half + i]) for i in range(half)]
    qt = jnp.concatenate([qt_ref[0], qt_ref[1]], axis=1)
    o = _attend(qt, segs, s_ref, DIFF_V)
    _diff_finish(o, qt_ref.shape[2], *params, o_ref, lambda_init)


def _diff_attn_call(g, qd, kd, vd, lq1, lk1, lq2, lk2, subln, lambda_init):
    tq = DIFF_Q_TILE
    nq = g.n // tq
    ctx_blk = g.b * g.n // g.c
    vec = lambda a: a.reshape(1, -1)
    params = [_full((1, DIFF_DIM)), _full((1, DIFF_DIM)), _full((1, DIFF_DIM)), _full((1, DIFF_DIM)),
              _full((DIFF_V, 1))]
    pvals = (vec(lq1), vec(lk1), vec(lq2), vec(lk2), subln.reshape(-1, 1))
    klat = pl.BlockSpec((None, g.n, 128), lambda b, j, *_: (j, b, 0))
    kctx = pl.BlockSpec((None, g.c, 128), lambda b, j, *_: (j, ctx_blk + b, 0))
    vlat = pl.BlockSpec((None, DIFF_V, g.n), lambda b, j, *_: (j, 0, b))
    vctx = pl.BlockSpec((None, DIFF_V, g.c), lambda b, j, *_: (j, 0, ctx_blk + b))
    kern = functools.partial(_diff_attn_kernel, lambda_init=lambda_init)
    lat = pl.pallas_call(
        kern, grid=(g.b, DIFF_HEADS, nq),
        scratch_shapes=[pltpu.VMEM((g.n + g.c, 2 * tq), F32)],
        in_specs=[pl.BlockSpec((2, 128, tq), lambda b, j, qi: (j, 0, b * nq + qi)), klat, kctx, vlat, vctx] + params,
        out_specs=pl.BlockSpec((DIFF_V, tq), lambda b, j, qi: (j, b * nq + qi)),
        out_shape=jax.ShapeDtypeStruct((DIFF_HEADS * DIFF_V, g.b * g.n), BF16),
        compiler_params=_cparams(("parallel", "parallel", "arbitrary")),
        name="attn_diff",
    )(qd, kd, kd, vd, vd, *pvals)
    ctx = pl.pallas_call(
        kern, grid=(g.b, DIFF_HEADS),
        scratch_shapes=[pltpu.VMEM((g.c, 2 * g.c), F32)],
        in_specs=[pl.BlockSpec((2, 128, g.c), lambda b, j: (j, 0, ctx_blk + b)), kctx, vctx] + params,
        out_specs=pl.BlockSpec((DIFF_V, g.c), lambda b, j: (j, b)),
        out_shape=jax.ShapeDtypeStruct((DIFF_HEADS * DIFF_V, g.b * g.c), BF16),
        compiler_params=_cparams(("parallel", "parallel")),
        name="attn_diff_ctx",
    )(qd, kd, vd, *pvals)
    return jnp.concatenate([lat, ctx], axis=1)


def _rec_post_kernel(yf_ref, yb_ref, xs_ref, z_ref, od_ref, dskip_ref, gn_ref, wa_ref, wb_ref,
                     h_ref, mod_ref, gpost_ref, gffn_ref, rw_ref, hn_ref, v_ref, lg_ref):
    y = (yf_ref[...] + yb_ref[...] + xs_ref[...] * dskip_ref[...]) * _silu(z_ref[...].astype(F32))
    half = SSD_INNER // SSD_GROUPS
    o = _dot_tn(od_ref[...], wb_ref[...])
    for gi in range(SSD_GROUPS):
        lo = half * gi
        yn = (_rms(y[:, lo:lo + half], -1) * gn_ref[:, lo:lo + half]).astype(BF16)
        o = o + _dot(yn, wa_ref[lo:lo + half, :])
    _post_tail(o, h_ref, mod_ref, gpost_ref, gffn_ref, rw_ref, hn_ref, v_ref, lg_ref)


def _rec_post_call(g, yf, yb, xbc_act, zs, od, d_skip, ssd_norm, w_out, h, mod, gpost, gffn, router_w):
    d = h.shape[1]
    rb = ROW_BLOCK
    ins, outs, shapes = _post_specs(g, d)
    wo = w_out.astype(BF16)
    row512 = pl.BlockSpec((rb, SSD_INNER), lambda i: (i, 0))
    return pl.pallas_call(
        _rec_post_kernel, grid=(g.n_blocks,),
        in_specs=[row512, row512, row512, row512, pl.BlockSpec((SSD_INNER, rb), lambda i: (0, i)),
                  _full((1, SSD_INNER)), _full((1, SSD_INNER)), _full((SSD_INNER, d)), _full((SSD_INNER, d))] + ins,
        out_specs=outs, out_shape=shapes,
        compiler_params=_cparams(("parallel",)), name="rec_post",
    )(yf, yb, xbc_act, zs, od, jnp.repeat(d_skip, SSD_HEAD_DIM).reshape(1, -1), ssd_norm.reshape(1, -1),
      wo[:SSD_INNER], wo[SSD_INNER:], h, mod, gpost.reshape(1, d), gffn.reshape(1, d), router_w.T)


def _ffn_res_kernel(h_ref, f_ref, mod_ref, g_ref, o_ref):
    o_ref[...] = h_ref[...] + mod_ref[5:6, :] * (_rms(f_ref[...], -1) * g_ref[...])


def _ffn_res_call(g, h, f, mod, gain, n_rows):
    d = h.shape[1]
    rb = ROW_BLOCK
    row = pl.BlockSpec((rb, d), lambda i: (i, 0))
    return pl.pallas_call(
        _ffn_res_kernel, grid=(n_rows // rb,),
        in_specs=[row, row, pl.BlockSpec((None, 6, d), lambda i: (g.mod_row(i), 0, 0)), _full((1, d))],
        out_specs=row, out_shape=jax.ShapeDtypeStruct((n_rows, d), F32),
        compiler_params=_cparams(("parallel",)), name="ffn_res",
    )(h, f, mod, gain.reshape(1, d))


def kernel(x, c, ctx, c_ctx, ada_w, ada_b, norm_mix_pre, norm_mix_post, norm_ffn_pre, norm_ffn_post, mix_w_out, att_w_in, mla_q_norm, mla_w_q_up, mla_kv_norm, mla_w_kv_up, gqa_q_norm, gqa_k_norm, rec_w_in, ssd_conv_w, ssd_conv_b, ssd_dt_bias_f, ssd_dt_bias_b, ssd_a_log_f, ssd_a_log_b, ssd_d, ssd_norm, diff_lambda_q1, diff_lambda_k1, diff_lambda_q2, diff_lambda_k2, diff_subln, router_w, router_b, exp_w_gate, exp_w_up, exp_w_down, sh_w_gate, sh_w_up, sh_w_down):
    b, n, d = x.shape
    n_ctx = ctx.shape[1]
    depth = ada_w.shape[0]
    g = _Geom(b, n, n_ctx)
    mod_rows = -(-(b + 1) // 8) * 8
    c_all = jnp.concatenate([c, c_ctx[None, :], jnp.zeros((mod_rows - b - 1, d), F32)], axis=0)
    mods = _ada_call(c_all, ada_w, ada_b).reshape(depth, mod_rows, 6, d)
    h = jnp.concatenate([x.reshape(b * n, d), ctx.reshape(b * n_ctx, d)], axis=0)
    for i in range(depth):
        last = i == depth - 1
        jdx = i // 2
        mod = mods[i]
        if i % 2 == 0:
            w = _attn_pre_weights(g, att_w_in[jdx], mla_q_norm[jdx], mla_w_q_up[jdx], mla_kv_norm[jdx],
                                  mla_w_kv_up[jdx], gqa_q_norm[jdx], gqa_k_norm[jdx])
            qm, km, vm, qg, kg, vg = _attn_pre_call(g, h, mod, norm_mix_pre[i].reshape(1, d), w)
            oa = _attn_call(g, qm, km, vm, kv_heads=MLA_HEADS, shared_k=False, name="attn_mla")
            ob = _attn_call(g, qg, kg, vg, kv_heads=GQA_KV_HEADS, shared_k=True, name="attn_gqa")
            h, v, logits_t = _attn_post_call(g, oa, ob, mix_w_out[i], h, mod, norm_mix_post[i], norm_ffn_pre[i],
                                             router_w[i])
        else:
            lambda_init = 0.8 - 0.6 * math.exp(-0.3 * i)
            w = _rec_pre_weights(g, rec_w_in[jdx])
            zs, xbc, dt, kd, qd, vd = _rec_pre_call(g, h, mod, norm_mix_pre[i].reshape(1, d), w)
            xbc_act = _conv_call(g, xbc, ssd_conv_w[jdx], ssd_conv_b[jdx])
            yf = _ssd_call(g, xbc_act, dt, ssd_dt_bias_f[jdx], ssd_a_log_f[jdx], reverse=False)
            yb = _ssd_call(g, xbc_act, dt, ssd_dt_bias_b[jdx], ssd_a_log_b[jdx], reverse=True)
            od = _diff_attn_call(g, qd, kd, vd, diff_lambda_q1[jdx], diff_lambda_k1[jdx], diff_lambda_q2[jdx],
                                 diff_lambda_k2[jdx], diff_subln[jdx], lambda_init)
            h, v, logits_t = _rec_post_call(g, yf, yb, xbc_act, zs, od, ssd_d[jdx], ssd_norm[jdx], mix_w_out[i], h,
                                            mod, norm_mix_post[i], norm_ffn_pre[i], router_w[i])
        gates, rank, counts = _router_call(g, logits_t, router_b[i])
        f = _moe_call(g, v, gates, rank, counts, exp_w_gate[i], exp_w_up[i], exp_w_down[i],
                      sh_w_gate[i], sh_w_up[i], sh_w_down[i])
        h = _ffn_res_call(g, h, f, mod, norm_ffn_post[i], b * n if last else g.t)
    return h.reshape(b, n, d)
```

```python
import functools
import math

import numpy as np
import jax
import jax.numpy as jnp
from jax import lax
from jax.experimental import pallas as pl
from jax.experimental.pallas import tpu as pltpu

F32 = jnp.float32
BF16 = jnp.bfloat16
LOG2E = 1.4426950408889634

GRID_W = 64
ROPE_BASE = 10000.0
NORM_EPS = 1e-6

MLA_HEADS, MLA_Q_LORA, MLA_KV_LORA, MLA_NOPE, MLA_ROPE, MLA_V = 8, 256, 128, 64, 32, 64
MLA_SCALE = (MLA_NOPE + MLA_ROPE) ** -0.5
GQA_HEADS, GQA_KV_HEADS, GQA_DIM = 8, 2, 64
GQA_SCALE = GQA_DIM ** -0.5
SSD_HEADS, SSD_HEAD_DIM, SSD_GROUPS, SSD_STATE, SSD_CONV, SSD_CHUNK = 8, 64, 2, 128, 3, 128
SSD_INNER = SSD_HEADS * SSD_HEAD_DIM
SSD_XBC = SSD_INNER + 2 * SSD_GROUPS * SSD_STATE
DIFF_HEADS, DIFF_DIM = 4, 64
DIFF_V = 2 * DIFF_DIM
DIFF_SCALE = DIFF_DIM ** -0.5
N_EXPERTS, TOP_K, N_GROUPS, TOPK_GROUPS, EXPERT_FF, SHARED_FF = 64, 8, 8, 4, 256, 256
ROUTED_SCALE = 2.5

LANES = 128
VMEM_LIMIT_BYTES = 56 * 1024 * 1024

ROW_BLOCK = 512
ATTN_Q_TILE = 1024
DIFF_Q_TILE = 512
ATTN_KV_CHUNK = 512
ATTN_EXP_ROWS = 4096
MOE_TOKENS = 768
MOE_ROWS = 128
MOE_PAIR = 8
MOE_SLOTS = 8


def _cparams(sem, flags=None):
    return pltpu.CompilerParams(dimension_semantics=sem, vmem_limit_bytes=VMEM_LIMIT_BYTES, flags=flags)


def _rms(x, axis):
    return x * lax.rsqrt(jnp.mean(x * x, axis=axis, keepdims=True) + NORM_EPS)


def _silu(x):
    return x * jax.nn.sigmoid(x)


def _dot(a, b):
    return jnp.dot(a, b, preferred_element_type=F32)


def _dot_nt(a, b):
    return lax.dot_general(a, b, (((1,), (1,)), ((), ())), preferred_element_type=F32)


def _dot_tn(a, b):
    return lax.dot_general(a, b, (((0,), (0,)), ((), ())), preferred_element_type=F32)


def _ada_kernel(c_ref, w_ref, b_ref, o_ref):
    s = _silu(c_ref[...])
    o_ref[...] = jnp.dot(s, w_ref[...], preferred_element_type=F32, precision=lax.Precision.HIGHEST) + b_ref[...]


def _ada_call(c_all, ada_w, ada_b):
    depth, d, six_d = ada_w.shape
    rows = c_all.shape[0]
    cols = six_d // 4
    return pl.pallas_call(
        _ada_kernel,
        grid=(depth, six_d // cols),
        in_specs=[
            pl.BlockSpec((rows, d), lambda i, j: (0, 0)),
            pl.BlockSpec((None, d, cols), lambda i, j: (i, 0, j)),
            pl.BlockSpec((None, 1, cols), lambda i, j: (i, 0, j)),
        ],
        out_specs=pl.BlockSpec((None, rows, cols), lambda i, j: (i, 0, j)),
        out_shape=jax.ShapeDtypeStruct((depth, rows, six_d), F32),
        compiler_params=_cparams(("arbitrary", "arbitrary")),
        name="ada_mod",
    )(c_all, ada_w, ada_b.reshape(depth, 1, six_d))


def _attn_pre_kernel(h_ref, mod_ref, gpre_ref, wfm_ref, wtm_ref, qn_ref, kvn_col_ref, kvn_row_ref, wq_ref, wk_ref,
                     wv_ref, gq_g_ref, gq_gr_ref, gk_g_ref, gk_gr_ref,
                     cqm_ref, sqm_ref, cqh_ref, sqh_ref, ckm_ref, skm_ref, ckh_ref, skh_ref,
                     qm_ref, km_ref, vm_ref, qg_ref, kg_ref, vg_ref):
    h = h_ref[...]
    u = _rms(h, -1) * gpre_ref[...]
    u = u * (1.0 + mod_ref[1:2, :]) + mod_ref[0:1, :]
    ub = u.astype(BF16)
    zt = _dot_nt(wfm_ref[...], ub)
    zk = _dot(ub, wtm_ref[...])

    qn = (_rms(zt[0:256], 0) * qn_ref[...]).astype(BF16)
    qt = _dot(wq_ref[...], qn)
    cq, sq = cqm_ref[...], sqm_ref[...]
    qs = MLA_SCALE * LOG2E
    zero32 = jnp.zeros((32, h.shape[0]), BF16)
    for hd in range(MLA_HEADS):
        pe = qt[512 + 32 * hd:544 + 32 * hd] * cq + qt[768 + 32 * hd:800 + 32 * hd] * sq
        qm_ref[hd, 0:64, :] = (qt[64 * hd:64 * hd + 64] * qs).astype(BF16)
        qm_ref[hd, 64:96, :] = (pe * qs).astype(BF16)
        qm_ref[hd, 96:128, :] = zero32

    kvn_t = (_rms(zt[256:384], 0) * kvn_col_ref[...]).astype(BF16)
    vt = _dot(wv_ref[...], kvn_t)
    for hd in range(MLA_HEADS):
        vm_ref[hd] = vt[64 * hd:64 * hd + 64].astype(BF16)
    kvn = (_rms(zk[:, 0:128], -1) * kvn_row_ref[...]).astype(BF16)
    kn = _dot(kvn, wk_ref[...])
    kpe = zk[:, 128:256] * ckm_ref[...] + zk[:, 256:384] * skm_ref[...]
    for hd in range(MLA_HEADS):
        km_ref[hd] = (kn[:, 128 * hd:128 * hd + 128] + kpe).astype(BF16)

    cqh, sqh = cqh_ref[...], sqh_ref[...]
    gs = GQA_SCALE * LOG2E
    zero64 = jnp.zeros((64, h.shape[0]), BF16)
    grp = GQA_HEADS // GQA_KV_HEADS
    for hd in range(GQA_HEADS):
        raw = zt[384 + 64 * hd:448 + 64 * hd]
        rot = zt[896 + 64 * hd:960 + 64 * hd]
        r = lax.rsqrt(jnp.mean(raw * raw, axis=0, keepdims=True) + NORM_EPS)
        q = (raw * r * gq_g_ref[...]) * cqh + (rot * r * gq_gr_ref[...]) * sqh
        q = (q * gs).astype(BF16)
        if hd // grp == 0:
            qg_ref[hd, 0:64, :] = q
            qg_ref[hd, 64:128, :] = zero64
        else:
            qg_ref[hd, 0:64, :] = zero64
            qg_ref[hd, 64:128, :] = q
    for kvh in range(GQA_KV_HEADS):
        vg_ref[kvh] = zt[1408 + 64 * kvh:1472 + 64 * kvh].astype(BF16)

    gk, gkr = zk[:, 384:512], zk[:, 512:640]
    lane = lax.broadcasted_iota(jnp.int32, gk.shape, 1)
    lo = lane < 64
    sq0 = jnp.sum(jnp.where(lo, gk * gk, 0.0), axis=-1, keepdims=True)
    sq1 = jnp.sum(jnp.where(lo, 0.0, gk * gk), axis=-1, keepdims=True)
    r = jnp.where(lo, lax.rsqrt(sq0 / GQA_DIM + NORM_EPS), lax.rsqrt(sq1 / GQA_DIM + NORM_EPS))
    kg_ref[...] = ((gk * r * gk_g_ref[...]) * ckh_ref[...] + (gkr * r * gk_gr_ref[...]) * skh_ref[...]).astype(BF16)


def _rot_map(dim):
    q = dim // 4
    j = np.arange(dim)
    even = (j // q) % 2 == 0
    return np.where(even, j + q, j - q), np.where(even, -1.0, 1.0).astype(np.float32)


def _rope_tables(n_tokens, dim, pad_rows):
    rows = n_tokens // GRID_W
    row = jnp.repeat(jnp.arange(rows, dtype=F32), GRID_W)
    col = jnp.tile(jnp.arange(GRID_W, dtype=F32), rows)
    half = dim // 2
    inv = ROPE_BASE ** (-(jnp.arange(half // 2, dtype=F32) * 2.0 / half))
    ang_r = row[:, None] * inv
    ang_c = col[:, None] * inv
    ang = jnp.concatenate([ang_r, ang_r, ang_c, ang_c], axis=-1)
    cos = jnp.concatenate([jnp.cos(ang), jnp.ones((pad_rows, dim), F32)], axis=0)
    sin = jnp.concatenate([jnp.sin(ang), jnp.zeros((pad_rows, dim), F32)], axis=0)
    return cos, sin


class _Geom:
    def __init__(self, b, n, c):
        assert n % ROW_BLOCK == 0 and (b * c) % ROW_BLOCK == 0 and n % c == 0 and n % GRID_W == 0
        self.b, self.n, self.c = b, n, c
        self.t = b * n + b * c
        self.lat_blocks = b * n // ROW_BLOCK
        self.blocks_per_seq = n // ROW_BLOCK
        self.n_blocks = self.t // ROW_BLOCK
        assert self.t % MOE_TOKENS == 0

    def mod_row(self, i):
        return jnp.where(i < self.lat_blocks, i // self.blocks_per_seq, self.b)

    def pos_block(self, i):
        return jnp.where(i < self.lat_blocks, i % self.blocks_per_seq, self.blocks_per_seq)


def _full(shape):
    nd = len(shape)
    return pl.BlockSpec(shape, lambda *_: (0,) * nd)


def _attn_pre_call(g, h, mod, gpre, w):
    t, d = h.shape
    rb = ROW_BLOCK
    row = lambda i: (i, 0)
    tm_tab = pl.BlockSpec((rb, LANES), lambda i: (g.pos_block(i), 0))
    fm32 = pl.BlockSpec((32, rb), lambda i: (0, g.pos_block(i)))
    fm64 = pl.BlockSpec((64, rb), lambda i: (0, g.pos_block(i)))
    in_specs = [
        pl.BlockSpec((rb, d), row),
        pl.BlockSpec((None, 6, d), lambda i: (g.mod_row(i), 0, 0)),
        _full(gpre.shape), _full(w["wfm"].shape), _full(w["wtm"].shape), _full(w["qn"].shape),
        _full(w["kvn_col"].shape), _full(w["kvn_row"].shape), _full(w["wq"].shape), _full(w["wk"].shape),
        _full(w["wv"].shape), _full(w["gq_g"].shape), _full(w["gq_gr"].shape), _full(w["gk_g"].shape),
        _full(w["gk_gr"].shape),
        fm32, fm32, fm64, fm64, tm_tab, tm_tab, tm_tab, tm_tab,
    ]
    out_shape = [
        jax.ShapeDtypeStruct((MLA_HEADS, 128, t), BF16),
        jax.ShapeDtypeStruct((MLA_HEADS, t, 128), BF16),
        jax.ShapeDtypeStruct((MLA_HEADS, MLA_V, t), BF16),
        jax.ShapeDtypeStruct((GQA_HEADS, 128, t), BF16),
        jax.ShapeDtypeStruct((t, 128), BF16),
        jax.ShapeDtypeStruct((GQA_KV_HEADS, GQA_DIM, t), BF16),
    ]
    out_specs = [
        pl.BlockSpec((MLA_HEADS, 128, rb), lambda i: (0, 0, i)),
        pl.BlockSpec((MLA_HEADS, rb, 128), lambda i: (0, i, 0)),
        pl.BlockSpec((MLA_HEADS, MLA_V, rb), lambda i: (0, 0, i)),
        pl.BlockSpec((GQA_HEADS, 128, rb), lambda i: (0, 0, i)),
        pl.BlockSpec((rb, 128), row),
        pl.BlockSpec((GQA_KV_HEADS, GQA_DIM, rb), lambda i: (0, 0, i)),
    ]
    return pl.pallas_call(
        _attn_pre_kernel, grid=(g.n_blocks,), in_specs=in_specs, out_specs=out_specs, out_shape=out_shape,
        compiler_params=_cparams(("parallel",)), name="attn_pre",
    )(h, mod, gpre, w["wfm"], w["wtm"], w["qn"], w["kvn_col"], w["kvn_row"], w["wq"], w["wk"], w["wv"],
      w["gq_g"], w["gq_gr"], w["gk_g"], w["gk_gr"],
      w["cqm"], w["sqm"], w["cqh"], w["sqh"], w["ckm"], w["skm"], w["ckh"], w["skh"])


def _attn_pre_weights(g, att_w_in, q_norm, w_q_up, kv_norm, w_kv_up, gqa_qn, gqa_kn):
    d = att_w_in.shape[0]
    o = np.cumsum([0, MLA_Q_LORA, MLA_KV_LORA, MLA_ROPE, GQA_HEADS * GQA_DIM, GQA_KV_HEADS * GQA_DIM,
                   GQA_KV_HEADS * GQA_DIM])
    w_qlat, w_kvlat, w_kpe, w_gq, w_gk, w_gv = (att_w_in[:, o[i]:o[i + 1]] for i in range(6))
    src32, sgn32 = _rot_map(MLA_ROPE)
    src64, sgn64 = _rot_map(GQA_DIM)

    def rot_heads(wcols, heads, dim, src, sgn):
        wh = wcols.reshape(d, heads, dim)
        return (wh[:, :, src] * sgn).reshape(d, heads * dim)

    wfm = jnp.concatenate([w_qlat, w_kvlat, w_gq, rot_heads(w_gq, GQA_HEADS, GQA_DIM, src64, sgn64), w_gv], axis=1)
    zpad = lambda x, lo, hi: jnp.pad(x, ((0, 0), (lo, hi)))
    wtm = jnp.concatenate([
        w_kvlat, zpad(w_kpe, 64, 32), zpad(w_kpe[:, src32] * sgn32, 64, 32),
        w_gk, rot_heads(w_gk, GQA_KV_HEADS, GQA_DIM, src64, sgn64)], axis=1)
    wq = w_q_up.reshape(MLA_Q_LORA, MLA_HEADS, MLA_NOPE + MLA_ROPE)
    wq_pe = wq[:, :, MLA_NOPE:]
    wq_all = jnp.concatenate([
        wq[:, :, :MLA_NOPE].reshape(MLA_Q_LORA, -1), wq_pe.reshape(MLA_Q_LORA, -1),
        (wq_pe[:, :, src32] * sgn32).reshape(MLA_Q_LORA, -1)], axis=1)
    wkv = w_kv_up.reshape(MLA_KV_LORA, MLA_HEADS, MLA_NOPE + MLA_V)
    wk = jnp.pad(wkv[:, :, :MLA_NOPE], ((0, 0), (0, 0), (0, 128 - MLA_NOPE))).reshape(MLA_KV_LORA, -1)
    wv = wkv[:, :, MLA_NOPE:].reshape(MLA_KV_LORA, -1)
    cos_m, sin_m = _rope_tables(g.n, MLA_ROPE, ROW_BLOCK)
    cos_h, sin_h = _rope_tables(g.n, GQA_DIM, ROW_BLOCK)
    two = lambda x: jnp.concatenate([x, x], axis=1)
    return dict(
        wfm=wfm.T.astype(BF16), wtm=wtm.astype(BF16),
        qn=q_norm.reshape(-1, 1), kvn_col=kv_norm.reshape(-1, 1), kvn_row=kv_norm.reshape(1, -1),
        wq=wq_all.T.astype(BF16), wk=wk.astype(BF16), wv=wv.T.astype(BF16),
        gq_g=gqa_qn.reshape(-1, 1), gq_gr=gqa_qn[src64].reshape(-1, 1),
        gk_g=two(gqa_kn.reshape(1, -1)), gk_gr=two(gqa_kn[src64].reshape(1, -1)),
        cqm=cos_m.T, sqm=sin_m.T, cqh=cos_h.T, sqh=sin_h.T,
        ckm=zpad(cos_m, 64, 32), skm=zpad(sin_m, 64, 32), ckh=two(cos_h), skh=two(sin_h),
    )


def _fold8(x, op):
    r, w = x.shape
    return op(x.reshape(r // 8, 8, w), axis=0)


def _attend(qt, segments, s_ref, dv):
    tq = qt.shape[1]
    pieces, off = [], 0
    for k_ref, vt_ref in segments:
        n = k_ref.shape[0]
        for lo in range(0, n, ATTN_KV_CHUNK):
            rows = min(ATTN_KV_CHUNK, n - lo)
            pieces.append((off, lo, rows, k_ref, vt_ref))
            off += rows
    m8 = None
    for so, lo, rows, k_ref, _ in pieces:
        s = _dot(k_ref[lo:lo + rows, :], qt)
        s_ref[so:so + rows, :] = s
        part = _fold8(s, jnp.max)
        m8 = part if m8 is None else jnp.maximum(m8, part)
    m = jnp.max(m8, axis=0, keepdims=True)
    acc = jnp.zeros((dv + 16, tq), F32)
    off = 0
    for _, vt_ref in segments:
        n = vt_ref.shape[1]
        step = min(n, ATTN_EXP_ROWS)

        def body(i, acc, off=off, vt_ref=vt_ref, step=step):
            base = i * step if isinstance(i, int) else pl.multiple_of(i * step, step)
            for lo in range(0, step, ATTN_KV_CHUNK):
                rows = min(ATTN_KV_CHUNK, step - lo)
                p = jnp.exp2((s_ref[pl.ds(off + base + lo, rows), :] - m).astype(BF16))
                vt1 = jnp.concatenate([vt_ref[:, pl.ds(base + lo, rows)], jnp.ones((16, rows), BF16)], axis=0)
                acc = acc + _dot(vt1, p)
            return acc

        acc = body(0, acc) if n == step else lax.fori_loop(0, n // step, body, acc)
        off += n
    return acc[0:dv] / acc[dv:dv + 1]


def _attn_kernel(qt_ref, *refs, dv):
    o_ref, s_ref = refs[-2:]
    kv = refs[:-2]
    half = len(kv) // 2
    segs = [(kv[i], kv[half + i]) for i in range(half)]
    o_ref[...] = _attend(qt_ref[...], segs, s_ref, dv).astype(o_ref.dtype)


def _attn_call(g, qt, k, vt, *, kv_heads, shared_k, name):
    heads = qt.shape[0]
    dv = vt.shape[1]
    grp = heads // kv_heads
    tq = ATTN_Q_TILE
    nq = g.n // tq
    ctx_blk = g.b * g.n // g.c
    if shared_k:
        klat = pl.BlockSpec((g.n, 128), lambda b, h, *_: (b, 0))
        kctx = pl.BlockSpec((g.c, 128), lambda b, h, *_: (ctx_blk + b, 0))
    else:
        klat = pl.BlockSpec((None, g.n, 128), lambda b, h, *_: (h // grp, b, 0))
        kctx = pl.BlockSpec((None, g.c, 128), lambda b, h, *_: (h // grp, ctx_blk + b, 0))
    vlat = pl.BlockSpec((None, dv, g.n), lambda b, h, *_: (h // grp, 0, b))
    vctx = pl.BlockSpec((None, dv, g.c), lambda b, h, *_: (h // grp, 0, ctx_blk + b))
    lat = pl.pallas_call(
        functools.partial(_attn_kernel, dv=dv),
        grid=(g.b, heads, nq),
        scratch_shapes=[pltpu.VMEM((g.n + g.c, tq), F32)],
        in_specs=[pl.BlockSpec((None, 128, tq), lambda b, h, qi: (h, 0, b * nq + qi)), klat, kctx, vlat, vctx],
        out_specs=pl.BlockSpec((dv, tq), lambda b, h, qi: (h, b * nq + qi)),
        out_shape=jax.ShapeDtypeStruct((heads * dv, g.b * g.n), BF16),
        compiler_params=_cparams(("parallel", "parallel", "arbitrary")),
        name=name,
    )(qt, k, k, vt, vt)
    ctx = pl.pallas_call(
        functools.partial(_attn_kernel, dv=dv),
        grid=(g.b, heads),
        scratch_shapes=[pltpu.VMEM((g.c, g.c), F32)],
        in_specs=[pl.BlockSpec((None, 128, g.c), lambda b, h: (h, 0, ctx_blk + b)), kctx, vctx],
        out_specs=pl.BlockSpec((dv, g.c), lambda b, h: (h, b)),
        out_shape=jax.ShapeDtypeStruct((heads * dv, g.b * g.c), BF16),
        compiler_params=_cparams(("parallel", "parallel")),
        name=name + "_ctx",
    )(qt, k, vt)
    return jnp.concatenate([lat, ctx], axis=1)


def _split_bf16(x):
    hi = x.astype(BF16)
    return hi, (x - hi.astype(F32)).astype(BF16)


def _post_tail(o, h_ref, mod_ref, gpost_ref, gffn_ref, rw_ref, hn_ref, v_ref, lg_ref):
    hn = h_ref[...] + mod_ref[2:3, :] * (_rms(o, -1) * gpost_ref[...])
    hn_ref[...] = hn
    v = (_rms(hn, -1) * gffn_ref[...]) * (1.0 + mod_ref[4:5, :]) + mod_ref[3:4, :]
    v_hi, v_lo = _split_bf16(v)
    v_ref[...] = v_hi
    w_hi, w_lo = _split_bf16(rw_ref[...])
    lg_ref[...] = _dot_nt(w_hi, v_hi) + (_dot_nt(w_hi, v_lo) + _dot_nt(w_lo, v_hi))


def _attn_post_kernel(oa_ref, ob_ref, wa_ref, wb_ref, h_ref, mod_ref, gpost_ref, gffn_ref, rw_ref,
                      hn_ref, v_ref, lg_ref):
    o = _dot_tn(oa_ref[...], wa_ref[...]) + _dot_tn(ob_ref[...], wb_ref[...])
    _post_tail(o, h_ref, mod_ref, gpost_ref, gffn_ref, rw_ref, hn_ref, v_ref, lg_ref)


def _post_specs(g, d):
    rb = ROW_BLOCK
    ins = [
        pl.BlockSpec((rb, d), lambda i: (i, 0)),
        pl.BlockSpec((None, 6, d), lambda i: (g.mod_row(i), 0, 0)),
        _full((1, d)), _full((1, d)), _full((N_EXPERTS, d)),
    ]
    outs = [pl.BlockSpec((rb, d), lambda i: (i, 0)), pl.BlockSpec((rb, d), lambda i: (i, 0)),
            pl.BlockSpec((N_EXPERTS, rb), lambda i: (0, i))]
    shapes = [jax.ShapeDtypeStruct((g.t, d), F32), jax.ShapeDtypeStruct((g.t, d), BF16),
              jax.ShapeDtypeStruct((N_EXPERTS, g.t), F32)]
    return ins, outs, shapes


def _attn_post_call(g, oa, ob, w_out, h, mod, gpost, gffn, router_w):
    d = h.shape[1]
    rb = ROW_BLOCK
    half = oa.shape[0]
    ins, outs, shapes = _post_specs(g, d)
    wo = w_out.astype(BF16)
    return pl.pallas_call(
        _attn_post_kernel, grid=(g.n_blocks,),
        in_specs=[pl.BlockSpec((half, rb), lambda i: (0, i)), pl.BlockSpec((half, rb), lambda i: (0, i)),
                  _full((half, d)), _full((half, d))] + ins,
        out_specs=outs, out_shape=shapes,
        compiler_params=_cparams(("parallel",)), name="attn_post",
    )(oa, ob, wo[:half], wo[half:], h, mod, gpost.reshape(1, d), gffn.reshape(1, d), router_w.T)


def _router_kernel(lg_ref, bias_ref, tri_ref, gate_ref, rank_ref, cnt_ref):
    tb = lg_ref.shape[1]
    per = N_EXPERTS // N_GROUPS
    shp = (N_GROUPS, per, tb)
    scores = jax.nn.sigmoid(lg_ref[...])
    s3 = scores.reshape(shp)
    sel = (scores + bias_ref[...]).reshape(shp)
    sub = lax.broadcasted_iota(jnp.int32, shp, 1)
    grp = lax.broadcasted_iota(jnp.int32, shp, 0)
    neg = -jnp.inf
    m1 = jnp.max(sel, axis=1, keepdims=True)
    i1 = jnp.min(jnp.where(sel == m1, sub, per), axis=1, keepdims=True)
    m2 = jnp.max(jnp.where(sub == i1, neg, sel), axis=1, keepdims=True)
    cur = jnp.broadcast_to(m1 + m2, shp)
    gmask = jnp.zeros(shp, F32)
    for _ in range(TOPK_GROUPS):
        gm = jnp.max(cur, axis=0, keepdims=True)
        gi = jnp.min(jnp.where(cur == gm, grp, N_GROUPS), axis=0, keepdims=True)
        pick = grp == gi
        gmask = jnp.where(pick, 1.0, gmask)
        cur = jnp.where(pick, neg, cur)
    masked = jnp.where(gmask > 0.0, sel, neg)
    eidx = grp * per + sub
    chosen = jnp.zeros(shp, F32)
    for _ in range(TOP_K):
        mx = jnp.max(jnp.max(masked, axis=1, keepdims=True), axis=0, keepdims=True)
        cand = jnp.where(masked == mx, eidx, N_EXPERTS)
        ei = jnp.min(jnp.min(cand, axis=1, keepdims=True), axis=0, keepdims=True)
        pick = eidx == ei
        chosen = jnp.where(pick, 1.0, chosen)
        masked = jnp.where(pick, neg, masked)
    top_w = jnp.where(chosen > 0.0, s3, 0.0)
    denom = jnp.sum(jnp.sum(top_w, axis=1, keepdims=True), axis=0, keepdims=True)
    gate_ref[...] = (top_w / denom * ROUTED_SCALE).reshape(N_EXPERTS, tb)
    ch2 = chosen.reshape(N_EXPERTS, tb)
    chb = ch2.astype(BF16)
    before = _dot(chb, tri_ref[...])
    rank_ref[...] = jnp.where(ch2 > 0.0, before, -1.0).astype(jnp.int32)
    cnt_ref[...] = _dot(chb, jnp.ones((tb, LANES), BF16))


def _router_call(g, logits_t, router_b):
    tb = MOE_TOKENS
    nsb = g.t // tb
    tri = (np.arange(tb)[:, None] < np.arange(tb)[None, :]).astype(np.float32)
    blk = pl.BlockSpec((N_EXPERTS, tb), lambda s: (0, s))
    gates, rank, cnt = pl.pallas_call(
        _router_kernel, grid=(nsb,),
        in_specs=[blk, _full((N_EXPERTS, 1)), _full((tb, tb))],
        out_specs=[blk, blk, pl.BlockSpec((None, N_EXPERTS, LANES), lambda s: (s, 0, 0))],
        out_shape=[jax.ShapeDtypeStruct((N_EXPERTS, g.t), F32), jax.ShapeDtypeStruct((N_EXPERTS, g.t), jnp.int32),
                   jax.ShapeDtypeStruct((nsb, N_EXPERTS, LANES), F32)],
        compiler_params=_cparams(("parallel",)), name="router",
    )(logits_t, router_b.reshape(-1, 1), jnp.asarray(tri, BF16))
    return gates, rank, cnt[:, :, 0].astype(jnp.int32).reshape(-1)


def _moe_kernel(cnt_ref, v_ref, gate_ref, rank_ref, wg_ref, wu_ref, wd_ref, sg_ref, su_ref, sd_ref, o_ref,
                psel_ref, ysel_ref, slot_ref):
    s, j = pl.program_id(0), pl.program_id(1)
    tb = v_ref.shape[0]
    rows = MOE_ROWS

    @pl.when(j == 0)
    def _():
        vb = v_ref[...]
        hid = (_silu(_dot(vb, sg_ref[...])) * _dot(vb, su_ref[...])).astype(BF16)
        o_ref[...] = _dot(hid, sd_ref[...])
        psel_ref[...] = jnp.zeros_like(psel_ref)
        ysel_ref[...] = jnp.zeros_like(ysel_ref)
        slot_ref[0] = 0

    row_id = lax.broadcasted_iota(jnp.int32, (rows, tb), 0)

    def flush(n_slots):
        stage_row = lax.broadcasted_iota(jnp.int32, psel_ref.shape, 0)
        sel = jnp.where(stage_row < n_slots * rows, psel_ref[...], jnp.zeros_like(psel_ref))
        o_ref[...] += _dot_tn(sel, ysel_ref[...])

    ends, total = [], 0
    for e in range(MOE_PAIR):
        total = total + (cnt_ref[s * N_EXPERTS + j * MOE_PAIR + e] + rows - 1) // rows
        ends.append(total)

    def run_item(i, slot):
        e = sum((i >= end).astype(jnp.int32) for end in ends[:-1])
        first = sum(jnp.where(e == k + 1, ends[k], 0) for k in range(MOE_PAIR - 1))
        ex = j * MOE_PAIR + e
        rk = jnp.where(i < total, rank_ref[pl.ds(ex, 1), :] - (i - first) * rows, -1)
        hit = row_id == rk
        onehot = jnp.where(hit, 1.0, 0.0).astype(BF16)
        w_row = jnp.sum(jnp.where(hit, gate_ref[pl.ds(ex, 1), :], 0.0), axis=1, keepdims=True)
        xs = _dot(onehot, v_ref[...]).astype(BF16)
        hid = (_silu(_dot(xs, wg_ref[e])) * _dot(xs, wu_ref[e])).astype(BF16)
        y = _dot(hid, wd_ref[e])
        at = pl.multiple_of(slot * rows, rows)
        psel_ref[pl.ds(at, rows), :] = onehot
        ysel_ref[pl.ds(at, rows), :] = (y * w_row).astype(BF16)

    def body(it, slot):
        run_item(2 * it, slot)
        run_item(2 * it + 1, slot + 1)

        @pl.when(slot == MOE_SLOTS - 2)
        def _():
            flush(MOE_SLOTS)

        return jnp.where(slot == MOE_SLOTS - 2, 0, slot + 2)

    slot = lax.fori_loop(0, (total + 1) // 2, body, slot_ref[0])
    slot_ref[0] = slot

    @pl.when(jnp.logical_and(j == pl.num_programs(1) - 1, slot > 0))
    def _():
        flush(slot)


def _moe_call(g, v, gates, rank, counts, wg, wu, wd, sg, su, sd):
    t, d = v.shape
    tb = MOE_TOKENS
    ff = wg.shape[2]
    grid_spec = pltpu.PrefetchScalarGridSpec(
        num_scalar_prefetch=1,
        grid=(t // tb, N_EXPERTS // MOE_PAIR),
        in_specs=[
            pl.BlockSpec((tb, d), lambda s, j, c: (s, 0)),
            pl.BlockSpec((N_EXPERTS, tb), lambda s, j, c: (0, s)),
            pl.BlockSpec((N_EXPERTS, tb), lambda s, j, c: (0, s)),
            pl.BlockSpec((MOE_PAIR, d, ff), lambda s, j, c: (j, 0, 0)),
            pl.BlockSpec((MOE_PAIR, d, ff), lambda s, j, c: (j, 0, 0)),
            pl.BlockSpec((MOE_PAIR, ff, d), lambda s, j, c: (j, 0, 0)),
            pl.BlockSpec((d, sg.shape[1]), lambda s, j, c: (0, 0)),
            pl.BlockSpec((d, su.shape[1]), lambda s, j, c: (0, 0)),
            pl.BlockSpec((sd.shape[0], d), lambda s, j, c: (0, 0)),
        ],
        out_specs=pl.BlockSpec((tb, d), lambda s, j, c: (s, 0)),
        scratch_shapes=[pltpu.VMEM((MOE_SLOTS * MOE_ROWS, tb), BF16), pltpu.VMEM((MOE_SLOTS * MOE_ROWS, d), BF16),
                        pltpu.SMEM((1,), jnp.int32)],
    )
    return pl.pallas_call(
        _moe_kernel, grid_spec=grid_spec, out_shape=jax.ShapeDtypeStruct((t, d), F32),
        compiler_params=_cparams(("parallel", "arbitrary")), name="moe",
    )(counts, v, gates, rank, wg.astype(BF16), wu.astype(BF16), wd.astype(BF16),
      sg.astype(BF16), su.astype(BF16), sd.astype(BF16))


def _rec_pre_kernel(h_ref, mod_ref, gpre_ref, wtm_ref, wfm_ref, cqh_ref, sqh_ref, ckh_ref, skh_ref,
                    z_ref, xbc_ref, dt_ref, kd_ref, qd_ref, vd_ref):
    h = h_ref[...]
    u = _rms(h, -1) * gpre_ref[...]
    u = u * (1.0 + mod_ref[1:2, :]) + mod_ref[0:1, :]
    ub = u.astype(BF16)
    zk = _dot(ub, wtm_ref[...])
    zt = _dot_nt(wfm_ref[...], ub)
    z_ref[...] = zk[:, 0:512].astype(z_ref.dtype)
    xbc_ref[...] = zk[:, 512:1536]
    dt_ref[...] = zk[:, 1536:1664]
    ck, sk = ckh_ref[...], skh_ref[...]
    for j in range(DIFF_HEADS):
        lo = 1664 + 128 * j
        kd_ref[j] = (zk[:, lo:lo + 128] * ck + zk[:, lo + 512:lo + 640] * sk).astype(BF16)
    cq, sq = cqh_ref[...], sqh_ref[...]
    qs = DIFF_SCALE * LOG2E
    zero64 = jnp.zeros((64, h.shape[0]), BF16)
    for hd in range(2 * DIFF_HEADS):
        q = ((zt[64 * hd:64 * hd + 64] * cq + zt[512 + 64 * hd:576 + 64 * hd] * sq) * qs).astype(BF16)
        if hd % 2 == 0:
            qd_ref[hd, 0:64, :] = q
            qd_ref[hd, 64:128, :] = zero64
        else:
            qd_ref[hd, 0:64, :] = zero64
            qd_ref[hd, 64:128, :] = q
    for j in range(DIFF_HEADS):
        vd_ref[j] = zt[1024 + 128 * j:1152 + 128 * j].astype(BF16)


def _rec_pre_weights(g, rec_w_in):
    d = rec_w_in.shape[0]
    o = np.cumsum([0, SSD_INNER, SSD_XBC, SSD_HEADS, SSD_HEADS, 2 * DIFF_HEADS * DIFF_DIM, 2 * DIFF_HEADS * DIFF_DIM,
                   DIFF_HEADS * DIFF_V])
    w_z, w_xbc, w_dtf, w_dtb, w_dq, w_dk, w_dv = (rec_w_in[:, o[i]:o[i + 1]] for i in range(7))
    src64, sgn64 = _rot_map(DIFF_DIM)

    def rot_heads(wcols):
        wh = wcols.reshape(d, 2 * DIFF_HEADS, DIFF_DIM)
        return (wh[:, :, src64] * sgn64).reshape(d, -1)

    w_dt = jnp.pad(jnp.concatenate([w_dtf, w_dtb], axis=1), ((0, 0), (0, LANES - 2 * SSD_HEADS)))
    wtm = jnp.concatenate([w_z, w_xbc, w_dt, w_dk, rot_heads(w_dk)], axis=1)
    wfm = jnp.concatenate([w_dq, rot_heads(w_dq), w_dv], axis=1)
    cos_h, sin_h = _rope_tables(g.n, DIFF_DIM, ROW_BLOCK)
    two = lambda x: jnp.concatenate([x, x], axis=1)
    return dict(wtm=wtm.astype(BF16), wfm=wfm.T.astype(BF16), cqh=cos_h.T, sqh=sin_h.T, ckh=two(cos_h), skh=two(sin_h))


def _rec_pre_call(g, h, mod, gpre, w):
    t, d = h.shape
    rb = ROW_BLOCK
    row = lambda i: (i, 0)
    tm_tab = pl.BlockSpec((rb, LANES), lambda i: (g.pos_block(i), 0))
    fm64 = pl.BlockSpec((64, rb), lambda i: (0, g.pos_block(i)))
    nd = 2 * DIFF_HEADS
    return pl.pallas_call(
        _rec_pre_kernel, grid=(g.n_blocks,),
        in_specs=[pl.BlockSpec((rb, d), row), pl.BlockSpec((None, 6, d), lambda i: (g.mod_row(i), 0, 0)),
                  _full(gpre.shape), _full(w["wtm"].shape), _full(w["wfm"].shape), fm64, fm64, tm_tab, tm_tab],
        out_specs=[
            pl.BlockSpec((rb, SSD_INNER), row), pl.BlockSpec((rb, SSD_XBC), row), pl.BlockSpec((rb, LANES), row),
            pl.BlockSpec((DIFF_HEADS, rb, 128), lambda i: (0, i, 0)),
            pl.BlockSpec((nd, 128, rb), lambda i: (0, 0, i)),
            pl.BlockSpec((DIFF_HEADS, DIFF_V, rb), lambda i: (0, 0, i)),
        ],
        out_shape=[
            jax.ShapeDtypeStruct((t, SSD_INNER), BF16), jax.ShapeDtypeStruct((t, SSD_XBC), F32),
            jax.ShapeDtypeStruct((t, LANES), F32), jax.ShapeDtypeStruct((DIFF_HEADS, t, 128), BF16),
            jax.ShapeDtypeStruct((nd, 128, t), BF16), jax.ShapeDtypeStruct((DIFF_HEADS, DIFF_V, t), BF16),
        ],
        compiler_params=_cparams(("parallel",)), name="rec_pre",
    )(h, mod, gpre, w["wtm"], w["wfm"], w["cqh"], w["sqh"], w["ckh"], w["skh"])


def _conv_kernel(x_ref, prev_ref, next_ref, w_ref, b_ref, o_ref, *, seq_blocks, lat_blocks):
    i = pl.program_id(0)
    rb = x_ref.shape[0]
    pos = i % seq_blocks
    is_ctx = i >= lat_blocks
    first = jnp.logical_or(is_ctx, pos == 0)
    last = jnp.logical_or(is_ctx, pos == seq_blocks - 1)
    x = x_ref[...]
    prev_row = jnp.where(first, 0.0, prev_ref[7:8, :])
    next_row = jnp.where(last, 0.0, next_ref[0:1, :])
    rid = lax.broadcasted_iota(jnp.int32, x.shape, 0)
    x_prev = jnp.where(rid == 0, prev_row, pltpu.roll(x, 1, axis=0))
    x_next = jnp.where(rid == rb - 1, next_row, pltpu.roll(x, rb - 1, axis=0))
    y = w_ref[0:1, :] * x_prev + w_ref[1:2, :] * x + w_ref[2:3, :] * x_next + b_ref[...]
    o_ref[...] = _silu(y)


def _conv_call(g, xbc, conv_w, conv_b):
    t, ch = xbc.shape
    rb = g.c
    halo = 8
    per = rb // halo
    last_halo = t // halo - 1
    return pl.pallas_call(
        functools.partial(_conv_kernel, seq_blocks=g.n // rb, lat_blocks=g.b * g.n // rb),
        grid=(t // rb,),
        in_specs=[
            pl.BlockSpec((rb, ch), lambda i: (i, 0)),
            pl.BlockSpec((halo, ch), lambda i: (jnp.maximum(i * per - 1, 0), 0)),
            pl.BlockSpec((halo, ch), lambda i: (jnp.minimum((i + 1) * per, last_halo), 0)),
            _full((SSD_CONV, ch)), _full((1, ch)),
        ],
        out_specs=pl.BlockSpec((rb, ch), lambda i: (i, 0)),
        out_shape=jax.ShapeDtypeStruct((t, ch), F32),
        compiler_params=_cparams(("parallel",)), name="ssd_conv",
    )(xbc, xbc, xbc, conv_w, conv_b.reshape(1, ch))


def _ssd_kernel(xf_ref, dtf_ref, xb_ref, dtb_ref, bias_ref, alog_ref, tril_ref, triu_ref, eye_ref, yf_ref, yb_ref,
                sf_ref, sb_ref):
    @pl.when(pl.program_id(1) == 0)
    def _():
        sf_ref[...] = jnp.zeros_like(sf_ref)
        sb_ref[...] = jnp.zeros_like(sb_ref)

    _ssd_chunk(xf_ref, dtf_ref, bias_ref, alog_ref, tril_ref, triu_ref, eye_ref, yf_ref, sf_ref, 0, False)
    _ssd_chunk(xb_ref, dtb_ref, bias_ref, alog_ref, triu_ref, tril_ref, eye_ref, yb_ref, sb_ref, SSD_HEADS, True)


def _ssd_chunk(x_ref, dt_ref, bias_ref, alog_ref, tri_ref, trit_ref, eye_ref, y_ref, state_ref, lane_off, reverse):
    ln = SSD_CHUNK
    hi = lax.Precision.HIGHEST
    x = x_ref[:, 0:SSD_INNER]
    raw = dt_ref[...] + bias_ref[...]
    dt = jnp.maximum(raw, 0.0) + jnp.log1p(jnp.exp(-jnp.abs(raw)))
    adt = dt * (-jnp.exp(alog_ref[...]))
    acs = jnp.dot(tri_ref[...], adt, preferred_element_type=F32, precision=hi)
    acs_row = lax.dot_general(adt, trit_ref[...], (((0,), (0,)), ((), ())), preferred_element_type=F32, precision=hi)
    dt_row = lax.dot_general(dt, eye_ref[...], (((0,), (0,)), ((), ())), preferred_element_type=F32, precision=hi)
    tot = acs[0:1, :] if reverse else acs[ln - 1:ln, :]
    w_all = jnp.exp(tot - acs) * dt
    ea_all = jnp.exp(acs)
    etot = jnp.exp(tot)
    li = lax.broadcasted_iota(jnp.int32, (ln, ln), 0)
    si = lax.broadcasted_iota(jnp.int32, (ln, ln), 1)
    keep = (si >= li) if reverse else (si <= li)
    lane = lax.broadcasted_iota(jnp.int32, (ln, LANES), 1)
    left = lane < SSD_HEAD_DIM
    per_group = SSD_HEADS // SSD_GROUPS
    for gi in range(SSD_GROUPS):
        bm = x_ref[:, SSD_INNER + SSD_STATE * gi:SSD_INNER + SSD_STATE * (gi + 1)].astype(BF16)
        cm = x_ref[:, SSD_INNER + SSD_STATE * (SSD_GROUPS + gi):SSD_INNER + SSD_STATE * (SSD_GROUPS + gi + 1)]
        cm = cm.astype(BF16)
        cb = _dot_nt(cm, bm)
        for pr in range(per_group // 2):
            h0 = gi * per_group + 2 * pr
            xp = x[:, SSD_HEAD_DIM * h0:SSD_HEAD_DIM * (h0 + 2)]
            mats = []
            for hd in (h0, h0 + 1):
                c = lane_off + hd
                seg = acs[:, c:c + 1] - acs_row[c:c + 1, :]
                lmat = jnp.exp(jnp.where(keep, seg, -jnp.inf))
                mats.append((cb * lmat * dt_row[c:c + 1, :]).astype(BF16))
            xb = xp.astype(BF16)
            zero = jnp.zeros_like(xb)
            rhs = jnp.concatenate([jnp.where(left, xb, zero), jnp.where(left, zero, xb)], axis=0)
            y_diag = _dot(jnp.concatenate(mats, axis=1), rhs)
            st = state_ref[h0 // 2]
            c0 = lane_off + h0
            ea = jnp.where(left, ea_all[:, c0:c0 + 1], ea_all[:, c0 + 1:c0 + 2])
            y_off = _dot(cm, st.astype(BF16)) * ea
            y_ref[:, SSD_HEAD_DIM * h0:SSD_HEAD_DIM * (h0 + 2)] = y_diag + y_off
            wcol = jnp.where(left, w_all[:, c0:c0 + 1], w_all[:, c0 + 1:c0 + 2])
            cs = _dot_tn(bm, (xp * wcol).astype(BF16))
            dec = jnp.where(left[0:1, :], etot[:, c0:c0 + 1], etot[:, c0 + 1:c0 + 2])
            state_ref[h0 // 2] = st * dec + cs


def _ssd_call(g, xbc_act, dt, dt_bias_f, dt_bias_b, a_log_f, a_log_b):
    t = xbc_act.shape[0]
    ln = SSD_CHUNK
    cc = g.c // ln
    nl = g.n // ln
    ctx0 = g.b * g.n // ln
    fwd = lambda b, k: (jnp.where(k < cc, ctx0 + b * cc + k, b * nl + (k - cc)), 0)
    bwd = lambda b, k: (jnp.where(k < cc, ctx0 + b * cc + (cc - 1 - k), b * nl + (nl - 1 - (k - cc))), 0)
    idx = np.arange(ln)
    lower = (idx[:, None] >= idx[None, :]).astype(np.float32)
    lanes = lambda pf, pb: jnp.pad(jnp.concatenate([pf, pb]).reshape(1, -1), ((0, 0), (0, LANES - 2 * SSD_HEADS)))
    state = pltpu.VMEM((SSD_HEADS // 2, SSD_STATE, 2 * SSD_HEAD_DIM), F32)
    y_shape = jax.ShapeDtypeStruct((t, SSD_INNER), F32)
    return pl.pallas_call(
        _ssd_kernel, grid=(g.b, cc + nl),
        in_specs=[pl.BlockSpec((ln, SSD_XBC), fwd), pl.BlockSpec((ln, LANES), fwd),
                  pl.BlockSpec((ln, SSD_XBC), bwd), pl.BlockSpec((ln, LANES), bwd),
                  _full((1, LANES)), _full((1, LANES)), _full((ln, ln)), _full((ln, ln)), _full((ln, ln))],
        out_specs=[pl.BlockSpec((ln, SSD_INNER), fwd), pl.BlockSpec((ln, SSD_INNER), bwd)],
        out_shape=[y_shape, y_shape],
        scratch_shapes=[state, state],
        compiler_params=_cparams(("parallel", "arbitrary")),
        name="ssd_scan",
    )(xbc_act, dt, xbc_act, dt, lanes(dt_bias_f, dt_bias_b), lanes(a_log_f, a_log_b), jnp.asarray(lower),
      jnp.asarray(lower.T), jnp.eye(ln, dtype=F32))


def _diff_finish(o, tq, lq1_ref, lk1_ref, lq2_ref, lk2_ref, sub_ref, o_ref, lambda_init):
    lam = (jnp.exp(jnp.sum(lq1_ref[...] * lk1_ref[...], axis=1, keepdims=True))
           - jnp.exp(jnp.sum(lq2_ref[...] * lk2_ref[...], axis=1, keepdims=True)) + lambda_init)
    od = o[:, 0:tq] - lam * o[:, tq:2 * tq]
    o_ref[...] = ((_rms(od, 0) * sub_ref[...]) * (1.0 - lambda_init)).astype(o_ref.dtype)


def _diff_attn_kernel(qt_ref, *refs, lambda_init):
    o_ref, s_ref = refs[-2:]
    params = refs[-7:-2]
    kv = refs[:-7]
    half = len(kv) // 2
    segs = [(kv[i], kv[half + i]) for i in range(half)]
    qt = jnp.concatenate([qt_ref[0], qt_ref[1]], axis=1)
    o = _attend(qt, segs, s_ref, DIFF_V)
    _diff_finish(o, qt_ref.shape[2], *params, o_ref, lambda_init)


def _diff_attn_call(g, qd, kd, vd, lq1, lk1, lq2, lk2, subln, lambda_init):
    tq = DIFF_Q_TILE
    nq = g.n // tq
    ctx_blk = g.b * g.n // g.c
    vec = lambda a: a.reshape(1, -1)
    params = [_full((1, DIFF_DIM)), _full((1, DIFF_DIM)), _full((1, DIFF_DIM)), _full((1, DIFF_DIM)),
              _full((DIFF_V, 1))]
    pvals = (vec(lq1), vec(lk1), vec(lq2), vec(lk2), subln.reshape(-1, 1))
    klat = pl.BlockSpec((None, g.n, 128), lambda b, j, *_: (j, b, 0))
    kctx = pl.BlockSpec((None, g.c, 128), lambda b, j, *_: (j, ctx_blk + b, 0))
    vlat = pl.BlockSpec((None, DIFF_V, g.n), lambda b, j, *_: (j, 0, b))
    vctx = pl.BlockSpec((None, DIFF_V, g.c), lambda b, j, *_: (j, 0, ctx_blk + b))
    kern = functools.partial(_diff_attn_kernel, lambda_init=lambda_init)
    lat = pl.pallas_call(
        kern, grid=(g.b, DIFF_HEADS, nq),
        scratch_shapes=[pltpu.VMEM((g.n + g.c, 2 * tq), F32)],
        in_specs=[pl.BlockSpec((2, 128, tq), lambda b, j, qi: (j, 0, b * nq + qi)), klat, kctx, vlat, vctx] + params,
        out_specs=pl.BlockSpec((DIFF_V, tq), lambda b, j, qi: (j, b * nq + qi)),
        out_shape=jax.ShapeDtypeStruct((DIFF_HEADS * DIFF_V, g.b * g.n), BF16),
        compiler_params=_cparams(("parallel", "parallel", "arbitrary")),
        name="attn_diff",
    )(qd, kd, kd, vd, vd, *pvals)
    ctx = pl.pallas_call(
        kern, grid=(g.b, DIFF_HEADS),
        scratch_shapes=[pltpu.VMEM((g.c, 2 * g.c), F32)],
        in_specs=[pl.BlockSpec((2, 128, g.c), lambda b, j: (j, 0, ctx_blk + b)), kctx, vctx] + params,
        out_specs=pl.BlockSpec((DIFF_V, g.c), lambda b, j: (j, b)),
        out_shape=jax.ShapeDtypeStruct((DIFF_HEADS * DIFF_V, g.b * g.c), BF16),
        compiler_params=_cparams(("parallel", "parallel")),
        name="attn_diff_ctx",
    )(qd, kd, vd, *pvals)
    return jnp.concatenate([lat, ctx], axis=1)


def _rec_post_kernel(yf_ref, yb_ref, xs_ref, z_ref, od_ref, dskip_ref, gn_ref, wa_ref, wb_ref,
                     h_ref, mod_ref, gpost_ref, gffn_ref, rw_ref, hn_ref, v_ref, lg_ref):
    y = (yf_ref[...] + yb_ref[...] + xs_ref[...] * dskip_ref[...]) * _silu(z_ref[...].astype(F32))
    half = SSD_INNER // SSD_GROUPS
    o = _dot_tn(od_ref[...], wb_ref[...])
    for gi in range(SSD_GROUPS):
        lo = half * gi
        yn = (_rms(y[:, lo:lo + half], -1) * gn_ref[:, lo:lo + half]).astype(BF16)
        o = o + _dot(yn, wa_ref[lo:lo + half, :])
    _post_tail(o, h_ref, mod_ref, gpost_ref, gffn_ref, rw_ref, hn_ref, v_ref, lg_ref)


def _rec_post_call(g, yf, yb, xbc_act, zs, od, d_skip, ssd_norm, w_out, h, mod, gpost, gffn, router_w):
    d = h.shape[1]
    rb = ROW_BLOCK
    ins, outs, shapes = _post_specs(g, d)
    wo = w_out.astype(BF16)
    row512 = pl.BlockSpec((rb, SSD_INNER), lambda i: (i, 0))
    return pl.pallas_call(
        _rec_post_kernel, grid=(g.n_blocks,),
        in_specs=[row512, row512, row512, row512, pl.BlockSpec((SSD_INNER, rb), lambda i: (0, i)),
                  _full((1, SSD_INNER)), _full((1, SSD_INNER)), _full((SSD_INNER, d)), _full((SSD_INNER, d))] + ins,
        out_specs=outs, out_shape=shapes,
        compiler_params=_cparams(("parallel",)), name="rec_post",
    )(yf, yb, xbc_act, zs, od, jnp.repeat(d_skip, SSD_HEAD_DIM).reshape(1, -1), ssd_norm.reshape(1, -1),
      wo[:SSD_INNER], wo[SSD_INNER:], h, mod, gpost.reshape(1, d), gffn.reshape(1, d), router_w.T)


def _ffn_res_kernel(h_ref, f_ref, mod_ref, g_ref, o_ref):
    o_ref[...] = h_ref[...] + mod_ref[5:6, :] * (_rms(f_ref[...], -1) * g_ref[...])


def _ffn_res_call(g, h, f, mod, gain, n_rows):
    d = h.shape[1]
    rb = ROW_BLOCK
    row = pl.BlockSpec((rb, d), lambda i: (i, 0))
    return pl.pallas_call(
        _ffn_res_kernel, grid=(n_rows // rb,),
        in_specs=[row, row, pl.BlockSpec((None, 6, d), lambda i: (g.mod_row(i), 0, 0)), _full((1, d))],
        out_specs=row, out_shape=jax.ShapeDtypeStruct((n_rows, d), F32),
        compiler_params=_cparams(("parallel",)), name="ffn_res",
    )(h, f, mod, gain.reshape(1, d))


def kernel(x, c, ctx, c_ctx, ada_w, ada_b, norm_mix_pre, norm_mix_post, norm_ffn_pre, norm_ffn_post, mix_w_out, att_w_in, mla_q_norm, mla_w_q_up, mla_kv_norm, mla_w_kv_up, gqa_q_norm, gqa_k_norm, rec_w_in, ssd_conv_w, ssd_conv_b, ssd_dt_bias_f, ssd_dt_bias_b, ssd_a_log_f, ssd_a_log_b, ssd_d, ssd_norm, diff_lambda_q1, diff_lambda_k1, diff_lambda_q2, diff_lambda_k2, diff_subln, router_w, router_b, exp_w_gate, exp_w_up, exp_w_down, sh_w_gate, sh_w_up, sh_w_down):
    b, n, d = x.shape
    n_ctx = ctx.shape[1]
    depth = ada_w.shape[0]
    g = _Geom(b, n, n_ctx)
    mod_rows = -(-(b + 1) // 8) * 8
    c_all = jnp.concatenate([c, c_ctx[None, :], jnp.zeros((mod_rows - b - 1, d), F32)], axis=0)
    mods = _ada_call(c_all, ada_w, ada_b).reshape(depth, mod_rows, 6, d)
    h = jnp.concatenate([x.reshape(b * n, d), ctx.reshape(b * n_ctx, d)], axis=0)
    for i in range(depth):
        last = i == depth - 1
        jdx = i // 2
        mod = mods[i]
        if i % 2 == 0:
            w = _attn_pre_weights(g, att_w_in[jdx], mla_q_norm[jdx], mla_w_q_up[jdx], mla_kv_norm[jdx],
                                  mla_w_kv_up[jdx], gqa_q_norm[jdx], gqa_k_norm[jdx])
            qm, km, vm, qg, kg, vg = _attn_pre_call(g, h, mod, norm_mix_pre[i].reshape(1, d), w)
            oa = _attn_call(g, qm, km, vm, kv_heads=MLA_HEADS, shared_k=False, name="attn_mla")
            ob = _attn_call(g, qg, kg, vg, kv_heads=GQA_KV_HEADS, shared_k=True, name="attn_gqa")
            h, v, logits_t = _attn_post_call(g, oa, ob, mix_w_out[i], h, mod, norm_mix_post[i], norm_ffn_pre[i],
                                             router_w[i])
        else:
            lambda_init = 0.8 - 0.6 * math.exp(-0.3 * i)
            w = _rec_pre_weights(g, rec_w_in[jdx])
            zs, xbc, dt, kd, qd, vd = _rec_pre_call(g, h, mod, norm_mix_pre[i].reshape(1, d), w)
            xbc_act = _conv_call(g, xbc, ssd_conv_w[jdx], ssd_conv_b[jdx])
            yf, yb = _ssd_call(g, xbc_act, dt, ssd_dt_bias_f[jdx], ssd_dt_bias_b[jdx], ssd_a_log_f[jdx],
                               ssd_a_log_b[jdx])
            od = _diff_attn_call(g, qd, kd, vd, diff_lambda_q1[jdx], diff_lambda_k1[jdx], diff_lambda_q2[jdx],
                                 diff_lambda_k2[jdx], diff_subln[jdx], lambda_init)
            h, v, logits_t = _rec_post_call(g, yf, yb, xbc_act, zs, od, ssd_d[jdx], ssd_norm[jdx], mix_w_out[i], h,
                                            mod, norm_mix_post[i], norm_ffn_pre[i], router_w[i])
        gates, rank, counts = _router_call(g, logits_t, router_b[i])
        f = _moe_call(g, v, gates, rank, counts, exp_w_gate[i], exp_w_up[i], exp_w_down[i],
                      sh_w_gate[i], sh_w_up[i], sh_w_down[i])
        h = _ffn_res_call(g, h, f, mod, norm_ffn_post[i], b * n if last else g.t)
    return h.reshape(b, n, d)
```

```python
import functools
import math

import numpy as np
import jax
import jax.numpy as jnp
from jax import lax
from jax.experimental import pallas as pl
from jax.experimental.pallas import tpu as pltpu

F32 = jnp.float32
BF16 = jnp.bfloat16
LOG2E = 1.4426950408889634

GRID_W = 64
ROPE_BASE = 10000.0
NORM_EPS = 1e-6

MLA_HEADS, MLA_Q_LORA, MLA_KV_LORA, MLA_NOPE, MLA_ROPE, MLA_V = 8, 256, 128, 64, 32, 64
MLA_SCALE = (MLA_NOPE + MLA_ROPE) ** -0.5
GQA_HEADS, GQA_KV_HEADS, GQA_DIM = 8, 2, 64
GQA_SCALE = GQA_DIM ** -0.5
SSD_HEADS, SSD_HEAD_DIM, SSD_GROUPS, SSD_STATE, SSD_CONV, SSD_CHUNK = 8, 64, 2, 128, 3, 128
SSD_INNER = SSD_HEADS * SSD_HEAD_DIM
SSD_XBC = SSD_INNER + 2 * SSD_GROUPS * SSD_STATE
DIFF_HEADS, DIFF_DIM = 4, 64
DIFF_V = 2 * DIFF_DIM
DIFF_SCALE = DIFF_DIM ** -0.5
N_EXPERTS, TOP_K, N_GROUPS, TOPK_GROUPS, EXPERT_FF, SHARED_FF = 64, 8, 8, 4, 256, 256
ROUTED_SCALE = 2.5

LANES = 128
VMEM_LIMIT_BYTES = 56 * 1024 * 1024

ROW_BLOCK = 512
ATTN_Q_TILE = 1024
DIFF_Q_TILE = 512
ATTN_KV_CHUNK = 512
ATTN_EXP_ROWS = 4096
MOE_TOKENS = 768
MOE_ROWS = 128
MOE_PAIR = 8
MOE_SLOTS = 8


def _cparams(sem, flags=None):
    return pltpu.CompilerParams(dimension_semantics=sem, vmem_limit_bytes=VMEM_LIMIT_BYTES, flags=flags)


def _rms(x, axis):
    return x * lax.rsqrt(jnp.mean(x * x, axis=axis, keepdims=True) + NORM_EPS)


def _silu(x):
    return x * jax.nn.sigmoid(x)


def _dot(a, b):
    return jnp.dot(a, b, preferred_element_type=F32)


def _dot_nt(a, b):
    return lax.dot_general(a, b, (((1,), (1,)), ((), ())), preferred_element_type=F32)


def _dot_tn(a, b):
    return lax.dot_general(a, b, (((0,), (0,)), ((), ())), preferred_element_type=F32)


def _ada_kernel(c_ref, w_ref, b_ref, o_ref):
    s = _silu(c_ref[...])
    o_ref[...] = jnp.dot(s, w_ref[...], preferred_element_type=F32, precision=lax.Precision.HIGHEST) + b_ref[...]


def _ada_call(c_all, ada_w, ada_b):
    depth, d, six_d = ada_w.shape
    rows = c_all.shape[0]
    cols = six_d // 4
    return pl.pallas_call(
        _ada_kernel,
        grid=(depth, six_d // cols),
        in_specs=[
            pl.BlockSpec((rows, d), lambda i, j: (0, 0)),
            pl.BlockSpec((None, d, cols), lambda i, j: (i, 0, j)),
            pl.BlockSpec((None, 1, cols), lambda i, j: (i, 0, j)),
        ],
        out_specs=pl.BlockSpec((None, rows, cols), lambda i, j: (i, 0, j)),
        out_shape=jax.ShapeDtypeStruct((depth, rows, six_d), F32),
        compiler_params=_cparams(("arbitrary", "arbitrary")),
        name="ada_mod",
    )(c_all, ada_w, ada_b.reshape(depth, 1, six_d))


def _attn_pre_kernel(h_ref, mod_ref, gpre_ref, wfm_ref, wtm_ref, qn_ref, kvn_col_ref, kvn_row_ref, wq_ref, wk_ref,
                     wv_ref, gq_g_ref, gq_gr_ref, gk_g_ref, gk_gr_ref,
                     cqm_ref, sqm_ref, cqh_ref, sqh_ref, ckm_ref, skm_ref, ckh_ref, skh_ref,
                     qm_ref, km_ref, vm_ref, qg_ref, kg_ref, vg_ref):
    h = h_ref[...]
    u = _rms(h, -1) * gpre_ref[...]
    u = u * (1.0 + mod_ref[1:2, :]) + mod_ref[0:1, :]
    ub = u.astype(BF16)
    zt = _dot_nt(wfm_ref[...], ub)
    zk = _dot(ub, wtm_ref[...])

    qn = (_rms(zt[0:256], 0) * qn_ref[...]).astype(BF16)
    qt = _dot(wq_ref[...], qn)
    cq, sq = cqm_ref[...], sqm_ref[...]
    qs = MLA_SCALE * LOG2E
    zero32 = jnp.zeros((32, h.shape[0]), BF16)
    for hd in range(MLA_HEADS):
        pe = qt[512 + 32 * hd:544 + 32 * hd] * cq + qt[768 + 32 * hd:800 + 32 * hd] * sq
        qm_ref[hd, 0:64, :] = (qt[64 * hd:64 * hd + 64] * qs).astype(BF16)
        qm_ref[hd, 64:96, :] = (pe * qs).astype(BF16)
        qm_ref[hd, 96:128, :] = zero32

    kvn_t = (_rms(zt[256:384], 0) * kvn_col_ref[...]).astype(BF16)
    vt = _dot(wv_ref[...], kvn_t)
    for hd in range(MLA_HEADS):
        vm_ref[hd] = vt[64 * hd:64 * hd + 64].astype(BF16)
    kvn = (_rms(zk[:, 0:128], -1) * kvn_row_ref[...]).astype(BF16)
    kn = _dot(kvn, wk_ref[...])
    kpe = zk[:, 128:256] * ckm_ref[...] + zk[:, 256:384] * skm_ref[...]
    for hd in range(MLA_HEADS):
        km_ref[hd] = (kn[:, 128 * hd:128 * hd + 128] + kpe).astype(BF16)

    cqh, sqh = cqh_ref[...], sqh_ref[...]
    gs = GQA_SCALE * LOG2E
    zero64 = jnp.zeros((64, h.shape[0]), BF16)
    grp = GQA_HEADS // GQA_KV_HEADS
    for hd in range(GQA_HEADS):
        raw = zt[384 + 64 * hd:448 + 64 * hd]
        rot = zt[896 + 64 * hd:960 + 64 * hd]
        r = lax.rsqrt(jnp.mean(raw * raw, axis=0, keepdims=True) + NORM_EPS)
        q = (raw * r * gq_g_ref[...]) * cqh + (rot * r * gq_gr_ref[...]) * sqh
        q = (q * gs).astype(BF16)
        if hd // grp == 0:
            qg_ref[hd, 0:64, :] = q
            qg_ref[hd, 64:128, :] = zero64
        else:
            qg_ref[hd, 0:64, :] = zero64
            qg_ref[hd, 64:128, :] = q
    for kvh in range(GQA_KV_HEADS):
        vg_ref[kvh] = zt[1408 + 64 * kvh:1472 + 64 * kvh].astype(BF16)

    gk, gkr = zk[:, 384:512], zk[:, 512:640]
    lane = lax.broadcasted_iota(jnp.int32, gk.shape, 1)
    lo = lane < 64
    sq0 = jnp.sum(jnp.where(lo, gk * gk, 0.0), axis=-1, keepdims=True)
    sq1 = jnp.sum(jnp.where(lo, 0.0, gk * gk), axis=-1, keepdims=True)
    r = jnp.where(lo, lax.rsqrt(sq0 / GQA_DIM + NORM_EPS), lax.rsqrt(sq1 / GQA_DIM + NORM_EPS))
    kg_ref[...] = ((gk * r * gk_g_ref[...]) * ckh_ref[...] + (gkr * r * gk_gr_ref[...]) * skh_ref[...]).astype(BF16)


def _rot_map(dim):
    q = dim // 4
    j = np.arange(dim)
    even = (j // q) % 2 == 0
    return np.where(even, j + q, j - q), np.where(even, -1.0, 1.0).astype(np.float32)


def _rope_tables(n_tokens, dim, pad_rows):
    rows = n_tokens // GRID_W
    row = jnp.repeat(jnp.arange(rows, dtype=F32), GRID_W)
    col = jnp.tile(jnp.arange(GRID_W, dtype=F32), rows)
    half = dim // 2
    inv = ROPE_BASE ** (-(jnp.arange(half // 2, dtype=F32) * 2.0 / half))
    ang_r = row[:, None] * inv
    ang_c = col[:, None] * inv
    ang = jnp.concatenate([ang_r, ang_r, ang_c, ang_c], axis=-1)
    cos = jnp.concatenate([jnp.cos(ang), jnp.ones((pad_rows, dim), F32)], axis=0)
    sin = jnp.concatenate([jnp.sin(ang), jnp.zeros((pad_rows, dim), F32)], axis=0)
    return cos, sin


class _Geom:
    def __init__(self, b, n, c):
        assert n % ROW_BLOCK == 0 and (b * c) % ROW_BLOCK == 0 and n % c == 0 and n % GRID_W == 0
        self.b, self.n, self.c = b, n, c
        self.t = b * n + b * c
        self.lat_blocks = b * n // ROW_BLOCK
        self.blocks_per_seq = n // ROW_BLOCK
        self.n_blocks = self.t // ROW_BLOCK
        assert self.t % MOE_TOKENS == 0

    def mod_row(self, i):
        return jnp.where(i < self.lat_blocks, i // self.blocks_per_seq, self.b)

    def pos_block(self, i):
        return jnp.where(i < self.lat_blocks, i % self.blocks_per_seq, self.blocks_per_seq)


def _full(shape):
    nd = len(shape)
    return pl.BlockSpec(shape, lambda *_: (0,) * nd)


def _attn_pre_call(g, h, mod, gpre, w):
    t, d = h.shape
    rb = ROW_BLOCK
    row = lambda i: (i, 0)
    tm_tab = pl.BlockSpec((rb, LANES), lambda i: (g.pos_block(i), 0))
    fm32 = pl.BlockSpec((32, rb), lambda i: (0, g.pos_block(i)))
    fm64 = pl.BlockSpec((64, rb), lambda i: (0, g.pos_block(i)))
    in_specs = [
        pl.BlockSpec((rb, d), row),
        pl.BlockSpec((None, 6, d), lambda i: (g.mod_row(i), 0, 0)),
        _full(gpre.shape), _full(w["wfm"].shape), _full(w["wtm"].shape), _full(w["qn"].shape),
        _full(w["kvn_col"].shape), _full(w["kvn_row"].shape), _full(w["wq"].shape), _full(w["wk"].shape),
        _full(w["wv"].shape), _full(w["gq_g"].shape), _full(w["gq_gr"].shape), _full(w["gk_g"].shape),
        _full(w["gk_gr"].shape),
        fm32, fm32, fm64, fm64, tm_tab, tm_tab, tm_tab, tm_tab,
    ]
    out_shape = [
        jax.ShapeDtypeStruct((MLA_HEADS, 128, t), BF16),
        jax.ShapeDtypeStruct((MLA_HEADS, t, 128), BF16),
        jax.ShapeDtypeStruct((MLA_HEADS, MLA_V, t), BF16),
        jax.ShapeDtypeStruct((GQA_HEADS, 128, t), BF16),
        jax.ShapeDtypeStruct((t, 128), BF16),
        jax.ShapeDtypeStruct((GQA_KV_HEADS, GQA_DIM, t), BF16),
    ]
    out_specs = [
        pl.BlockSpec((MLA_HEADS, 128, rb), lambda i: (0, 0, i)),
        pl.BlockSpec((MLA_HEADS, rb, 128), lambda i: (0, i, 0)),
        pl.BlockSpec((MLA_HEADS, MLA_V, rb), lambda i: (0, 0, i)),
        pl.BlockSpec((GQA_HEADS, 128, rb), lambda i: (0, 0, i)),
        pl.BlockSpec((rb, 128), row),
        pl.BlockSpec((GQA_KV_HEADS, GQA_DIM, rb), lambda i: (0, 0, i)),
    ]
    return pl.pallas_call(
        _attn_pre_kernel, grid=(g.n_blocks,), in_specs=in_specs, out_specs=out_specs, out_shape=out_shape,
        compiler_params=_cparams(("parallel",)), name="attn_pre",
    )(h, mod, gpre, w["wfm"], w["wtm"], w["qn"], w["kvn_col"], w["kvn_row"], w["wq"], w["wk"], w["wv"],
      w["gq_g"], w["gq_gr"], w["gk_g"], w["gk_gr"],
      w["cqm"], w["sqm"], w["cqh"], w["sqh"], w["ckm"], w["skm"], w["ckh"], w["skh"])


def _attn_pre_weights(g, att_w_in, q_norm, w_q_up, kv_norm, w_kv_up, gqa_qn, gqa_kn):
    d = att_w_in.shape[0]
    o = np.cumsum([0, MLA_Q_LORA, MLA_KV_LORA, MLA_ROPE, GQA_HEADS * GQA_DIM, GQA_KV_HEADS * GQA_DIM,
                   GQA_KV_HEADS * GQA_DIM])
    w_qlat, w_kvlat, w_kpe, w_gq, w_gk, w_gv = (att_w_in[:, o[i]:o[i + 1]] for i in range(6))
    src32, sgn32 = _rot_map(MLA_ROPE)
    src64, sgn64 = _rot_map(GQA_DIM)

    def rot_heads(wcols, heads, dim, src, sgn):
        wh = wcols.reshape(d, heads, dim)
        return (wh[:, :, src] * sgn).reshape(d, heads * dim)

    wfm = jnp.concatenate([w_qlat, w_kvlat, w_gq, rot_heads(w_gq, GQA_HEADS, GQA_DIM, src64, sgn64), w_gv], axis=1)
    zpad = lambda x, lo, hi: jnp.pad(x, ((0, 0), (lo, hi)))
    wtm = jnp.concatenate([
        w_kvlat, zpad(w_kpe, 64, 32), zpad(w_kpe[:, src32] * sgn32, 64, 32),
        w_gk, rot_heads(w_gk, GQA_KV_HEADS, GQA_DIM, src64, sgn64)], axis=1)
    wq = w_q_up.reshape(MLA_Q_LORA, MLA_HEADS, MLA_NOPE + MLA_ROPE)
    wq_pe = wq[:, :, MLA_NOPE:]
    wq_all = jnp.concatenate([
        wq[:, :, :MLA_NOPE].reshape(MLA_Q_LORA, -1), wq_pe.reshape(MLA_Q_LORA, -1),
        (wq_pe[:, :, src32] * sgn32).reshape(MLA_Q_LORA, -1)], axis=1)
    wkv = w_kv_up.reshape(MLA_KV_LORA, MLA_HEADS, MLA_NOPE + MLA_V)
    wk = jnp.pad(wkv[:, :, :MLA_NOPE], ((0, 0), (0, 0), (0, 128 - MLA_NOPE))).reshape(MLA_KV_LORA, -1)
    wv = wkv[:, :, MLA_NOPE:].reshape(MLA_KV_LORA, -1)
    cos_m, sin_m = _rope_tables(g.n, MLA_ROPE, ROW_BLOCK)
    cos_h, sin_h = _rope_tables(g.n, GQA_DIM, ROW_BLOCK)
    two = lambda x: jnp.concatenate([x, x], axis=1)
    return dict(
        wfm=wfm.T.astype(BF16), wtm=wtm.astype(BF16),
        qn=q_norm.reshape(-1, 1), kvn_col=kv_norm.reshape(-1, 1), kvn_row=kv_norm.reshape(1, -1),
        wq=wq_all.T.astype(BF16), wk=wk.astype(BF16), wv=wv.T.astype(BF16),
        gq_g=gqa_qn.reshape(-1, 1), gq_gr=gqa_qn[src64].reshape(-1, 1),
        gk_g=two(gqa_kn.reshape(1, -1)), gk_gr=two(gqa_kn[src64].reshape(1, -1)),
        cqm=cos_m.T, sqm=sin_m.T, cqh=cos_h.T, sqh=sin_h.T,
        ckm=zpad(cos_m, 64, 32), skm=zpad(sin_m, 64, 32), ckh=two(cos_h), skh=two(sin_h),
    )


def _fold8(x, op):
    r, w = x.shape
    return op(x.reshape(r // 8, 8, w), axis=0)


def _attend(qt, segments, s_ref, dv):
    tq = qt.shape[1]
    pieces, off = [], 0
    for k_ref, vt_ref in segments:
        n = k_ref.shape[0]
        for lo in range(0, n, ATTN_KV_CHUNK):
            rows = min(ATTN_KV_CHUNK, n - lo)
            pieces.append((off, lo, rows, k_ref, vt_ref))
            off += rows
    m8 = None
    for so, lo, rows, k_ref, _ in pieces:
        s = _dot(k_ref[lo:lo + rows, :], qt)
        s_ref[so:so + rows, :] = s
        part = _fold8(s, jnp.max)
        m8 = part if m8 is None else jnp.maximum(m8, part)
    m = jnp.max(m8, axis=0, keepdims=True)
    acc = jnp.zeros((dv + 16, tq), F32)
    off = 0
    for _, vt_ref in segments:
        n = vt_ref.shape[1]
        step = min(n, ATTN_EXP_ROWS)

        def body(i, acc, off=off, vt_ref=vt_ref, step=step):
            base = i * step if isinstance(i, int) else pl.multiple_of(i * step, step)
            for lo in range(0, step, ATTN_KV_CHUNK):
                rows = min(ATTN_KV_CHUNK, step - lo)
                p = jnp.exp2((s_ref[pl.ds(off + base + lo, rows), :] - m).astype(BF16))
                vt1 = jnp.concatenate([vt_ref[:, pl.ds(base + lo, rows)], jnp.ones((16, rows), BF16)], axis=0)
                acc = acc + _dot(vt1, p)
            return acc

        acc = body(0, acc) if n == step else lax.fori_loop(0, n // step, body, acc)
        off += n
    return acc[0:dv] / acc[dv:dv + 1]


def _attn_kernel(qt_ref, *refs, dv):
    o_ref, s_ref = refs[-2:]
    kv = refs[:-2]
    half = len(kv) // 2
    segs = [(kv[i], kv[half + i]) for i in range(half)]
    o_ref[...] = _attend(qt_ref[...], segs, s_ref, dv).astype(o_ref.dtype)


def _attn_call(g, qt, k, vt, *, kv_heads, shared_k, name):
    heads = qt.shape[0]
    dv = vt.shape[1]
    grp = heads // kv_heads
    tq = ATTN_Q_TILE
    nq = g.n // tq
    ctx_blk = g.b * g.n // g.c
    if shared_k:
        klat = pl.BlockSpec((g.n, 128), lambda b, h, *_: (b, 0))
        kctx = pl.BlockSpec((g.c, 128), lambda b, h, *_: (ctx_blk + b, 0))
    else:
        klat = pl.BlockSpec((None, g.n, 128), lambda b, h, *_: (h // grp, b, 0))
        kctx = pl.BlockSpec((None, g.c, 128), lambda b, h, *_: (h // grp, ctx_blk + b, 0))
    vlat = pl.BlockSpec((None, dv, g.n), lambda b, h, *_: (h // grp, 0, b))
    vctx = pl.BlockSpec((None, dv, g.c), lambda b, h, *_: (h // grp, 0, ctx_blk + b))
    lat = pl.pallas_call(
        functools.partial(_attn_kernel, dv=dv),
        grid=(g.b, heads, nq),
        scratch_shapes=[pltpu.VMEM((g.n + g.c, tq), F32)],
        in_specs=[pl.BlockSpec((None, 128, tq), lambda b, h, qi: (h, 0, b * nq + qi)), klat, kctx, vlat, vctx],
        out_specs=pl.BlockSpec((dv, tq), lambda b, h, qi: (h, b * nq + qi)),
        out_shape=jax.ShapeDtypeStruct((heads * dv, g.b * g.n), BF16),
        compiler_params=_cparams(("parallel", "parallel", "arbitrary")),
        name=name,
    )(qt, k, k, vt, vt)
    ctx = pl.pallas_call(
        functools.partial(_attn_kernel, dv=dv),
        grid=(g.b, heads),
        scratch_shapes=[pltpu.VMEM((g.c, g.c), F32)],
        in_specs=[pl.BlockSpec((None, 128, g.c), lambda b, h: (h, 0, ctx_blk + b)), kctx, vctx],
        out_specs=pl.BlockSpec((dv, g.c), lambda b, h: (h, b)),
        out_shape=jax.ShapeDtypeStruct((heads * dv, g.b * g.c), BF16),
        compiler_params=_cparams(("parallel", "parallel")),
        name=name + "_ctx",
    )(qt, k, vt)
    return jnp.concatenate([lat, ctx], axis=1)


def _split_bf16(x):
    hi = x.astype(BF16)
    return hi, (x - hi.astype(F32)).astype(BF16)


def _post_tail(o, h_ref, mod_ref, gpost_ref, gffn_ref, rw_ref, hn_ref, v_ref, lg_ref):
    hn = h_ref[...] + mod_ref[2:3, :] * (_rms(o, -1) * gpost_ref[...])
    hn_ref[...] = hn
    v = (_rms(hn, -1) * gffn_ref[...]) * (1.0 + mod_ref[4:5, :]) + mod_ref[3:4, :]
    v_hi, v_lo = _split_bf16(v)
    v_ref[...] = v_hi
    w_hi, w_lo = _split_bf16(rw_ref[...])
    lg_ref[...] = _dot_nt(w_hi, v_hi) + (_dot_nt(w_hi, v_lo) + _dot_nt(w_lo, v_hi))


def _attn_post_kernel(oa_ref, ob_ref, wa_ref, wb_ref, h_ref, mod_ref, gpost_ref, gffn_ref, rw_ref,
                      hn_ref, v_ref, lg_ref):
    o = _dot_tn(oa_ref[...], wa_ref[...]) + _dot_tn(ob_ref[...], wb_ref[...])
    _post_tail(o, h_ref, mod_ref, gpost_ref, gffn_ref, rw_ref, hn_ref, v_ref, lg_ref)


def _post_specs(g, d):
    rb = ROW_BLOCK
    ins = [
        pl.BlockSpec((rb, d), lambda i: (i, 0)),
        pl.BlockSpec((None, 6, d), lambda i: (g.mod_row(i), 0, 0)),
        _full((1, d)), _full((1, d)), _full((N_EXPERTS, d)),
    ]
    outs = [pl.BlockSpec((rb, d), lambda i: (i, 0)), pl.BlockSpec((rb, d), lambda i: (i, 0)),
            pl.BlockSpec((N_EXPERTS, rb), lambda i: (0, i))]
    shapes = [jax.ShapeDtypeStruct((g.t, d), F32), jax.ShapeDtypeStruct((g.t, d), BF16),
              jax.ShapeDtypeStruct((N_EXPERTS, g.t), F32)]
    return ins, outs, shapes


def _attn_post_call(g, oa, ob, w_out, h, mod, gpost, gffn, router_w):
    d = h.shape[1]
    rb = ROW_BLOCK
    half = oa.shape[0]
    ins, outs, shapes = _post_specs(g, d)
    wo = w_out.astype(BF16)
    return pl.pallas_call(
        _attn_post_kernel, grid=(g.n_blocks,),
        in_specs=[pl.BlockSpec((half, rb), lambda i: (0, i)), pl.BlockSpec((half, rb), lambda i: (0, i)),
                  _full((half, d)), _full((half, d))] + ins,
        out_specs=outs, out_shape=shapes,
        compiler_params=_cparams(("parallel",)), name="attn_post",
    )(oa, ob, wo[:half], wo[half:], h, mod, gpost.reshape(1, d), gffn.reshape(1, d), router_w.T)


def _router_kernel(lg_ref, bias_ref, tri_ref, gate_ref, rank_ref, cnt_ref):
    tb = lg_ref.shape[1]
    per = N_EXPERTS // N_GROUPS
    shp = (N_GROUPS, per, tb)
    scores = jax.nn.sigmoid(lg_ref[...])
    s3 = scores.reshape(shp)
    sel = (scores + bias_ref[...]).reshape(shp)
    sub = lax.broadcasted_iota(jnp.int32, shp, 1)
    grp = lax.broadcasted_iota(jnp.int32, shp, 0)
    neg = -jnp.inf
    m1 = jnp.max(sel, axis=1, keepdims=True)
    i1 = jnp.min(jnp.where(sel == m1, sub, per), axis=1, keepdims=True)
    m2 = jnp.max(jnp.where(sub == i1, neg, sel), axis=1, keepdims=True)
    cur = jnp.broadcast_to(m1 + m2, shp)
    gmask = jnp.zeros(shp, F32)
    for _ in range(TOPK_GROUPS):
        gm = jnp.max(cur, axis=0, keepdims=True)
        gi = jnp.min(jnp.where(cur == gm, grp, N_GROUPS), axis=0, keepdims=True)
        pick = grp == gi
        gmask = jnp.where(pick, 1.0, gmask)
        cur = jnp.where(pick, neg, cur)
    masked = jnp.where(gmask > 0.0, sel, neg)
    eidx = grp * per + sub
    chosen = jnp.zeros(shp, F32)
    for _ in range(TOP_K):
        mx = jnp.max(jnp.max(masked, axis=1, keepdims=True), axis=0, keepdims=True)
        cand = jnp.where(masked == mx, eidx, N_EXPERTS)
        ei = jnp.min(jnp.min(cand, axis=1, keepdims=True), axis=0, keepdims=True)
        pick = eidx == ei
        chosen = jnp.where(pick, 1.0, chosen)
        masked = jnp.where(pick, neg, masked)
    top_w = jnp.where(chosen > 0.0, s3, 0.0)
    denom = jnp.sum(jnp.sum(top_w, axis=1, keepdims=True), axis=0, keepdims=True)
    gate_ref[...] = (top_w / denom * ROUTED_SCALE).reshape(N_EXPERTS, tb)
    ch2 = chosen.reshape(N_EXPERTS, tb)
    chb = ch2.astype(BF16)
    before = _dot(chb, tri_ref[...])
    rank_ref[...] = jnp.where(ch2 > 0.0, before, -1.0).astype(jnp.int32)
    cnt_ref[...] = _dot(chb, jnp.ones((tb, LANES), BF16))


def _router_call(g, logits_t, router_b):
    tb = MOE_TOKENS
    nsb = g.t // tb
    tri = (np.arange(tb)[:, None] < np.arange(tb)[None, :]).astype(np.float32)
    blk = pl.BlockSpec((N_EXPERTS, tb), lambda s: (0, s))
    gates, rank, cnt = pl.pallas_call(
        _router_kernel, grid=(nsb,),
        in_specs=[blk, _full((N_EXPERTS, 1)), _full((tb, tb))],
        out_specs=[blk, blk, pl.BlockSpec((None, N_EXPERTS, LANES), lambda s: (s, 0, 0))],
        out_shape=[jax.ShapeDtypeStruct((N_EXPERTS, g.t), F32), jax.ShapeDtypeStruct((N_EXPERTS, g.t), jnp.int32),
                   jax.ShapeDtypeStruct((nsb, N_EXPERTS, LANES), F32)],
        compiler_params=_cparams(("parallel",)), name="router",
    )(logits_t, router_b.reshape(-1, 1), jnp.asarray(tri, BF16))
    return gates, rank, cnt[:, :, 0].astype(jnp.int32).reshape(-1)


def _moe_kernel(cnt_ref, v_ref, gate_ref, rank_ref, wg_ref, wu_ref, wd_ref, sg_ref, su_ref, sd_ref, o_ref,
                psel_ref, ysel_ref, slot_ref):
    s, j = pl.program_id(0), pl.program_id(1)
    tb = v_ref.shape[0]
    rows = MOE_ROWS

    @pl.when(j == 0)
    def _():
        vb = v_ref[...]
        hid = (_silu(_dot(vb, sg_ref[...])) * _dot(vb, su_ref[...])).astype(BF16)
        o_ref[...] = _dot(hid, sd_ref[...])
        psel_ref[...] = jnp.zeros_like(psel_ref)
        ysel_ref[...] = jnp.zeros_like(ysel_ref)
        slot_ref[0] = 0

    row_id = lax.broadcasted_iota(jnp.int32, (rows, tb), 0)

    def flush(n_slots):
        stage_row = lax.broadcasted_iota(jnp.int32, psel_ref.shape, 0)
        sel = jnp.where(stage_row < n_slots * rows, psel_ref[...], jnp.zeros_like(psel_ref))
        o_ref[...] += _dot_tn(sel, ysel_ref[...])

    ends, total = [], 0
    for e in range(MOE_PAIR):
        total = total + (cnt_ref[s * N_EXPERTS + j * MOE_PAIR + e] + rows - 1) // rows
        ends.append(total)

    def select(i):
        e = sum((i >= end).astype(jnp.int32) for end in ends[:-1])
        first = sum(jnp.where(e == k + 1, ends[k], 0) for k in range(MOE_PAIR - 1))
        ex = j * MOE_PAIR + e
        rk = jnp.where(i < total, rank_ref[pl.ds(ex, 1), :] - (i - first) * rows, -1)
        hit = row_id == rk
        onehot = jnp.where(hit, 1.0, 0.0).astype(BF16)
        w_row = jnp.sum(jnp.where(hit, gate_ref[pl.ds(ex, 1), :], 0.0), axis=1, keepdims=True)
        return e, onehot, w_row

    def body(it, slot):
        picks = [select(2 * it), select(2 * it + 1)]
        both = jnp.concatenate([p[1] for p in picks], axis=0)
        at = pl.multiple_of(slot * rows, rows)
        psel_ref[pl.ds(at, 2 * rows), :] = both
        xs = _dot(both, v_ref[...]).astype(BF16)
        hids = [(_silu(_dot(xs[k * rows:(k + 1) * rows], wg_ref[e])) * _dot(xs[k * rows:(k + 1) * rows], wu_ref[e]))
                .astype(BF16) for k, (e, _, _) in enumerate(picks)]
        outs = [(_dot(hid, wd_ref[e]) * w_row).astype(BF16) for hid, (e, _, w_row) in zip(hids, picks)]
        ysel_ref[pl.ds(at, 2 * rows), :] = jnp.concatenate(outs, axis=0)

        @pl.when(slot == MOE_SLOTS - 2)
        def _():
            flush(MOE_SLOTS)

        return jnp.where(slot == MOE_SLOTS - 2, 0, slot + 2)

    slot = lax.fori_loop(0, (total + 1) // 2, body, slot_ref[0])
    slot_ref[0] = slot

    @pl.when(jnp.logical_and(j == pl.num_programs(1) - 1, slot > 0))
    def _():
        flush(slot)


def _moe_call(g, v, gates, rank, counts, wg, wu, wd, sg, su, sd):
    t, d = v.shape
    tb = MOE_TOKENS
    ff = wg.shape[2]
    grid_spec = pltpu.PrefetchScalarGridSpec(
        num_scalar_prefetch=1,
        grid=(t // tb, N_EXPERTS // MOE_PAIR),
        in_specs=[
            pl.BlockSpec((tb, d), lambda s, j, c: (s, 0)),
            pl.BlockSpec((N_EXPERTS, tb), lambda s, j, c: (0, s)),
            pl.BlockSpec((N_EXPERTS, tb), lambda s, j, c: (0, s)),
            pl.BlockSpec((MOE_PAIR, d, ff), lambda s, j, c: (j, 0, 0)),
            pl.BlockSpec((MOE_PAIR, d, ff), lambda s, j, c: (j, 0, 0)),
            pl.BlockSpec((MOE_PAIR, ff, d), lambda s, j, c: (j, 0, 0)),
            pl.BlockSpec((d, sg.shape[1]), lambda s, j, c: (0, 0)),
            pl.BlockSpec((d, su.shape[1]), lambda s, j, c: (0, 0)),
            pl.BlockSpec((sd.shape[0], d), lambda s, j, c: (0, 0)),
        ],
        out_specs=pl.BlockSpec((tb, d), lambda s, j, c: (s, 0)),
        scratch_shapes=[pltpu.VMEM((MOE_SLOTS * MOE_ROWS, tb), BF16), pltpu.VMEM((MOE_SLOTS * MOE_ROWS, d), BF16),
                        pltpu.SMEM((1,), jnp.int32)],
    )
    return pl.pallas_call(
        _moe_kernel, grid_spec=grid_spec, out_shape=jax.ShapeDtypeStruct((t, d), F32),
        compiler_params=_cparams(("parallel", "arbitrary")), name="moe",
    )(counts, v, gates, rank, wg.astype(BF16), wu.astype(BF16), wd.astype(BF16),
      sg.astype(BF16), su.astype(BF16), sd.astype(BF16))


def _rec_pre_kernel(h_ref, f_ref, pmod_ref, gprev_ref, mod_ref, gpre_ref, wtm_ref, wfm_ref, cqh_ref, sqh_ref, ckh_ref,
                    skh_ref, hn_ref, z_ref, xbc_ref, dt_ref, kd_ref, qd_ref, vd_ref):
    h = h_ref[...] + pmod_ref[5:6, :] * (_rms(f_ref[...], -1) * gprev_ref[...])
    hn_ref[...] = h
    u = _rms(h, -1) * gpre_ref[...]
    u = u * (1.0 + mod_ref[1:2, :]) + mod_ref[0:1, :]
    ub = u.astype(BF16)
    zk = _dot(ub, wtm_ref[...])
    zt = _dot_nt(wfm_ref[...], ub)
    z_ref[...] = zk[:, 0:512].astype(z_ref.dtype)
    xbc_ref[...] = zk[:, 512:1536]
    dt_ref[...] = zk[:, 1536:1664]
    ck, sk = ckh_ref[...], skh_ref[...]
    for j in range(DIFF_HEADS):
        lo = 1664 + 128 * j
        kd_ref[j] = (zk[:, lo:lo + 128] * ck + zk[:, lo + 512:lo + 640] * sk).astype(BF16)
    cq, sq = cqh_ref[...], sqh_ref[...]
    qs = DIFF_SCALE * LOG2E
    zero64 = jnp.zeros((64, h.shape[0]), BF16)
    for hd in range(2 * DIFF_HEADS):
        q = ((zt[64 * hd:64 * hd + 64] * cq + zt[512 + 64 * hd:576 + 64 * hd] * sq) * qs).astype(BF16)
        if hd % 2 == 0:
            qd_ref[hd, 0:64, :] = q
            qd_ref[hd, 64:128, :] = zero64
        else:
            qd_ref[hd, 0:64, :] = zero64
            qd_ref[hd, 64:128, :] = q
    for j in range(DIFF_HEADS):
        vd_ref[j] = zt[1024 + 128 * j:1152 + 128 * j].astype(BF16)


def _rec_pre_weights(g, rec_w_in):
    d = rec_w_in.shape[0]
    o = np.cumsum([0, SSD_INNER, SSD_XBC, SSD_HEADS, SSD_HEADS, 2 * DIFF_HEADS * DIFF_DIM, 2 * DIFF_HEADS * DIFF_DIM,
                   DIFF_HEADS * DIFF_V])
    w_z, w_xbc, w_dtf, w_dtb, w_dq, w_dk, w_dv = (rec_w_in[:, o[i]:o[i + 1]] for i in range(7))
    src64, sgn64 = _rot_map(DIFF_DIM)

    def rot_heads(wcols):
        wh = wcols.reshape(d, 2 * DIFF_HEADS, DIFF_DIM)
        return (wh[:, :, src64] * sgn64).reshape(d, -1)

    w_dt = jnp.pad(jnp.concatenate([w_dtf, w_dtb], axis=1), ((0, 0), (0, LANES - 2 * SSD_HEADS)))
    wtm = jnp.concatenate([w_z, w_xbc, w_dt, w_dk, rot_heads(w_dk)], axis=1)
    wfm = jnp.concatenate([w_dq, rot_heads(w_dq), w_dv], axis=1)
    cos_h, sin_h = _rope_tables(g.n, DIFF_DIM, ROW_BLOCK)
    two = lambda x: jnp.concatenate([x, x], axis=1)
    return dict(wtm=wtm.astype(BF16), wfm=wfm.T.astype(BF16), cqh=cos_h.T, sqh=sin_h.T, ckh=two(cos_h), skh=two(sin_h))


def _rec_pre_call(g, h, f, prev_mod, prev_gain, mod, gpre, w):
    t, d = h.shape
    rb = ROW_BLOCK
    row = lambda i: (i, 0)
    modspec = pl.BlockSpec((None, 6, d), lambda i: (g.mod_row(i), 0, 0))
    tm_tab = pl.BlockSpec((rb, LANES), lambda i: (g.pos_block(i), 0))
    fm64 = pl.BlockSpec((64, rb), lambda i: (0, g.pos_block(i)))
    nd = 2 * DIFF_HEADS
    return pl.pallas_call(
        _rec_pre_kernel, grid=(g.n_blocks,),
        in_specs=[pl.BlockSpec((rb, d), row), pl.BlockSpec((rb, d), row), modspec, _full((1, d)), modspec,
                  _full(gpre.shape), _full(w["wtm"].shape), _full(w["wfm"].shape), fm64, fm64, tm_tab, tm_tab],
        out_specs=[
            pl.BlockSpec((rb, d), row), pl.BlockSpec((rb, SSD_INNER), row), pl.BlockSpec((rb, SSD_XBC), row), pl.BlockSpec((rb, LANES), row),
            pl.BlockSpec((DIFF_HEADS, rb, 128), lambda i: (0, i, 0)),
            pl.BlockSpec((nd, 128, rb), lambda i: (0, 0, i)),
            pl.BlockSpec((DIFF_HEADS, DIFF_V, rb), lambda i: (0, 0, i)),
        ],
        out_shape=[
            jax.ShapeDtypeStruct((t, d), F32),
            jax.ShapeDtypeStruct((t, SSD_INNER), BF16), jax.ShapeDtypeStruct((t, SSD_XBC), F32),
            jax.ShapeDtypeStruct((t, LANES), F32), jax.ShapeDtypeStruct((DIFF_HEADS, t, 128), BF16),
            jax.ShapeDtypeStruct((nd, 128, t), BF16), jax.ShapeDtypeStruct((DIFF_HEADS, DIFF_V, t), BF16),
        ],
        compiler_params=_cparams(("parallel",)), name="rec_pre",
    )(h, f, prev_mod, prev_gain.reshape(1, d), mod, gpre, w["wtm"], w["wfm"], w["cqh"], w["sqh"], w["ckh"], w["skh"])


def _conv_kernel(x_ref, prev_ref, next_ref, w_ref, b_ref, o_ref, *, seq_blocks, lat_blocks):
    i = pl.program_id(0)
    rb = x_ref.shape[0]
    pos = i % seq_blocks
    is_ctx = i >= lat_blocks
    first = jnp.logical_or(is_ctx, pos == 0)
    last = jnp.logical_or(is_ctx, pos == seq_blocks - 1)
    x = x_ref[...]
    prev_row = jnp.where(first, 0.0, prev_ref[7:8, :])
    next_row = jnp.where(last, 0.0, next_ref[0:1, :])
    rid = lax.broadcasted_iota(jnp.int32, x.shape, 0)
    x_prev = jnp.where(rid == 0, prev_row, pltpu.roll(x, 1, axis=0))
    x_next = jnp.where(rid == rb - 1, next_row, pltpu.roll(x, rb - 1, axis=0))
    y = w_ref[0:1, :] * x_prev + w_ref[1:2, :] * x + w_ref[2:3, :] * x_next + b_ref[...]
    o_ref[...] = _silu(y)


def _conv_call(g, xbc, conv_w, conv_b):
    t, ch = xbc.shape
    rb = g.c
    halo = 8
    per = rb // halo
    last_halo = t // halo - 1
    return pl.pallas_call(
        functools.partial(_conv_kernel, seq_blocks=g.n // rb, lat_blocks=g.b * g.n // rb),
        grid=(t // rb,),
        in_specs=[
            pl.BlockSpec((rb, ch), lambda i: (i, 0)),
            pl.BlockSpec((halo, ch), lambda i: (jnp.maximum(i * per - 1, 0), 0)),
            pl.BlockSpec((halo, ch), lambda i: (jnp.minimum((i + 1) * per, last_halo), 0)),
            _full((SSD_CONV, ch)), _full((1, ch)),
        ],
        out_specs=pl.BlockSpec((rb, ch), lambda i: (i, 0)),
        out_shape=jax.ShapeDtypeStruct((t, ch), F32),
        compiler_params=_cparams(("parallel",)), name="ssd_conv",
    )(xbc, xbc, xbc, conv_w, conv_b.reshape(1, ch))


def _ssd_kernel(xf_ref, dtf_ref, xb_ref, dtb_ref, bias_ref, alog_ref, tril_ref, triu_ref, eye_ref, yf_ref, yb_ref,
                sf_ref, sb_ref):
    @pl.when(pl.program_id(1) == 0)
    def _():
        sf_ref[...] = jnp.zeros_like(sf_ref)
        sb_ref[...] = jnp.zeros_like(sb_ref)

    _ssd_chunk(xf_ref, dtf_ref, bias_ref, alog_ref, tril_ref, triu_ref, eye_ref, yf_ref, sf_ref, 0, False)
    _ssd_chunk(xb_ref, dtb_ref, bias_ref, alog_ref, triu_ref, tril_ref, eye_ref, yb_ref, sb_ref, SSD_HEADS, True)


def _ssd_chunk(x_ref, dt_ref, bias_ref, alog_ref, tri_ref, trit_ref, eye_ref, y_ref, state_ref, lane_off, reverse):
    ln = SSD_CHUNK
    hi = lax.Precision.HIGHEST
    x = x_ref[:, 0:SSD_INNER]
    raw = dt_ref[...] + bias_ref[...]
    dt = jnp.maximum(raw, 0.0) + jnp.log1p(jnp.exp(-jnp.abs(raw)))
    adt = dt * (-jnp.exp(alog_ref[...]))
    acs = jnp.dot(tri_ref[...], adt, preferred_element_type=F32, precision=hi)
    acs_row = lax.dot_general(adt, trit_ref[...], (((0,), (0,)), ((), ())), preferred_element_type=F32, precision=hi)
    dt_row = lax.dot_general(dt, eye_ref[...], (((0,), (0,)), ((), ())), preferred_element_type=F32, precision=hi)
    tot = acs[0:1, :] if reverse else acs[ln - 1:ln, :]
    w_all = jnp.exp(tot - acs) * dt
    ea_all = jnp.exp(acs)
    etot = jnp.exp(tot)
    li = lax.broadcasted_iota(jnp.int32, (ln, ln), 0)
    si = lax.broadcasted_iota(jnp.int32, (ln, ln), 1)
    keep = (si >= li) if reverse else (si <= li)
    lane = lax.broadcasted_iota(jnp.int32, (ln, LANES), 1)
    left = lane < SSD_HEAD_DIM
    per_group = SSD_HEADS // SSD_GROUPS
    for gi in range(SSD_GROUPS):
        bm = x_ref[:, SSD_INNER + SSD_STATE * gi:SSD_INNER + SSD_STATE * (gi + 1)].astype(BF16)
        cm = x_ref[:, SSD_INNER + SSD_STATE * (SSD_GROUPS + gi):SSD_INNER + SSD_STATE * (SSD_GROUPS + gi + 1)]
        cm = cm.astype(BF16)
        cb = _dot_nt(cm, bm)
        for pr in range(per_group // 2):
            h0 = gi * per_group + 2 * pr
            xp = x[:, SSD_HEAD_DIM * h0:SSD_HEAD_DIM * (h0 + 2)]
            mats = []
            for hd in (h0, h0 + 1):
                c = lane_off + hd
                seg = acs[:, c:c + 1] - acs_row[c:c + 1, :]
                lmat = jnp.exp(jnp.where(keep, seg, -jnp.inf))
                mats.append((cb * lmat * dt_row[c:c + 1, :]).astype(BF16))
            xb = xp.astype(BF16)
            zero = jnp.zeros_like(xb)
            rhs = jnp.concatenate([jnp.where(left, xb, zero), jnp.where(left, zero, xb)], axis=0)
            y_diag = _dot(jnp.concatenate(mats, axis=1), rhs)
            st = state_ref[h0 // 2]
            c0 = lane_off + h0
            ea = jnp.where(left, ea_all[:, c0:c0 + 1], ea_all[:, c0 + 1:c0 + 2])
            y_off = _dot(cm, st.astype(BF16)) * ea
            y_ref[:, SSD_HEAD_DIM * h0:SSD_HEAD_DIM * (h0 + 2)] = y_diag + y_off
            wcol = jnp.where(left, w_all[:, c0:c0 + 1], w_all[:, c0 + 1:c0 + 2])
            cs = _dot_tn(bm, (xp * wcol).astype(BF16))
            dec = jnp.where(left[0:1, :], etot[:, c0:c0 + 1], etot[:, c0 + 1:c0 + 2])
            state_ref[h0 // 2] = st * dec + cs


def _ssd_call(g, xbc_act, dt, dt_bias_f, dt_bias_b, a_log_f, a_log_b):
    t = xbc_act.shape[0]
    ln = SSD_CHUNK
    cc = g.c // ln
    nl = g.n // ln
    ctx0 = g.b * g.n // ln
    fwd = lambda b, k: (jnp.where(k < cc, ctx0 + b * cc + k, b * nl + (k - cc)), 0)
    bwd = lambda b, k: (jnp.where(k < cc, ctx0 + b * cc + (cc - 1 - k), b * nl + (nl - 1 - (k - cc))), 0)
    idx = np.arange(ln)
    lower = (idx[:, None] >= idx[None, :]).astype(np.float32)
    lanes = lambda pf, pb: jnp.pad(jnp.concatenate([pf, pb]).reshape(1, -1), ((0, 0), (0, LANES - 2 * SSD_HEADS)))
    state = pltpu.VMEM((SSD_HEADS // 2, SSD_STATE, 2 * SSD_HEAD_DIM), F32)
    y_shape = jax.ShapeDtypeStruct((t, SSD_INNER), F32)
    return pl.pallas_call(
        _ssd_kernel, grid=(g.b, cc + nl),
        in_specs=[pl.BlockSpec((ln, SSD_XBC), fwd), pl.BlockSpec((ln, LANES), fwd),
                  pl.BlockSpec((ln, SSD_XBC), bwd), pl.BlockSpec((ln, LANES), bwd),
                  _full((1, LANES)), _full((1, LANES)), _full((ln, ln)), _full((ln, ln)), _full((ln, ln))],
        out_specs=[pl.BlockSpec((ln, SSD_INNER), fwd), pl.BlockSpec((ln, SSD_INNER), bwd)],
        out_shape=[y_shape, y_shape],
        scratch_shapes=[state, state],
        compiler_params=_cparams(("parallel", "arbitrary")),
        name="ssd_scan",
    )(xbc_act, dt, xbc_act, dt, lanes(dt_bias_f, dt_bias_b), lanes(a_log_f, a_log_b), jnp.asarray(lower),
      jnp.asarray(lower.T), jnp.eye(ln, dtype=F32))


def _diff_finish(o, tq, lq1_ref, lk1_ref, lq2_ref, lk2_ref, sub_ref, o_ref, lambda_init):
    lam = (jnp.exp(jnp.sum(lq1_ref[...] * lk1_ref[...], axis=1, keepdims=True))
           - jnp.exp(jnp.sum(lq2_ref[...] * lk2_ref[...], axis=1, keepdims=True)) + lambda_init)
    od = o[:, 0:tq] - lam * o[:, tq:2 * tq]
    o_ref[...] = ((_rms(od, 0) * sub_ref[...]) * (1.0 - lambda_init)).astype(o_ref.dtype)


def _diff_attn_kernel(qt_ref, *refs, lambda_init):
    o_ref, s_ref = refs[-2:]
    params = refs[-7:-2]
    kv = refs[:-7]
    half = len(kv) // 2
    segs = [(kv[i], kv[half + i]) for i in range(half)]
    qt = jnp.concatenate([qt_ref[0], qt_ref[1]], axis=1)
    o = _attend(qt, segs, s_ref, DIFF_V)
    _diff_finish(o, qt_ref.shape[2], *params, o_ref, lambda_init)


def _diff_attn_call(g, qd, kd, vd, lq1, lk1, lq2, lk2, subln, lambda_init):
    tq = DIFF_Q_TILE
    nq = g.n // tq
    ctx_blk = g.b * g.n // g.c
    vec = lambda a: a.reshape(1, -1)
    params = [_full((1, DIFF_DIM)), _full((1, DIFF_DIM)), _full((1, DIFF_DIM)), _full((1, DIFF_DIM)),
              _full((DIFF_V, 1))]
    pvals = (vec(lq1), vec(lk1), vec(lq2), vec(lk2), subln.reshape(-1, 1))
    klat = pl.BlockSpec((None, g.n, 128), lambda b, j, *_: (j, b, 0))
    kctx = pl.BlockSpec((None, g.c, 128), lambda b, j, *_: (j, ctx_blk + b, 0))
    vlat = pl.BlockSpec((None, DIFF_V, g.n), lambda b, j, *_: (j, 0, b))
    vctx = pl.BlockSpec((None, DIFF_V, g.c), lambda b, j, *_: (j, 0, ctx_blk + b))
    kern = functools.partial(_diff_attn_kernel, lambda_init=lambda_init)
    lat = pl.pallas_call(
        kern, grid=(g.b, DIFF_HEADS, nq),
        scratch_shapes=[pltpu.VMEM((g.n + g.c, 2 * tq), F32)],
        in_specs=[pl.BlockSpec((2, 128, tq), lambda b, j, qi: (j, 0, b * nq + qi)), klat, kctx, vlat, vctx] + params,
        out_specs=pl.BlockSpec((DIFF_V, tq), lambda b, j, qi: (j, b * nq + qi)),
        out_shape=jax.ShapeDtypeStruct((DIFF_HEADS * DIFF_V, g.b * g.n), BF16),
        compiler_params=_cparams(("parallel", "parallel", "arbitrary")),
        name="attn_diff",
    )(qd, kd, kd, vd, vd, *pvals)
    ctx = pl.pallas_call(
        kern, grid=(g.b, DIFF_HEADS),
        scratch_shapes=[pltpu.VMEM((g.c, 2 * g.c), F32)],
        in_specs=[pl.BlockSpec((2, 128, g.c), lambda b, j: (j, 0, ctx_blk + b)), kctx, vctx] + params,
        out_specs=pl.BlockSpec((DIFF_V, g.c), lambda b, j: (j, b)),
        out_shape=jax.ShapeDtypeStruct((DIFF_HEADS * DIFF_V, g.b * g.c), BF16),
        compiler_params=_cparams(("parallel", "parallel")),
        name="attn_diff_ctx",
    )(qd, kd, vd, *pvals)
    return jnp.concatenate([lat, ctx], axis=1)


def _rec_post_kernel(yf_ref, yb_ref, xs_ref, z_ref, od_ref, dskip_ref, gn_ref, wa_ref, wb_ref,
                     h_ref, mod_ref, gpost_ref, gffn_ref, rw_ref, hn_ref, v_ref, lg_ref):
    y = (yf_ref[...] + yb_ref[...] + xs_ref[...] * dskip_ref[...]) * _silu(z_ref[...].astype(F32))
    half = SSD_INNER // SSD_GROUPS
    o = _dot_tn(od_ref[...], wb_ref[...])
    for gi in range(SSD_GROUPS):
        lo = half * gi
        yn = (_rms(y[:, lo:lo + half], -1) * gn_ref[:, lo:lo + half]).astype(BF16)
        o = o + _dot(yn, wa_ref[lo:lo + half, :])
    _post_tail(o, h_ref, mod_ref, gpost_ref, gffn_ref, rw_ref, hn_ref, v_ref, lg_ref)


def _rec_post_call(g, yf, yb, xbc_act, zs, od, d_skip, ssd_norm, w_out, h, mod, gpost, gffn, router_w):
    d = h.shape[1]
    rb = ROW_BLOCK
    ins, outs, shapes = _post_specs(g, d)
    wo = w_out.astype(BF16)
    row512 = pl.BlockSpec((rb, SSD_INNER), lambda i: (i, 0))
    return pl.pallas_call(
        _rec_post_kernel, grid=(g.n_blocks,),
        in_specs=[row512, row512, row512, row512, pl.BlockSpec((SSD_INNER, rb), lambda i: (0, i)),
                  _full((1, SSD_INNER)), _full((1, SSD_INNER)), _full((SSD_INNER, d)), _full((SSD_INNER, d))] + ins,
        out_specs=outs, out_shape=shapes,
        compiler_params=_cparams(("parallel",)), name="rec_post",
    )(yf, yb, xbc_act, zs, od, jnp.repeat(d_skip, SSD_HEAD_DIM).reshape(1, -1), ssd_norm.reshape(1, -1),
      wo[:SSD_INNER], wo[SSD_INNER:], h, mod, gpost.reshape(1, d), gffn.reshape(1, d), router_w.T)


def _ffn_res_kernel(h_ref, f_ref, mod_ref, g_ref, o_ref):
    o_ref[...] = h_ref[...] + mod_ref[5:6, :] * (_rms(f_ref[...], -1) * g_ref[...])


def _ffn_res_call(g, h, f, mod, gain, n_rows):
    d = h.shape[1]
    rb = ROW_BLOCK
    row = pl.BlockSpec((rb, d), lambda i: (i, 0))
    return pl.pallas_call(
        _ffn_res_kernel, grid=(n_rows // rb,),
        in_specs=[row, row, pl.BlockSpec((None, 6, d), lambda i: (g.mod_row(i), 0, 0)), _full((1, d))],
        out_specs=row, out_shape=jax.ShapeDtypeStruct((n_rows, d), F32),
        compiler_params=_cparams(("parallel",)), name="ffn_res",
    )(h, f, mod, gain.reshape(1, d))


def kernel(x, c, ctx, c_ctx, ada_w, ada_b, norm_mix_pre, norm_mix_post, norm_ffn_pre, norm_ffn_post, mix_w_out, att_w_in, mla_q_norm, mla_w_q_up, mla_kv_norm, mla_w_kv_up, gqa_q_norm, gqa_k_norm, rec_w_in, ssd_conv_w, ssd_conv_b, ssd_dt_bias_f, ssd_dt_bias_b, ssd_a_log_f, ssd_a_log_b, ssd_d, ssd_norm, diff_lambda_q1, diff_lambda_k1, diff_lambda_q2, diff_lambda_k2, diff_subln, router_w, router_b, exp_w_gate, exp_w_up, exp_w_down, sh_w_gate, sh_w_up, sh_w_down):
    b, n, d = x.shape
    n_ctx = ctx.shape[1]
    depth = ada_w.shape[0]
    g = _Geom(b, n, n_ctx)
    mod_rows = -(-(b + 1) // 8) * 8
    c_all = jnp.concatenate([c, c_ctx[None, :], jnp.zeros((mod_rows - b - 1, d), F32)], axis=0)
    mods = _ada_call(c_all, ada_w, ada_b).reshape(depth, mod_rows, 6, d)
    h = jnp.concatenate([x.reshape(b * n, d), ctx.reshape(b * n_ctx, d)], axis=0)
    for i in range(depth):
        last = i == depth - 1
        jdx = i // 2
        mod = mods[i]
        if i % 2 == 0:
            w = _attn_pre_weights(g, att_w_in[jdx], mla_q_norm[jdx], mla_w_q_up[jdx], mla_kv_norm[jdx],
                                  mla_w_kv_up[jdx], gqa_q_norm[jdx], gqa_k_norm[jdx])
            qm, km, vm, qg, kg, vg = _attn_pre_call(g, h, mod, norm_mix_pre[i].reshape(1, d), w)
            oa = _attn_call(g, qm, km, vm, kv_heads=MLA_HEADS, shared_k=False, name="attn_mla")
            ob = _attn_call(g, qg, kg, vg, kv_heads=GQA_KV_HEADS, shared_k=True, name="attn_gqa")
            h, v, logits_t = _attn_post_call(g, oa, ob, mix_w_out[i], h, mod, norm_mix_post[i], norm_ffn_pre[i],
                                             router_w[i])
        else:
            lambda_init = 0.8 - 0.6 * math.exp(-0.3 * i)
            w = _rec_pre_weights(g, rec_w_in[jdx])
            h, zs, xbc, dt, kd, qd, vd = _rec_pre_call(g, h, f, mods[i - 1], norm_ffn_post[i - 1], mod,
                                                       norm_mix_pre[i].reshape(1, d), w)
            xbc_act = _conv_call(g, xbc, ssd_conv_w[jdx], ssd_conv_b[jdx])
            yf, yb = _ssd_call(g, xbc_act, dt, ssd_dt_bias_f[jdx], ssd_dt_bias_b[jdx], ssd_a_log_f[jdx],
                               ssd_a_log_b[jdx])
            od = _diff_attn_call(g, qd, kd, vd, diff_lambda_q1[jdx], diff_lambda_k1[jdx], diff_lambda_q2[jdx],
                                 diff_lambda_k2[jdx], diff_subln[jdx], lambda_init)
            h, v, logits_t = _rec_post_call(g, yf, yb, xbc_act, zs, od, ssd_d[jdx], ssd_norm[jdx], mix_w_out[i], h,
                                            mod, norm_mix_post[i], norm_ffn_pre[i], router_w[i])
        gates, rank, counts = _router_call(g, logits_t, router_b[i])
        f = _moe_call(g, v, gates, rank, counts, exp_w_gate[i], exp_w_up[i], exp_w_down[i],
                      sh_w_gate[i], sh_w_up[i], sh_w_down[i])
        if last or i % 2 == 1:
            h = _ffn_res_call(g, h, f, mod, norm_ffn_post[i], b * n if last else g.t)
    return h.reshape(b, n, d)
```

```python
import functools
import math

import numpy as np
import jax
import jax.numpy as jnp
from jax import lax
from jax.experimental import pallas as pl
from jax.experimental.pallas import tpu as pltpu

F32 = jnp.float32
BF16 = jnp.bfloat16
LOG2E = 1.4426950408889634

GRID_W = 64
ROPE_BASE = 10000.0
NORM_EPS = 1e-6

MLA_HEADS, MLA_Q_LORA, MLA_KV_LORA, MLA_NOPE, MLA_ROPE, MLA_V = 8, 256, 128, 64, 32, 64
MLA_SCALE = (MLA_NOPE + MLA_ROPE) ** -0.5
GQA_HEADS, GQA_KV_HEADS, GQA_DIM = 8, 2, 64
GQA_SCALE = GQA_DIM ** -0.5
SSD_HEADS, SSD_HEAD_DIM, SSD_GROUPS, SSD_STATE, SSD_CONV, SSD_CHUNK = 8, 64, 2, 128, 3, 128
SSD_INNER = SSD_HEADS * SSD_HEAD_DIM
SSD_XBC = SSD_INNER + 2 * SSD_GROUPS * SSD_STATE
DIFF_HEADS, DIFF_DIM = 4, 64
DIFF_V = 2 * DIFF_DIM
DIFF_SCALE = DIFF_DIM ** -0.5
N_EXPERTS, TOP_K, N_GROUPS, TOPK_GROUPS, EXPERT_FF, SHARED_FF = 64, 8, 8, 4, 256, 256
ROUTED_SCALE = 2.5

LANES = 128
VMEM_LIMIT_BYTES = 56 * 1024 * 1024

ROW_BLOCK = 512
ATTN_Q_TILE = 1024
DIFF_Q_TILE = 512
ATTN_KV_CHUNK = 512
ATTN_EXP_ROWS = 4096
MOE_TOKENS = 768
MOE_ROWS = 128
MOE_PAIR = 8
MOE_SLOTS = 8


def _cparams(sem, flags=None):
    return pltpu.CompilerParams(dimension_semantics=sem, vmem_limit_bytes=VMEM_LIMIT_BYTES, flags=flags)


def _rms(x, axis):
    return x * lax.rsqrt(jnp.mean(x * x, axis=axis, keepdims=True) + NORM_EPS)


def _silu(x):
    return x * jax.nn.sigmoid(x)


def _dot(a, b):
    return jnp.dot(a, b, preferred_element_type=F32)


def _dot_nt(a, b):
    return lax.dot_general(a, b, (((1,), (1,)), ((), ())), preferred_element_type=F32)


def _dot_tn(a, b):
    return lax.dot_general(a, b, (((0,), (0,)), ((), ())), preferred_element_type=F32)


def _ada_kernel(c_ref, w_ref, b_ref, o_ref):
    s = _silu(c_ref[...])
    o_ref[...] = jnp.dot(s, w_ref[...], preferred_element_type=F32, precision=lax.Precision.HIGHEST) + b_ref[...]


def _ada_call(c_all, ada_w, ada_b):
    depth, d, six_d = ada_w.shape
    rows = c_all.shape[0]
    cols = six_d // 4
    return pl.pallas_call(
        _ada_kernel,
        grid=(depth, six_d // cols),
        in_specs=[
            pl.BlockSpec((rows, d), lambda i, j: (0, 0)),
            pl.BlockSpec((None, d, cols), lambda i, j: (i, 0, j)),
            pl.BlockSpec((None, 1, cols), lambda i, j: (i, 0, j)),
        ],
        out_specs=pl.BlockSpec((None, rows, cols), lambda i, j: (i, 0, j)),
        out_shape=jax.ShapeDtypeStruct((depth, rows, six_d), F32),
        compiler_params=_cparams(("arbitrary", "arbitrary")),
        name="ada_mod",
    )(c_all, ada_w, ada_b.reshape(depth, 1, six_d))


def _attn_pre_kernel(hl_ref, hc_ref, mod_ref, gpre_ref, wfm_ref, wtm_ref, qn_ref, kvn_col_ref, kvn_row_ref, wq_ref,
                     wk_ref, wv_ref, gq_g_ref, gq_gr_ref, gk_g_ref, gk_gr_ref,
                     cqm_ref, sqm_ref, cqh_ref, sqh_ref, ckm_ref, skm_ref, ckh_ref, skh_ref,
                     qm_ref, km_ref, vm_ref, qg_ref, kg_ref, vg_ref, *, lat_blocks):
    h = _pick_part(lat_blocks, hl_ref, hc_ref)
    u = _rms(h, -1) * gpre_ref[...]
    u = u * (1.0 + mod_ref[1:2, :]) + mod_ref[0:1, :]
    ub = u.astype(BF16)
    zt = _dot_nt(wfm_ref[...], ub)
    zk = _dot(ub, wtm_ref[...])

    qn = (_rms(zt[0:256], 0) * qn_ref[...]).astype(BF16)
    qt = _dot(wq_ref[...], qn)
    cq, sq = cqm_ref[...], sqm_ref[...]
    qs = MLA_SCALE * LOG2E
    zero32 = jnp.zeros((32, h.shape[0]), BF16)
    for hd in range(MLA_HEADS):
        pe = qt[512 + 32 * hd:544 + 32 * hd] * cq + qt[768 + 32 * hd:800 + 32 * hd] * sq
        qm_ref[hd, 0:64, :] = (qt[64 * hd:64 * hd + 64] * qs).astype(BF16)
        qm_ref[hd, 64:96, :] = (pe * qs).astype(BF16)
        qm_ref[hd, 96:128, :] = zero32

    kvn_t = (_rms(zt[256:384], 0) * kvn_col_ref[...]).astype(BF16)
    vt = _dot(wv_ref[...], kvn_t)
    for hd in range(MLA_HEADS):
        vm_ref[hd] = vt[64 * hd:64 * hd + 64].astype(BF16)
    kvn = (_rms(zk[:, 0:128], -1) * kvn_row_ref[...]).astype(BF16)
    kn = _dot(kvn, wk_ref[...])
    kpe = zk[:, 128:256] * ckm_ref[...] + zk[:, 256:384] * skm_ref[...]
    for hd in range(MLA_HEADS):
        km_ref[hd] = (kn[:, 128 * hd:128 * hd + 128] + kpe).astype(BF16)

    cqh, sqh = cqh_ref[...], sqh_ref[...]
    gs = GQA_SCALE * LOG2E
    zero64 = jnp.zeros((64, h.shape[0]), BF16)
    grp = GQA_HEADS // GQA_KV_HEADS
    for hd in range(GQA_HEADS):
        raw = zt[384 + 64 * hd:448 + 64 * hd]
        rot = zt[896 + 64 * hd:960 + 64 * hd]
        r = lax.rsqrt(jnp.mean(raw * raw, axis=0, keepdims=True) + NORM_EPS)
        q = (raw * r * gq_g_ref[...]) * cqh + (rot * r * gq_gr_ref[...]) * sqh
        q = (q * gs).astype(BF16)
        if hd // grp == 0:
            qg_ref[hd, 0:64, :] = q
            qg_ref[hd, 64:128, :] = zero64
        else:
            qg_ref[hd, 0:64, :] = zero64
            qg_ref[hd, 64:128, :] = q
    for kvh in range(GQA_KV_HEADS):
        vg_ref[kvh] = zt[1408 + 64 * kvh:1472 + 64 * kvh].astype(BF16)

    gk, gkr = zk[:, 384:512], zk[:, 512:640]
    lane = lax.broadcasted_iota(jnp.int32, gk.shape, 1)
    lo = lane < 64
    sq0 = jnp.sum(jnp.where(lo, gk * gk, 0.0), axis=-1, keepdims=True)
    sq1 = jnp.sum(jnp.where(lo, 0.0, gk * gk), axis=-1, keepdims=True)
    r = jnp.where(lo, lax.rsqrt(sq0 / GQA_DIM + NORM_EPS), lax.rsqrt(sq1 / GQA_DIM + NORM_EPS))
    kg_ref[...] = ((gk * r * gk_g_ref[...]) * ckh_ref[...] + (gkr * r * gk_gr_ref[...]) * skh_ref[...]).astype(BF16)


def _rot_map(dim):
    q = dim // 4
    j = np.arange(dim)
    even = (j // q) % 2 == 0
    return np.where(even, j + q, j - q), np.where(even, -1.0, 1.0).astype(np.float32)


def _rope_tables(n_tokens, dim, pad_rows):
    rows = n_tokens // GRID_W
    row = jnp.repeat(jnp.arange(rows, dtype=F32), GRID_W)
    col = jnp.tile(jnp.arange(GRID_W, dtype=F32), rows)
    half = dim // 2
    inv = ROPE_BASE ** (-(jnp.arange(half // 2, dtype=F32) * 2.0 / half))
    ang_r = row[:, None] * inv
    ang_c = col[:, None] * inv
    ang = jnp.concatenate([ang_r, ang_r, ang_c, ang_c], axis=-1)
    cos = jnp.concatenate([jnp.cos(ang), jnp.ones((pad_rows, dim), F32)], axis=0)
    sin = jnp.concatenate([jnp.sin(ang), jnp.zeros((pad_rows, dim), F32)], axis=0)
    return cos, sin


class _Geom:
    def __init__(self, b, n, c):
        assert n % ROW_BLOCK == 0 and (b * c) % ROW_BLOCK == 0 and n % c == 0 and n % GRID_W == 0
        self.b, self.n, self.c = b, n, c
        self.t = b * n + b * c
        self.lat_blocks = b * n // ROW_BLOCK
        self.blocks_per_seq = n // ROW_BLOCK
        self.n_blocks = self.t // ROW_BLOCK
        assert self.t % MOE_TOKENS == 0

    def mod_row(self, i):
        return jnp.where(i < self.lat_blocks, i // self.blocks_per_seq, self.b)

    def pos_block(self, i):
        return jnp.where(i < self.lat_blocks, i % self.blocks_per_seq, self.blocks_per_seq)


def _full(shape):
    nd = len(shape)
    return pl.BlockSpec(shape, lambda *_: (0,) * nd)


def _part_specs(g, block, axis):
    def spec(latent):
        def index(i):
            j = jnp.minimum(i, g.lat_blocks - 1) if latent else jnp.maximum(i - g.lat_blocks, 0)
            return tuple(j if a == axis else 0 for a in range(len(block)))
        return pl.BlockSpec(block, index)
    return [spec(True), spec(False)]


def _pick_part(lat_blocks, lat_ref, ctx_ref):
    return jnp.where(pl.program_id(0) < lat_blocks, lat_ref[...], ctx_ref[...])


def _attn_pre_call(g, h_parts, mod, gpre, w):
    d = h_parts[0].shape[1]
    t = g.t
    rb = ROW_BLOCK
    row = lambda i: (i, 0)
    tm_tab = pl.BlockSpec((rb, LANES), lambda i: (g.pos_block(i), 0))
    fm32 = pl.BlockSpec((32, rb), lambda i: (0, g.pos_block(i)))
    fm64 = pl.BlockSpec((64, rb), lambda i: (0, g.pos_block(i)))
    in_specs = _part_specs(g, (rb, d), 0) + [
        pl.BlockSpec((None, 6, d), lambda i: (g.mod_row(i), 0, 0)),
        _full(gpre.shape), _full(w["wfm"].shape), _full(w["wtm"].shape), _full(w["qn"].shape),
        _full(w["kvn_col"].shape), _full(w["kvn_row"].shape), _full(w["wq"].shape), _full(w["wk"].shape),
        _full(w["wv"].shape), _full(w["gq_g"].shape), _full(w["gq_gr"].shape), _full(w["gk_g"].shape),
        _full(w["gk_gr"].shape),
        fm32, fm32, fm64, fm64, tm_tab, tm_tab, tm_tab, tm_tab,
    ]
    out_shape = [
        jax.ShapeDtypeStruct((MLA_HEADS, 128, t), BF16),
        jax.ShapeDtypeStruct((MLA_HEADS, t, 128), BF16),
        jax.ShapeDtypeStruct((MLA_HEADS, MLA_V, t), BF16),
        jax.ShapeDtypeStruct((GQA_HEADS, 128, t), BF16),
        jax.ShapeDtypeStruct((t, 128), BF16),
        jax.ShapeDtypeStruct((GQA_KV_HEADS, GQA_DIM, t), BF16),
    ]
    out_specs = [
        pl.BlockSpec((MLA_HEADS, 128, rb), lambda i: (0, 0, i)),
        pl.BlockSpec((MLA_HEADS, rb, 128), lambda i: (0, i, 0)),
        pl.BlockSpec((MLA_HEADS, MLA_V, rb), lambda i: (0, 0, i)),
        pl.BlockSpec((GQA_HEADS, 128, rb), lambda i: (0, 0, i)),
        pl.BlockSpec((rb, 128), row),
        pl.BlockSpec((GQA_KV_HEADS, GQA_DIM, rb), lambda i: (0, 0, i)),
    ]
    return pl.pallas_call(
        functools.partial(_attn_pre_kernel, lat_blocks=g.lat_blocks),
        grid=(g.n_blocks,), in_specs=in_specs, out_specs=out_specs, out_shape=out_shape,
        compiler_params=_cparams(("parallel",)), name="attn_pre",
    )(*h_parts, mod, gpre, w["wfm"], w["wtm"], w["qn"], w["kvn_col"], w["kvn_row"], w["wq"], w["wk"], w["wv"],
      w["gq_g"], w["gq_gr"], w["gk_g"], w["gk_gr"],
      w["cqm"], w["sqm"], w["cqh"], w["sqh"], w["ckm"], w["skm"], w["ckh"], w["skh"])


def _attn_pre_weights(g, att_w_in, q_norm, w_q_up, kv_norm, w_kv_up, gqa_qn, gqa_kn):
    d = att_w_in.shape[0]
    o = np.cumsum([0, MLA_Q_LORA, MLA_KV_LORA, MLA_ROPE, GQA_HEADS * GQA_DIM, GQA_KV_HEADS * GQA_DIM,
                   GQA_KV_HEADS * GQA_DIM])
    w_qlat, w_kvlat, w_kpe, w_gq, w_gk, w_gv = (att_w_in[:, o[i]:o[i + 1]] for i in range(6))
    src32, sgn32 = _rot_map(MLA_ROPE)
    src64, sgn64 = _rot_map(GQA_DIM)

    def rot_heads(wcols, heads, dim, src, sgn):
        wh = wcols.reshape(d, heads, dim)
        return (wh[:, :, src] * sgn).reshape(d, heads * dim)

    wfm = jnp.concatenate([w_qlat, w_kvlat, w_gq, rot_heads(w_gq, GQA_HEADS, GQA_DIM, src64, sgn64), w_gv], axis=1)
    zpad = lambda x, lo, hi: jnp.pad(x, ((0, 0), (lo, hi)))
    wtm = jnp.concatenate([
        w_kvlat, zpad(w_kpe, 64, 32), zpad(w_kpe[:, src32] * sgn32, 64, 32),
        w_gk, rot_heads(w_gk, GQA_KV_HEADS, GQA_DIM, src64, sgn64)], axis=1)
    wq = w_q_up.reshape(MLA_Q_LORA, MLA_HEADS, MLA_NOPE + MLA_ROPE)
    wq_pe = wq[:, :, MLA_NOPE:]
    wq_all = jnp.concatenate([
        wq[:, :, :MLA_NOPE].reshape(MLA_Q_LORA, -1), wq_pe.reshape(MLA_Q_LORA, -1),
        (wq_pe[:, :, src32] * sgn32).reshape(MLA_Q_LORA, -1)], axis=1)
    wkv = w_kv_up.reshape(MLA_KV_LORA, MLA_HEADS, MLA_NOPE + MLA_V)
    wk = jnp.pad(wkv[:, :, :MLA_NOPE], ((0, 0), (0, 0), (0, 128 - MLA_NOPE))).reshape(MLA_KV_LORA, -1)
    wv = wkv[:, :, MLA_NOPE:].reshape(MLA_KV_LORA, -1)
    cos_m, sin_m = _rope_tables(g.n, MLA_ROPE, ROW_BLOCK)
    cos_h, sin_h = _rope_tables(g.n, GQA_DIM, ROW_BLOCK)
    two = lambda x: jnp.concatenate([x, x], axis=1)
    return dict(
        wfm=wfm.T.astype(BF16), wtm=wtm.astype(BF16),
        qn=q_norm.reshape(-1, 1), kvn_col=kv_norm.reshape(-1, 1), kvn_row=kv_norm.reshape(1, -1),
        wq=wq_all.T.astype(BF16), wk=wk.astype(BF16), wv=wv.T.astype(BF16),
        gq_g=gqa_qn.reshape(-1, 1), gq_gr=gqa_qn[src64].reshape(-1, 1),
        gk_g=two(gqa_kn.reshape(1, -1)), gk_gr=two(gqa_kn[src64].reshape(1, -1)),
        cqm=cos_m.T, sqm=sin_m.T, cqh=cos_h.T, sqh=sin_h.T,
        ckm=zpad(cos_m, 64, 32), skm=zpad(sin_m, 64, 32), ckh=two(cos_h), skh=two(sin_h),
    )


def _fold8(x, op):
    r, w = x.shape
    return op(x.reshape(r // 8, 8, w), axis=0)


def _attend(qt, segments, s_ref, dv):
    tq = qt.shape[1]
    pieces, off = [], 0
    for k_ref, vt_ref in segments:
        n = k_ref.shape[0]
        for lo in range(0, n, ATTN_KV_CHUNK):
            rows = min(ATTN_KV_CHUNK, n - lo)
            pieces.append((off, lo, rows, k_ref, vt_ref))
            off += rows
    m8 = None
    for so, lo, rows, k_ref, _ in pieces:
        s = _dot(k_ref[lo:lo + rows, :], qt)
        s_ref[so:so + rows, :] = s
        part = _fold8(s, jnp.max)
        m8 = part if m8 is None else jnp.maximum(m8, part)
    m = jnp.max(m8, axis=0, keepdims=True)
    acc = jnp.zeros((dv + 16, tq), F32)
    off = 0
    for _, vt_ref in segments:
        n = vt_ref.shape[1]
        step = min(n, ATTN_EXP_ROWS)

        def body(i, acc, off=off, vt_ref=vt_ref, step=step):
            base = i * step if isinstance(i, int) else pl.multiple_of(i * step, step)
            for lo in range(0, step, ATTN_KV_CHUNK):
                rows = min(ATTN_KV_CHUNK, step - lo)
                p = jnp.exp2((s_ref[pl.ds(off + base + lo, rows), :] - m).astype(BF16))
                vt1 = jnp.concatenate([vt_ref[:, pl.ds(base + lo, rows)], jnp.ones((16, rows), BF16)], axis=0)
                acc = acc + _dot(vt1, p)
            return acc

        acc = body(0, acc) if n == step else lax.fori_loop(0, n // step, body, acc)
        off += n
    return acc[0:dv] / acc[dv:dv + 1]


def _attn_kernel(qt_ref, *refs, dv):
    o_ref, s_ref = refs[-2:]
    kv = refs[:-2]
    half = len(kv) // 2
    segs = [(kv[i], kv[half + i]) for i in range(half)]
    o_ref[...] = _attend(qt_ref[...], segs, s_ref, dv).astype(o_ref.dtype)


def _attn_call(g, qt, k, vt, *, kv_heads, shared_k, name):
    heads = qt.shape[0]
    dv = vt.shape[1]
    grp = heads // kv_heads
    tq = ATTN_Q_TILE
    nq = g.n // tq
    ctx_blk = g.b * g.n // g.c
    if shared_k:
        klat = pl.BlockSpec((g.n, 128), lambda b, h, *_: (b, 0))
        kctx = pl.BlockSpec((g.c, 128), lambda b, h, *_: (ctx_blk + b, 0))
    else:
        klat = pl.BlockSpec((None, g.n, 128), lambda b, h, *_: (h // grp, b, 0))
        kctx = pl.BlockSpec((None, g.c, 128), lambda b, h, *_: (h // grp, ctx_blk + b, 0))
    vlat = pl.BlockSpec((None, dv, g.n), lambda b, h, *_: (h // grp, 0, b))
    vctx = pl.BlockSpec((None, dv, g.c), lambda b, h, *_: (h // grp, 0, ctx_blk + b))
    lat = pl.pallas_call(
        functools.partial(_attn_kernel, dv=dv),
        grid=(g.b, heads, nq),
        scratch_shapes=[pltpu.VMEM((g.n + g.c, tq), F32)],
        in_specs=[pl.BlockSpec((None, 128, tq), lambda b, h, qi: (h, 0, b * nq + qi)), klat, kctx, vlat, vctx],
        out_specs=pl.BlockSpec((dv, tq), lambda b, h, qi: (h, b * nq + qi)),
        out_shape=jax.ShapeDtypeStruct((heads * dv, g.b * g.n), BF16),
        compiler_params=_cparams(("parallel", "parallel", "arbitrary")),
        name=name,
    )(qt, k, k, vt, vt)
    ctx = pl.pallas_call(
        functools.partial(_attn_kernel, dv=dv),
        grid=(g.b, heads),
        scratch_shapes=[pltpu.VMEM((g.c, g.c), F32)],
        in_specs=[pl.BlockSpec((None, 128, g.c), lambda b, h: (h, 0, ctx_blk + b)), kctx, vctx],
        out_specs=pl.BlockSpec((dv, g.c), lambda b, h: (h, b)),
        out_shape=jax.ShapeDtypeStruct((heads * dv, g.b * g.c), BF16),
        compiler_params=_cparams(("parallel", "parallel")),
        name=name + "_ctx",
    )(qt, k, vt)
    return lat, ctx


def _split_bf16(x):
    hi = x.astype(BF16)
    return hi, (x - hi.astype(F32)).astype(BF16)


def _post_tail(o, h, mod_ref, gpost_ref, gffn_ref, rw_ref, hn_ref, v_ref, lg_ref):
    hn = h + mod_ref[2:3, :] * (_rms(o, -1) * gpost_ref[...])
    hn_ref[...] = hn
    v = (_rms(hn, -1) * gffn_ref[...]) * (1.0 + mod_ref[4:5, :]) + mod_ref[3:4, :]
    v_hi, v_lo = _split_bf16(v)
    v_ref[...] = v_hi
    w_hi, w_lo = _split_bf16(rw_ref[...])
    lg_ref[...] = _dot_nt(w_hi, v_hi) + (_dot_nt(w_hi, v_lo) + _dot_nt(w_lo, v_hi))


def _attn_post_kernel(oal_ref, oac_ref, obl_ref, obc_ref, wa_ref, wb_ref, hl_ref, hc_ref, mod_ref, gpost_ref, gffn_ref,
                      rw_ref, hn_ref, v_ref, lg_ref, *, lat_blocks):
    oa = _pick_part(lat_blocks, oal_ref, oac_ref)
    ob = _pick_part(lat_blocks, obl_ref, obc_ref)
    o = _dot_tn(oa, wa_ref[...]) + _dot_tn(ob, wb_ref[...])
    h = _pick_part(lat_blocks, hl_ref, hc_ref)
    _post_tail(o, h, mod_ref, gpost_ref, gffn_ref, rw_ref, hn_ref, v_ref, lg_ref)


def _post_specs(g, d):
    rb = ROW_BLOCK
    ins = [
        pl.BlockSpec((None, 6, d), lambda i: (g.mod_row(i), 0, 0)),
        _full((1, d)), _full((1, d)), _full((N_EXPERTS, d)),
    ]
    outs = [pl.BlockSpec((rb, d), lambda i: (i, 0)), pl.BlockSpec((rb, d), lambda i: (i, 0)),
            pl.BlockSpec((N_EXPERTS, rb), lambda i: (0, i))]
    shapes = [jax.ShapeDtypeStruct((g.t, d), F32), jax.ShapeDtypeStruct((g.t, d), BF16),
              jax.ShapeDtypeStruct((N_EXPERTS, g.t), F32)]
    return ins, outs, shapes


def _attn_post_call(g, oa_parts, ob_parts, w_out, h_parts, mod, gpost, gffn, router_w):
    d = h_parts[0].shape[1]
    rb = ROW_BLOCK
    half = oa_parts[0].shape[0]
    ins, outs, shapes = _post_specs(g, d)
    wo = w_out.astype(BF16)
    o_specs = _part_specs(g, (half, rb), 1)
    return pl.pallas_call(
        functools.partial(_attn_post_kernel, lat_blocks=g.lat_blocks), grid=(g.n_blocks,),
        in_specs=o_specs + o_specs + [_full((half, d)), _full((half, d))] + _part_specs(g, (rb, d), 0) + ins,
        out_specs=outs, out_shape=shapes,
        compiler_params=_cparams(("parallel",)), name="attn_post",
    )(*oa_parts, *ob_parts, wo[:half], wo[half:], *h_parts, mod, gpost.reshape(1, d), gffn.reshape(1, d), router_w.T)


def _router_kernel(lg_ref, bias_ref, tri_ref, gate_ref, rank_ref, cnt_ref):
    tb = lg_ref.shape[1]
    per = N_EXPERTS // N_GROUPS
    shp = (N_GROUPS, per, tb)
    scores = jax.nn.sigmoid(lg_ref[...])
    s3 = scores.reshape(shp)
    sel = (scores + bias_ref[...]).reshape(shp)
    sub = lax.broadcasted_iota(jnp.int32, shp, 1)
    grp = lax.broadcasted_iota(jnp.int32, shp, 0)
    neg = -jnp.inf
    m1 = jnp.max(sel, axis=1, keepdims=True)
    i1 = jnp.min(jnp.where(sel == m1, sub, per), axis=1, keepdims=True)
    m2 = jnp.max(jnp.where(sub == i1, neg, sel), axis=1, keepdims=True)
    cur = jnp.broadcast_to(m1 + m2, shp)
    gmask = jnp.zeros(shp, F32)
    for _ in range(TOPK_GROUPS):
        gm = jnp.max(cur, axis=0, keepdims=True)
        gi = jnp.min(jnp.where(cur == gm, grp, N_GROUPS), axis=0, keepdims=True)
        pick = grp == gi
        gmask = jnp.where(pick, 1.0, gmask)
        cur = jnp.where(pick, neg, cur)
    masked = jnp.where(gmask > 0.0, sel, neg)
    eidx = grp * per + sub
    chosen = jnp.zeros(shp, F32)
    for _ in range(TOP_K):
        mx = jnp.max(jnp.max(masked, axis=1, keepdims=True), axis=0, keepdims=True)
        cand = jnp.where(masked == mx, eidx, N_EXPERTS)
        ei = jnp.min(jnp.min(cand, axis=1, keepdims=True), axis=0, keepdims=True)
        pick = eidx == ei
        chosen = jnp.where(pick, 1.0, chosen)
        masked = jnp.where(pick, neg, masked)
    top_w = jnp.where(chosen > 0.0, s3, 0.0)
    denom = jnp.sum(jnp.sum(top_w, axis=1, keepdims=True), axis=0, keepdims=True)
    gate_ref[...] = (top_w / denom * ROUTED_SCALE).reshape(N_EXPERTS, tb)
    ch2 = chosen.reshape(N_EXPERTS, tb)
    chb = ch2.astype(BF16)
    before = _dot(chb, tri_ref[...])
    rank_ref[...] = jnp.where(ch2 > 0.0, before, -1.0).astype(jnp.int32)
    cnt_ref[...] = _dot(chb, jnp.ones((tb, LANES), BF16))


def _router_call(g, logits_t, router_b):
    tb = MOE_TOKENS
    nsb = g.t // tb
    tri = (np.arange(tb)[:, None] < np.arange(tb)[None, :]).astype(np.float32)
    blk = pl.BlockSpec((N_EXPERTS, tb), lambda s: (0, s))
    gates, rank, cnt = pl.pallas_call(
        _router_kernel, grid=(nsb,),
        in_specs=[blk, _full((N_EXPERTS, 1)), _full((tb, tb))],
        out_specs=[blk, blk, pl.BlockSpec((None, N_EXPERTS, LANES), lambda s: (s, 0, 0))],
        out_shape=[jax.ShapeDtypeStruct((N_EXPERTS, g.t), F32), jax.ShapeDtypeStruct((N_EXPERTS, g.t), jnp.int32),
                   jax.ShapeDtypeStruct((nsb, N_EXPERTS, LANES), F32)],
        compiler_params=_cparams(("parallel",)), name="router",
    )(logits_t, router_b.reshape(-1, 1), jnp.asarray(tri, BF16))
    return gates, rank, cnt[:, :, 0].astype(jnp.int32).reshape(-1)


def _moe_kernel(cnt_ref, v_ref, gate_ref, rank_ref, wg_ref, wu_ref, wd_ref, sg_ref, su_ref, sd_ref, o_ref,
                psel_ref, ysel_ref, slot_ref):
    s, j = pl.program_id(0), pl.program_id(1)
    tb = v_ref.shape[0]
    rows = MOE_ROWS

    @pl.when(j == 0)
    def _():
        vb = v_ref[...]
        hid = (_silu(_dot(vb, sg_ref[...])) * _dot(vb, su_ref[...])).astype(BF16)
        o_ref[...] = _dot(hid, sd_ref[...])
        psel_ref[...] = jnp.zeros_like(psel_ref)
        ysel_ref[...] = jnp.zeros_like(ysel_ref)
        slot_ref[0] = 0

    row_id = lax.broadcasted_iota(jnp.int32, (rows, tb), 0)

    def flush(n_slots):
        stage_row = lax.broadcasted_iota(jnp.int32, psel_ref.shape, 0)
        sel = jnp.where(stage_row < n_slots * rows, psel_ref[...], jnp.zeros_like(psel_ref))
        o_ref[...] += _dot_tn(sel, ysel_ref[...])

    ends, total = [], 0
    for e in range(MOE_PAIR):
        total = total + (cnt_ref[s * N_EXPERTS + j * MOE_PAIR + e] + rows - 1) // rows
        ends.append(total)

    def select(i):
        e = sum((i >= end).astype(jnp.int32) for end in ends[:-1])
        first = sum(jnp.where(e == k + 1, ends[k], 0) for k in range(MOE_PAIR - 1))
        ex = j * MOE_PAIR + e
        rk = jnp.where(i < total, rank_ref[pl.ds(ex, 1), :] - (i - first) * rows, -1)
        hit = row_id == rk
        onehot = jnp.where(hit, 1.0, 0.0).astype(BF16)
        w_row = jnp.sum(jnp.where(hit, gate_ref[pl.ds(ex, 1), :], 0.0), axis=1, keepdims=True)
        return e, onehot, w_row

    def body(it, slot):
        picks = [select(2 * it), select(2 * it + 1)]
        both = jnp.concatenate([p[1] for p in picks], axis=0)
        at = pl.multiple_of(slot * rows, rows)
        psel_ref[pl.ds(at, 2 * rows), :] = both
        xs = _dot(both, v_ref[...]).astype(BF16)
        hids = [(_silu(_dot(xs[k * rows:(k + 1) * rows], wg_ref[e])) * _dot(xs[k * rows:(k + 1) * rows], wu_ref[e]))
                .astype(BF16) for k, (e, _, _) in enumerate(picks)]
        outs = [(_dot(hid, wd_ref[e]) * w_row).astype(BF16) for hid, (e, _, w_row) in zip(hids, picks)]
        ysel_ref[pl.ds(at, 2 * rows), :] = jnp.concatenate(outs, axis=0)

        @pl.when(slot == MOE_SLOTS - 2)
        def _():
            flush(MOE_SLOTS)

        return jnp.where(slot == MOE_SLOTS - 2, 0, slot + 2)

    slot = lax.fori_loop(0, (total + 1) // 2, body, slot_ref[0])
    slot_ref[0] = slot

    @pl.when(jnp.logical_and(j == pl.num_programs(1) - 1, slot > 0))
    def _():
        flush(slot)


def _moe_call(g, v, gates, rank, counts, wg, wu, wd, sg, su, sd):
    t, d = v.shape
    tb = MOE_TOKENS
    ff = wg.shape[2]
    grid_spec = pltpu.PrefetchScalarGridSpec(
        num_scalar_prefetch=1,
        grid=(t // tb, N_EXPERTS // MOE_PAIR),
        in_specs=[
            pl.BlockSpec((tb, d), lambda s, j, c: (s, 0)),
            pl.BlockSpec((N_EXPERTS, tb), lambda s, j, c: (0, s)),
            pl.BlockSpec((N_EXPERTS, tb), lambda s, j, c: (0, s)),
            pl.BlockSpec((MOE_PAIR, d, ff), lambda s, j, c: (j, 0, 0)),
            pl.BlockSpec((MOE_PAIR, d, ff), lambda s, j, c: (j, 0, 0)),
            pl.BlockSpec((MOE_PAIR, ff, d), lambda s, j, c: (j, 0, 0)),
            pl.BlockSpec((d, sg.shape[1]), lambda s, j, c: (0, 0)),
            pl.BlockSpec((d, su.shape[1]), lambda s, j, c: (0, 0)),
            pl.BlockSpec((sd.shape[0], d), lambda s, j, c: (0, 0)),
        ],
        out_specs=pl.BlockSpec((tb, d), lambda s, j, c: (s, 0)),
        scratch_shapes=[pltpu.VMEM((MOE_SLOTS * MOE_ROWS, tb), BF16), pltpu.VMEM((MOE_SLOTS * MOE_ROWS, d), BF16),
                        pltpu.SMEM((1,), jnp.int32)],
    )
    return pl.pallas_call(
        _moe_kernel, grid_spec=grid_spec, out_shape=jax.ShapeDtypeStruct((t, d), F32),
        compiler_params=_cparams(("parallel", "arbitrary")), name="moe",
    )(counts, v, gates, rank, wg.astype(BF16), wu.astype(BF16), wd.astype(BF16),
      sg.astype(BF16), su.astype(BF16), sd.astype(BF16))


def _rec_pre_kernel(h_ref, f_ref, pmod_ref, gprev_ref, mod_ref, gpre_ref, wtm_ref, wfm_ref, cqh_ref, sqh_ref, ckh_ref,
                    skh_ref, hn_ref, z_ref, xbc_ref, dt_ref, kd_ref, qd_ref, vd_ref):
    h = h_ref[...] + pmod_ref[5:6, :] * (_rms(f_ref[...], -1) * gprev_ref[...])
    hn_ref[...] = h
    u = _rms(h, -1) * gpre_ref[...]
    u = u * (1.0 + mod_ref[1:2, :]) + mod_ref[0:1, :]
    ub = u.astype(BF16)
    zk = _dot(ub, wtm_ref[...])
    zt = _dot_nt(wfm_ref[...], ub)
    z_ref[...] = zk[:, 0:512].astype(z_ref.dtype)
    xbc_ref[...] = zk[:, 512:1536]
    dt_ref[...] = zk[:, 1536:1664]
    ck, sk = ckh_ref[...], skh_ref[...]
    for j in range(DIFF_HEADS):
        lo = 1664 + 128 * j
        kd_ref[j] = (zk[:, lo:lo + 128] * ck + zk[:, lo + 512:lo + 640] * sk).astype(BF16)
    cq, sq = cqh_ref[...], sqh_ref[...]
    qs = DIFF_SCALE * LOG2E
    zero64 = jnp.zeros((64, h.shape[0]), BF16)
    for hd in range(2 * DIFF_HEADS):
        q = ((zt[64 * hd:64 * hd + 64] * cq + zt[512 + 64 * hd:576 + 64 * hd] * sq) * qs).astype(BF16)
        if hd % 2 == 0:
            qd_ref[hd, 0:64, :] = q
            qd_ref[hd, 64:128, :] = zero64
        else:
            qd_ref[hd, 0:64, :] = zero64
            qd_ref[hd, 64:128, :] = q
    for j in range(DIFF_HEADS):
        vd_ref[j] = zt[1024 + 128 * j:1152 + 128 * j].astype(BF16)


def _rec_pre_weights(g, rec_w_in):
    d = rec_w_in.shape[0]
    o = np.cumsum([0, SSD_INNER, SSD_XBC, SSD_HEADS, SSD_HEADS, 2 * DIFF_HEADS * DIFF_DIM, 2 * DIFF_HEADS * DIFF_DIM,
                   DIFF_HEADS * DIFF_V])
    w_z, w_xbc, w_dtf, w_dtb, w_dq, w_dk, w_dv = (rec_w_in[:, o[i]:o[i + 1]] for i in range(7))
    src64, sgn64 = _rot_map(DIFF_DIM)

    def rot_heads(wcols):
        wh = wcols.reshape(d, 2 * DIFF_HEADS, DIFF_DIM)
        return (wh[:, :, src64] * sgn64).reshape(d, -1)

    w_dt = jnp.pad(jnp.concatenate([w_dtf, w_dtb], axis=1), ((0, 0), (0, LANES - 2 * SSD_HEADS)))
    wtm = jnp.concatenate([w_z, w_xbc, w_dt, w_dk, rot_heads(w_dk)], axis=1)
    wfm = jnp.concatenate([w_dq, rot_heads(w_dq), w_dv], axis=1)
    cos_h, sin_h = _rope_tables(g.n, DIFF_DIM, ROW_BLOCK)
    two = lambda x: jnp.concatenate([x, x], axis=1)
    return dict(wtm=wtm.astype(BF16), wfm=wfm.T.astype(BF16), cqh=cos_h.T, sqh=sin_h.T, ckh=two(cos_h), skh=two(sin_h))


def _rec_pre_call(g, h, f, prev_mod, prev_gain, mod, gpre, w):
    t, d = h.shape
    rb = ROW_BLOCK
    row = lambda i: (i, 0)
    modspec = pl.BlockSpec((None, 6, d), lambda i: (g.mod_row(i), 0, 0))
    tm_tab = pl.BlockSpec((rb, LANES), lambda i: (g.pos_block(i), 0))
    fm64 = pl.BlockSpec((64, rb), lambda i: (0, g.pos_block(i)))
    nd = 2 * DIFF_HEADS
    return pl.pallas_call(
        _rec_pre_kernel, grid=(g.n_blocks,),
        in_specs=[pl.BlockSpec((rb, d), row), pl.BlockSpec((rb, d), row), modspec, _full((1, d)), modspec,
                  _full(gpre.shape), _full(w["wtm"].shape), _full(w["wfm"].shape), fm64, fm64, tm_tab, tm_tab],
        out_specs=[
            pl.BlockSpec((rb, d), row), pl.BlockSpec((rb, SSD_INNER), row), pl.BlockSpec((rb, SSD_XBC), row), pl.BlockSpec((rb, LANES), row),
            pl.BlockSpec((DIFF_HEADS, rb, 128), lambda i: (0, i, 0)),
            pl.BlockSpec((nd, 128, rb), lambda i: (0, 0, i)),
            pl.BlockSpec((DIFF_HEADS, DIFF_V, rb), lambda i: (0, 0, i)),
        ],
        out_shape=[
            jax.ShapeDtypeStruct((t, d), F32),
            jax.ShapeDtypeStruct((t, SSD_INNER), BF16), jax.ShapeDtypeStruct((t, SSD_XBC), F32),
            jax.ShapeDtypeStruct((t, LANES), F32), jax.ShapeDtypeStruct((DIFF_HEADS, t, 128), BF16),
            jax.ShapeDtypeStruct((nd, 128, t), BF16), jax.ShapeDtypeStruct((DIFF_HEADS, DIFF_V, t), BF16),
        ],
        compiler_params=_cparams(("parallel",)), name="rec_pre",
    )(h, f, prev_mod, prev_gain.reshape(1, d), mod, gpre, w["wtm"], w["wfm"], w["cqh"], w["sqh"], w["ckh"], w["skh"])


def _conv_kernel(x_ref, prev_ref, next_ref, w_ref, b_ref, o_ref, *, seq_blocks, lat_blocks):
    i = pl.program_id(0)
    rb = x_ref.shape[0]
    pos = i % seq_blocks
    is_ctx = i >= lat_blocks
    first = jnp.logical_or(is_ctx, pos == 0)
    last = jnp.logical_or(is_ctx, pos == seq_blocks - 1)
    x = x_ref[...]
    prev_row = jnp.where(first, 0.0, prev_ref[7:8, :])
    next_row = jnp.where(last, 0.0, next_ref[0:1, :])
    rid = lax.broadcasted_iota(jnp.int32, x.shape, 0)
    x_prev = jnp.where(rid == 0, prev_row, pltpu.roll(x, 1, axis=0))
    x_next = jnp.where(rid == rb - 1, next_row, pltpu.roll(x, rb - 1, axis=0))
    y = w_ref[0:1, :] * x_prev + w_ref[1:2, :] * x + w_ref[2:3, :] * x_next + b_ref[...]
    o_ref[...] = _silu(y)


def _conv_call(g, xbc, conv_w, conv_b):
    t, ch = xbc.shape
    rb = g.c
    halo = 8
    per = rb // halo
    last_halo = t // halo - 1
    return pl.pallas_call(
        functools.partial(_conv_kernel, seq_blocks=g.n // rb, lat_blocks=g.b * g.n // rb),
        grid=(t // rb,),
        in_specs=[
            pl.BlockSpec((rb, ch), lambda i: (i, 0)),
            pl.BlockSpec((halo, ch), lambda i: (jnp.maximum(i * per - 1, 0), 0)),
            pl.BlockSpec((halo, ch), lambda i: (jnp.minimum((i + 1) * per, last_halo), 0)),
            _full((SSD_CONV, ch)), _full((1, ch)),
        ],
        out_specs=pl.BlockSpec((rb, ch), lambda i: (i, 0)),
        out_shape=jax.ShapeDtypeStruct((t, ch), F32),
        compiler_params=_cparams(("parallel",)), name="ssd_conv",
    )(xbc, xbc, xbc, conv_w, conv_b.reshape(1, ch))


def _ssd_kernel(xf_ref, dtf_ref, xb_ref, dtb_ref, bias_ref, alog_ref, tril_ref, triu_ref, eye_ref, yf_ref, yb_ref,
                sf_ref, sb_ref):
    @pl.when(pl.program_id(1) == 0)
    def _():
        sf_ref[...] = jnp.zeros_like(sf_ref)
        sb_ref[...] = jnp.zeros_like(sb_ref)

    _ssd_chunk(xf_ref, dtf_ref, bias_ref, alog_ref, tril_ref, triu_ref, eye_ref, yf_ref, sf_ref, 0, False)
    _ssd_chunk(xb_ref, dtb_ref, bias_ref, alog_ref, triu_ref, tril_ref, eye_ref, yb_ref, sb_ref, SSD_HEADS, True)


def _ssd_chunk(x_ref, dt_ref, bias_ref, alog_ref, tri_ref, trit_ref, eye_ref, y_ref, state_ref, lane_off, reverse):
    ln = SSD_CHUNK
    hi = lax.Precision.HIGHEST
    x = x_ref[:, 0:SSD_INNER]
    raw = dt_ref[...] + bias_ref[...]
    dt = jnp.maximum(raw, 0.0) + jnp.log1p(jnp.exp(-jnp.abs(raw)))
    adt = dt * (-jnp.exp(alog_ref[...]))
    acs = jnp.dot(tri_ref[...], adt, preferred_element_type=F32, precision=hi)
    acs_row = lax.dot_general(adt, trit_ref[...], (((0,), (0,)), ((), ())), preferred_element_type=F32, precision=hi)
    dt_row = lax.dot_general(dt, eye_ref[...], (((0,), (0,)), ((), ())), preferred_element_type=F32, precision=hi)
    tot = acs[0:1, :] if reverse else acs[ln - 1:ln, :]
    w_all = jnp.exp(tot - acs) * dt
    ea_all = jnp.exp(acs)
    etot = jnp.exp(tot)
    li = lax.broadcasted_iota(jnp.int32, (ln, ln), 0)
    si = lax.broadcasted_iota(jnp.int32, (ln, ln), 1)
    keep = (si >= li) if reverse else (si <= li)
    lane = lax.broadcasted_iota(jnp.int32, (ln, LANES), 1)
    left = lane < SSD_HEAD_DIM
    per_group = SSD_HEADS // SSD_GROUPS
    for gi in range(SSD_GROUPS):
        bm = x_ref[:, SSD_INNER + SSD_STATE * gi:SSD_INNER + SSD_STATE * (gi + 1)].astype(BF16)
        cm = x_ref[:, SSD_INNER + SSD_STATE * (SSD_GROUPS + gi):SSD_INNER + SSD_STATE * (SSD_GROUPS + gi + 1)]
        cm = cm.astype(BF16)
        cb = _dot_nt(cm, bm)
        for pr in range(per_group // 2):
            h0 = gi * per_group + 2 * pr
            xp = x[:, SSD_HEAD_DIM * h0:SSD_HEAD_DIM * (h0 + 2)]
            mats = []
            for hd in (h0, h0 + 1):
                c = lane_off + hd
                seg = acs[:, c:c + 1] - acs_row[c:c + 1, :]
                lmat = jnp.exp(jnp.where(keep, seg, -jnp.inf))
                mats.append((cb * lmat * dt_row[c:c + 1, :]).astype(BF16))
            xb = xp.astype(BF16)
            zero = jnp.zeros_like(xb)
            rhs = jnp.concatenate([jnp.where(left, xb, zero), jnp.where(left, zero, xb)], axis=0)
            y_diag = _dot(jnp.concatenate(mats, axis=1), rhs)
            st = state_ref[h0 // 2]
            c0 = lane_off + h0
            ea = jnp.where(left, ea_all[:, c0:c0 + 1], ea_all[:, c0 + 1:c0 + 2])
            y_off = _dot(cm, st.astype(BF16)) * ea
            y_ref[:, SSD_HEAD_DIM * h0:SSD_HEAD_DIM * (h0 + 2)] = y_diag + y_off
            wcol = jnp.where(left, w_all[:, c0:c0 + 1], w_all[:, c0 + 1:c0 + 2])
            cs = _dot_tn(bm, (xp * wcol).astype(BF16))
            dec = jnp.where(left[0:1, :], etot[:, c0:c0 + 1], etot[:, c0 + 1:c0 + 2])
            state_ref[h0 // 2] = st * dec + cs


def _ssd_call(g, xbc_act, dt, dt_bias_f, dt_bias_b, a_log_f, a_log_b):
    t = xbc_act.shape[0]
    ln = SSD_CHUNK
    cc = g.c // ln
    nl = g.n // ln
    ctx0 = g.b * g.n // ln
    fwd = lambda b, k: (jnp.where(k < cc, ctx0 + b * cc + k, b * nl + (k - cc)), 0)
    bwd = lambda b, k: (jnp.where(k < cc, ctx0 + b * cc + (cc - 1 - k), b * nl + (nl - 1 - (k - cc))), 0)
    idx = np.arange(ln)
    lower = (idx[:, None] >= idx[None, :]).astype(np.float32)
    lanes = lambda pf, pb: jnp.pad(jnp.concatenate([pf, pb]).reshape(1, -1), ((0, 0), (0, LANES - 2 * SSD_HEADS)))
    state = pltpu.VMEM((SSD_HEADS // 2, SSD_STATE, 2 * SSD_HEAD_DIM), F32)
    y_shape = jax.ShapeDtypeStruct((t, SSD_INNER), F32)
    return pl.pallas_call(
        _ssd_kernel, grid=(g.b, cc + nl),
        in_specs=[pl.BlockSpec((ln, SSD_XBC), fwd), pl.BlockSpec((ln, LANES), fwd),
                  pl.BlockSpec((ln, SSD_XBC), bwd), pl.BlockSpec((ln, LANES), bwd),
                  _full((1, LANES)), _full((1, LANES)), _full((ln, ln)), _full((ln, ln)), _full((ln, ln))],
        out_specs=[pl.BlockSpec((ln, SSD_INNER), fwd), pl.BlockSpec((ln, SSD_INNER), bwd)],
        out_shape=[y_shape, y_shape],
        scratch_shapes=[state, state],
        compiler_params=_cparams(("parallel", "arbitrary")),
        name="ssd_scan",
    )(xbc_act, dt, xbc_act, dt, lanes(dt_bias_f, dt_bias_b), lanes(a_log_f, a_log_b), jnp.asarray(lower),
      jnp.asarray(lower.T), jnp.eye(ln, dtype=F32))


def _diff_finish(o, tq, lq1_ref, lk1_ref, lq2_ref, lk2_ref, sub_ref, o_ref, lambda_init):
    lam = (jnp.exp(jnp.sum(lq1_ref[...] * lk1_ref[...], axis=1, keepdims=True))
           - jnp.exp(jnp.sum(lq2_ref[...] * lk2_ref[...], axis=1, keepdims=True)) + lambda_init)
    od = o[:, 0:tq] - lam * o[:, tq:2 * tq]
    o_ref[...] = ((_rms(od, 0) * sub_ref[...]) * (1.0 - lambda_init)).astype(o_ref.dtype)


def _diff_attn_kernel(qt_ref, *refs, lambda_init):
    o_ref, s_ref = refs[-2:]
    params = refs[-7:-2]
    kv = refs[:-7]
    half = len(kv) // 2
    segs = [(kv[i], kv[half + i]) for i in range(half)]
    qt = jnp.concatenate([qt_ref[0], qt_ref[1]], axis=1)
    o = _attend(qt, segs, s_ref, DIFF_V)
    _diff_finish(o, qt_ref.shape[2], *params, o_ref, lambda_init)


def _diff_attn_call(g, qd, kd, vd, lq1, lk1, lq2, lk2, subln, lambda_init):
    tq = DIFF_Q_TILE
    nq = g.n // tq
    ctx_blk = g.b * g.n // g.c
    vec = lambda a: a.reshape(1, -1)
    params = [_full((1, DIFF_DIM)), _full((1, DIFF_DIM)), _full((1, DIFF_DIM)), _full((1, DIFF_DIM)),
              _full((DIFF_V, 1))]
    pvals = (vec(lq1), vec(lk1), vec(lq2), vec(lk2), subln.reshape(-1, 1))
    klat = pl.BlockSpec((None, g.n, 128), lambda b, j, *_: (j, b, 0))
    kctx = pl.BlockSpec((None, g.c, 128), lambda b, j, *_: (j, ctx_blk + b, 0))
    vlat = pl.BlockSpec((None, DIFF_V, g.n), lambda b, j, *_: (j, 0, b))
    vctx = pl.BlockSpec((None, DIFF_V, g.c), lambda b, j, *_: (j, 0, ctx_blk + b))
    kern = functools.partial(_diff_attn_kernel, lambda_init=lambda_init)
    lat = pl.pallas_call(
        kern, grid=(g.b, DIFF_HEADS, nq),
        scratch_shapes=[pltpu.VMEM((g.n + g.c, 2 * tq), F32)],
        in_specs=[pl.BlockSpec((2, 128, tq), lambda b, j, qi: (j, 0, b * nq + qi)), klat, kctx, vlat, vctx] + params,
        out_specs=pl.BlockSpec((DIFF_V, tq), lambda b, j, qi: (j, b * nq + qi)),
        out_shape=jax.ShapeDtypeStruct((DIFF_HEADS * DIFF_V, g.b * g.n), BF16),
        compiler_params=_cparams(("parallel", "parallel", "arbitrary")),
        name="attn_diff",
    )(qd, kd, kd, vd, vd, *pvals)
    ctx = pl.pallas_call(
        kern, grid=(g.b, DIFF_HEADS),
        scratch_shapes=[pltpu.VMEM((g.c, 2 * g.c), F32)],
        in_specs=[pl.BlockSpec((2, 128, g.c), lambda b, j: (j, 0, ctx_blk + b)), kctx, vctx] + params,
        out_specs=pl.BlockSpec((DIFF_V, g.c), lambda b, j: (j, b)),
        out_shape=jax.ShapeDtypeStruct((DIFF_HEADS * DIFF_V, g.b * g.c), BF16),
        compiler_params=_cparams(("parallel", "parallel")),
        name="attn_diff_ctx",
    )(qd, kd, vd, *pvals)
    return lat, ctx


def _rec_post_kernel(yf_ref, yb_ref, xs_ref, z_ref, odl_ref, odc_ref, dskip_ref, gn_ref, wa_ref, wb_ref,
                     h_ref, mod_ref, gpost_ref, gffn_ref, rw_ref, hn_ref, v_ref, lg_ref, *, lat_blocks):
    y = (yf_ref[...] + yb_ref[...] + xs_ref[...] * dskip_ref[...]) * _silu(z_ref[...].astype(F32))
    half = SSD_INNER // SSD_GROUPS
    o = _dot_tn(_pick_part(lat_blocks, odl_ref, odc_ref), wb_ref[...])
    for gi in range(SSD_GROUPS):
        lo = half * gi
        yn = (_rms(y[:, lo:lo + half], -1) * gn_ref[:, lo:lo + half]).astype(BF16)
        o = o + _dot(yn, wa_ref[lo:lo + half, :])
    _post_tail(o, h_ref[...], mod_ref, gpost_ref, gffn_ref, rw_ref, hn_ref, v_ref, lg_ref)


def _rec_post_call(g, yf, yb, xbc_act, zs, od_parts, d_skip, ssd_norm, w_out, h, mod, gpost, gffn, router_w):
    d = h.shape[1]
    rb = ROW_BLOCK
    ins, outs, shapes = _post_specs(g, d)
    wo = w_out.astype(BF16)
    row512 = pl.BlockSpec((rb, SSD_INNER), lambda i: (i, 0))
    return pl.pallas_call(
        functools.partial(_rec_post_kernel, lat_blocks=g.lat_blocks), grid=(g.n_blocks,),
        in_specs=[row512, row512, row512, row512] + _part_specs(g, (SSD_INNER, rb), 1)
        + [_full((1, SSD_INNER)), _full((1, SSD_INNER)), _full((SSD_INNER, d)), _full((SSD_INNER, d)),
           pl.BlockSpec((rb, d), lambda i: (i, 0))] + ins,
        out_specs=outs, out_shape=shapes,
        compiler_params=_cparams(("parallel",)), name="rec_post",
    )(yf, yb, xbc_act, zs, *od_parts, jnp.repeat(d_skip, SSD_HEAD_DIM).reshape(1, -1), ssd_norm.reshape(1, -1),
      wo[:SSD_INNER], wo[SSD_INNER:], h, mod, gpost.reshape(1, d), gffn.reshape(1, d), router_w.T)


def _ffn_res_kernel(h_ref, f_ref, mod_ref, g_ref, o_ref):
    o_ref[...] = h_ref[...] + mod_ref[5:6, :] * (_rms(f_ref[...], -1) * g_ref[...])


def _ffn_res_call(g, h, f, mod, gain, n_rows):
    d = h.shape[1]
    rb = ROW_BLOCK
    row = pl.BlockSpec((rb, d), lambda i: (i, 0))
    return pl.pallas_call(
        _ffn_res_kernel, grid=(n_rows // rb,),
        in_specs=[row, row, pl.BlockSpec((None, 6, d), lambda i: (g.mod_row(i), 0, 0)), _full((1, d))],
        out_specs=row, out_shape=jax.ShapeDtypeStruct((n_rows, d), F32),
        compiler_params=_cparams(("parallel",)), name="ffn_res",
    )(h, f, mod, gain.reshape(1, d))


def kernel(x, c, ctx, c_ctx, ada_w, ada_b, norm_mix_pre, norm_mix_post, norm_ffn_pre, norm_ffn_post, mix_w_out, att_w_in, mla_q_norm, mla_w_q_up, mla_kv_norm, mla_w_kv_up, gqa_q_norm, gqa_k_norm, rec_w_in, ssd_conv_w, ssd_conv_b, ssd_dt_bias_f, ssd_dt_bias_b, ssd_a_log_f, ssd_a_log_b, ssd_d, ssd_norm, diff_lambda_q1, diff_lambda_k1, diff_lambda_q2, diff_lambda_k2, diff_subln, router_w, router_b, exp_w_gate, exp_w_up, exp_w_down, sh_w_gate, sh_w_up, sh_w_down):
    b, n, d = x.shape
    n_ctx = ctx.shape[1]
    depth = ada_w.shape[0]
    g = _Geom(b, n, n_ctx)
    mod_rows = -(-(b + 1) // 8) * 8
    c_all = jnp.concatenate([c, c_ctx[None, :], jnp.zeros((mod_rows - b - 1, d), F32)], axis=0)
    mods = _ada_call(c_all, ada_w, ada_b).reshape(depth, mod_rows, 6, d)
    h = None
    for i in range(depth):
        last = i == depth - 1
        jdx = i // 2
        mod = mods[i]
        if i % 2 == 0:
            w = _attn_pre_weights(g, att_w_in[jdx], mla_q_norm[jdx], mla_w_q_up[jdx], mla_kv_norm[jdx],
                                  mla_w_kv_up[jdx], gqa_q_norm[jdx], gqa_k_norm[jdx])
            h_parts = (x.reshape(b * n, d), ctx.reshape(b * n_ctx, d)) if i == 0 else (h[:b * n], h[b * n:])
            qm, km, vm, qg, kg, vg = _attn_pre_call(g, h_parts, mod, norm_mix_pre[i].reshape(1, d), w)
            oa = _attn_call(g, qm, km, vm, kv_heads=MLA_HEADS, shared_k=False, name="attn_mla")
            ob = _attn_call(g, qg, kg, vg, kv_heads=GQA_KV_HEADS, shared_k=True, name="attn_gqa")
            h, v, logits_t = _attn_post_call(g, oa, ob, mix_w_out[i], h_parts, mod, norm_mix_post[i], norm_ffn_pre[i],
                                             router_w[i])
        else:
            lambda_init = 0.8 - 0.6 * math.exp(-0.3 * i)
            w = _rec_pre_weights(g, rec_w_in[jdx])
            h, zs, xbc, dt, kd, qd, vd = _rec_pre_call(g, h, f, mods[i - 1], norm_ffn_post[i - 1], mod,
                                                       norm_mix_pre[i].reshape(1, d), w)
            xbc_act = _conv_call(g, xbc, ssd_conv_w[jdx], ssd_conv_b[jdx])
            yf, yb = _ssd_call(g, xbc_act, dt, ssd_dt_bias_f[jdx], ssd_dt_bias_b[jdx], ssd_a_log_f[jdx],
                               ssd_a_log_b[jdx])
            od = _diff_attn_call(g, qd, kd, vd, diff_lambda_q1[jdx], diff_lambda_k1[jdx], diff_lambda_q2[jdx],
                                 diff_lambda_k2[jdx], diff_subln[jdx], lambda_init)
            h, v, logits_t = _rec_post_call(g, yf, yb, xbc_act, zs, od, ssd_d[jdx], ssd_norm[jdx], mix_w_out[i], h,
                                            mod, norm_mix_post[i], norm_ffn_pre[i], router_w[i])
        gates, rank, counts = _router_call(g, logits_t, router_b[i])
        f = _moe_call(g, v, gates, rank, counts, exp_w_gate[i], exp_w_up[i], exp_w_down[i],
                      sh_w_gate[i], sh_w_up[i], sh_w_down[i])
        if last or i % 2 == 1:
            h = _ffn_res_call(g, h, f, mod, norm_ffn_post[i], b * n if last else g.t)
    return h.reshape(b, n, d)
```

```python
import functools
import math

import numpy as np
import jax
import jax.numpy as jnp
from jax import lax
from jax.experimental import pallas as pl
from jax.experimental.pallas import tpu as pltpu

F32 = jnp.float32
BF16 = jnp.bfloat16
LOG2E = 1.4426950408889634

GRID_W = 64
ROPE_BASE = 10000.0
NORM_EPS = 1e-6

MLA_HEADS, MLA_Q_LORA, MLA_KV_LORA, MLA_NOPE, MLA_ROPE, MLA_V = 8, 256, 128, 64, 32, 64
MLA_SCALE = (MLA_NOPE + MLA_ROPE) ** -0.5
GQA_HEADS, GQA_KV_HEADS, GQA_DIM = 8, 2, 64
GQA_SCALE = GQA_DIM ** -0.5
SSD_HEADS, SSD_HEAD_DIM, SSD_GROUPS, SSD_STATE, SSD_CONV, SSD_CHUNK = 8, 64, 2, 128, 3, 128
SSD_INNER = SSD_HEADS * SSD_HEAD_DIM
SSD_XBC = SSD_INNER + 2 * SSD_GROUPS * SSD_STATE
DIFF_HEADS, DIFF_DIM = 4, 64
DIFF_V = 2 * DIFF_DIM
DIFF_SCALE = DIFF_DIM ** -0.5
N_EXPERTS, TOP_K, N_GROUPS, TOPK_GROUPS, EXPERT_FF, SHARED_FF = 64, 8, 8, 4, 256, 256
ROUTED_SCALE = 2.5

LANES = 128
VMEM_LIMIT_BYTES = 56 * 1024 * 1024

ROW_BLOCK = 512
ATTN_Q_TILE = 1024
DIFF_Q_TILE = 512
ATTN_KV_CHUNK = 512
ATTN_EXP_ROWS = 4096
MOE_TOKENS = 768
MOE_ROWS = 128
MOE_PAIR = 8
MOE_SLOTS = 8


def _cparams(sem):
    return pltpu.CompilerParams(dimension_semantics=sem, vmem_limit_bytes=VMEM_LIMIT_BYTES)


def _rms(x, axis):
    return x * lax.rsqrt(jnp.mean(x * x, axis=axis, keepdims=True) + NORM_EPS)


def _silu(x):
    return x * jax.nn.sigmoid(x)


def _dot(a, b):
    return jnp.dot(a, b, preferred_element_type=F32)


def _dot_nt(a, b):
    return lax.dot_general(a, b, (((1,), (1,)), ((), ())), preferred_element_type=F32)


def _dot_tn(a, b):
    return lax.dot_general(a, b, (((0,), (0,)), ((), ())), preferred_element_type=F32)


def _ada_kernel(c_ref, w_ref, b_ref, o_ref):
    s = _silu(c_ref[...])
    o_ref[...] = jnp.dot(s, w_ref[...], preferred_element_type=F32, precision=lax.Precision.HIGHEST) + b_ref[...]


def _ada_call(c_all, ada_w, ada_b):
    depth, d, six_d = ada_w.shape
    rows = c_all.shape[0]
    cols = six_d // 4
    return pl.pallas_call(
        _ada_kernel,
        grid=(depth, six_d // cols),
        in_specs=[
            pl.BlockSpec((rows, d), lambda i, j: (0, 0)),
            pl.BlockSpec((None, d, cols), lambda i, j: (i, 0, j)),
            pl.BlockSpec((None, 1, cols), lambda i, j: (i, 0, j)),
        ],
        out_specs=pl.BlockSpec((None, rows, cols), lambda i, j: (i, 0, j)),
        out_shape=jax.ShapeDtypeStruct((depth, rows, six_d), F32),
        compiler_params=_cparams(("arbitrary", "arbitrary")),
        name="ada_mod",
    )(c_all, ada_w, ada_b.reshape(depth, 1, six_d))


def _attn_pre_kernel(hl_ref, hc_ref, mod_ref, gpre_ref, wfm_ref, wtm_ref, qn_ref, kvn_col_ref, kvn_row_ref, wq_ref,
                     wk_ref, wv_ref, gq_g_ref, gq_gr_ref, gk_g_ref, gk_gr_ref,
                     cqm_ref, sqm_ref, cqh_ref, sqh_ref, ckm_ref, skm_ref, ckh_ref, skh_ref,
                     qm_ref, km_ref, vm_ref, qg_ref, kg_ref, vg_ref, *, lat_blocks):
    h = _pick_part(lat_blocks, hl_ref, hc_ref)
    u = _rms(h, -1) * gpre_ref[...]
    u = u * (1.0 + mod_ref[1:2, :]) + mod_ref[0:1, :]
    ub = u.astype(BF16)
    zt = _dot_nt(wfm_ref[...], ub)
    zk = _dot(ub, wtm_ref[...])

    qn = (_rms(zt[0:256], 0) * qn_ref[...]).astype(BF16)
    qt = _dot(wq_ref[...], qn)
    cq, sq = cqm_ref[...], sqm_ref[...]
    qs = MLA_SCALE * LOG2E
    zero32 = jnp.zeros((32, h.shape[0]), BF16)
    for hd in range(MLA_HEADS):
        pe = qt[512 + 32 * hd:544 + 32 * hd] * cq + qt[768 + 32 * hd:800 + 32 * hd] * sq
        qm_ref[hd, 0:64, :] = (qt[64 * hd:64 * hd + 64] * qs).astype(BF16)
        qm_ref[hd, 64:96, :] = (pe * qs).astype(BF16)
        qm_ref[hd, 96:128, :] = zero32

    kvn_t = (_rms(zt[256:384], 0) * kvn_col_ref[...]).astype(BF16)
    vt = _dot(wv_ref[...], kvn_t)
    for hd in range(MLA_HEADS):
        vm_ref[hd] = vt[64 * hd:64 * hd + 64].astype(BF16)
    kvn = (_rms(zk[:, 0:128], -1) * kvn_row_ref[...]).astype(BF16)
    kn = _dot(kvn, wk_ref[...])
    kpe = zk[:, 128:256] * ckm_ref[...] + zk[:, 256:384] * skm_ref[...]
    for hd in range(MLA_HEADS):
        km_ref[hd] = (kn[:, 128 * hd:128 * hd + 128] + kpe).astype(BF16)

    cqh, sqh = cqh_ref[...], sqh_ref[...]
    gs = GQA_SCALE * LOG2E
    zero64 = jnp.zeros((64, h.shape[0]), BF16)
    grp = GQA_HEADS // GQA_KV_HEADS
    for hd in range(GQA_HEADS):
        raw = zt[384 + 64 * hd:448 + 64 * hd]
        rot = zt[896 + 64 * hd:960 + 64 * hd]
        r = lax.rsqrt(jnp.mean(raw * raw, axis=0, keepdims=True) + NORM_EPS)
        q = (raw * r * gq_g_ref[...]) * cqh + (rot * r * gq_gr_ref[...]) * sqh
        q = (q * gs).astype(BF16)
        if hd // grp == 0:
            qg_ref[hd, 0:64, :] = q
            qg_ref[hd, 64:128, :] = zero64
        else:
            qg_ref[hd, 0:64, :] = zero64
            qg_ref[hd, 64:128, :] = q
    for kvh in range(GQA_KV_HEADS):
        vg_ref[kvh] = zt[1408 + 64 * kvh:1472 + 64 * kvh].astype(BF16)

    gk, gkr = zk[:, 384:512], zk[:, 512:640]
    lane = lax.broadcasted_iota(jnp.int32, gk.shape, 1)
    lo = lane < 64
    sq0 = jnp.sum(jnp.where(lo, gk * gk, 0.0), axis=-1, keepdims=True)
    sq1 = jnp.sum(jnp.where(lo, 0.0, gk * gk), axis=-1, keepdims=True)
    r = jnp.where(lo, lax.rsqrt(sq0 / GQA_DIM + NORM_EPS), lax.rsqrt(sq1 / GQA_DIM + NORM_EPS))
    kg_ref[...] = ((gk * r * gk_g_ref[...]) * ckh_ref[...] + (gkr * r * gk_gr_ref[...]) * skh_ref[...]).astype(BF16)


def _rot_map(dim):
    q = dim // 4
    j = np.arange(dim)
    even = (j // q) % 2 == 0
    return np.where(even, j + q, j - q), np.where(even, -1.0, 1.0).astype(np.float32)


def _rope_tables(n_tokens, dim, pad_rows):
    rows = n_tokens // GRID_W
    row = jnp.repeat(jnp.arange(rows, dtype=F32), GRID_W)
    col = jnp.tile(jnp.arange(GRID_W, dtype=F32), rows)
    half = dim // 2
    inv = ROPE_BASE ** (-(jnp.arange(half // 2, dtype=F32) * 2.0 / half))
    ang_r = row[:, None] * inv
    ang_c = col[:, None] * inv
    ang = jnp.concatenate([ang_r, ang_r, ang_c, ang_c], axis=-1)
    cos = jnp.concatenate([jnp.cos(ang), jnp.ones((pad_rows, dim), F32)], axis=0)
    sin = jnp.concatenate([jnp.sin(ang), jnp.zeros((pad_rows, dim), F32)], axis=0)
    return cos, sin


class _Geom:
    def __init__(self, b, n, c):
        assert n % ROW_BLOCK == 0 and (b * c) % ROW_BLOCK == 0 and n % c == 0 and n % GRID_W == 0
        self.b, self.n, self.c = b, n, c
        self.t = b * n + b * c
        self.lat_blocks = b * n // ROW_BLOCK
        self.blocks_per_seq = n // ROW_BLOCK
        self.n_blocks = self.t // ROW_BLOCK
        assert self.t % MOE_TOKENS == 0

    def mod_row(self, i):
        return jnp.where(i < self.lat_blocks, i // self.blocks_per_seq, self.b)

    def pos_block(self, i):
        return jnp.where(i < self.lat_blocks, i % self.blocks_per_seq, self.blocks_per_seq)


def _full(shape):
    nd = len(shape)
    return pl.BlockSpec(shape, lambda *_: (0,) * nd)


def _part_specs(g, block, axis):
    def spec(latent):
        def index(i):
            j = jnp.minimum(i, g.lat_blocks - 1) if latent else jnp.maximum(i - g.lat_blocks, 0)
            return tuple(j if a == axis else 0 for a in range(len(block)))
        return pl.BlockSpec(block, index)
    return [spec(True), spec(False)]


def _pick_part(lat_blocks, lat_ref, ctx_ref):
    return jnp.where(pl.program_id(0) < lat_blocks, lat_ref[...], ctx_ref[...])


def _attn_pre_call(g, h_parts, mod, gpre, w):
    d = h_parts[0].shape[1]
    t = g.t
    rb = ROW_BLOCK
    row = lambda i: (i, 0)
    tm_tab = pl.BlockSpec((rb, LANES), lambda i: (g.pos_block(i), 0))
    fm32 = pl.BlockSpec((32, rb), lambda i: (0, g.pos_block(i)))
    fm64 = pl.BlockSpec((64, rb), lambda i: (0, g.pos_block(i)))
    in_specs = _part_specs(g, (rb, d), 0) + [
        pl.BlockSpec((None, 6, d), lambda i: (g.mod_row(i), 0, 0)),
        _full(gpre.shape), _full(w["wfm"].shape), _full(w["wtm"].shape), _full(w["qn"].shape),
        _full(w["kvn_col"].shape), _full(w["kvn_row"].shape), _full(w["wq"].shape), _full(w["wk"].shape),
        _full(w["wv"].shape), _full(w["gq_g"].shape), _full(w["gq_gr"].shape), _full(w["gk_g"].shape),
        _full(w["gk_gr"].shape),
        fm32, fm32, fm64, fm64, tm_tab, tm_tab, tm_tab, tm_tab,
    ]
    out_shape = [
        jax.ShapeDtypeStruct((MLA_HEADS, 128, t), BF16),
        jax.ShapeDtypeStruct((MLA_HEADS, t, 128), BF16),
        jax.ShapeDtypeStruct((MLA_HEADS, MLA_V, t), BF16),
        jax.ShapeDtypeStruct((GQA_HEADS, 128, t), BF16),
        jax.ShapeDtypeStruct((t, 128), BF16),
        jax.ShapeDtypeStruct((GQA_KV_HEADS, GQA_DIM, t), BF16),
    ]
    out_specs = [
        pl.BlockSpec((MLA_HEADS, 128, rb), lambda i: (0, 0, i)),
        pl.BlockSpec((MLA_HEADS, rb, 128), lambda i: (0, i, 0)),
        pl.BlockSpec((MLA_HEADS, MLA_V, rb), lambda i: (0, 0, i)),
        pl.BlockSpec((GQA_HEADS, 128, rb), lambda i: (0, 0, i)),
        pl.BlockSpec((rb, 128), row),
        pl.BlockSpec((GQA_KV_HEADS, GQA_DIM, rb), lambda i: (0, 0, i)),
    ]
    return pl.pallas_call(
        functools.partial(_attn_pre_kernel, lat_blocks=g.lat_blocks),
        grid=(g.n_blocks,), in_specs=in_specs, out_specs=out_specs, out_shape=out_shape,
        compiler_params=_cparams(("parallel",)), name="attn_pre",
    )(*h_parts, mod, gpre, w["wfm"], w["wtm"], w["qn"], w["kvn_col"], w["kvn_row"], w["wq"], w["wk"], w["wv"],
      w["gq_g"], w["gq_gr"], w["gk_g"], w["gk_gr"],
      w["cqm"], w["sqm"], w["cqh"], w["sqh"], w["ckm"], w["skm"], w["ckh"], w["skh"])


def _attn_pre_weights(g, att_w_in, q_norm, w_q_up, kv_norm, w_kv_up, gqa_qn, gqa_kn):
    d = att_w_in.shape[0]
    o = np.cumsum([0, MLA_Q_LORA, MLA_KV_LORA, MLA_ROPE, GQA_HEADS * GQA_DIM, GQA_KV_HEADS * GQA_DIM,
                   GQA_KV_HEADS * GQA_DIM])
    w_qlat, w_kvlat, w_kpe, w_gq, w_gk, w_gv = (att_w_in[:, o[i]:o[i + 1]] for i in range(6))
    src32, sgn32 = _rot_map(MLA_ROPE)
    src64, sgn64 = _rot_map(GQA_DIM)

    def rot_heads(wcols, heads, dim, src, sgn):
        wh = wcols.reshape(d, heads, dim)
        return (wh[:, :, src] * sgn).reshape(d, heads * dim)

    wfm = jnp.concatenate([w_qlat, w_kvlat, w_gq, rot_heads(w_gq, GQA_HEADS, GQA_DIM, src64, sgn64), w_gv], axis=1)
    zpad = lambda x, lo, hi: jnp.pad(x, ((0, 0), (lo, hi)))
    wtm = jnp.concatenate([
        w_kvlat, zpad(w_kpe, 64, 32), zpad(w_kpe[:, src32] * sgn32, 64, 32),
        w_gk, rot_heads(w_gk, GQA_KV_HEADS, GQA_DIM, src64, sgn64)], axis=1)
    wq = w_q_up.reshape(MLA_Q_LORA, MLA_HEADS, MLA_NOPE + MLA_ROPE)
    wq_pe = wq[:, :, MLA_NOPE:]
    wq_all = jnp.concatenate([
        wq[:, :, :MLA_NOPE].reshape(MLA_Q_LORA, -1), wq_pe.reshape(MLA_Q_LORA, -1),
        (wq_pe[:, :, src32] * sgn32).reshape(MLA_Q_LORA, -1)], axis=1)
    wkv = w_kv_up.reshape(MLA_KV_LORA, MLA_HEADS, MLA_NOPE + MLA_V)
    wk = jnp.pad(wkv[:, :, :MLA_NOPE], ((0, 0), (0, 0), (0, 128 - MLA_NOPE))).reshape(MLA_KV_LORA, -1)
    wv = wkv[:, :, MLA_NOPE:].reshape(MLA_KV_LORA, -1)
    cos_m, sin_m = _rope_tables(g.n, MLA_ROPE, ROW_BLOCK)
    cos_h, sin_h = _rope_tables(g.n, GQA_DIM, ROW_BLOCK)
    two = lambda x: jnp.concatenate([x, x], axis=1)
    return dict(
        wfm=wfm.T.astype(BF16), wtm=wtm.astype(BF16),
        qn=q_norm.reshape(-1, 1), kvn_col=kv_norm.reshape(-1, 1), kvn_row=kv_norm.reshape(1, -1),
        wq=wq_all.T.astype(BF16), wk=wk.astype(BF16), wv=wv.T.astype(BF16),
        gq_g=gqa_qn.reshape(-1, 1), gq_gr=gqa_qn[src64].reshape(-1, 1),
        gk_g=two(gqa_kn.reshape(1, -1)), gk_gr=two(gqa_kn[src64].reshape(1, -1)),
        cqm=cos_m.T, sqm=sin_m.T, cqh=cos_h.T, sqh=sin_h.T,
        ckm=zpad(cos_m, 64, 32), skm=zpad(sin_m, 64, 32), ckh=two(cos_h), skh=two(sin_h),
    )


def _fold8(x, op):
    r, w = x.shape
    return op(x.reshape(r // 8, 8, w), axis=0)


def _attend(qt, segments, s_ref, dv):
    tq = qt.shape[1]
    pieces, off = [], 0
    for k_ref, vt_ref in segments:
        n = k_ref.shape[0]
        for lo in range(0, n, ATTN_KV_CHUNK):
            rows = min(ATTN_KV_CHUNK, n - lo)
            pieces.append((off, lo, rows, k_ref, vt_ref))
            off += rows
    m8 = None
    for so, lo, rows, k_ref, _ in pieces:
        s = _dot(k_ref[lo:lo + rows, :], qt)
        s_ref[so:so + rows, :] = s
        part = _fold8(s, jnp.max)
        m8 = part if m8 is None else jnp.maximum(m8, part)
    m = jnp.max(m8, axis=0, keepdims=True)
    acc = jnp.zeros((dv + 16, tq), F32)
    off = 0
    for _, vt_ref in segments:
        n = vt_ref.shape[1]
        step = min(n, ATTN_EXP_ROWS)

        def body(i, acc, off=off, vt_ref=vt_ref, step=step):
            base = i * step if isinstance(i, int) else pl.multiple_of(i * step, step)
            for lo in range(0, step, ATTN_KV_CHUNK):
                rows = min(ATTN_KV_CHUNK, step - lo)
                p = jnp.exp2((s_ref[pl.ds(off + base + lo, rows), :] - m).astype(BF16))
                vt1 = jnp.concatenate([vt_ref[:, pl.ds(base + lo, rows)], jnp.ones((16, rows), BF16)], axis=0)
                acc = acc + _dot(vt1, p)
            return acc

        acc = body(0, acc) if n == step else lax.fori_loop(0, n // step, body, acc)
        off += n
    return acc[0:dv] / acc[dv:dv + 1]


def _attn_kernel(qt_ref, *refs, dv):
    o_ref, s_ref = refs[-2:]
    kv = refs[:-2]
    half = len(kv) // 2
    segs = [(kv[i], kv[half + i]) for i in range(half)]
    o_ref[...] = _attend(qt_ref[...], segs, s_ref, dv).astype(o_ref.dtype)


def _attn_call(g, qt, k, vt, *, kv_heads, shared_k, name):
    heads = qt.shape[0]
    dv = vt.shape[1]
    grp = heads // kv_heads
    tq = ATTN_Q_TILE
    nq = g.n // tq
    ctx_blk = g.b * g.n // g.c
    if shared_k:
        klat = pl.BlockSpec((g.n, 128), lambda b, h, *_: (b, 0))
        kctx = pl.BlockSpec((g.c, 128), lambda b, h, *_: (ctx_blk + b, 0))
    else:
        klat = pl.BlockSpec((None, g.n, 128), lambda b, h, *_: (h // grp, b, 0))
        kctx = pl.BlockSpec((None, g.c, 128), lambda b, h, *_: (h // grp, ctx_blk + b, 0))
    vlat = pl.BlockSpec((None, dv, g.n), lambda b, h, *_: (h // grp, 0, b))
    vctx = pl.BlockSpec((None, dv, g.c), lambda b, h, *_: (h // grp, 0, ctx_blk + b))
    lat = pl.pallas_call(
        functools.partial(_attn_kernel, dv=dv),
        grid=(g.b, heads, nq),
        scratch_shapes=[pltpu.VMEM((g.n + g.c, tq), F32)],
        in_specs=[pl.BlockSpec((None, 128, tq), lambda b, h, qi: (h, 0, b * nq + qi)), klat, kctx, vlat, vctx],
        out_specs=pl.BlockSpec((dv, tq), lambda b, h, qi: (h, b * nq + qi)),
        out_shape=jax.ShapeDtypeStruct((heads * dv, g.b * g.n), BF16),
        compiler_params=_cparams(("parallel", "parallel", "arbitrary")),
        name=name,
    )(qt, k, k, vt, vt)
    ctx = pl.pallas_call(
        functools.partial(_attn_kernel, dv=dv),
        grid=(g.b, heads),
        scratch_shapes=[pltpu.VMEM((g.c, g.c), F32)],
        in_specs=[pl.BlockSpec((None, 128, g.c), lambda b, h: (h, 0, ctx_blk + b)), kctx, vctx],
        out_specs=pl.BlockSpec((dv, g.c), lambda b, h: (h, b)),
        out_shape=jax.ShapeDtypeStruct((heads * dv, g.b * g.c), BF16),
        compiler_params=_cparams(("parallel", "parallel")),
        name=name + "_ctx",
    )(qt, k, vt)
    return lat, ctx


def _split_bf16(x):
    hi = x.astype(BF16)
    return hi, (x - hi.astype(F32)).astype(BF16)


def _post_tail(o, h, mod_ref, gpost_ref, gffn_ref, rw_ref, hn_ref, v_ref, lg_ref):
    hn = h + mod_ref[2:3, :] * (_rms(o, -1) * gpost_ref[...])
    hn_ref[...] = hn
    v = (_rms(hn, -1) * gffn_ref[...]) * (1.0 + mod_ref[4:5, :]) + mod_ref[3:4, :]
    v_hi, v_lo = _split_bf16(v)
    v_ref[...] = v_hi
    w_hi, w_lo = _split_bf16(rw_ref[...])
    lg_ref[...] = _dot_nt(w_hi, v_hi) + (_dot_nt(w_hi, v_lo) + _dot_nt(w_lo, v_hi))


def _attn_post_kernel(oal_ref, oac_ref, obl_ref, obc_ref, wa_ref, wb_ref, hl_ref, hc_ref, mod_ref, gpost_ref, gffn_ref,
                      rw_ref, hn_ref, v_ref, lg_ref, *, lat_blocks):
    oa = _pick_part(lat_blocks, oal_ref, oac_ref)
    ob = _pick_part(lat_blocks, obl_ref, obc_ref)
    o = _dot_tn(oa, wa_ref[...]) + _dot_tn(ob, wb_ref[...])
    h = _pick_part(lat_blocks, hl_ref, hc_ref)
    _post_tail(o, h, mod_ref, gpost_ref, gffn_ref, rw_ref, hn_ref, v_ref, lg_ref)


def _post_specs(g, d):
    rb = ROW_BLOCK
    ins = [
        pl.BlockSpec((None, 6, d), lambda i: (g.mod_row(i), 0, 0)),
        _full((1, d)), _full((1, d)), _full((N_EXPERTS, d)),
    ]
    outs = [pl.BlockSpec((rb, d), lambda i: (i, 0)), pl.BlockSpec((rb, d), lambda i: (i, 0)),
            pl.BlockSpec((N_EXPERTS, rb), lambda i: (0, i))]
    shapes = [jax.ShapeDtypeStruct((g.t, d), F32), jax.ShapeDtypeStruct((g.t, d), BF16),
              jax.ShapeDtypeStruct((N_EXPERTS, g.t), F32)]
    return ins, outs, shapes


def _attn_post_call(g, oa_parts, ob_parts, w_out, h_parts, mod, gpost, gffn, router_w):
    d = h_parts[0].shape[1]
    rb = ROW_BLOCK
    half = oa_parts[0].shape[0]
    ins, outs, shapes = _post_specs(g, d)
    wo = w_out.astype(BF16)
    o_specs = _part_specs(g, (half, rb), 1)
    return pl.pallas_call(
        functools.partial(_attn_post_kernel, lat_blocks=g.lat_blocks), grid=(g.n_blocks,),
        in_specs=o_specs + o_specs + [_full((half, d)), _full((half, d))] + _part_specs(g, (rb, d), 0) + ins,
        out_specs=outs, out_shape=shapes,
        compiler_params=_cparams(("parallel",)), name="attn_post",
    )(*oa_parts, *ob_parts, wo[:half], wo[half:], *h_parts, mod, gpost.reshape(1, d), gffn.reshape(1, d), router_w.T)


def _router_kernel(lg_ref, bias_ref, tri_ref, gate_ref, rank_ref, cnt_ref):
    tb = lg_ref.shape[1]
    per = N_EXPERTS // N_GROUPS
    shp = (N_GROUPS, per, tb)
    scores = jax.nn.sigmoid(lg_ref[...])
    s3 = scores.reshape(shp)
    sel = (scores + bias_ref[...]).reshape(shp)
    sub = lax.broadcasted_iota(jnp.int32, shp, 1)
    grp = lax.broadcasted_iota(jnp.int32, shp, 0)
    neg = -jnp.inf
    m1 = jnp.max(sel, axis=1, keepdims=True)
    i1 = jnp.min(jnp.where(sel == m1, sub, per), axis=1, keepdims=True)
    m2 = jnp.max(jnp.where(sub == i1, neg, sel), axis=1, keepdims=True)
    cur = jnp.broadcast_to(m1 + m2, shp)
    gmask = jnp.zeros(shp, F32)
    for _ in range(TOPK_GROUPS):
        gm = jnp.max(cur, axis=0, keepdims=True)
        gi = jnp.min(jnp.where(cur == gm, grp, N_GROUPS), axis=0, keepdims=True)
        pick = grp == gi
        gmask = jnp.where(pick, 1.0, gmask)
        cur = jnp.where(pick, neg, cur)
    masked = jnp.where(gmask > 0.0, sel, neg)
    eidx = grp * per + sub
    chosen = jnp.zeros(shp, F32)
    for _ in range(TOP_K):
        mx = jnp.max(jnp.max(masked, axis=1, keepdims=True), axis=0, keepdims=True)
        cand = jnp.where(masked == mx, eidx, N_EXPERTS)
        ei = jnp.min(jnp.min(cand, axis=1, keepdims=True), axis=0, keepdims=True)
        pick = eidx == ei
        chosen = jnp.where(pick, 1.0, chosen)
        masked = jnp.where(pick, neg, masked)
    top_w = jnp.where(chosen > 0.0, s3, 0.0)
    denom = jnp.sum(jnp.sum(top_w, axis=1, keepdims=True), axis=0, keepdims=True)
    gate_ref[...] = (top_w / denom * ROUTED_SCALE).reshape(N_EXPERTS, tb)
    ch2 = chosen.reshape(N_EXPERTS, tb)
    chb = ch2.astype(BF16)
    before = _dot(chb, tri_ref[...])
    rank_ref[...] = jnp.where(ch2 > 0.0, before, -1.0).astype(jnp.int32)
    cnt_ref[...] = _dot(chb, jnp.ones((tb, LANES), BF16))


def _router_call(g, logits_t, router_b, nsb):
    tb = MOE_TOKENS
    tri = (np.arange(tb)[:, None] < np.arange(tb)[None, :]).astype(np.float32)
    blk = pl.BlockSpec((N_EXPERTS, tb), lambda s: (0, s))
    gates, rank, cnt = pl.pallas_call(
        _router_kernel, grid=(nsb,),
        in_specs=[blk, _full((N_EXPERTS, 1)), _full((tb, tb))],
        out_specs=[blk, blk, pl.BlockSpec((None, N_EXPERTS, LANES), lambda s: (s, 0, 0))],
        out_shape=[jax.ShapeDtypeStruct((N_EXPERTS, g.t), F32), jax.ShapeDtypeStruct((N_EXPERTS, g.t), jnp.int32),
                   jax.ShapeDtypeStruct((nsb, N_EXPERTS, LANES), F32)],
        compiler_params=_cparams(("parallel",)), name="router",
    )(logits_t, router_b.reshape(-1, 1), jnp.asarray(tri, BF16))
    return gates, rank, cnt[:, :, 0].astype(jnp.int32).reshape(-1)


def _moe_kernel(cnt_ref, v_ref, gate_ref, rank_ref, wg_ref, wu_ref, wd_ref, sg_ref, su_ref, sd_ref, o_ref,
                psel_ref, ysel_ref, slot_ref):
    s, j = pl.program_id(0), pl.program_id(1)
    tb = v_ref.shape[0]
    rows = MOE_ROWS

    @pl.when(j == 0)
    def _():
        vb = v_ref[...]
        hid = (_silu(_dot(vb, sg_ref[...])) * _dot(vb, su_ref[...])).astype(BF16)
        o_ref[...] = _dot(hid, sd_ref[...])
        psel_ref[...] = jnp.zeros_like(psel_ref)
        ysel_ref[...] = jnp.zeros_like(ysel_ref)
        slot_ref[0] = 0

    row_id = lax.broadcasted_iota(jnp.int32, (rows, tb), 0)

    def flush(n_slots):
        stage_row = lax.broadcasted_iota(jnp.int32, psel_ref.shape, 0)
        sel = jnp.where(stage_row < n_slots * rows, psel_ref[...], jnp.zeros_like(psel_ref))
        o_ref[...] += _dot_tn(sel, ysel_ref[...])

    ends, total = [], 0
    for e in range(MOE_PAIR):
        total = total + (cnt_ref[s * N_EXPERTS + j * MOE_PAIR + e] + rows - 1) // rows
        ends.append(total)

    def select(i):
        e = sum((i >= end).astype(jnp.int32) for end in ends[:-1])
        first = sum(jnp.where(e == k + 1, ends[k], 0) for k in range(MOE_PAIR - 1))
        ex = j * MOE_PAIR + e
        rk = jnp.where(i < total, rank_ref[pl.ds(ex, 1), :] - (i - first) * rows, -1)
        hit = row_id == rk
        onehot = jnp.where(hit, 1.0, 0.0).astype(BF16)
        w_row = jnp.sum(jnp.where(hit, gate_ref[pl.ds(ex, 1), :], 0.0), axis=1, keepdims=True)
        return e, onehot, w_row

    def body(it, slot):
        picks = [select(2 * it), select(2 * it + 1)]
        both = jnp.concatenate([p[1] for p in picks], axis=0)
        at = pl.multiple_of(slot * rows, rows)
        psel_ref[pl.ds(at, 2 * rows), :] = both
        xs = _dot(both, v_ref[...]).astype(BF16)
        hids = [(_silu(_dot(xs[k * rows:(k + 1) * rows], wg_ref[e])) * _dot(xs[k * rows:(k + 1) * rows], wu_ref[e]))
                .astype(BF16) for k, (e, _, _) in enumerate(picks)]
        outs = [(_dot(hid, wd_ref[e]) * w_row).astype(BF16) for hid, (e, _, w_row) in zip(hids, picks)]
        ysel_ref[pl.ds(at, 2 * rows), :] = jnp.concatenate(outs, axis=0)

        @pl.when(slot == MOE_SLOTS - 2)
        def _():
            flush(MOE_SLOTS)

        return jnp.where(slot == MOE_SLOTS - 2, 0, slot + 2)

    slot = lax.fori_loop(0, (total + 1) // 2, body, slot_ref[0])
    slot_ref[0] = slot

    @pl.when(jnp.logical_and(j == pl.num_programs(1) - 1, slot > 0))
    def _():
        flush(slot)


def _moe_call(g, v, gates, rank, counts, nsb, wg, wu, wd, sg, su, sd):
    t, d = v.shape
    tb = MOE_TOKENS
    ff = wg.shape[2]
    grid_spec = pltpu.PrefetchScalarGridSpec(
        num_scalar_prefetch=1,
        grid=(nsb, N_EXPERTS // MOE_PAIR),
        in_specs=[
            pl.BlockSpec((tb, d), lambda s, j, c: (s, 0)),
            pl.BlockSpec((N_EXPERTS, tb), lambda s, j, c: (0, s)),
            pl.BlockSpec((N_EXPERTS, tb), lambda s, j, c: (0, s)),
            pl.BlockSpec((MOE_PAIR, d, ff), lambda s, j, c: (j, 0, 0)),
            pl.BlockSpec((MOE_PAIR, d, ff), lambda s, j, c: (j, 0, 0)),
            pl.BlockSpec((MOE_PAIR, ff, d), lambda s, j, c: (j, 0, 0)),
            pl.BlockSpec((d, sg.shape[1]), lambda s, j, c: (0, 0)),
            pl.BlockSpec((d, su.shape[1]), lambda s, j, c: (0, 0)),
            pl.BlockSpec((sd.shape[0], d), lambda s, j, c: (0, 0)),
        ],
        out_specs=pl.BlockSpec((tb, d), lambda s, j, c: (s, 0)),
        scratch_shapes=[pltpu.VMEM((MOE_SLOTS * MOE_ROWS, tb), BF16), pltpu.VMEM((MOE_SLOTS * MOE_ROWS, d), BF16),
                        pltpu.SMEM((1,), jnp.int32)],
    )
    return pl.pallas_call(
        _moe_kernel, grid_spec=grid_spec, out_shape=jax.ShapeDtypeStruct((t, d), F32),
        compiler_params=_cparams(("parallel", "arbitrary")), name="moe",
    )(counts, v, gates, rank, wg.astype(BF16), wu.astype(BF16), wd.astype(BF16),
      sg.astype(BF16), su.astype(BF16), sd.astype(BF16))


def _rec_pre_kernel(h_ref, f_ref, pmod_ref, gprev_ref, mod_ref, gpre_ref, wtm_ref, wfm_ref, cqh_ref, sqh_ref, ckh_ref,
                    skh_ref, hn_ref, z_ref, xbc_ref, dt_ref, kd_ref, qd_ref, vd_ref):
    h = h_ref[...] + pmod_ref[5:6, :] * (_rms(f_ref[...], -1) * gprev_ref[...])
    hn_ref[...] = h
    u = _rms(h, -1) * gpre_ref[...]
    u = u * (1.0 + mod_ref[1:2, :]) + mod_ref[0:1, :]
    ub = u.astype(BF16)
    zk = _dot(ub, wtm_ref[...])
    zt = _dot_nt(wfm_ref[...], ub)
    z_ref[...] = zk[:, 0:512].astype(z_ref.dtype)
    xbc_ref[...] = zk[:, 512:1536]
    dt_ref[...] = zk[:, 1536:1664]
    ck, sk = ckh_ref[...], skh_ref[...]
    for j in range(DIFF_HEADS):
        lo = 1664 + 128 * j
        kd_ref[j] = (zk[:, lo:lo + 128] * ck + zk[:, lo + 512:lo + 640] * sk).astype(BF16)
    cq, sq = cqh_ref[...], sqh_ref[...]
    qs = DIFF_SCALE * LOG2E
    zero64 = jnp.zeros((64, h.shape[0]), BF16)
    for hd in range(2 * DIFF_HEADS):
        q = ((zt[64 * hd:64 * hd + 64] * cq + zt[512 + 64 * hd:576 + 64 * hd] * sq) * qs).astype(BF16)
        if hd % 2 == 0:
            qd_ref[hd, 0:64, :] = q
            qd_ref[hd, 64:128, :] = zero64
        else:
            qd_ref[hd, 0:64, :] = zero64
            qd_ref[hd, 64:128, :] = q
    for j in range(DIFF_HEADS):
        vd_ref[j] = zt[1024 + 128 * j:1152 + 128 * j].astype(BF16)


def _rec_pre_weights(g, rec_w_in):
    d = rec_w_in.shape[0]
    o = np.cumsum([0, SSD_INNER, SSD_XBC, SSD_HEADS, SSD_HEADS, 2 * DIFF_HEADS * DIFF_DIM, 2 * DIFF_HEADS * DIFF_DIM,
                   DIFF_HEADS * DIFF_V])
    w_z, w_xbc, w_dtf, w_dtb, w_dq, w_dk, w_dv = (rec_w_in[:, o[i]:o[i + 1]] for i in range(7))
    src64, sgn64 = _rot_map(DIFF_DIM)

    def rot_heads(wcols):
        wh = wcols.reshape(d, 2 * DIFF_HEADS, DIFF_DIM)
        return (wh[:, :, src64] * sgn64).reshape(d, -1)

    w_dt = jnp.pad(jnp.concatenate([w_dtf, w_dtb], axis=1), ((0, 0), (0, LANES - 2 * SSD_HEADS)))
    wtm = jnp.concatenate([w_z, w_xbc, w_dt, w_dk, rot_heads(w_dk)], axis=1)
    wfm = jnp.concatenate([w_dq, rot_heads(w_dq), w_dv], axis=1)
    cos_h, sin_h = _rope_tables(g.n, DIFF_DIM, ROW_BLOCK)
    two = lambda x: jnp.concatenate([x, x], axis=1)
    return dict(wtm=wtm.astype(BF16), wfm=wfm.T.astype(BF16), cqh=cos_h.T, sqh=sin_h.T, ckh=two(cos_h), skh=two(sin_h))


def _rec_pre_call(g, h, f, prev_mod, prev_gain, mod, gpre, w):
    t, d = h.shape
    rb = ROW_BLOCK
    row = lambda i: (i, 0)
    modspec = pl.BlockSpec((None, 6, d), lambda i: (g.mod_row(i), 0, 0))
    tm_tab = pl.BlockSpec((rb, LANES), lambda i: (g.pos_block(i), 0))
    fm64 = pl.BlockSpec((64, rb), lambda i: (0, g.pos_block(i)))
    nd = 2 * DIFF_HEADS
    return pl.pallas_call(
        _rec_pre_kernel, grid=(g.n_blocks,),
        in_specs=[pl.BlockSpec((rb, d), row), pl.BlockSpec((rb, d), row), modspec, _full((1, d)), modspec,
                  _full(gpre.shape), _full(w["wtm"].shape), _full(w["wfm"].shape), fm64, fm64, tm_tab, tm_tab],
        out_specs=[
            pl.BlockSpec((rb, d), row), pl.BlockSpec((rb, SSD_INNER), row), pl.BlockSpec((rb, SSD_XBC), row), pl.BlockSpec((rb, LANES), row),
            pl.BlockSpec((DIFF_HEADS, rb, 128), lambda i: (0, i, 0)),
            pl.BlockSpec((nd, 128, rb), lambda i: (0, 0, i)),
            pl.BlockSpec((DIFF_HEADS, DIFF_V, rb), lambda i: (0, 0, i)),
        ],
        out_shape=[
            jax.ShapeDtypeStruct((t, d), F32),
            jax.ShapeDtypeStruct((t, SSD_INNER), BF16), jax.ShapeDtypeStruct((t, SSD_XBC), F32),
            jax.ShapeDtypeStruct((t, LANES), F32), jax.ShapeDtypeStruct((DIFF_HEADS, t, 128), BF16),
            jax.ShapeDtypeStruct((nd, 128, t), BF16), jax.ShapeDtypeStruct((DIFF_HEADS, DIFF_V, t), BF16),
        ],
        compiler_params=_cparams(("parallel",)), name="rec_pre",
    )(h, f, prev_mod, prev_gain.reshape(1, d), mod, gpre, w["wtm"], w["wfm"], w["cqh"], w["sqh"], w["ckh"], w["skh"])


def _conv_kernel(x_ref, prev_ref, next_ref, w_ref, b_ref, o_ref, *, seq_blocks, lat_blocks):
    i = pl.program_id(0)
    rb = x_ref.shape[0]
    pos = i % seq_blocks
    is_ctx = i >= lat_blocks
    first = jnp.logical_or(is_ctx, pos == 0)
    last = jnp.logical_or(is_ctx, pos == seq_blocks - 1)
    x = x_ref[...]
    prev_row = jnp.where(first, 0.0, prev_ref[7:8, :])
    next_row = jnp.where(last, 0.0, next_ref[0:1, :])
    rid = lax.broadcasted_iota(jnp.int32, x.shape, 0)
    x_prev = jnp.where(rid == 0, prev_row, pltpu.roll(x, 1, axis=0))
    x_next = jnp.where(rid == rb - 1, next_row, pltpu.roll(x, rb - 1, axis=0))
    y = w_ref[0:1, :] * x_prev + w_ref[1:2, :] * x + w_ref[2:3, :] * x_next + b_ref[...]
    o_ref[...] = _silu(y)


def _conv_call(g, xbc, conv_w, conv_b):
    t, ch = xbc.shape
    rb = g.c
    halo = 8
    per = rb // halo
    last_halo = t // halo - 1
    return pl.pallas_call(
        functools.partial(_conv_kernel, seq_blocks=g.n // rb, lat_blocks=g.b * g.n // rb),
        grid=(t // rb,),
        in_specs=[
            pl.BlockSpec((rb, ch), lambda i: (i, 0)),
            pl.BlockSpec((halo, ch), lambda i: (jnp.maximum(i * per - 1, 0), 0)),
            pl.BlockSpec((halo, ch), lambda i: (jnp.minimum((i + 1) * per, last_halo), 0)),
            _full((SSD_CONV, ch)), _full((1, ch)),
        ],
        out_specs=pl.BlockSpec((rb, ch), lambda i: (i, 0)),
        out_shape=jax.ShapeDtypeStruct((t, ch), F32),
        compiler_params=_cparams(("parallel",)), name="ssd_conv",
    )(xbc, xbc, xbc, conv_w, conv_b.reshape(1, ch))


def _ssd_kernel(xf_ref, dtf_ref, xb_ref, dtb_ref, bias_ref, alog_ref, tril_ref, triu_ref, eye_ref, yf_ref, yb_ref,
                sf_ref, sb_ref):
    @pl.when(pl.program_id(1) == 0)
    def _():
        sf_ref[...] = jnp.zeros_like(sf_ref)
        sb_ref[...] = jnp.zeros_like(sb_ref)

    _ssd_chunk(xf_ref, dtf_ref, bias_ref, alog_ref, tril_ref, triu_ref, eye_ref, yf_ref, sf_ref, 0, False)
    _ssd_chunk(xb_ref, dtb_ref, bias_ref, alog_ref, triu_ref, tril_ref, eye_ref, yb_ref, sb_ref, SSD_HEADS, True)


def _ssd_chunk(x_ref, dt_ref, bias_ref, alog_ref, tri_ref, trit_ref, eye_ref, y_ref, state_ref, lane_off, reverse):
    ln = SSD_CHUNK
    hi = lax.Precision.HIGHEST
    x = x_ref[:, 0:SSD_INNER]
    raw = dt_ref[...] + bias_ref[...]
    dt = jnp.maximum(raw, 0.0) + jnp.log1p(jnp.exp(-jnp.abs(raw)))
    adt = dt * (-jnp.exp(alog_ref[...]))
    acs = jnp.dot(tri_ref[...], adt, preferred_element_type=F32, precision=hi)
    acs_row = lax.dot_general(adt, trit_ref[...], (((0,), (0,)), ((), ())), preferred_element_type=F32, precision=hi)
    dt_row = lax.dot_general(dt, eye_ref[...], (((0,), (0,)), ((), ())), preferred_element_type=F32, precision=hi)
    tot = acs[0:1, :] if reverse else acs[ln - 1:ln, :]
    w_all = jnp.exp(tot - acs) * dt
    ea_all = jnp.exp(acs)
    etot = jnp.exp(tot)
    li = lax.broadcasted_iota(jnp.int32, (ln, ln), 0)
    si = lax.broadcasted_iota(jnp.int32, (ln, ln), 1)
    keep = (si >= li) if reverse else (si <= li)
    lane = lax.broadcasted_iota(jnp.int32, (ln, LANES), 1)
    left = lane < SSD_HEAD_DIM
    per_group = SSD_HEADS // SSD_GROUPS
    for gi in range(SSD_GROUPS):
        bm = x_ref[:, SSD_INNER + SSD_STATE * gi:SSD_INNER + SSD_STATE * (gi + 1)].astype(BF16)
        cm = x_ref[:, SSD_INNER + SSD_STATE * (SSD_GROUPS + gi):SSD_INNER + SSD_STATE * (SSD_GROUPS + gi + 1)]
        cm = cm.astype(BF16)
        cb = _dot_nt(cm, bm)
        for pr in range(per_group // 2):
            h0 = gi * per_group + 2 * pr
            xp = x[:, SSD_HEAD_DIM * h0:SSD_HEAD_DIM * (h0 + 2)]
            mats = []
            for hd in (h0, h0 + 1):
                c = lane_off + hd
                seg = acs[:, c:c + 1] - acs_row[c:c + 1, :]
                lmat = jnp.exp(jnp.where(keep, seg, -jnp.inf))
                mats.append((cb * lmat * dt_row[c:c + 1, :]).astype(BF16))
            xb = xp.astype(BF16)
            zero = jnp.zeros_like(xb)
            rhs = jnp.concatenate([jnp.where(left, xb, zero), jnp.where(left, zero, xb)], axis=0)
            y_diag = _dot(jnp.concatenate(mats, axis=1), rhs)
            st = state_ref[h0 // 2]
            c0 = lane_off + h0
            ea = jnp.where(left, ea_all[:, c0:c0 + 1], ea_all[:, c0 + 1:c0 + 2])
            y_off = _dot(cm, st.astype(BF16)) * ea
            y_ref[:, SSD_HEAD_DIM * h0:SSD_HEAD_DIM * (h0 + 2)] = y_diag + y_off
            wcol = jnp.where(left, w_all[:, c0:c0 + 1], w_all[:, c0 + 1:c0 + 2])
            cs = _dot_tn(bm, (xp * wcol).astype(BF16))
            dec = jnp.where(left[0:1, :], etot[:, c0:c0 + 1], etot[:, c0 + 1:c0 + 2])
            state_ref[h0 // 2] = st * dec + cs


def _ssd_call(g, xbc_act, dt, dt_bias_f, dt_bias_b, a_log_f, a_log_b):
    t = xbc_act.shape[0]
    ln = SSD_CHUNK
    cc = g.c // ln
    nl = g.n // ln
    ctx0 = g.b * g.n // ln
    fwd = lambda b, k: (jnp.where(k < cc, ctx0 + b * cc + k, b * nl + (k - cc)), 0)
    bwd = lambda b, k: (jnp.where(k < cc, ctx0 + b * cc + (cc - 1 - k), b * nl + (nl - 1 - (k - cc))), 0)
    idx = np.arange(ln)
    lower = (idx[:, None] >= idx[None, :]).astype(np.float32)
    lanes = lambda pf, pb: jnp.pad(jnp.concatenate([pf, pb]).reshape(1, -1), ((0, 0), (0, LANES - 2 * SSD_HEADS)))
    state = pltpu.VMEM((SSD_HEADS // 2, SSD_STATE, 2 * SSD_HEAD_DIM), F32)
    y_shape = jax.ShapeDtypeStruct((t, SSD_INNER), F32)
    return pl.pallas_call(
        _ssd_kernel, grid=(g.b, cc + nl),
        in_specs=[pl.BlockSpec((ln, SSD_XBC), fwd), pl.BlockSpec((ln, LANES), fwd),
                  pl.BlockSpec((ln, SSD_XBC), bwd), pl.BlockSpec((ln, LANES), bwd),
                  _full((1, LANES)), _full((1, LANES)), _full((ln, ln)), _full((ln, ln)), _full((ln, ln))],
        out_specs=[pl.BlockSpec((ln, SSD_INNER), fwd), pl.BlockSpec((ln, SSD_INNER), bwd)],
        out_shape=[y_shape, y_shape],
        scratch_shapes=[state, state],
        compiler_params=_cparams(("parallel", "arbitrary")),
        name="ssd_scan",
    )(xbc_act, dt, xbc_act, dt, lanes(dt_bias_f, dt_bias_b), lanes(a_log_f, a_log_b), jnp.asarray(lower),
      jnp.asarray(lower.T), jnp.eye(ln, dtype=F32))


def _diff_finish(o, tq, lq1_ref, lk1_ref, lq2_ref, lk2_ref, sub_ref, o_ref, lambda_init):
    lam = (jnp.exp(jnp.sum(lq1_ref[...] * lk1_ref[...], axis=1, keepdims=True))
           - jnp.exp(jnp.sum(lq2_ref[...] * lk2_ref[...], axis=1, keepdims=True)) + lambda_init)
    od = o[:, 0:tq] - lam * o[:, tq:2 * tq]
    o_ref[...] = ((_rms(od, 0) * sub_ref[...]) * (1.0 - lambda_init)).astype(o_ref.dtype)


def _diff_attn_kernel(qt_ref, *refs, lambda_init):
    o_ref, s_ref = refs[-2:]
    params = refs[-7:-2]
    kv = refs[:-7]
    half = len(kv) // 2
    segs = [(kv[i], kv[half + i]) for i in range(half)]
    qt = jnp.concatenate([qt_ref[0], qt_ref[1]], axis=1)
    o = _attend(qt, segs, s_ref, DIFF_V)
    _diff_finish(o, qt_ref.shape[2], *params, o_ref, lambda_init)


def _diff_attn_call(g, qd, kd, vd, lq1, lk1, lq2, lk2, subln, lambda_init):
    tq = DIFF_Q_TILE
    nq = g.n // tq
    ctx_blk = g.b * g.n // g.c
    vec = lambda a: a.reshape(1, -1)
    params = [_full((1, DIFF_DIM)), _full((1, DIFF_DIM)), _full((1, DIFF_DIM)), _full((1, DIFF_DIM)),
              _full((DIFF_V, 1))]
    pvals = (vec(lq1), vec(lk1), vec(lq2), vec(lk2), subln.reshape(-1, 1))
    klat = pl.BlockSpec((None, g.n, 128), lambda b, j, *_: (j, b, 0))
    kctx = pl.BlockSpec((None, g.c, 128), lambda b, j, *_: (j, ctx_blk + b, 0))
    vlat = pl.BlockSpec((None, DIFF_V, g.n), lambda b, j, *_: (j, 0, b))
    vctx = pl.BlockSpec((None, DIFF_V, g.c), lambda b, j, *_: (j, 0, ctx_blk + b))
    kern = functools.partial(_diff_attn_kernel, lambda_init=lambda_init)
    lat = pl.pallas_call(
        kern, grid=(g.b, DIFF_HEADS, nq),
        scratch_shapes=[pltpu.VMEM((g.n + g.c, 2 * tq), F32)],
        in_specs=[pl.BlockSpec((2, 128, tq), lambda b, j, qi: (j, 0, b * nq + qi)), klat, kctx, vlat, vctx] + params,
        out_specs=pl.BlockSpec((DIFF_V, tq), lambda b, j, qi: (j, b * nq + qi)),
        out_shape=jax.ShapeDtypeStruct((DIFF_HEADS * DIFF_V, g.b * g.n), BF16),
        compiler_params=_cparams(("parallel", "parallel", "arbitrary")),
        name="attn_diff",
    )(qd, kd, kd, vd, vd, *pvals)
    ctx = pl.pallas_call(
        kern, grid=(g.b, DIFF_HEADS),
        scratch_shapes=[pltpu.VMEM((g.c, 2 * g.c), F32)],
        in_specs=[pl.BlockSpec((2, 128, g.c), lambda b, j: (j, 0, ctx_blk + b)), kctx, vctx] + params,
        out_specs=pl.BlockSpec((DIFF_V, g.c), lambda b, j: (j, b)),
        out_shape=jax.ShapeDtypeStruct((DIFF_HEADS * DIFF_V, g.b * g.c), BF16),
        compiler_params=_cparams(("parallel", "parallel")),
        name="attn_diff_ctx",
    )(qd, kd, vd, *pvals)
    return lat, ctx


def _rec_post_kernel(yf_ref, yb_ref, xs_ref, z_ref, odl_ref, odc_ref, dskip_ref, gn_ref, wa_ref, wb_ref,
                     h_ref, mod_ref, gpost_ref, gffn_ref, rw_ref, hn_ref, v_ref, lg_ref, *, lat_blocks):
    y = (yf_ref[...] + yb_ref[...] + xs_ref[...] * dskip_ref[...]) * _silu(z_ref[...].astype(F32))
    half = SSD_INNER // SSD_GROUPS
    o = _dot_tn(_pick_part(lat_blocks, odl_ref, odc_ref), wb_ref[...])
    for gi in range(SSD_GROUPS):
        lo = half * gi
        yn = (_rms(y[:, lo:lo + half], -1) * gn_ref[:, lo:lo + half]).astype(BF16)
        o = o + _dot(yn, wa_ref[lo:lo + half, :])
    _post_tail(o, h_ref[...], mod_ref, gpost_ref, gffn_ref, rw_ref, hn_ref, v_ref, lg_ref)


def _rec_post_call(g, yf, yb, xbc_act, zs, od_parts, d_skip, ssd_norm, w_out, h, mod, gpost, gffn, router_w):
    d = h.shape[1]
    rb = ROW_BLOCK
    ins, outs, shapes = _post_specs(g, d)
    wo = w_out.astype(BF16)
    row512 = pl.BlockSpec((rb, SSD_INNER), lambda i: (i, 0))
    return pl.pallas_call(
        functools.partial(_rec_post_kernel, lat_blocks=g.lat_blocks), grid=(g.n_blocks,),
        in_specs=[row512, row512, row512, row512] + _part_specs(g, (SSD_INNER, rb), 1)
        + [_full((1, SSD_INNER)), _full((1, SSD_INNER)), _full((SSD_INNER, d)), _full((SSD_INNER, d)),
           pl.BlockSpec((rb, d), lambda i: (i, 0))] + ins,
        out_specs=outs, out_shape=shapes,
        compiler_params=_cparams(("parallel",)), name="rec_post",
    )(yf, yb, xbc_act, zs, *od_parts, jnp.repeat(d_skip, SSD_HEAD_DIM).reshape(1, -1), ssd_norm.reshape(1, -1),
      wo[:SSD_INNER], wo[SSD_INNER:], h, mod, gpost.reshape(1, d), gffn.reshape(1, d), router_w.T)


def _ffn_res_kernel(h_ref, f_ref, mod_ref, g_ref, o_ref):
    o_ref[...] = h_ref[...] + mod_ref[5:6, :] * (_rms(f_ref[...], -1) * g_ref[...])


def _ffn_res_call(g, h, f, mod, gain, n_rows):
    d = h.shape[1]
    rb = ROW_BLOCK
    row = pl.BlockSpec((rb, d), lambda i: (i, 0))
    return pl.pallas_call(
        _ffn_res_kernel, grid=(n_rows // rb,),
        in_specs=[row, row, pl.BlockSpec((None, 6, d), lambda i: (g.mod_row(i), 0, 0)), _full((1, d))],
        out_specs=row, out_shape=jax.ShapeDtypeStruct((n_rows, d), F32),
        compiler_params=_cparams(("parallel",)), name="ffn_res",
    )(h, f, mod, gain.reshape(1, d))


def kernel(x, c, ctx, c_ctx, ada_w, ada_b, norm_mix_pre, norm_mix_post, norm_ffn_pre, norm_ffn_post, mix_w_out, att_w_in, mla_q_norm, mla_w_q_up, mla_kv_norm, mla_w_kv_up, gqa_q_norm, gqa_k_norm, rec_w_in, ssd_conv_w, ssd_conv_b, ssd_dt_bias_f, ssd_dt_bias_b, ssd_a_log_f, ssd_a_log_b, ssd_d, ssd_norm, diff_lambda_q1, diff_lambda_k1, diff_lambda_q2, diff_lambda_k2, diff_subln, router_w, router_b, exp_w_gate, exp_w_up, exp_w_down, sh_w_gate, sh_w_up, sh_w_down):
    b, n, d = x.shape
    n_ctx = ctx.shape[1]
    depth = ada_w.shape[0]
    g = _Geom(b, n, n_ctx)
    mod_rows = -(-(b + 1) // 8) * 8
    c_all = jnp.concatenate([c, c_ctx[None, :], jnp.zeros((mod_rows - b - 1, d), F32)], axis=0)
    mods = _ada_call(c_all, ada_w, ada_b).reshape(depth, mod_rows, 6, d)
    h = None
    for i in range(depth):
        last = i == depth - 1
        jdx = i // 2
        mod = mods[i]
        if i % 2 == 0:
            w = _attn_pre_weights(g, att_w_in[jdx], mla_q_norm[jdx], mla_w_q_up[jdx], mla_kv_norm[jdx],
                                  mla_w_kv_up[jdx], gqa_q_norm[jdx], gqa_k_norm[jdx])
            h_parts = (x.reshape(b * n, d), ctx.reshape(b * n_ctx, d)) if i == 0 else (h[:b * n], h[b * n:])
            qm, km, vm, qg, kg, vg = _attn_pre_call(g, h_parts, mod, norm_mix_pre[i].reshape(1, d), w)
            oa = _attn_call(g, qm, km, vm, kv_heads=MLA_HEADS, shared_k=False, name="attn_mla")
            ob = _attn_call(g, qg, kg, vg, kv_heads=GQA_KV_HEADS, shared_k=True, name="attn_gqa")
            h, v, logits_t = _attn_post_call(g, oa, ob, mix_w_out[i], h_parts, mod, norm_mix_post[i], norm_ffn_pre[i],
                                             router_w[i])
        else:
            lambda_init = 0.8 - 0.6 * math.exp(-0.3 * i)
            w = _rec_pre_weights(g, rec_w_in[jdx])
            h, zs, xbc, dt, kd, qd, vd = _rec_pre_call(g, h, f, mods[i - 1], norm_ffn_post[i - 1], mod,
                                                       norm_mix_pre[i].reshape(1, d), w)
            xbc_act = _conv_call(g, xbc, ssd_conv_w[jdx], ssd_conv_b[jdx])
            yf, yb = _ssd_call(g, xbc_act, dt, ssd_dt_bias_f[jdx], ssd_dt_bias_b[jdx], ssd_a_log_f[jdx],
                               ssd_a_log_b[jdx])
            od = _diff_attn_call(g, qd, kd, vd, diff_lambda_q1[jdx], diff_lambda_k1[jdx], diff_lambda_q2[jdx],
                                 diff_lambda_k2[jdx], diff_subln[jdx], lambda_init)
            h, v, logits_t = _rec_post_call(g, yf, yb, xbc_act, zs, od, ssd_d[jdx], ssd_norm[jdx], mix_w_out[i], h,
                                            mod, norm_mix_post[i], norm_ffn_pre[i], router_w[i])
        nsb = -(-(b * n) // MOE_TOKENS) if last else g.t // MOE_TOKENS
        gates, rank, counts = _router_call(g, logits_t, router_b[i], nsb)
        f = _moe_call(g, v, gates, rank, counts, nsb, exp_w_gate[i], exp_w_up[i], exp_w_down[i],
                      sh_w_gate[i], sh_w_up[i], sh_w_down[i])
        if last or i % 2 == 1:
            h = _ffn_res_call(g, h, f, mod, norm_ffn_post[i], b * n if last else g.t)
    return h.reshape(b, n, d)
```

```python
import functools
import math

import numpy as np
import jax
import jax.numpy as jnp
from jax import lax
from jax.experimental import pallas as pl
from jax.experimental.pallas import tpu as pltpu

F32 = jnp.float32
BF16 = jnp.bfloat16
LOG2E = 1.4426950408889634

GRID_W = 64
ROPE_BASE = 10000.0
NORM_EPS = 1e-6

MLA_HEADS, MLA_Q_LORA, MLA_KV_LORA, MLA_NOPE, MLA_ROPE, MLA_V = 8, 256, 128, 64, 32, 64
MLA_SCALE = (MLA_NOPE + MLA_ROPE) ** -0.5
GQA_HEADS, GQA_KV_HEADS, GQA_DIM = 8, 2, 64
GQA_SCALE = GQA_DIM ** -0.5
SSD_HEADS, SSD_HEAD_DIM, SSD_GROUPS, SSD_STATE, SSD_CONV, SSD_CHUNK = 8, 64, 2, 128, 3, 128
SSD_INNER = SSD_HEADS * SSD_HEAD_DIM
SSD_XBC = SSD_INNER + 2 * SSD_GROUPS * SSD_STATE
DIFF_HEADS, DIFF_DIM = 4, 64
DIFF_V = 2 * DIFF_DIM
DIFF_SCALE = DIFF_DIM ** -0.5
N_EXPERTS, TOP_K, N_GROUPS, TOPK_GROUPS, EXPERT_FF, SHARED_FF = 64, 8, 8, 4, 256, 256
ROUTED_SCALE = 2.5

LANES = 128
VMEM_LIMIT_BYTES = 56 * 1024 * 1024

ROW_BLOCK = 512
ATTN_Q_TILE = 1024
DIFF_Q_TILE = 512
ATTN_KV_CHUNK = 512
ATTN_EXP_ROWS = 4096
MOE_TOKENS = 768
MOE_ROWS = 128
MOE_PAIR = 8
MOE_SLOTS = 8


def _cparams(sem):
    return pltpu.CompilerParams(dimension_semantics=sem, vmem_limit_bytes=VMEM_LIMIT_BYTES)


def _rms(x, axis):
    return x * lax.rsqrt(jnp.mean(x * x, axis=axis, keepdims=True) + NORM_EPS)


def _silu(x):
    return x * jax.nn.sigmoid(x)


def _dot(a, b):
    return jnp.dot(a, b, preferred_element_type=F32)


def _dot_nt(a, b):
    return lax.dot_general(a, b, (((1,), (1,)), ((), ())), preferred_element_type=F32)


def _dot_tn(a, b):
    return lax.dot_general(a, b, (((0,), (0,)), ((), ())), preferred_element_type=F32)


def _ada_kernel(c_ref, w_ref, b_ref, o_ref):
    s = _silu(c_ref[...])
    o_ref[...] = jnp.dot(s, w_ref[...], preferred_element_type=F32, precision=lax.Precision.HIGHEST) + b_ref[...]


def _ada_call(c_all, ada_w, ada_b):
    depth, d, six_d = ada_w.shape
    rows = c_all.shape[0]
    cols = six_d // 4
    return pl.pallas_call(
        _ada_kernel,
        grid=(depth, six_d // cols),
        in_specs=[
            pl.BlockSpec((rows, d), lambda i, j: (0, 0)),
            pl.BlockSpec((None, d, cols), lambda i, j: (i, 0, j)),
            pl.BlockSpec((None, 1, cols), lambda i, j: (i, 0, j)),
        ],
        out_specs=pl.BlockSpec((None, rows, cols), lambda i, j: (i, 0, j)),
        out_shape=jax.ShapeDtypeStruct((depth, rows, six_d), F32),
        compiler_params=_cparams(("arbitrary", "arbitrary")),
        name="ada_mod",
    )(c_all, ada_w, ada_b.reshape(depth, 1, six_d))


def _attn_pre_kernel(hl_ref, hc_ref, mod_ref, gpre_ref, wfm_ref, wtm_ref, qn_ref, kvn_col_ref, kvn_row_ref, wq_ref,
                     wk_ref, wv_ref, gq_g_ref, gq_gr_ref, gk_g_ref, gk_gr_ref,
                     cqm_ref, sqm_ref, cqh_ref, sqh_ref, ckm_ref, skm_ref, ckh_ref, skh_ref,
                     qm_ref, km_ref, vm_ref, qg_ref, kg_ref, vg_ref, *, lat_blocks):
    h = _pick_part(lat_blocks, hl_ref, hc_ref)
    u = _rms(h, -1) * gpre_ref[...]
    u = u * (1.0 + mod_ref[1:2, :]) + mod_ref[0:1, :]
    ub = u.astype(BF16)
    zt = _dot_nt(wfm_ref[...], ub)
    zk = _dot(ub, wtm_ref[...])

    qn = (_rms(zt[0:256], 0) * qn_ref[...]).astype(BF16)
    qt = _dot(wq_ref[...], qn)
    cq, sq = cqm_ref[...], sqm_ref[...]
    qs = MLA_SCALE * LOG2E
    zero32 = jnp.zeros((32, h.shape[0]), BF16)
    for hd in range(MLA_HEADS):
        pe = qt[512 + 32 * hd:544 + 32 * hd] * cq + qt[768 + 32 * hd:800 + 32 * hd] * sq
        qm_ref[hd, 0:64, :] = (qt[64 * hd:64 * hd + 64] * qs).astype(BF16)
        qm_ref[hd, 64:96, :] = (pe * qs).astype(BF16)
        qm_ref[hd, 96:128, :] = zero32

    kvn_t = (_rms(zt[256:384], 0) * kvn_col_ref[...]).astype(BF16)
    vt = _dot(wv_ref[...], kvn_t)
    for hd in range(MLA_HEADS):
        vm_ref[hd] = vt[64 * hd:64 * hd + 64].astype(BF16)
    kvn = (_rms(zk[:, 0:128], -1) * kvn_row_ref[...]).astype(BF16)
    kn = _dot(kvn, wk_ref[...])
    kpe = zk[:, 128:256] * ckm_ref[...] + zk[:, 256:384] * skm_ref[...]
    for hd in range(MLA_HEADS):
        km_ref[hd] = (kn[:, 128 * hd:128 * hd + 128] + kpe).astype(BF16)

    cqh, sqh = cqh_ref[...], sqh_ref[...]
    gs = GQA_SCALE * LOG2E
    zero64 = jnp.zeros((64, h.shape[0]), BF16)
    grp = GQA_HEADS // GQA_KV_HEADS
    for hd in range(GQA_HEADS):
        raw = zt[384 + 64 * hd:448 + 64 * hd]
        rot = zt[896 + 64 * hd:960 + 64 * hd]
        r = lax.rsqrt(jnp.mean(raw * raw, axis=0, keepdims=True) + NORM_EPS)
        q = (raw * r * gq_g_ref[...]) * cqh + (rot * r * gq_gr_ref[...]) * sqh
        q = (q * gs).astype(BF16)
        if hd // grp == 0:
            qg_ref[hd, 0:64, :] = q
            qg_ref[hd, 64:128, :] = zero64
        else:
            qg_ref[hd, 0:64, :] = zero64
            qg_ref[hd, 64:128, :] = q
    for kvh in range(GQA_KV_HEADS):
        vg_ref[kvh] = zt[1408 + 64 * kvh:1472 + 64 * kvh].astype(BF16)

    gk, gkr = zk[:, 384:512], zk[:, 512:640]
    lane = lax.broadcasted_iota(jnp.int32, gk.shape, 1)
    lo = lane < 64
    sq0 = jnp.sum(jnp.where(lo, gk * gk, 0.0), axis=-1, keepdims=True)
    sq1 = jnp.sum(jnp.where(lo, 0.0, gk * gk), axis=-1, keepdims=True)
    r = jnp.where(lo, lax.rsqrt(sq0 / GQA_DIM + NORM_EPS), lax.rsqrt(sq1 / GQA_DIM + NORM_EPS))
    kg_ref[...] = ((gk * r * gk_g_ref[...]) * ckh_ref[...] + (gkr * r * gk_gr_ref[...]) * skh_ref[...]).astype(BF16)


def _rot_map(dim):
    q = dim // 4
    j = np.arange(dim)
    even = (j // q) % 2 == 0
    return np.where(even, j + q, j - q), np.where(even, -1.0, 1.0).astype(np.float32)


def _rope_tables(n_tokens, dim, pad_rows):
    rows = n_tokens // GRID_W
    row = jnp.repeat(jnp.arange(rows, dtype=F32), GRID_W)
    col = jnp.tile(jnp.arange(GRID_W, dtype=F32), rows)
    half = dim // 2
    inv = ROPE_BASE ** (-(jnp.arange(half // 2, dtype=F32) * 2.0 / half))
    ang_r = row[:, None] * inv
    ang_c = col[:, None] * inv
    ang = jnp.concatenate([ang_r, ang_r, ang_c, ang_c], axis=-1)
    cos = jnp.concatenate([jnp.cos(ang), jnp.ones((pad_rows, dim), F32)], axis=0)
    sin = jnp.concatenate([jnp.sin(ang), jnp.zeros((pad_rows, dim), F32)], axis=0)
    return cos, sin


class _Geom:
    def __init__(self, b, n, c):
        assert n % ROW_BLOCK == 0 and (b * c) % ROW_BLOCK == 0 and n % c == 0 and n % GRID_W == 0
        self.b, self.n, self.c = b, n, c
        self.t = b * n + b * c
        self.lat_blocks = b * n // ROW_BLOCK
        self.blocks_per_seq = n // ROW_BLOCK
        self.n_blocks = self.t // ROW_BLOCK
        assert self.t % MOE_TOKENS == 0

    def mod_row(self, i):
        return jnp.where(i < self.lat_blocks, i // self.blocks_per_seq, self.b)

    def pos_block(self, i):
        return jnp.where(i < self.lat_blocks, i % self.blocks_per_seq, self.blocks_per_seq)


def _full(shape):
    nd = len(shape)
    return pl.BlockSpec(shape, lambda *_: (0,) * nd)


def _part_specs(g, block, axis):
    def spec(latent):
        def index(i):
            j = jnp.minimum(i, g.lat_blocks - 1) if latent else jnp.maximum(i - g.lat_blocks, 0)
            return tuple(j if a == axis else 0 for a in range(len(block)))
        return pl.BlockSpec(block, index)
    return [spec(True), spec(False)]


def _pick_part(lat_blocks, lat_ref, ctx_ref):
    return jnp.where(pl.program_id(0) < lat_blocks, lat_ref[...], ctx_ref[...])


def _attn_pre_call(g, h_parts, mod, gpre, w):
    d = h_parts[0].shape[1]
    t = g.t
    rb = ROW_BLOCK
    row = lambda i: (i, 0)
    tm_tab = pl.BlockSpec((rb, LANES), lambda i: (g.pos_block(i), 0))
    fm32 = pl.BlockSpec((32, rb), lambda i: (0, g.pos_block(i)))
    fm64 = pl.BlockSpec((64, rb), lambda i: (0, g.pos_block(i)))
    in_specs = _part_specs(g, (rb, d), 0) + [
        pl.BlockSpec((None, 6, d), lambda i: (g.mod_row(i), 0, 0)),
        _full(gpre.shape), _full(w["wfm"].shape), _full(w["wtm"].shape), _full(w["qn"].shape),
        _full(w["kvn_col"].shape), _full(w["kvn_row"].shape), _full(w["wq"].shape), _full(w["wk"].shape),
        _full(w["wv"].shape), _full(w["gq_g"].shape), _full(w["gq_gr"].shape), _full(w["gk_g"].shape),
        _full(w["gk_gr"].shape),
        fm32, fm32, fm64, fm64, tm_tab, tm_tab, tm_tab, tm_tab,
    ]
    out_shape = [
        jax.ShapeDtypeStruct((MLA_HEADS, 128, t), BF16),
        jax.ShapeDtypeStruct((MLA_HEADS, t, 128), BF16),
        jax.ShapeDtypeStruct((MLA_HEADS, MLA_V, t), BF16),
        jax.ShapeDtypeStruct((GQA_HEADS, 128, t), BF16),
        jax.ShapeDtypeStruct((t, 128), BF16),
        jax.ShapeDtypeStruct((GQA_KV_HEADS, GQA_DIM, t), BF16),
    ]
    out_specs = [
        pl.BlockSpec((MLA_HEADS, 128, rb), lambda i: (0, 0, i)),
        pl.BlockSpec((MLA_HEADS, rb, 128), lambda i: (0, i, 0)),
        pl.BlockSpec((MLA_HEADS, MLA_V, rb), lambda i: (0, 0, i)),
        pl.BlockSpec((GQA_HEADS, 128, rb), lambda i: (0, 0, i)),
        pl.BlockSpec((rb, 128), row),
        pl.BlockSpec((GQA_KV_HEADS, GQA_DIM, rb), lambda i: (0, 0, i)),
    ]
    return pl.pallas_call(
        functools.partial(_attn_pre_kernel, lat_blocks=g.lat_blocks),
        grid=(g.n_blocks,), in_specs=in_specs, out_specs=out_specs, out_shape=out_shape,
        compiler_params=_cparams(("parallel",)), name="attn_pre",
    )(*h_parts, mod, gpre, w["wfm"], w["wtm"], w["qn"], w["kvn_col"], w["kvn_row"], w["wq"], w["wk"], w["wv"],
      w["gq_g"], w["gq_gr"], w["gk_g"], w["gk_gr"],
      w["cqm"], w["sqm"], w["cqh"], w["sqh"], w["ckm"], w["skm"], w["ckh"], w["skh"])


def _attn_pre_weights(g, att_w_in, q_norm, w_q_up, kv_norm, w_kv_up, gqa_qn, gqa_kn):
    d = att_w_in.shape[0]
    o = np.cumsum([0, MLA_Q_LORA, MLA_KV_LORA, MLA_ROPE, GQA_HEADS * GQA_DIM, GQA_KV_HEADS * GQA_DIM,
                   GQA_KV_HEADS * GQA_DIM])
    w_qlat, w_kvlat, w_kpe, w_gq, w_gk, w_gv = (att_w_in[:, o[i]:o[i + 1]] for i in range(6))
    src32, sgn32 = _rot_map(MLA_ROPE)
    src64, sgn64 = _rot_map(GQA_DIM)

    def rot_heads(wcols, heads, dim, src, sgn):
        wh = wcols.reshape(d, heads, dim)
        return (wh[:, :, src] * sgn).reshape(d, heads * dim)

    wfm = jnp.concatenate([w_qlat, w_kvlat, w_gq, rot_heads(w_gq, GQA_HEADS, GQA_DIM, src64, sgn64), w_gv], axis=1)
    zpad = lambda x, lo, hi: jnp.pad(x, ((0, 0), (lo, hi)))
    wtm = jnp.concatenate([
        w_kvlat, zpad(w_kpe, 64, 32), zpad(w_kpe[:, src32] * sgn32, 64, 32),
        w_gk, rot_heads(w_gk, GQA_KV_HEADS, GQA_DIM, src64, sgn64)], axis=1)
    wq = w_q_up.reshape(MLA_Q_LORA, MLA_HEADS, MLA_NOPE + MLA_ROPE)
    wq_pe = wq[:, :, MLA_NOPE:]
    wq_all = jnp.concatenate([
        wq[:, :, :MLA_NOPE].reshape(MLA_Q_LORA, -1), wq_pe.reshape(MLA_Q_LORA, -1),
        (wq_pe[:, :, src32] * sgn32).reshape(MLA_Q_LORA, -1)], axis=1)
    wkv = w_kv_up.reshape(MLA_KV_LORA, MLA_HEADS, MLA_NOPE + MLA_V)
    wk = jnp.pad(wkv[:, :, :MLA_NOPE], ((0, 0), (0, 0), (0, 128 - MLA_NOPE))).reshape(MLA_KV_LORA, -1)
    wv = wkv[:, :, MLA_NOPE:].reshape(MLA_KV_LORA, -1)
    cos_m, sin_m = _rope_tables(g.n, MLA_ROPE, ROW_BLOCK)
    cos_h, sin_h = _rope_tables(g.n, GQA_DIM, ROW_BLOCK)
    two = lambda x: jnp.concatenate([x, x], axis=1)
    return dict(
        wfm=wfm.T.astype(BF16), wtm=wtm.astype(BF16),
        qn=q_norm.reshape(-1, 1), kvn_col=kv_norm.reshape(-1, 1), kvn_row=kv_norm.reshape(1, -1),
        wq=wq_all.T.astype(BF16), wk=wk.astype(BF16), wv=wv.T.astype(BF16),
        gq_g=gqa_qn.reshape(-1, 1), gq_gr=gqa_qn[src64].reshape(-1, 1),
        gk_g=two(gqa_kn.reshape(1, -1)), gk_gr=two(gqa_kn[src64].reshape(1, -1)),
        cqm=cos_m.T, sqm=sin_m.T, cqh=cos_h.T, sqh=sin_h.T,
        ckm=zpad(cos_m, 64, 32), skm=zpad(sin_m, 64, 32), ckh=two(cos_h), skh=two(sin_h),
    )


def _fold8(x, op):
    r, w = x.shape
    return op(x.reshape(r // 8, 8, w), axis=0)


def _attend(qt, segments, s_ref, dv):
    tq = qt.shape[1]
    pieces, off = [], 0
    for k_ref, vt_ref in segments:
        n = k_ref.shape[0]
        for lo in range(0, n, ATTN_KV_CHUNK):
            rows = min(ATTN_KV_CHUNK, n - lo)
            pieces.append((off, lo, rows, k_ref, vt_ref))
            off += rows
    m8 = None
    for so, lo, rows, k_ref, _ in pieces:
        s = _dot(k_ref[lo:lo + rows, :], qt)
        s_ref[so:so + rows, :] = s
        part = _fold8(s, jnp.max)
        m8 = part if m8 is None else jnp.maximum(m8, part)
    m = jnp.max(m8, axis=0, keepdims=True)
    acc = jnp.zeros((dv + 16, tq), F32)
    off = 0
    for _, vt_ref in segments:
        n = vt_ref.shape[1]
        step = min(n, ATTN_EXP_ROWS)

        def body(i, acc, off=off, vt_ref=vt_ref, step=step):
            base = i * step if isinstance(i, int) else pl.multiple_of(i * step, step)
            for lo in range(0, step, ATTN_KV_CHUNK):
                rows = min(ATTN_KV_CHUNK, step - lo)
                p = jnp.exp2((s_ref[pl.ds(off + base + lo, rows), :] - m).astype(BF16))
                vt1 = jnp.concatenate([vt_ref[:, pl.ds(base + lo, rows)], jnp.ones((16, rows), BF16)], axis=0)
                acc = acc + _dot(vt1, p)
            return acc

        acc = body(0, acc) if n == step else lax.fori_loop(0, n // step, body, acc)
        off += n
    return acc[0:dv] / acc[dv:dv + 1]


def _attn_kernel(qt_ref, *refs, dv):
    o_ref, s_ref = refs[-2:]
    kv = refs[:-2]
    half = len(kv) // 2
    segs = [(kv[i], kv[half + i]) for i in range(half)]
    o_ref[...] = _attend(qt_ref[...], segs, s_ref, dv).astype(o_ref.dtype)


def _attn_call(g, qt, k, vt, *, kv_heads, shared_k, name):
    heads = qt.shape[0]
    dv = vt.shape[1]
    grp = heads // kv_heads
    tq = ATTN_Q_TILE
    nq = g.n // tq
    ctx_blk = g.b * g.n // g.c
    if shared_k:
        klat = pl.BlockSpec((g.n, 128), lambda b, h, *_: (b, 0))
        kctx = pl.BlockSpec((g.c, 128), lambda b, h, *_: (ctx_blk + b, 0))
    else:
        klat = pl.BlockSpec((None, g.n, 128), lambda b, h, *_: (h // grp, b, 0))
        kctx = pl.BlockSpec((None, g.c, 128), lambda b, h, *_: (h // grp, ctx_blk + b, 0))
    vlat = pl.BlockSpec((None, dv, g.n), lambda b, h, *_: (h // grp, 0, b))
    vctx = pl.BlockSpec((None, dv, g.c), lambda b, h, *_: (h // grp, 0, ctx_blk + b))
    lat = pl.pallas_call(
        functools.partial(_attn_kernel, dv=dv),
        grid=(g.b, heads, nq),
        scratch_shapes=[pltpu.VMEM((g.n + g.c, tq), F32)],
        in_specs=[pl.BlockSpec((None, 128, tq), lambda b, h, qi: (h, 0, b * nq + qi)), klat, kctx, vlat, vctx],
        out_specs=pl.BlockSpec((dv, tq), lambda b, h, qi: (h, b * nq + qi)),
        out_shape=jax.ShapeDtypeStruct((heads * dv, g.b * g.n), BF16),
        compiler_params=_cparams(("parallel", "parallel", "arbitrary")),
        name=name,
    )(qt, k, k, vt, vt)
    ctx = pl.pallas_call(
        functools.partial(_attn_kernel, dv=dv),
        grid=(g.b, heads),
        scratch_shapes=[pltpu.VMEM((g.c, g.c), F32)],
        in_specs=[pl.BlockSpec((None, 128, g.c), lambda b, h: (h, 0, ctx_blk + b)), kctx, vctx],
        out_specs=pl.BlockSpec((dv, g.c), lambda b, h: (h, b)),
        out_shape=jax.ShapeDtypeStruct((heads * dv, g.b * g.c), BF16),
        compiler_params=_cparams(("parallel", "parallel")),
        name=name + "_ctx",
    )(qt, k, vt)
    return lat, ctx


def _split_bf16(x):
    hi = x.astype(BF16)
    return hi, (x - hi.astype(F32)).astype(BF16)


def _post_tail(o, h, mod_ref, gpost_ref, gffn_ref, rw_ref, hn_ref, v_ref, lg_ref):
    hn = h + mod_ref[2:3, :] * (_rms(o, -1) * gpost_ref[...])
    hn_ref[...] = hn
    v = (_rms(hn, -1) * gffn_ref[...]) * (1.0 + mod_ref[4:5, :]) + mod_ref[3:4, :]
    v_hi, v_lo = _split_bf16(v)
    v_ref[...] = v_hi
    w_hi, w_lo = _split_bf16(rw_ref[...])
    lg_ref[...] = _dot_nt(w_hi, v_hi) + (_dot_nt(w_hi, v_lo) + _dot_nt(w_lo, v_hi))


def _attn_post_kernel(oal_ref, oac_ref, obl_ref, obc_ref, wa_ref, wb_ref, hl_ref, hc_ref, mod_ref, gpost_ref, gffn_ref,
                      rw_ref, hn_ref, v_ref, lg_ref, *, lat_blocks):
    oa = _pick_part(lat_blocks, oal_ref, oac_ref)
    ob = _pick_part(lat_blocks, obl_ref, obc_ref)
    o = _dot_tn(oa, wa_ref[...]) + _dot_tn(ob, wb_ref[...])
    h = _pick_part(lat_blocks, hl_ref, hc_ref)
    _post_tail(o, h, mod_ref, gpost_ref, gffn_ref, rw_ref, hn_ref, v_ref, lg_ref)


def _post_specs(g, d):
    rb = ROW_BLOCK
    ins = [
        pl.BlockSpec((None, 6, d), lambda i: (g.mod_row(i), 0, 0)),
        _full((1, d)), _full((1, d)), _full((N_EXPERTS, d)),
    ]
    outs = [pl.BlockSpec((rb, d), lambda i: (i, 0)), pl.BlockSpec((rb, d), lambda i: (i, 0)),
            pl.BlockSpec((N_EXPERTS, rb), lambda i: (0, i))]
    shapes = [jax.ShapeDtypeStruct((g.t, d), F32), jax.ShapeDtypeStruct((g.t, d), BF16),
              jax.ShapeDtypeStruct((N_EXPERTS, g.t), F32)]
    return ins, outs, shapes


def _attn_post_call(g, oa_parts, ob_parts, w_out, h_parts, mod, gpost, gffn, router_w):
    d = h_parts[0].shape[1]
    rb = ROW_BLOCK
    half = oa_parts[0].shape[0]
    ins, outs, shapes = _post_specs(g, d)
    wo = w_out.astype(BF16)
    o_specs = _part_specs(g, (half, rb), 1)
    return pl.pallas_call(
        functools.partial(_attn_post_kernel, lat_blocks=g.lat_blocks), grid=(g.n_blocks,),
        in_specs=o_specs + o_specs + [_full((half, d)), _full((half, d))] + _part_specs(g, (rb, d), 0) + ins,
        out_specs=outs, out_shape=shapes,
        compiler_params=_cparams(("parallel",)), name="attn_post",
    )(*oa_parts, *ob_parts, wo[:half], wo[half:], *h_parts, mod, gpost.reshape(1, d), gffn.reshape(1, d), router_w.T)


def _router_kernel(lg_ref, bias_ref, tri_ref, gate_ref, rank_ref, cnt_ref):
    tb = lg_ref.shape[1]
    per = N_EXPERTS // N_GROUPS
    shp = (N_GROUPS, per, tb)
    scores = jax.nn.sigmoid(lg_ref[...])
    s3 = scores.reshape(shp)
    sel = (scores + bias_ref[...]).reshape(shp)
    sub = lax.broadcasted_iota(jnp.int32, shp, 1)
    grp = lax.broadcasted_iota(jnp.int32, shp, 0)
    neg = -jnp.inf
    m1 = jnp.max(sel, axis=1, keepdims=True)
    i1 = jnp.min(jnp.where(sel == m1, sub, per), axis=1, keepdims=True)
    m2 = jnp.max(jnp.where(sub == i1, neg, sel), axis=1, keepdims=True)
    cur = jnp.broadcast_to(m1 + m2, shp)
    gmask = jnp.zeros(shp, F32)
    for _ in range(TOPK_GROUPS):
        gm = jnp.max(cur, axis=0, keepdims=True)
        gi = jnp.min(jnp.where(cur == gm, grp, N_GROUPS), axis=0, keepdims=True)
        pick = grp == gi
        gmask = jnp.where(pick, 1.0, gmask)
        cur = jnp.where(pick, neg, cur)
    masked = jnp.where(gmask > 0.0, sel, neg)
    eidx = grp * per + sub
    chosen = jnp.zeros(shp, F32)
    for _ in range(TOP_K):
        mx = jnp.max(jnp.max(masked, axis=1, keepdims=True), axis=0, keepdims=True)
        cand = jnp.where(masked == mx, eidx, N_EXPERTS)
        ei = jnp.min(jnp.min(cand, axis=1, keepdims=True), axis=0, keepdims=True)
        pick = eidx == ei
        chosen = jnp.where(pick, 1.0, chosen)
        masked = jnp.where(pick, neg, masked)
    top_w = jnp.where(chosen > 0.0, s3, 0.0)
    denom = jnp.sum(jnp.sum(top_w, axis=1, keepdims=True), axis=0, keepdims=True)
    gate_ref[...] = (top_w / denom * ROUTED_SCALE).reshape(N_EXPERTS, tb)
    ch2 = chosen.reshape(N_EXPERTS, tb)
    chb = ch2.astype(BF16)
    before = _dot(chb, tri_ref[...])
    rank_ref[...] = jnp.where(ch2 > 0.0, before, -1.0).astype(jnp.int32)
    cnt_ref[...] = _dot(chb, jnp.ones((tb, LANES), BF16))


def _router_call(g, logits_t, router_b, nsb):
    tb = MOE_TOKENS
    tri = (np.arange(tb)[:, None] < np.arange(tb)[None, :]).astype(np.float32)
    blk = pl.BlockSpec((N_EXPERTS, tb), lambda s: (0, s))
    gates, rank, cnt = pl.pallas_call(
        _router_kernel, grid=(nsb,),
        in_specs=[blk, _full((N_EXPERTS, 1)), _full((tb, tb))],
        out_specs=[blk, blk, pl.BlockSpec((None, N_EXPERTS, LANES), lambda s: (s, 0, 0))],
        out_shape=[jax.ShapeDtypeStruct((N_EXPERTS, g.t), F32), jax.ShapeDtypeStruct((N_EXPERTS, g.t), jnp.int32),
                   jax.ShapeDtypeStruct((nsb, N_EXPERTS, LANES), F32)],
        compiler_params=_cparams(("parallel",)), name="router",
    )(logits_t, router_b.reshape(-1, 1), jnp.asarray(tri, BF16))
    return gates, rank, cnt[:, :, 0].astype(jnp.int32).reshape(-1)


def _moe_kernel(cnt_ref, v_ref, gate_ref, rank_ref, wg_ref, wu_ref, wd_ref, sg_ref, su_ref, sd_ref, o_ref,
                psel_ref, ysel_ref, slot_ref):
    s, j = pl.program_id(0), pl.program_id(1)
    tb = v_ref.shape[0]
    rows = MOE_ROWS

    @pl.when(j == 0)
    def _():
        vb = v_ref[...]
        hid = (_silu(_dot(vb, sg_ref[...])) * _dot(vb, su_ref[...])).astype(BF16)
        o_ref[...] = _dot(hid, sd_ref[...])
        psel_ref[...] = jnp.zeros_like(psel_ref)
        ysel_ref[...] = jnp.zeros_like(ysel_ref)
        slot_ref[0] = 0

    row_id = lax.broadcasted_iota(jnp.int32, (rows, tb), 0)

    def flush(n_slots):
        stage_row = lax.broadcasted_iota(jnp.int32, psel_ref.shape, 0)
        sel = jnp.where(stage_row < n_slots * rows, psel_ref[...], jnp.zeros_like(psel_ref))
        o_ref[...] += _dot_tn(sel, ysel_ref[...])

    ends, total = [], 0
    for e in range(MOE_PAIR):
        total = total + (cnt_ref[s * N_EXPERTS + j * MOE_PAIR + e] + rows - 1) // rows
        ends.append(total)

    def select(i):
        e = sum((i >= end).astype(jnp.int32) for end in ends[:-1])
        first = sum(jnp.where(e == k + 1, ends[k], 0) for k in range(MOE_PAIR - 1))
        ex = j * MOE_PAIR + e
        hit = row_id == rank_ref[pl.ds(ex, 1), :] - (i - first) * rows
        onehot = jnp.where(hit, 1.0, 0.0).astype(BF16)
        w_row = jnp.sum(jnp.where(hit, gate_ref[pl.ds(ex, 1), :], 0.0), axis=1, keepdims=True)
        return e, onehot, w_row

    def run_items(i0, count, slot):
        picks = [select(i0 + k) for k in range(count)]
        sel = picks[0][1] if count == 1 else jnp.concatenate([p[1] for p in picks], axis=0)
        at = pl.multiple_of(slot * rows, rows)
        psel_ref[pl.ds(at, count * rows), :] = sel
        xs = _dot(sel, v_ref[...]).astype(BF16)
        hids = [(_silu(_dot(xs[k * rows:(k + 1) * rows], wg_ref[e])) * _dot(xs[k * rows:(k + 1) * rows], wu_ref[e]))
                .astype(BF16) for k, (e, _, _) in enumerate(picks)]
        outs = [(_dot(hid, wd_ref[e]) * w_row).astype(BF16) for hid, (e, _, w_row) in zip(hids, picks)]
        ysel_ref[pl.ds(at, count * rows), :] = outs[0] if count == 1 else jnp.concatenate(outs, axis=0)

        @pl.when(slot + count == MOE_SLOTS)
        def _():
            flush(MOE_SLOTS)

        return jnp.where(slot + count == MOE_SLOTS, 0, slot + count)

    def pair(it, slot):
        no_room = slot > MOE_SLOTS - 2

        @pl.when(no_room)
        def _():
            flush(slot)

        return run_items(2 * it, 2, jnp.where(no_room, 0, slot))

    slot_ref[0] = lax.fori_loop(0, total // 2, pair, slot_ref[0])

    @pl.when(total % 2 == 1)
    def _():
        slot_ref[0] = run_items(total - 1, 1, slot_ref[0])

    @pl.when(jnp.logical_and(j == pl.num_programs(1) - 1, slot_ref[0] > 0))
    def _():
        flush(slot_ref[0])


def _moe_call(g, v, gates, rank, counts, nsb, wg, wu, wd, sg, su, sd):
    t, d = v.shape
    tb = MOE_TOKENS
    ff = wg.shape[2]
    grid_spec = pltpu.PrefetchScalarGridSpec(
        num_scalar_prefetch=1,
        grid=(nsb, N_EXPERTS // MOE_PAIR),
        in_specs=[
            pl.BlockSpec((tb, d), lambda s, j, c: (s, 0)),
            pl.BlockSpec((N_EXPERTS, tb), lambda s, j, c: (0, s)),
            pl.BlockSpec((N_EXPERTS, tb), lambda s, j, c: (0, s)),
            pl.BlockSpec((MOE_PAIR, d, ff), lambda s, j, c: (j, 0, 0)),
            pl.BlockSpec((MOE_PAIR, d, ff), lambda s, j, c: (j, 0, 0)),
            pl.BlockSpec((MOE_PAIR, ff, d), lambda s, j, c: (j, 0, 0)),
            pl.BlockSpec((d, sg.shape[1]), lambda s, j, c: (0, 0)),
            pl.BlockSpec((d, su.shape[1]), lambda s, j, c: (0, 0)),
            pl.BlockSpec((sd.shape[0], d), lambda s, j, c: (0, 0)),
        ],
        out_specs=pl.BlockSpec((tb, d), lambda s, j, c: (s, 0)),
        scratch_shapes=[pltpu.VMEM((MOE_SLOTS * MOE_ROWS, tb), BF16), pltpu.VMEM((MOE_SLOTS * MOE_ROWS, d), BF16),
                        pltpu.SMEM((1,), jnp.int32)],
    )
    return pl.pallas_call(
        _moe_kernel, grid_spec=grid_spec, out_shape=jax.ShapeDtypeStruct((t, d), F32),
        compiler_params=_cparams(("parallel", "arbitrary")), name="moe",
    )(counts, v, gates, rank, wg.astype(BF16), wu.astype(BF16), wd.astype(BF16),
      sg.astype(BF16), su.astype(BF16), sd.astype(BF16))


def _rec_pre_kernel(h_ref, f_ref, pmod_ref, gprev_ref, mod_ref, gpre_ref, wtm_ref, wfm_ref, cqh_ref, sqh_ref, ckh_ref,
                    skh_ref, hn_ref, z_ref, xbc_ref, dt_ref, kd_ref, qd_ref, vd_ref):
    h = h_ref[...] + pmod_ref[5:6, :] * (_rms(f_ref[...], -1) * gprev_ref[...])
    hn_ref[...] = h
    u = _rms(h, -1) * gpre_ref[...]
    u = u * (1.0 + mod_ref[1:2, :]) + mod_ref[0:1, :]
    ub = u.astype(BF16)
    zk = _dot(ub, wtm_ref[...])
    zt = _dot_nt(wfm_ref[...], ub)
    z_ref[...] = zk[:, 0:512].astype(z_ref.dtype)
    xbc_ref[...] = zk[:, 512:1536]
    dt_ref[...] = zk[:, 1536:1664]
    ck, sk = ckh_ref[...], skh_ref[...]
    for j in range(DIFF_HEADS):
        lo = 1664 + 128 * j
        kd_ref[j] = (zk[:, lo:lo + 128] * ck + zk[:, lo + 512:lo + 640] * sk).astype(BF16)
    cq, sq = cqh_ref[...], sqh_ref[...]
    qs = DIFF_SCALE * LOG2E
    zero64 = jnp.zeros((64, h.shape[0]), BF16)
    for hd in range(2 * DIFF_HEADS):
        q = ((zt[64 * hd:64 * hd + 64] * cq + zt[512 + 64 * hd:576 + 64 * hd] * sq) * qs).astype(BF16)
        if hd % 2 == 0:
            qd_ref[hd, 0:64, :] = q
            qd_ref[hd, 64:128, :] = zero64
        else:
            qd_ref[hd, 0:64, :] = zero64
            qd_ref[hd, 64:128, :] = q
    for j in range(DIFF_HEADS):
        vd_ref[j] = zt[1024 + 128 * j:1152 + 128 * j].astype(BF16)


def _rec_pre_weights(g, rec_w_in):
    d = rec_w_in.shape[0]
    o = np.cumsum([0, SSD_INNER, SSD_XBC, SSD_HEADS, SSD_HEADS, 2 * DIFF_HEADS * DIFF_DIM, 2 * DIFF_HEADS * DIFF_DIM,
                   DIFF_HEADS * DIFF_V])
    w_z, w_xbc, w_dtf, w_dtb, w_dq, w_dk, w_dv = (rec_w_in[:, o[i]:o[i + 1]] for i in range(7))
    src64, sgn64 = _rot_map(DIFF_DIM)

    def rot_heads(wcols):
        wh = wcols.reshape(d, 2 * DIFF_HEADS, DIFF_DIM)
        return (wh[:, :, src64] * sgn64).reshape(d, -1)

    w_dt = jnp.pad(jnp.concatenate([w_dtf, w_dtb], axis=1), ((0, 0), (0, LANES - 2 * SSD_HEADS)))
    wtm = jnp.concatenate([w_z, w_xbc, w_dt, w_dk, rot_heads(w_dk)], axis=1)
    wfm = jnp.concatenate([w_dq, rot_heads(w_dq), w_dv], axis=1)
    cos_h, sin_h = _rope_tables(g.n, DIFF_DIM, ROW_BLOCK)
    two = lambda x: jnp.concatenate([x, x], axis=1)
    return dict(wtm=wtm.astype(BF16), wfm=wfm.T.astype(BF16), cqh=cos_h.T, sqh=sin_h.T, ckh=two(cos_h), skh=two(sin_h))


def _rec_pre_call(g, h, f, prev_mod, prev_gain, mod, gpre, w):
    t, d = h.shape
    rb = ROW_BLOCK
    row = lambda i: (i, 0)
    modspec = pl.BlockSpec((None, 6, d), lambda i: (g.mod_row(i), 0, 0))
    tm_tab = pl.BlockSpec((rb, LANES), lambda i: (g.pos_block(i), 0))
    fm64 = pl.BlockSpec((64, rb), lambda i: (0, g.pos_block(i)))
    nd = 2 * DIFF_HEADS
    return pl.pallas_call(
        _rec_pre_kernel, grid=(g.n_blocks,),
        in_specs=[pl.BlockSpec((rb, d), row), pl.BlockSpec((rb, d), row), modspec, _full((1, d)), modspec,
                  _full(gpre.shape), _full(w["wtm"].shape), _full(w["wfm"].shape), fm64, fm64, tm_tab, tm_tab],
        out_specs=[
            pl.BlockSpec((rb, d), row), pl.BlockSpec((rb, SSD_INNER), row), pl.BlockSpec((rb, SSD_XBC), row), pl.BlockSpec((rb, LANES), row),
            pl.BlockSpec((DIFF_HEADS, rb, 128), lambda i: (0, i, 0)),
            pl.BlockSpec((nd, 128, rb), lambda i: (0, 0, i)),
            pl.BlockSpec((DIFF_HEADS, DIFF_V, rb), lambda i: (0, 0, i)),
        ],
        out_shape=[
            jax.ShapeDtypeStruct((t, d), F32),
            jax.ShapeDtypeStruct((t, SSD_INNER), BF16), jax.ShapeDtypeStruct((t, SSD_XBC), F32),
            jax.ShapeDtypeStruct((t, LANES), F32), jax.ShapeDtypeStruct((DIFF_HEADS, t, 128), BF16),
            jax.ShapeDtypeStruct((nd, 128, t), BF16), jax.ShapeDtypeStruct((DIFF_HEADS, DIFF_V, t), BF16),
        ],
        compiler_params=_cparams(("parallel",)), name="rec_pre",
    )(h, f, prev_mod, prev_gain.reshape(1, d), mod, gpre, w["wtm"], w["wfm"], w["cqh"], w["sqh"], w["ckh"], w["skh"])


def _conv_kernel(x_ref, prev_ref, next_ref, w_ref, b_ref, o_ref, *, seq_blocks, lat_blocks):
    i = pl.program_id(0)
    rb = x_ref.shape[0]
    pos = i % seq_blocks
    is_ctx = i >= lat_blocks
    first = jnp.logical_or(is_ctx, pos == 0)
    last = jnp.logical_or(is_ctx, pos == seq_blocks - 1)
    x = x_ref[...]
    prev_row = jnp.where(first, 0.0, prev_ref[7:8, :])
    next_row = jnp.where(last, 0.0, next_ref[0:1, :])
    rid = lax.broadcasted_iota(jnp.int32, x.shape, 0)
    x_prev = jnp.where(rid == 0, prev_row, pltpu.roll(x, 1, axis=0))
    x_next = jnp.where(rid == rb - 1, next_row, pltpu.roll(x, rb - 1, axis=0))
    y = w_ref[0:1, :] * x_prev + w_ref[1:2, :] * x + w_ref[2:3, :] * x_next + b_ref[...]
    o_ref[...] = _silu(y)


def _conv_call(g, xbc, conv_w, conv_b):
    t, ch = xbc.shape
    rb = g.c
    halo = 8
    per = rb // halo
    last_halo = t // halo - 1
    return pl.pallas_call(
        functools.partial(_conv_kernel, seq_blocks=g.n // rb, lat_blocks=g.b * g.n // rb),
        grid=(t // rb,),
        in_specs=[
            pl.BlockSpec((rb, ch), lambda i: (i, 0)),
            pl.BlockSpec((halo, ch), lambda i: (jnp.maximum(i * per - 1, 0), 0)),
            pl.BlockSpec((halo, ch), lambda i: (jnp.minimum((i + 1) * per, last_halo), 0)),
            _full((SSD_CONV, ch)), _full((1, ch)),
        ],
        out_specs=pl.BlockSpec((rb, ch), lambda i: (i, 0)),
        out_shape=jax.ShapeDtypeStruct((t, ch), F32),
        compiler_params=_cparams(("parallel",)), name="ssd_conv",
    )(xbc, xbc, xbc, conv_w, conv_b.reshape(1, ch))


def _ssd_kernel(xf_ref, dtf_ref, xb_ref, dtb_ref, bias_ref, alog_ref, tril_ref, triu_ref, eye_ref, yf_ref, yb_ref,
                sf_ref, sb_ref):
    @pl.when(pl.program_id(1) == 0)
    def _():
        sf_ref[...] = jnp.zeros_like(sf_ref)
        sb_ref[...] = jnp.zeros_like(sb_ref)

    _ssd_chunk(xf_ref, dtf_ref, bias_ref, alog_ref, tril_ref, triu_ref, eye_ref, yf_ref, sf_ref, 0, False)
    _ssd_chunk(xb_ref, dtb_ref, bias_ref, alog_ref, triu_ref, tril_ref, eye_ref, yb_ref, sb_ref, SSD_HEADS, True)


def _ssd_chunk(x_ref, dt_ref, bias_ref, alog_ref, tri_ref, trit_ref, eye_ref, y_ref, state_ref, lane_off, reverse):
    ln = SSD_CHUNK
    hi = lax.Precision.HIGHEST
    x = x_ref[:, 0:SSD_INNER]
    raw = dt_ref[...] + bias_ref[...]
    dt = jnp.maximum(raw, 0.0) + jnp.log1p(jnp.exp(-jnp.abs(raw)))
    adt = dt * (-jnp.exp(alog_ref[...]))
    acs = jnp.dot(tri_ref[...], adt, preferred_element_type=F32, precision=hi)
    acs_row = lax.dot_general(adt, trit_ref[...], (((0,), (0,)), ((), ())), preferred_element_type=F32, precision=hi)
    dt_row = lax.dot_general(dt, eye_ref[...], (((0,), (0,)), ((), ())), preferred_element_type=F32, precision=hi)
    tot = acs[0:1, :] if reverse else acs[ln - 1:ln, :]
    w_all = jnp.exp(tot - acs) * dt
    ea_all = jnp.exp(acs)
    etot = jnp.exp(tot)
    li = lax.broadcasted_iota(jnp.int32, (ln, ln), 0)
    si = lax.broadcasted_iota(jnp.int32, (ln, ln), 1)
    keep = (si >= li) if reverse else (si <= li)
    lane = lax.broadcasted_iota(jnp.int32, (ln, LANES), 1)
    left = lane < SSD_HEAD_DIM
    per_group = SSD_HEADS // SSD_GROUPS
    for gi in range(SSD_GROUPS):
        bm = x_ref[:, SSD_INNER + SSD_STATE * gi:SSD_INNER + SSD_STATE * (gi + 1)].astype(BF16)
        cm = x_ref[:, SSD_INNER + SSD_STATE * (SSD_GROUPS + gi):SSD_INNER + SSD_STATE * (SSD_GROUPS + gi + 1)]
        cm = cm.astype(BF16)
        cb = _dot_nt(cm, bm)
        for pr in range(per_group // 2):
            h0 = gi * per_group + 2 * pr
            xp = x[:, SSD_HEAD_DIM * h0:SSD_HEAD_DIM * (h0 + 2)]
            mats = []
            for hd in (h0, h0 + 1):
                c = lane_off + hd
                seg = acs[:, c:c + 1] - acs_row[c:c + 1, :]
                lmat = jnp.exp(jnp.where(keep, seg, -jnp.inf))
                mats.append((cb * lmat * dt_row[c:c + 1, :]).astype(BF16))
            xb = xp.astype(BF16)
            zero = jnp.zeros_like(xb)
            rhs = jnp.concatenate([jnp.where(left, xb, zero), jnp.where(left, zero, xb)], axis=0)
            y_diag = _dot(jnp.concatenate(mats, axis=1), rhs)
            st = state_ref[h0 // 2]
            c0 = lane_off + h0
            ea = jnp.where(left, ea_all[:, c0:c0 + 1], ea_all[:, c0 + 1:c0 + 2])
            y_off = _dot(cm, st.astype(BF16)) * ea
            y_ref[:, SSD_HEAD_DIM * h0:SSD_HEAD_DIM * (h0 + 2)] = y_diag + y_off
            wcol = jnp.where(left, w_all[:, c0:c0 + 1], w_all[:, c0 + 1:c0 + 2])
            cs = _dot_tn(bm, (xp * wcol).astype(BF16))
            dec = jnp.where(left[0:1, :], etot[:, c0:c0 + 1], etot[:, c0 + 1:c0 + 2])
            state_ref[h0 // 2] = st * dec + cs


def _ssd_call(g, xbc_act, dt, dt_bias_f, dt_bias_b, a_log_f, a_log_b):
    t = xbc_act.shape[0]
    ln = SSD_CHUNK
    cc = g.c // ln
    nl = g.n // ln
    ctx0 = g.b * g.n // ln
    fwd = lambda b, k: (jnp.where(k < cc, ctx0 + b * cc + k, b * nl + (k - cc)), 0)
    bwd = lambda b, k: (jnp.where(k < cc, ctx0 + b * cc + (cc - 1 - k), b * nl + (nl - 1 - (k - cc))), 0)
    idx = np.arange(ln)
    lower = (idx[:, None] >= idx[None, :]).astype(np.float32)
    lanes = lambda pf, pb: jnp.pad(jnp.concatenate([pf, pb]).reshape(1, -1), ((0, 0), (0, LANES - 2 * SSD_HEADS)))
    state = pltpu.VMEM((SSD_HEADS // 2, SSD_STATE, 2 * SSD_HEAD_DIM), F32)
    y_shape = jax.ShapeDtypeStruct((t, SSD_INNER), F32)
    return pl.pallas_call(
        _ssd_kernel, grid=(g.b, cc + nl),
        in_specs=[pl.BlockSpec((ln, SSD_XBC), fwd), pl.BlockSpec((ln, LANES), fwd),
                  pl.BlockSpec((ln, SSD_XBC), bwd), pl.BlockSpec((ln, LANES), bwd),
                  _full((1, LANES)), _full((1, LANES)), _full((ln, ln)), _full((ln, ln)), _full((ln, ln))],
        out_specs=[pl.BlockSpec((ln, SSD_INNER), fwd), pl.BlockSpec((ln, SSD_INNER), bwd)],
        out_shape=[y_shape, y_shape],
        scratch_shapes=[state, state],
        compiler_params=_cparams(("parallel", "arbitrary")),
        name="ssd_scan",
    )(xbc_act, dt, xbc_act, dt, lanes(dt_bias_f, dt_bias_b), lanes(a_log_f, a_log_b), jnp.asarray(lower),
      jnp.asarray(lower.T), jnp.eye(ln, dtype=F32))


def _diff_finish(o, tq, lq1_ref, lk1_ref, lq2_ref, lk2_ref, sub_ref, o_ref, lambda_init):
    lam = (jnp.exp(jnp.sum(lq1_ref[...] * lk1_ref[...], axis=1, keepdims=True))
           - jnp.exp(jnp.sum(lq2_ref[...] * lk2_ref[...], axis=1, keepdims=True)) + lambda_init)
    od = o[:, 0:tq] - lam * o[:, tq:2 * tq]
    o_ref[...] = ((_rms(od, 0) * sub_ref[...]) * (1.0 - lambda_init)).astype(o_ref.dtype)


def _diff_attn_kernel(qt_ref, *refs, lambda_init):
    o_ref, s_ref = refs[-2:]
    params = refs[-7:-2]
    kv = refs[:-7]
    half = len(kv) // 2
    segs = [(kv[i], kv[half + i]) for i in range(half)]
    qt = jnp.concatenate([qt_ref[0], qt_ref[1]], axis=1)
    o = _attend(qt, segs, s_ref, DIFF_V)
    _diff_finish(o, qt_ref.shape[2], *params, o_ref, lambda_init)


def _diff_attn_call(g, qd, kd, vd, lq1, lk1, lq2, lk2, subln, lambda_init):
    tq = DIFF_Q_TILE
    nq = g.n // tq
    ctx_blk = g.b * g.n // g.c
    vec = lambda a: a.reshape(1, -1)
    params = [_full((1, DIFF_DIM)), _full((1, DIFF_DIM)), _full((1, DIFF_DIM)), _full((1, DIFF_DIM)),
              _full((DIFF_V, 1))]
    pvals = (vec(lq1), vec(lk1), vec(lq2), vec(lk2), subln.reshape(-1, 1))
    klat = pl.BlockSpec((None, g.n, 128), lambda b, j, *_: (j, b, 0))
    kctx = pl.BlockSpec((None, g.c, 128), lambda b, j, *_: (j, ctx_blk + b, 0))
    vlat = pl.BlockSpec((None, DIFF_V, g.n), lambda b, j, *_: (j, 0, b))
    vctx = pl.BlockSpec((None, DIFF_V, g.c), lambda b, j, *_: (j, 0, ctx_blk + b))
    kern = functools.partial(_diff_attn_kernel, lambda_init=lambda_init)
    lat = pl.pallas_call(
        kern, grid=(g.b, DIFF_HEADS, nq),
        scratch_shapes=[pltpu.VMEM((g.n + g.c, 2 * tq), F32)],
        in_specs=[pl.BlockSpec((2, 128, tq), lambda b, j, qi: (j, 0, b * nq + qi)), klat, kctx, vlat, vctx] + params,
        out_specs=pl.BlockSpec((DIFF_V, tq), lambda b, j, qi: (j, b * nq + qi)),
        out_shape=jax.ShapeDtypeStruct((DIFF_HEADS * DIFF_V, g.b * g.n), BF16),
        compiler_params=_cparams(("parallel", "parallel", "arbitrary")),
        name="attn_diff",
    )(qd, kd, kd, vd, vd, *pvals)
    ctx = pl.pallas_call(
        kern, grid=(g.b, DIFF_HEADS),
        scratch_shapes=[pltpu.VMEM((g.c, 2 * g.c), F32)],
        in_specs=[pl.BlockSpec((2, 128, g.c), lambda b, j: (j, 0, ctx_blk + b)), kctx, vctx] + params,
        out_specs=pl.BlockSpec((DIFF_V, g.c), lambda b, j: (j, b)),
        out_shape=jax.ShapeDtypeStruct((DIFF_HEADS * DIFF_V, g.b * g.c), BF16),
        compiler_params=_cparams(("parallel", "parallel")),
        name="attn_diff_ctx",
    )(qd, kd, vd, *pvals)
    return lat, ctx


def _rec_post_kernel(yf_ref, yb_ref, xs_ref, z_ref, odl_ref, odc_ref, dskip_ref, gn_ref, wa_ref, wb_ref,
                     h_ref, mod_ref, gpost_ref, gffn_ref, rw_ref, hn_ref, v_ref, lg_ref, *, lat_blocks):
    y = (yf_ref[...] + yb_ref[...] + xs_ref[...] * dskip_ref[...]) * _silu(z_ref[...].astype(F32))
    half = SSD_INNER // SSD_GROUPS
    o = _dot_tn(_pick_part(lat_blocks, odl_ref, odc_ref), wb_ref[...])
    for gi in range(SSD_GROUPS):
        lo = half * gi
        yn = (_rms(y[:, lo:lo + half], -1) * gn_ref[:, lo:lo + half]).astype(BF16)
        o = o + _dot(yn, wa_ref[lo:lo + half, :])
    _post_tail(o, h_ref[...], mod_ref, gpost_ref, gffn_ref, rw_ref, hn_ref, v_ref, lg_ref)


def _rec_post_call(g, yf, yb, xbc_act, zs, od_parts, d_skip, ssd_norm, w_out, h, mod, gpost, gffn, router_w):
    d = h.shape[1]
    rb = ROW_BLOCK
    ins, outs, shapes = _post_specs(g, d)
    wo = w_out.astype(BF16)
    row512 = pl.BlockSpec((rb, SSD_INNER), lambda i: (i, 0))
    return pl.pallas_call(
        functools.partial(_rec_post_kernel, lat_blocks=g.lat_blocks), grid=(g.n_blocks,),
        in_specs=[row512, row512, row512, row512] + _part_specs(g, (SSD_INNER, rb), 1)
        + [_full((1, SSD_INNER)), _full((1, SSD_INNER)), _full((SSD_INNER, d)), _full((SSD_INNER, d)),
           pl.BlockSpec((rb, d), lambda i: (i, 0))] + ins,
        out_specs=outs, out_shape=shapes,
        compiler_params=_cparams(("parallel",)), name="rec_post",
    )(yf, yb, xbc_act, zs, *od_parts, jnp.repeat(d_skip, SSD_HEAD_DIM).reshape(1, -1), ssd_norm.reshape(1, -1),
      wo[:SSD_INNER], wo[SSD_INNER:], h, mod, gpost.reshape(1, d), gffn.reshape(1, d), router_w.T)


def _ffn_res_kernel(h_ref, f_ref, mod_ref, g_ref, o_ref):
    o_ref[...] = h_ref[...] + mod_ref[5:6, :] * (_rms(f_ref[...], -1) * g_ref[...])


def _ffn_res_call(g, h, f, mod, gain, n_rows):
    d = h.shape[1]
    rb = ROW_BLOCK
    row = pl.BlockSpec((rb, d), lambda i: (i, 0))
    return pl.pallas_call(
        _ffn_res_kernel, grid=(n_rows // rb,),
        in_specs=[row, row, pl.BlockSpec((None, 6, d), lambda i: (g.mod_row(i), 0, 0)), _full((1, d))],
        out_specs=row, out_shape=jax.ShapeDtypeStruct((n_rows, d), F32),
        compiler_params=_cparams(("parallel",)), name="ffn_res",
    )(h, f, mod, gain.reshape(1, d))


def kernel(x, c, ctx, c_ctx, ada_w, ada_b, norm_mix_pre, norm_mix_post, norm_ffn_pre, norm_ffn_post, mix_w_out, att_w_in, mla_q_norm, mla_w_q_up, mla_kv_norm, mla_w_kv_up, gqa_q_norm, gqa_k_norm, rec_w_in, ssd_conv_w, ssd_conv_b, ssd_dt_bias_f, ssd_dt_bias_b, ssd_a_log_f, ssd_a_log_b, ssd_d, ssd_norm, diff_lambda_q1, diff_lambda_k1, diff_lambda_q2, diff_lambda_k2, diff_subln, router_w, router_b, exp_w_gate, exp_w_up, exp_w_down, sh_w_gate, sh_w_up, sh_w_down):
    b, n, d = x.shape
    n_ctx = ctx.shape[1]
    depth = ada_w.shape[0]
    g = _Geom(b, n, n_ctx)
    mod_rows = -(-(b + 1) // 8) * 8
    c_all = jnp.concatenate([c, c_ctx[None, :], jnp.zeros((mod_rows - b - 1, d), F32)], axis=0)
    mods = _ada_call(c_all, ada_w, ada_b).reshape(depth, mod_rows, 6, d)
    h = None
    for i in range(depth):
        last = i == depth - 1
        jdx = i // 2
        mod = mods[i]
        if i % 2 == 0:
            w = _attn_pre_weights(g, att_w_in[jdx], mla_q_norm[jdx], mla_w_q_up[jdx], mla_kv_norm[jdx],
                                  mla_w_kv_up[jdx], gqa_q_norm[jdx], gqa_k_norm[jdx])
            h_parts = (x.reshape(b * n, d), ctx.reshape(b * n_ctx, d)) if i == 0 else (h[:b * n], h[b * n:])
            qm, km, vm, qg, kg, vg = _attn_pre_call(g, h_parts, mod, norm_mix_pre[i].reshape(1, d), w)
            oa = _attn_call(g, qm, km, vm, kv_heads=MLA_HEADS, shared_k=False, name="attn_mla")
            ob = _attn_call(g, qg, kg, vg, kv_heads=GQA_KV_HEADS, shared_k=True, name="attn_gqa")
            h, v, logits_t = _attn_post_call(g, oa, ob, mix_w_out[i], h_parts, mod, norm_mix_post[i], norm_ffn_pre[i],
                                             router_w[i])
        else:
            lambda_init = 0.8 - 0.6 * math.exp(-0.3 * i)
            w = _rec_pre_weights(g, rec_w_in[jdx])
            h, zs, xbc, dt, kd, qd, vd = _rec_pre_call(g, h, f, mods[i - 1], norm_ffn_post[i - 1], mod,
                                                       norm_mix_pre[i].reshape(1, d), w)
            xbc_act = _conv_call(g, xbc, ssd_conv_w[jdx], ssd_conv_b[jdx])
            yf, yb = _ssd_call(g, xbc_act, dt, ssd_dt_bias_f[jdx], ssd_dt_bias_b[jdx], ssd_a_log_f[jdx],
                               ssd_a_log_b[jdx])
            od = _diff_attn_call(g, qd, kd, vd, diff_lambda_q1[jdx], diff_lambda_k1[jdx], diff_lambda_q2[jdx],
                                 diff_lambda_k2[jdx], diff_subln[jdx], lambda_init)
            h, v, logits_t = _rec_post_call(g, yf, yb, xbc_act, zs, od, ssd_d[jdx], ssd_norm[jdx], mix_w_out[i], h,
                                            mod, norm_mix_post[i], norm_ffn_pre[i], router_w[i])
        nsb = -(-(b * n) // MOE_TOKENS) if last else g.t // MOE_TOKENS
        gates, rank, counts = _router_call(g, logits_t, router_b[i], nsb)
        f = _moe_call(g, v, gates, rank, counts, nsb, exp_w_gate[i], exp_w_up[i], exp_w_down[i],
                      sh_w_gate[i], sh_w_up[i], sh_w_down[i])
        if last or i % 2 == 1:
            h = _ffn_res_call(g, h, f, mod, norm_ffn_post[i], b * n if last else g.t)
    return h.reshape(b, n, d)
```

```python
import functools
import math

import numpy as np
import jax
import jax.numpy as jnp
from jax import lax
from jax.experimental import pallas as pl
from jax.experimental.pallas import tpu as pltpu

F32 = jnp.float32
BF16 = jnp.bfloat16
LOG2E = 1.4426950408889634

GRID_W = 64
ROPE_BASE = 10000.0
NORM_EPS = 1e-6

MLA_HEADS, MLA_Q_LORA, MLA_KV_LORA, MLA_NOPE, MLA_ROPE, MLA_V = 8, 256, 128, 64, 32, 64
MLA_SCALE = (MLA_NOPE + MLA_ROPE) ** -0.5
GQA_HEADS, GQA_KV_HEADS, GQA_DIM = 8, 2, 64
GQA_SCALE = GQA_DIM ** -0.5
SSD_HEADS, SSD_HEAD_DIM, SSD_GROUPS, SSD_STATE, SSD_CONV, SSD_CHUNK = 8, 64, 2, 128, 3, 128
SSD_INNER = SSD_HEADS * SSD_HEAD_DIM
SSD_XBC = SSD_INNER + 2 * SSD_GROUPS * SSD_STATE
DIFF_HEADS, DIFF_DIM = 4, 64
DIFF_V = 2 * DIFF_DIM
DIFF_SCALE = DIFF_DIM ** -0.5
N_EXPERTS, TOP_K, N_GROUPS, TOPK_GROUPS, EXPERT_FF, SHARED_FF = 64, 8, 8, 4, 256, 256
ROUTED_SCALE = 2.5

LANES = 128
VMEM_LIMIT_BYTES = 56 * 1024 * 1024

ROW_BLOCK = 512
ATTN_Q_TILE = 1024
DIFF_Q_TILE = 512
ATTN_KV_CHUNK = 512
ATTN_EXP_ROWS = 4096
MOE_TOKENS = 768
MOE_ROWS = 128
MOE_PAIR = 8
MOE_SLOTS = 8


def _cparams(sem):
    return pltpu.CompilerParams(dimension_semantics=sem, vmem_limit_bytes=VMEM_LIMIT_BYTES)


def _rms(x, axis):
    return x * lax.rsqrt(jnp.mean(x * x, axis=axis, keepdims=True) + NORM_EPS)


def _silu(x):
    return x * jax.nn.sigmoid(x)


def _dot(a, b):
    return jnp.dot(a, b, preferred_element_type=F32)


def _dot_nt(a, b):
    return lax.dot_general(a, b, (((1,), (1,)), ((), ())), preferred_element_type=F32)


def _dot_tn(a, b):
    return lax.dot_general(a, b, (((0,), (0,)), ((), ())), preferred_element_type=F32)


def _ada_kernel(c_ref, w_ref, b_ref, o_ref):
    s = _silu(c_ref[...])
    o_ref[...] = jnp.dot(s, w_ref[...], preferred_element_type=F32, precision=lax.Precision.HIGHEST) + b_ref[...]


def _ada_call(c_all, ada_w, ada_b):
    depth, d, six_d = ada_w.shape
    rows = c_all.shape[0]
    cols = six_d // 4
    return pl.pallas_call(
        _ada_kernel,
        grid=(depth, six_d // cols),
        in_specs=[
            pl.BlockSpec((rows, d), lambda i, j: (0, 0)),
            pl.BlockSpec((None, d, cols), lambda i, j: (i, 0, j)),
            pl.BlockSpec((None, 1, cols), lambda i, j: (i, 0, j)),
        ],
        out_specs=pl.BlockSpec((None, rows, cols), lambda i, j: (i, 0, j)),
        out_shape=jax.ShapeDtypeStruct((depth, rows, six_d), F32),
        compiler_params=_cparams(("arbitrary", "arbitrary")),
        name="ada_mod",
    )(c_all, ada_w, ada_b.reshape(depth, 1, six_d))


def _attn_pre_kernel(hl_ref, hc_ref, mod_ref, gpre_ref, wfm_ref, wtm_ref, qn_ref, kvn_col_ref, kvn_row_ref, wq_ref,
                     wk_ref, wv_ref, gq_g_ref, gq_gr_ref, gk_g_ref, gk_gr_ref,
                     cqm_ref, sqm_ref, cqh_ref, sqh_ref, ckm_ref, skm_ref, ckh_ref, skh_ref,
                     qm_ref, km_ref, vm_ref, qg_ref, kg_ref, vg_ref, *, lat_blocks):
    h = _pick_part(lat_blocks, hl_ref, hc_ref)
    u = _rms(h, -1) * gpre_ref[...]
    u = u * (1.0 + mod_ref[1:2, :]) + mod_ref[0:1, :]
    ub = u.astype(BF16)
    zt = _dot_nt(wfm_ref[...], ub)
    zk = _dot(ub, wtm_ref[...])

    qn = (_rms(zt[0:256], 0) * qn_ref[...]).astype(BF16)
    qt = _dot(wq_ref[...], qn)
    cq, sq = cqm_ref[...], sqm_ref[...]
    qs = MLA_SCALE * LOG2E
    zero32 = jnp.zeros((32, h.shape[0]), BF16)
    for hd in range(MLA_HEADS):
        pe = qt[512 + 32 * hd:544 + 32 * hd] * cq + qt[768 + 32 * hd:800 + 32 * hd] * sq
        qm_ref[hd, 0:64, :] = (qt[64 * hd:64 * hd + 64] * qs).astype(BF16)
        qm_ref[hd, 64:96, :] = (pe * qs).astype(BF16)
        qm_ref[hd, 96:128, :] = zero32

    kvn_t = (_rms(zt[256:384], 0) * kvn_col_ref[...]).astype(BF16)
    vt = _dot(wv_ref[...], kvn_t)
    for hd in range(MLA_HEADS):
        vm_ref[hd] = vt[64 * hd:64 * hd + 64].astype(BF16)
    kvn = (_rms(zk[:, 0:128], -1) * kvn_row_ref[...]).astype(BF16)
    kn = _dot(kvn, wk_ref[...])
    kpe = zk[:, 128:256] * ckm_ref[...] + zk[:, 256:384] * skm_ref[...]
    for hd in range(MLA_HEADS):
        km_ref[hd] = (kn[:, 128 * hd:128 * hd + 128] + kpe).astype(BF16)

    cqh, sqh = cqh_ref[...], sqh_ref[...]
    gs = GQA_SCALE * LOG2E
    zero64 = jnp.zeros((64, h.shape[0]), BF16)
    grp = GQA_HEADS // GQA_KV_HEADS
    for hd in range(GQA_HEADS):
        raw = zt[384 + 64 * hd:448 + 64 * hd]
        rot = zt[896 + 64 * hd:960 + 64 * hd]
        r = lax.rsqrt(jnp.mean(raw * raw, axis=0, keepdims=True) + NORM_EPS)
        q = (raw * r * gq_g_ref[...]) * cqh + (rot * r * gq_gr_ref[...]) * sqh
        q = (q * gs).astype(BF16)
        if hd // grp == 0:
            qg_ref[hd, 0:64, :] = q
            qg_ref[hd, 64:128, :] = zero64
        else:
            qg_ref[hd, 0:64, :] = zero64
            qg_ref[hd, 64:128, :] = q
    for kvh in range(GQA_KV_HEADS):
        vg_ref[kvh] = zt[1408 + 64 * kvh:1472 + 64 * kvh].astype(BF16)

    gk, gkr = zk[:, 384:512], zk[:, 512:640]
    lane = lax.broadcasted_iota(jnp.int32, gk.shape, 1)
    lo = lane < 64
    sq0 = jnp.sum(jnp.where(lo, gk * gk, 0.0), axis=-1, keepdims=True)
    sq1 = jnp.sum(jnp.where(lo, 0.0, gk * gk), axis=-1, keepdims=True)
    r = jnp.where(lo, lax.rsqrt(sq0 / GQA_DIM + NORM_EPS), lax.rsqrt(sq1 / GQA_DIM + NORM_EPS))
    kg_ref[...] = ((gk * r * gk_g_ref[...]) * ckh_ref[...] + (gkr * r * gk_gr_ref[...]) * skh_ref[...]).astype(BF16)


def _rot_map(dim):
    q = dim // 4
    j = np.arange(dim)
    even = (j // q) % 2 == 0
    return np.where(even, j + q, j - q), np.where(even, -1.0, 1.0).astype(np.float32)


def _rope_tables(n_tokens, dim, pad_rows):
    rows = n_tokens // GRID_W
    row = jnp.repeat(jnp.arange(rows, dtype=F32), GRID_W)
    col = jnp.tile(jnp.arange(GRID_W, dtype=F32), rows)
    half = dim // 2
    inv = ROPE_BASE ** (-(jnp.arange(half // 2, dtype=F32) * 2.0 / half))
    ang_r = row[:, None] * inv
    ang_c = col[:, None] * inv
    ang = jnp.concatenate([ang_r, ang_r, ang_c, ang_c], axis=-1)
    cos = jnp.concatenate([jnp.cos(ang), jnp.ones((pad_rows, dim), F32)], axis=0)
    sin = jnp.concatenate([jnp.sin(ang), jnp.zeros((pad_rows, dim), F32)], axis=0)
    return cos, sin


class _Geom:
    def __init__(self, b, n, c):
        assert n % ROW_BLOCK == 0 and (b * c) % ROW_BLOCK == 0 and n % c == 0 and n % GRID_W == 0
        self.b, self.n, self.c = b, n, c
        self.t = b * n + b * c
        self.lat_blocks = b * n // ROW_BLOCK
        self.blocks_per_seq = n // ROW_BLOCK
        self.n_blocks = self.t // ROW_BLOCK
        assert self.t % MOE_TOKENS == 0

    def mod_row(self, i):
        return jnp.where(i < self.lat_blocks, i // self.blocks_per_seq, self.b)

    def pos_block(self, i):
        return jnp.where(i < self.lat_blocks, i % self.blocks_per_seq, self.blocks_per_seq)


def _full(shape):
    nd = len(shape)
    return pl.BlockSpec(shape, lambda *_: (0,) * nd)


def _part_specs(g, block, axis):
    def spec(latent):
        def index(i):
            j = jnp.minimum(i, g.lat_blocks - 1) if latent else jnp.maximum(i - g.lat_blocks, 0)
            return tuple(j if a == axis else 0 for a in range(len(block)))
        return pl.BlockSpec(block, index)
    return [spec(True), spec(False)]


def _pick_part(lat_blocks, lat_ref, ctx_ref):
    return jnp.where(pl.program_id(0) < lat_blocks, lat_ref[...], ctx_ref[...])


def _attn_pre_call(g, h_parts, mod, gpre, w):
    d = h_parts[0].shape[1]
    t = g.t
    rb = ROW_BLOCK
    row = lambda i: (i, 0)
    tm_tab = pl.BlockSpec((rb, LANES), lambda i: (g.pos_block(i), 0))
    fm32 = pl.BlockSpec((32, rb), lambda i: (0, g.pos_block(i)))
    fm64 = pl.BlockSpec((64, rb), lambda i: (0, g.pos_block(i)))
    in_specs = _part_specs(g, (rb, d), 0) + [
        pl.BlockSpec((None, 6, d), lambda i: (g.mod_row(i), 0, 0)),
        _full(gpre.shape), _full(w["wfm"].shape), _full(w["wtm"].shape), _full(w["qn"].shape),
        _full(w["kvn_col"].shape), _full(w["kvn_row"].shape), _full(w["wq"].shape), _full(w["wk"].shape),
        _full(w["wv"].shape), _full(w["gq_g"].shape), _full(w["gq_gr"].shape), _full(w["gk_g"].shape),
        _full(w["gk_gr"].shape),
        fm32, fm32, fm64, fm64, tm_tab, tm_tab, tm_tab, tm_tab,
    ]
    out_shape = [
        jax.ShapeDtypeStruct((MLA_HEADS, 128, t), BF16),
        jax.ShapeDtypeStruct((MLA_HEADS, t, 128), BF16),
        jax.ShapeDtypeStruct((MLA_HEADS, MLA_V, t), BF16),
        jax.ShapeDtypeStruct((GQA_HEADS, 128, t), BF16),
        jax.ShapeDtypeStruct((t, 128), BF16),
        jax.ShapeDtypeStruct((GQA_KV_HEADS, GQA_DIM, t), BF16),
    ]
    out_specs = [
        pl.BlockSpec((MLA_HEADS, 128, rb), lambda i: (0, 0, i)),
        pl.BlockSpec((MLA_HEADS, rb, 128), lambda i: (0, i, 0)),
        pl.BlockSpec((MLA_HEADS, MLA_V, rb), lambda i: (0, 0, i)),
        pl.BlockSpec((GQA_HEADS, 128, rb), lambda i: (0, 0, i)),
        pl.BlockSpec((rb, 128), row),
        pl.BlockSpec((GQA_KV_HEADS, GQA_DIM, rb), lambda i: (0, 0, i)),
    ]
    return pl.pallas_call(
        functools.partial(_attn_pre_kernel, lat_blocks=g.lat_blocks),
        grid=(g.n_blocks,), in_specs=in_specs, out_specs=out_specs, out_shape=out_shape,
        compiler_params=_cparams(("parallel",)), name="attn_pre",
    )(*h_parts, mod, gpre, w["wfm"], w["wtm"], w["qn"], w["kvn_col"], w["kvn_row"], w["wq"], w["wk"], w["wv"],
      w["gq_g"], w["gq_gr"], w["gk_g"], w["gk_gr"],
      w["cqm"], w["sqm"], w["cqh"], w["sqh"], w["ckm"], w["skm"], w["ckh"], w["skh"])


def _attn_pre_weights(g, att_w_in, q_norm, w_q_up, kv_norm, w_kv_up, gqa_qn, gqa_kn):
    d = att_w_in.shape[0]
    o = np.cumsum([0, MLA_Q_LORA, MLA_KV_LORA, MLA_ROPE, GQA_HEADS * GQA_DIM, GQA_KV_HEADS * GQA_DIM,
                   GQA_KV_HEADS * GQA_DIM])
    w_qlat, w_kvlat, w_kpe, w_gq, w_gk, w_gv = (att_w_in[:, o[i]:o[i + 1]] for i in range(6))
    src32, sgn32 = _rot_map(MLA_ROPE)
    src64, sgn64 = _rot_map(GQA_DIM)

    def rot_heads(wcols, heads, dim, src, sgn):
        wh = wcols.reshape(d, heads, dim)
        return (wh[:, :, src] * sgn).reshape(d, heads * dim)

    wfm = jnp.concatenate([w_qlat, w_kvlat, w_gq, rot_heads(w_gq, GQA_HEADS, GQA_DIM, src64, sgn64), w_gv], axis=1)
    zpad = lambda x, lo, hi: jnp.pad(x, ((0, 0), (lo, hi)))
    wtm = jnp.concatenate([
        w_kvlat, zpad(w_kpe, 64, 32), zpad(w_kpe[:, src32] * sgn32, 64, 32),
        w_gk, rot_heads(w_gk, GQA_KV_HEADS, GQA_DIM, src64, sgn64)], axis=1)
    wq = w_q_up.reshape(MLA_Q_LORA, MLA_HEADS, MLA_NOPE + MLA_ROPE)
    wq_pe = wq[:, :, MLA_NOPE:]
    wq_all = jnp.concatenate([
        wq[:, :, :MLA_NOPE].reshape(MLA_Q_LORA, -1), wq_pe.reshape(MLA_Q_LORA, -1),
        (wq_pe[:, :, src32] * sgn32).reshape(MLA_Q_LORA, -1)], axis=1)
    wkv = w_kv_up.reshape(MLA_KV_LORA, MLA_HEADS, MLA_NOPE + MLA_V)
    wk = jnp.pad(wkv[:, :, :MLA_NOPE], ((0, 0), (0, 0), (0, 128 - MLA_NOPE))).reshape(MLA_KV_LORA, -1)
    wv = wkv[:, :, MLA_NOPE:].reshape(MLA_KV_LORA, -1)
    cos_m, sin_m = _rope_tables(g.n, MLA_ROPE, ROW_BLOCK)
    cos_h, sin_h = _rope_tables(g.n, GQA_DIM, ROW_BLOCK)
    two = lambda x: jnp.concatenate([x, x], axis=1)
    return dict(
        wfm=wfm.T.astype(BF16), wtm=wtm.astype(BF16),
        qn=q_norm.reshape(-1, 1), kvn_col=kv_norm.reshape(-1, 1), kvn_row=kv_norm.reshape(1, -1),
        wq=wq_all.T.astype(BF16), wk=wk.astype(BF16), wv=wv.T.astype(BF16),
        gq_g=gqa_qn.reshape(-1, 1), gq_gr=gqa_qn[src64].reshape(-1, 1),
        gk_g=two(gqa_kn.reshape(1, -1)), gk_gr=two(gqa_kn[src64].reshape(1, -1)),
        cqm=cos_m.T, sqm=sin_m.T, cqh=cos_h.T, sqh=sin_h.T,
        ckm=zpad(cos_m, 64, 32), skm=zpad(sin_m, 64, 32), ckh=two(cos_h), skh=two(sin_h),
    )


def _fold8(x, op):
    r, w = x.shape
    return op(x.reshape(r // 8, 8, w), axis=0)


def _attend(qt, segments, s_ref, dv):
    tq = qt.shape[1]
    pieces, off = [], 0
    for k_ref, vt_ref in segments:
        n = k_ref.shape[0]
        for lo in range(0, n, ATTN_KV_CHUNK):
            rows = min(ATTN_KV_CHUNK, n - lo)
            pieces.append((off, lo, rows, k_ref, vt_ref))
            off += rows
    m8 = None
    for so, lo, rows, k_ref, _ in pieces:
        s = _dot(k_ref[lo:lo + rows, :], qt)
        s_ref[so:so + rows, :] = s
        part = _fold8(s, jnp.max)
        m8 = part if m8 is None else jnp.maximum(m8, part)
    m = jnp.max(m8, axis=0, keepdims=True)
    acc = jnp.zeros((dv + 16, tq), F32)
    off = 0
    for _, vt_ref in segments:
        n = vt_ref.shape[1]
        step = min(n, ATTN_EXP_ROWS)

        def body(i, acc, off=off, vt_ref=vt_ref, step=step):
            base = i * step if isinstance(i, int) else pl.multiple_of(i * step, step)
            for lo in range(0, step, ATTN_KV_CHUNK):
                rows = min(ATTN_KV_CHUNK, step - lo)
                p = jnp.exp2((s_ref[pl.ds(off + base + lo, rows), :] - m).astype(BF16))
                vt1 = jnp.concatenate([vt_ref[:, pl.ds(base + lo, rows)], jnp.ones((16, rows), BF16)], axis=0)
                acc = acc + _dot(vt1, p)
            return acc

        acc = body(0, acc) if n == step else lax.fori_loop(0, n // step, body, acc)
        off += n
    return acc[0:dv] / acc[dv:dv + 1]


def _attn_kernel(qt_ref, *refs, dv):
    o_ref, s_ref = refs[-2:]
    kv = refs[:-2]
    half = len(kv) // 2
    segs = [(kv[i], kv[half + i]) for i in range(half)]
    o_ref[...] = _attend(qt_ref[...], segs, s_ref, dv).astype(o_ref.dtype)


def _attn_call(g, qt, k, vt, *, kv_heads, shared_k, name):
    heads = qt.shape[0]
    dv = vt.shape[1]
    grp = heads // kv_heads
    tq = ATTN_Q_TILE
    nq = g.n // tq
    ctx_blk = g.b * g.n // g.c
    if shared_k:
        klat = pl.BlockSpec((g.n, 128), lambda b, h, *_: (b, 0))
        kctx = pl.BlockSpec((g.c, 128), lambda b, h, *_: (ctx_blk + b, 0))
    else:
        klat = pl.BlockSpec((None, g.n, 128), lambda b, h, *_: (h // grp, b, 0))
        kctx = pl.BlockSpec((None, g.c, 128), lambda b, h, *_: (h // grp, ctx_blk + b, 0))
    vlat = pl.BlockSpec((None, dv, g.n), lambda b, h, *_: (h // grp, 0, b))
    vctx = pl.BlockSpec((None, dv, g.c), lambda b, h, *_: (h // grp, 0, ctx_blk + b))
    lat = pl.pallas_call(
        functools.partial(_attn_kernel, dv=dv),
        grid=(g.b, heads, nq),
        scratch_shapes=[pltpu.VMEM((g.n + g.c, tq), F32)],
        in_specs=[pl.BlockSpec((None, 128, tq), lambda b, h, qi: (h, 0, b * nq + qi)), klat, kctx, vlat, vctx],
        out_specs=pl.BlockSpec((dv, tq), lambda b, h, qi: (h, b * nq + qi)),
        out_shape=jax.ShapeDtypeStruct((heads * dv, g.b * g.n), BF16),
        compiler_params=_cparams(("parallel", "parallel", "arbitrary")),
        name=name,
    )(qt, k, k, vt, vt)
    ctx = pl.pallas_call(
        functools.partial(_attn_kernel, dv=dv),
        grid=(g.b, heads),
        scratch_shapes=[pltpu.VMEM((g.c, g.c), F32)],
        in_specs=[pl.BlockSpec((None, 128, g.c), lambda b, h: (h, 0, ctx_blk + b)), kctx, vctx],
        out_specs=pl.BlockSpec((dv, g.c), lambda b, h: (h, b)),
        out_shape=jax.ShapeDtypeStruct((heads * dv, g.b * g.c), BF16),
        compiler_params=_cparams(("parallel", "parallel")),
        name=name + "_ctx",
    )(qt, k, vt)
    return lat, ctx


def _split_bf16(x):
    hi = x.astype(BF16)
    return hi, (x - hi.astype(F32)).astype(BF16)


def _post_tail(o, h, mod_ref, gpost_ref, gffn_ref, rw_ref, hn_ref, v_ref, lg_ref):
    hn = h + mod_ref[2:3, :] * (_rms(o, -1) * gpost_ref[...])
    hn_ref[...] = hn
    v = (_rms(hn, -1) * gffn_ref[...]) * (1.0 + mod_ref[4:5, :]) + mod_ref[3:4, :]
    v_hi, v_lo = _split_bf16(v)
    v_ref[...] = v_hi
    w_hi, w_lo = _split_bf16(rw_ref[...])
    lg_ref[...] = _dot_nt(w_hi, v_hi) + (_dot_nt(w_hi, v_lo) + _dot_nt(w_lo, v_hi))


def _attn_post_kernel(oal_ref, oac_ref, obl_ref, obc_ref, wa_ref, wb_ref, hl_ref, hc_ref, mod_ref, gpost_ref, gffn_ref,
                      rw_ref, hn_ref, v_ref, lg_ref, *, lat_blocks):
    oa = _pick_part(lat_blocks, oal_ref, oac_ref)
    ob = _pick_part(lat_blocks, obl_ref, obc_ref)
    o = _dot_tn(oa, wa_ref[...]) + _dot_tn(ob, wb_ref[...])
    h = _pick_part(lat_blocks, hl_ref, hc_ref)
    _post_tail(o, h, mod_ref, gpost_ref, gffn_ref, rw_ref, hn_ref, v_ref, lg_ref)


def _post_specs(g, d):
    rb = ROW_BLOCK
    ins = [
        pl.BlockSpec((None, 6, d), lambda i: (g.mod_row(i), 0, 0)),
        _full((1, d)), _full((1, d)), _full((N_EXPERTS, d)),
    ]
    outs = [pl.BlockSpec((rb, d), lambda i: (i, 0)), pl.BlockSpec((rb, d), lambda i: (i, 0)),
            pl.BlockSpec((N_EXPERTS, rb), lambda i: (0, i))]
    shapes = [jax.ShapeDtypeStruct((g.t, d), F32), jax.ShapeDtypeStruct((g.t, d), BF16),
              jax.ShapeDtypeStruct((N_EXPERTS, g.t), F32)]
    return ins, outs, shapes


def _attn_post_call(g, oa_parts, ob_parts, w_out, h_parts, mod, gpost, gffn, router_w):
    d = h_parts[0].shape[1]
    rb = ROW_BLOCK
    half = oa_parts[0].shape[0]
    ins, outs, shapes = _post_specs(g, d)
    wo = w_out.astype(BF16)
    o_specs = _part_specs(g, (half, rb), 1)
    return pl.pallas_call(
        functools.partial(_attn_post_kernel, lat_blocks=g.lat_blocks), grid=(g.n_blocks,),
        in_specs=o_specs + o_specs + [_full((half, d)), _full((half, d))] + _part_specs(g, (rb, d), 0) + ins,
        out_specs=outs, out_shape=shapes,
        compiler_params=_cparams(("parallel",)), name="attn_post",
    )(*oa_parts, *ob_parts, wo[:half], wo[half:], *h_parts, mod, gpost.reshape(1, d), gffn.reshape(1, d), router_w.T)


def _router_kernel(lg_ref, bias_ref, tri_ref, gate_ref, rank_ref, cnt_ref):
    tb = lg_ref.shape[1]
    per = N_EXPERTS // N_GROUPS
    shp = (N_GROUPS, per, tb)
    scores = jax.nn.sigmoid(lg_ref[...])
    s3 = scores.reshape(shp)
    sel = (scores + bias_ref[...]).reshape(shp)
    sub = lax.broadcasted_iota(jnp.int32, shp, 1)
    grp = lax.broadcasted_iota(jnp.int32, shp, 0)
    neg = -jnp.inf
    m1 = jnp.max(sel, axis=1, keepdims=True)
    i1 = jnp.min(jnp.where(sel == m1, sub, per), axis=1, keepdims=True)
    m2 = jnp.max(jnp.where(sub == i1, neg, sel), axis=1, keepdims=True)
    cur = jnp.broadcast_to(m1 + m2, shp)
    gmask = jnp.zeros(shp, F32)
    for _ in range(TOPK_GROUPS):
        gm = jnp.max(cur, axis=0, keepdims=True)
        gi = jnp.min(jnp.where(cur == gm, grp, N_GROUPS), axis=0, keepdims=True)
        pick = grp == gi
        gmask = jnp.where(pick, 1.0, gmask)
        cur = jnp.where(pick, neg, cur)
    masked = jnp.where(gmask > 0.0, sel, neg)
    eidx = grp * per + sub
    chosen = jnp.zeros(shp, F32)
    for _ in range(TOP_K):
        mx = jnp.max(jnp.max(masked, axis=1, keepdims=True), axis=0, keepdims=True)
        cand = jnp.where(masked == mx, eidx, N_EXPERTS)
        ei = jnp.min(jnp.min(cand, axis=1, keepdims=True), axis=0, keepdims=True)
        pick = eidx == ei
        chosen = jnp.where(pick, 1.0, chosen)
        masked = jnp.where(pick, neg, masked)
    top_w = jnp.where(chosen > 0.0, s3, 0.0)
    denom = jnp.sum(jnp.sum(top_w, axis=1, keepdims=True), axis=0, keepdims=True)
    gate_ref[...] = (top_w / denom * ROUTED_SCALE).reshape(N_EXPERTS, tb)
    ch2 = chosen.reshape(N_EXPERTS, tb)
    chb = ch2.astype(BF16)
    before = _dot(chb, tri_ref[...])
    rank_ref[...] = jnp.where(ch2 > 0.0, before, -1.0).astype(jnp.int32)
    cnt_ref[...] = _dot(chb, jnp.ones((tb, LANES), BF16))


def _router_call(g, logits_t, router_b, nsb):
    tb = MOE_TOKENS
    tri = (np.arange(tb)[:, None] < np.arange(tb)[None, :]).astype(np.float32)
    blk = pl.BlockSpec((N_EXPERTS, tb), lambda s: (0, s))
    gates, rank, cnt = pl.pallas_call(
        _router_kernel, grid=(nsb,),
        in_specs=[blk, _full((N_EXPERTS, 1)), _full((tb, tb))],
        out_specs=[blk, blk, pl.BlockSpec((None, N_EXPERTS, LANES), lambda s: (s, 0, 0))],
        out_shape=[jax.ShapeDtypeStruct((N_EXPERTS, g.t), F32), jax.ShapeDtypeStruct((N_EXPERTS, g.t), jnp.int32),
                   jax.ShapeDtypeStruct((nsb, N_EXPERTS, LANES), F32)],
        compiler_params=_cparams(("parallel",)), name="router",
    )(logits_t, router_b.reshape(-1, 1), jnp.asarray(tri, BF16))
    return gates, rank, cnt[:, :, 0].astype(jnp.int32).reshape(-1)


def _moe_kernel(cnt_ref, v_ref, gate_ref, rank_ref, wg_ref, wu_ref, wd_ref, sg_ref, su_ref, sd_ref, o_ref,
                psel_ref, ysel_ref, slot_ref):
    s, j = pl.program_id(0), pl.program_id(1)
    tb = v_ref.shape[0]
    rows = MOE_ROWS

    @pl.when(j == 0)
    def _():
        vb = v_ref[...]
        hid = (_silu(_dot(vb, sg_ref[...])) * _dot(vb, su_ref[...])).astype(BF16)
        o_ref[...] = _dot(hid, sd_ref[...])
        psel_ref[...] = jnp.zeros_like(psel_ref)
        ysel_ref[...] = jnp.zeros_like(ysel_ref)
        slot_ref[0] = 0

    row_id = lax.broadcasted_iota(jnp.int32, (rows, tb), 0)

    def flush(n_slots):
        stage_row = lax.broadcasted_iota(jnp.int32, psel_ref.shape, 0)
        sel = jnp.where(stage_row < n_slots * rows, psel_ref[...], jnp.zeros_like(psel_ref))
        o_ref[...] += _dot_tn(sel, ysel_ref[...])

    ends, total = [], 0
    for e in range(MOE_PAIR):
        total = total + (cnt_ref[s * N_EXPERTS + j * MOE_PAIR + e] + rows - 1) // rows
        ends.append(total)

    def select(i):
        e = sum((i >= end).astype(jnp.int32) for end in ends[:-1])
        first = sum(jnp.where(e == k + 1, ends[k], 0) for k in range(MOE_PAIR - 1))
        ex = j * MOE_PAIR + e
        hit = row_id == rank_ref[pl.ds(ex, 1), :] - (i - first) * rows
        onehot = jnp.where(hit, 1.0, 0.0).astype(BF16)
        w_row = jnp.sum(jnp.where(hit, gate_ref[pl.ds(ex, 1), :], 0.0), axis=1, keepdims=True)
        return e, onehot, w_row

    def run_items(i0, count, slot):
        picks = [select(i0 + k) for k in range(count)]
        sel = picks[0][1] if count == 1 else jnp.concatenate([p[1] for p in picks], axis=0)
        at = pl.multiple_of(slot * rows, rows)
        psel_ref[pl.ds(at, count * rows), :] = sel
        xs = _dot(sel, v_ref[...]).astype(BF16)
        hids = [(_silu(_dot(xs[k * rows:(k + 1) * rows], wg_ref[e])) * _dot(xs[k * rows:(k + 1) * rows], wu_ref[e]))
                .astype(BF16) for k, (e, _, _) in enumerate(picks)]
        outs = [(_dot(hid, wd_ref[e]) * w_row).astype(BF16) for hid, (e, _, w_row) in zip(hids, picks)]
        ysel_ref[pl.ds(at, count * rows), :] = outs[0] if count == 1 else jnp.concatenate(outs, axis=0)

        @pl.when(slot + count == MOE_SLOTS)
        def _():
            flush(MOE_SLOTS)

        return jnp.where(slot + count == MOE_SLOTS, 0, slot + count)

    def pair(it, slot):
        no_room = slot > MOE_SLOTS - 2

        @pl.when(no_room)
        def _():
            flush(slot)

        return run_items(2 * it, 2, jnp.where(no_room, 0, slot))

    slot_ref[0] = lax.fori_loop(0, total // 2, pair, slot_ref[0])

    @pl.when(total % 2 == 1)
    def _():
        slot_ref[0] = run_items(total - 1, 1, slot_ref[0])

    @pl.when(jnp.logical_and(j == pl.num_programs(1) - 1, slot_ref[0] > 0))
    def _():
        flush(slot_ref[0])


def _moe_call(g, v, gates, rank, counts, nsb, wg, wu, wd, sg, su, sd):
    t, d = v.shape
    tb = MOE_TOKENS
    ff = wg.shape[2]
    grid_spec = pltpu.PrefetchScalarGridSpec(
        num_scalar_prefetch=1,
        grid=(nsb, N_EXPERTS // MOE_PAIR),
        in_specs=[
            pl.BlockSpec((tb, d), lambda s, j, c: (s, 0)),
            pl.BlockSpec((N_EXPERTS, tb), lambda s, j, c: (0, s)),
            pl.BlockSpec((N_EXPERTS, tb), lambda s, j, c: (0, s)),
            pl.BlockSpec((MOE_PAIR, d, ff), lambda s, j, c: (j, 0, 0)),
            pl.BlockSpec((MOE_PAIR, d, ff), lambda s, j, c: (j, 0, 0)),
            pl.BlockSpec((MOE_PAIR, ff, d), lambda s, j, c: (j, 0, 0)),
            pl.BlockSpec((d, sg.shape[1]), lambda s, j, c: (0, 0)),
            pl.BlockSpec((d, su.shape[1]), lambda s, j, c: (0, 0)),
            pl.BlockSpec((sd.shape[0], d), lambda s, j, c: (0, 0)),
        ],
        out_specs=pl.BlockSpec((tb, d), lambda s, j, c: (s, 0)),
        scratch_shapes=[pltpu.VMEM((MOE_SLOTS * MOE_ROWS, tb), BF16), pltpu.VMEM((MOE_SLOTS * MOE_ROWS, d), BF16),
                        pltpu.SMEM((1,), jnp.int32)],
    )
    return pl.pallas_call(
        _moe_kernel, grid_spec=grid_spec, out_shape=jax.ShapeDtypeStruct((t, d), F32),
        compiler_params=_cparams(("parallel", "arbitrary")), name="moe",
    )(counts, v, gates, rank, wg.astype(BF16), wu.astype(BF16), wd.astype(BF16),
      sg.astype(BF16), su.astype(BF16), sd.astype(BF16))


def _rec_pre_kernel(h_ref, f_ref, pmod_ref, gprev_ref, mod_ref, gpre_ref, wtm_ref, wfm_ref, cqh_ref, sqh_ref, ckh_ref,
                    skh_ref, hn_ref, z_ref, xbc_ref, dt_ref, kd_ref, qd_ref, vd_ref):
    h = h_ref[...] + pmod_ref[5:6, :] * (_rms(f_ref[...], -1) * gprev_ref[...])
    hn_ref[...] = h
    u = _rms(h, -1) * gpre_ref[...]
    u = u * (1.0 + mod_ref[1:2, :]) + mod_ref[0:1, :]
    ub = u.astype(BF16)
    zk = _dot(ub, wtm_ref[...])
    zt = _dot_nt(wfm_ref[...], ub)
    z_ref[...] = zk[:, 0:512].astype(z_ref.dtype)
    xbc_ref[...] = zk[:, 512:1536].astype(xbc_ref.dtype)
    dt_ref[...] = zk[:, 1536:1664]
    ck, sk = ckh_ref[...], skh_ref[...]
    for j in range(DIFF_HEADS):
        lo = 1664 + 128 * j
        kd_ref[j] = (zk[:, lo:lo + 128] * ck + zk[:, lo + 512:lo + 640] * sk).astype(BF16)
    cq, sq = cqh_ref[...], sqh_ref[...]
    qs = DIFF_SCALE * LOG2E
    zero64 = jnp.zeros((64, h.shape[0]), BF16)
    for hd in range(2 * DIFF_HEADS):
        q = ((zt[64 * hd:64 * hd + 64] * cq + zt[512 + 64 * hd:576 + 64 * hd] * sq) * qs).astype(BF16)
        if hd % 2 == 0:
            qd_ref[hd, 0:64, :] = q
            qd_ref[hd, 64:128, :] = zero64
        else:
            qd_ref[hd, 0:64, :] = zero64
            qd_ref[hd, 64:128, :] = q
    for j in range(DIFF_HEADS):
        vd_ref[j] = zt[1024 + 128 * j:1152 + 128 * j].astype(BF16)


def _rec_pre_weights(g, rec_w_in):
    d = rec_w_in.shape[0]
    o = np.cumsum([0, SSD_INNER, SSD_XBC, SSD_HEADS, SSD_HEADS, 2 * DIFF_HEADS * DIFF_DIM, 2 * DIFF_HEADS * DIFF_DIM,
                   DIFF_HEADS * DIFF_V])
    w_z, w_xbc, w_dtf, w_dtb, w_dq, w_dk, w_dv = (rec_w_in[:, o[i]:o[i + 1]] for i in range(7))
    src64, sgn64 = _rot_map(DIFF_DIM)

    def rot_heads(wcols):
        wh = wcols.reshape(d, 2 * DIFF_HEADS, DIFF_DIM)
        return (wh[:, :, src64] * sgn64).reshape(d, -1)

    w_dt = jnp.pad(jnp.concatenate([w_dtf, w_dtb], axis=1), ((0, 0), (0, LANES - 2 * SSD_HEADS)))
    wtm = jnp.concatenate([w_z, w_xbc, w_dt, w_dk, rot_heads(w_dk)], axis=1)
    wfm = jnp.concatenate([w_dq, rot_heads(w_dq), w_dv], axis=1)
    cos_h, sin_h = _rope_tables(g.n, DIFF_DIM, ROW_BLOCK)
    two = lambda x: jnp.concatenate([x, x], axis=1)
    return dict(wtm=wtm.astype(BF16), wfm=wfm.T.astype(BF16), cqh=cos_h.T, sqh=sin_h.T, ckh=two(cos_h), skh=two(sin_h))


def _rec_pre_call(g, h, f, prev_mod, prev_gain, mod, gpre, w):
    t, d = h.shape
    rb = ROW_BLOCK
    row = lambda i: (i, 0)
    modspec = pl.BlockSpec((None, 6, d), lambda i: (g.mod_row(i), 0, 0))
    tm_tab = pl.BlockSpec((rb, LANES), lambda i: (g.pos_block(i), 0))
    fm64 = pl.BlockSpec((64, rb), lambda i: (0, g.pos_block(i)))
    nd = 2 * DIFF_HEADS
    return pl.pallas_call(
        _rec_pre_kernel, grid=(g.n_blocks,),
        in_specs=[pl.BlockSpec((rb, d), row), pl.BlockSpec((rb, d), row), modspec, _full((1, d)), modspec,
                  _full(gpre.shape), _full(w["wtm"].shape), _full(w["wfm"].shape), fm64, fm64, tm_tab, tm_tab],
        out_specs=[
            pl.BlockSpec((rb, d), row), pl.BlockSpec((rb, SSD_INNER), row), pl.BlockSpec((rb, SSD_XBC), row), pl.BlockSpec((rb, LANES), row),
            pl.BlockSpec((DIFF_HEADS, rb, 128), lambda i: (0, i, 0)),
            pl.BlockSpec((nd, 128, rb), lambda i: (0, 0, i)),
            pl.BlockSpec((DIFF_HEADS, DIFF_V, rb), lambda i: (0, 0, i)),
        ],
        out_shape=[
            jax.ShapeDtypeStruct((t, d), F32),
            jax.ShapeDtypeStruct((t, SSD_INNER), BF16), jax.ShapeDtypeStruct((t, SSD_XBC), BF16),
            jax.ShapeDtypeStruct((t, LANES), F32), jax.ShapeDtypeStruct((DIFF_HEADS, t, 128), BF16),
            jax.ShapeDtypeStruct((nd, 128, t), BF16), jax.ShapeDtypeStruct((DIFF_HEADS, DIFF_V, t), BF16),
        ],
        compiler_params=_cparams(("parallel",)), name="rec_pre",
    )(h, f, prev_mod, prev_gain.reshape(1, d), mod, gpre, w["wtm"], w["wfm"], w["cqh"], w["sqh"], w["ckh"], w["skh"])


def _conv_kernel(x_ref, prev_ref, next_ref, w_ref, b_ref, o_ref, *, seq_blocks, lat_blocks):
    i = pl.program_id(0)
    rb = x_ref.shape[0]
    pos = i % seq_blocks
    is_ctx = i >= lat_blocks
    first = jnp.logical_or(is_ctx, pos == 0)
    last = jnp.logical_or(is_ctx, pos == seq_blocks - 1)
    x = x_ref[...].astype(F32)
    halo = prev_ref.shape[0]
    prev_row = jnp.where(first, 0.0, prev_ref[halo - 1:halo, :].astype(F32))
    next_row = jnp.where(last, 0.0, next_ref[0:1, :].astype(F32))
    rid = lax.broadcasted_iota(jnp.int32, x.shape, 0)
    x_prev = jnp.where(rid == 0, prev_row, pltpu.roll(x, 1, axis=0))
    x_next = jnp.where(rid == rb - 1, next_row, pltpu.roll(x, rb - 1, axis=0))
    y = w_ref[0:1, :] * x_prev + w_ref[1:2, :] * x + w_ref[2:3, :] * x_next + b_ref[...]
    o_ref[...] = _silu(y).astype(o_ref.dtype)


def _conv_call(g, xbc, conv_w, conv_b):
    t, ch = xbc.shape
    rb = g.c
    halo = 16
    per = rb // halo
    last_halo = t // halo - 1
    return pl.pallas_call(
        functools.partial(_conv_kernel, seq_blocks=g.n // rb, lat_blocks=g.b * g.n // rb),
        grid=(t // rb,),
        in_specs=[
            pl.BlockSpec((rb, ch), lambda i: (i, 0)),
            pl.BlockSpec((halo, ch), lambda i: (jnp.maximum(i * per - 1, 0), 0)),
            pl.BlockSpec((halo, ch), lambda i: (jnp.minimum((i + 1) * per, last_halo), 0)),
            _full((SSD_CONV, ch)), _full((1, ch)),
        ],
        out_specs=pl.BlockSpec((rb, ch), lambda i: (i, 0)),
        out_shape=jax.ShapeDtypeStruct((t, ch), BF16),
        compiler_params=_cparams(("parallel",)), name="ssd_conv",
    )(xbc, xbc, xbc, conv_w, conv_b.reshape(1, ch))


def _ssd_kernel(xf_ref, dtf_ref, xb_ref, dtb_ref, bias_ref, alog_ref, tril_ref, triu_ref, eye_ref, yf_ref, yb_ref,
                sf_ref, sb_ref):
    @pl.when(pl.program_id(1) == 0)
    def _():
        sf_ref[...] = jnp.zeros_like(sf_ref)
        sb_ref[...] = jnp.zeros_like(sb_ref)

    _ssd_chunk(xf_ref, dtf_ref, bias_ref, alog_ref, tril_ref, triu_ref, eye_ref, yf_ref, sf_ref, 0, False)
    _ssd_chunk(xb_ref, dtb_ref, bias_ref, alog_ref, triu_ref, tril_ref, eye_ref, yb_ref, sb_ref, SSD_HEADS, True)


def _ssd_chunk(x_ref, dt_ref, bias_ref, alog_ref, tri_ref, trit_ref, eye_ref, y_ref, state_ref, lane_off, reverse):
    ln = SSD_CHUNK
    hi = lax.Precision.HIGHEST
    x = x_ref[:, 0:SSD_INNER]
    raw = dt_ref[...] + bias_ref[...]
    dt = jnp.maximum(raw, 0.0) + jnp.log1p(jnp.exp(-jnp.abs(raw)))
    adt = dt * (-jnp.exp(alog_ref[...]))
    acs = jnp.dot(tri_ref[...], adt, preferred_element_type=F32, precision=hi)
    acs_row = lax.dot_general(adt, trit_ref[...], (((0,), (0,)), ((), ())), preferred_element_type=F32, precision=hi)
    dt_row = lax.dot_general(dt, eye_ref[...], (((0,), (0,)), ((), ())), preferred_element_type=F32, precision=hi)
    tot = acs[0:1, :] if reverse else acs[ln - 1:ln, :]
    w_all = jnp.exp(tot - acs) * dt
    ea_all = jnp.exp(acs)
    etot = jnp.exp(tot)
    li = lax.broadcasted_iota(jnp.int32, (ln, ln), 0)
    si = lax.broadcasted_iota(jnp.int32, (ln, ln), 1)
    keep = (si >= li) if reverse else (si <= li)
    lane = lax.broadcasted_iota(jnp.int32, (ln, LANES), 1)
    left = lane < SSD_HEAD_DIM
    per_group = SSD_HEADS // SSD_GROUPS
    for gi in range(SSD_GROUPS):
        bm = x_ref[:, SSD_INNER + SSD_STATE * gi:SSD_INNER + SSD_STATE * (gi + 1)].astype(BF16)
        cm = x_ref[:, SSD_INNER + SSD_STATE * (SSD_GROUPS + gi):SSD_INNER + SSD_STATE * (SSD_GROUPS + gi + 1)]
        cm = cm.astype(BF16)
        cb = _dot_nt(cm, bm)
        for pr in range(per_group // 2):
            h0 = gi * per_group + 2 * pr
            xp = x[:, SSD_HEAD_DIM * h0:SSD_HEAD_DIM * (h0 + 2)].astype(F32)
            mats = []
            for hd in (h0, h0 + 1):
                c = lane_off + hd
                seg = acs[:, c:c + 1] - acs_row[c:c + 1, :]
                lmat = jnp.exp(jnp.where(keep, seg, -jnp.inf))
                mats.append((cb * lmat * dt_row[c:c + 1, :]).astype(BF16))
            xb = xp.astype(BF16)
            zero = jnp.zeros_like(xb)
            rhs = jnp.concatenate([jnp.where(left, xb, zero), jnp.where(left, zero, xb)], axis=0)
            y_diag = _dot(jnp.concatenate(mats, axis=1), rhs)
            st = state_ref[h0 // 2]
            c0 = lane_off + h0
            ea = jnp.where(left, ea_all[:, c0:c0 + 1], ea_all[:, c0 + 1:c0 + 2])
            y_off = _dot(cm, st.astype(BF16)) * ea
            y_ref[:, SSD_HEAD_DIM * h0:SSD_HEAD_DIM * (h0 + 2)] = (y_diag + y_off).astype(y_ref.dtype)
            wcol = jnp.where(left, w_all[:, c0:c0 + 1], w_all[:, c0 + 1:c0 + 2])
            cs = _dot_tn(bm, (xp * wcol).astype(BF16))
            dec = jnp.where(left[0:1, :], etot[:, c0:c0 + 1], etot[:, c0 + 1:c0 + 2])
            state_ref[h0 // 2] = st * dec + cs


def _ssd_call(g, xbc_act, dt, dt_bias_f, dt_bias_b, a_log_f, a_log_b):
    t = xbc_act.shape[0]
    ln = SSD_CHUNK
    cc = g.c // ln
    nl = g.n // ln
    ctx0 = g.b * g.n // ln
    fwd = lambda b, k: (jnp.where(k < cc, ctx0 + b * cc + k, b * nl + (k - cc)), 0)
    bwd = lambda b, k: (jnp.where(k < cc, ctx0 + b * cc + (cc - 1 - k), b * nl + (nl - 1 - (k - cc))), 0)
    idx = np.arange(ln)
    lower = (idx[:, None] >= idx[None, :]).astype(np.float32)
    lanes = lambda pf, pb: jnp.pad(jnp.concatenate([pf, pb]).reshape(1, -1), ((0, 0), (0, LANES - 2 * SSD_HEADS)))
    state = pltpu.VMEM((SSD_HEADS // 2, SSD_STATE, 2 * SSD_HEAD_DIM), F32)
    y_shape = jax.ShapeDtypeStruct((t, SSD_INNER), BF16)
    return pl.pallas_call(
        _ssd_kernel, grid=(g.b, cc + nl),
        in_specs=[pl.BlockSpec((ln, SSD_XBC), fwd), pl.BlockSpec((ln, LANES), fwd),
                  pl.BlockSpec((ln, SSD_XBC), bwd), pl.BlockSpec((ln, LANES), bwd),
                  _full((1, LANES)), _full((1, LANES)), _full((ln, ln)), _full((ln, ln)), _full((ln, ln))],
        out_specs=[pl.BlockSpec((ln, SSD_INNER), fwd), pl.BlockSpec((ln, SSD_INNER), bwd)],
        out_shape=[y_shape, y_shape],
        scratch_shapes=[state, state],
        compiler_params=_cparams(("parallel", "arbitrary")),
        name="ssd_scan",
    )(xbc_act, dt, xbc_act, dt, lanes(dt_bias_f, dt_bias_b), lanes(a_log_f, a_log_b), jnp.asarray(lower),
      jnp.asarray(lower.T), jnp.eye(ln, dtype=F32))


def _diff_finish(o, tq, lq1_ref, lk1_ref, lq2_ref, lk2_ref, sub_ref, o_ref, lambda_init):
    lam = (jnp.exp(jnp.sum(lq1_ref[...] * lk1_ref[...], axis=1, keepdims=True))
           - jnp.exp(jnp.sum(lq2_ref[...] * lk2_ref[...], axis=1, keepdims=True)) + lambda_init)
    od = o[:, 0:tq] - lam * o[:, tq:2 * tq]
    o_ref[...] = ((_rms(od, 0) * sub_ref[...]) * (1.0 - lambda_init)).astype(o_ref.dtype)


def _diff_attn_kernel(qt_ref, *refs, lambda_init):
    o_ref, s_ref = refs[-2:]
    params = refs[-7:-2]
    kv = refs[:-7]
    half = len(kv) // 2
    segs = [(kv[i], kv[half + i]) for i in range(half)]
    qt = jnp.concatenate([qt_ref[0], qt_ref[1]], axis=1)
    o = _attend(qt, segs, s_ref, DIFF_V)
    _diff_finish(o, qt_ref.shape[2], *params, o_ref, lambda_init)


def _diff_attn_call(g, qd, kd, vd, lq1, lk1, lq2, lk2, subln, lambda_init):
    tq = DIFF_Q_TILE
    nq = g.n // tq
    ctx_blk = g.b * g.n // g.c
    vec = lambda a: a.reshape(1, -1)
    params = [_full((1, DIFF_DIM)), _full((1, DIFF_DIM)), _full((1, DIFF_DIM)), _full((1, DIFF_DIM)),
              _full((DIFF_V, 1))]
    pvals = (vec(lq1), vec(lk1), vec(lq2), vec(lk2), subln.reshape(-1, 1))
    klat = pl.BlockSpec((None, g.n, 128), lambda b, j, *_: (j, b, 0))
    kctx = pl.BlockSpec((None, g.c, 128), lambda b, j, *_: (j, ctx_blk + b, 0))
    vlat = pl.BlockSpec((None, DIFF_V, g.n), lambda b, j, *_: (j, 0, b))
    vctx = pl.BlockSpec((None, DIFF_V, g.c), lambda b, j, *_: (j, 0, ctx_blk + b))
    kern = functools.partial(_diff_attn_kernel, lambda_init=lambda_init)
    lat = pl.pallas_call(
        kern, grid=(g.b, DIFF_HEADS, nq),
        scratch_shapes=[pltpu.VMEM((g.n + g.c, 2 * tq), F32)],
        in_specs=[pl.BlockSpec((2, 128, tq), lambda b, j, qi: (j, 0, b * nq + qi)), klat, kctx, vlat, vctx] + params,
        out_specs=pl.BlockSpec((DIFF_V, tq), lambda b, j, qi: (j, b * nq + qi)),
        out_shape=jax.ShapeDtypeStruct((DIFF_HEADS * DIFF_V, g.b * g.n), BF16),
        compiler_params=_cparams(("parallel", "parallel", "arbitrary")),
        name="attn_diff",
    )(qd, kd, kd, vd, vd, *pvals)
    ctx = pl.pallas_call(
        kern, grid=(g.b, DIFF_HEADS),
        scratch_shapes=[pltpu.VMEM((g.c, 2 * g.c), F32)],
        in_specs=[pl.BlockSpec((2, 128, g.c), lambda b, j: (j, 0, ctx_blk + b)), kctx, vctx] + params,
        out_specs=pl.BlockSpec((DIFF_V, g.c), lambda b, j: (j, b)),
        out_shape=jax.ShapeDtypeStruct((DIFF_HEADS * DIFF_V, g.b * g.c), BF16),
        compiler_params=_cparams(("parallel", "parallel")),
        name="attn_diff_ctx",
    )(qd, kd, vd, *pvals)
    return lat, ctx


def _rec_post_kernel(yf_ref, yb_ref, xs_ref, z_ref, odl_ref, odc_ref, dskip_ref, gn_ref, wa_ref, wb_ref,
                     h_ref, mod_ref, gpost_ref, gffn_ref, rw_ref, hn_ref, v_ref, lg_ref, *, lat_blocks):
    f32 = lambda ref: ref[...].astype(F32)
    y = (f32(yf_ref) + f32(yb_ref) + f32(xs_ref) * dskip_ref[...]) * _silu(f32(z_ref))
    half = SSD_INNER // SSD_GROUPS
    o = _dot_tn(_pick_part(lat_blocks, odl_ref, odc_ref), wb_ref[...])
    for gi in range(SSD_GROUPS):
        lo = half * gi
        yn = (_rms(y[:, lo:lo + half], -1) * gn_ref[:, lo:lo + half]).astype(BF16)
        o = o + _dot(yn, wa_ref[lo:lo + half, :])
    _post_tail(o, h_ref[...], mod_ref, gpost_ref, gffn_ref, rw_ref, hn_ref, v_ref, lg_ref)


def _rec_post_call(g, yf, yb, xbc_act, zs, od_parts, d_skip, ssd_norm, w_out, h, mod, gpost, gffn, router_w):
    d = h.shape[1]
    rb = ROW_BLOCK
    ins, outs, shapes = _post_specs(g, d)
    wo = w_out.astype(BF16)
    row512 = pl.BlockSpec((rb, SSD_INNER), lambda i: (i, 0))
    return pl.pallas_call(
        functools.partial(_rec_post_kernel, lat_blocks=g.lat_blocks), grid=(g.n_blocks,),
        in_specs=[row512, row512, row512, row512] + _part_specs(g, (SSD_INNER, rb), 1)
        + [_full((1, SSD_INNER)), _full((1, SSD_INNER)), _full((SSD_INNER, d)), _full((SSD_INNER, d)),
           pl.BlockSpec((rb, d), lambda i: (i, 0))] + ins,
        out_specs=outs, out_shape=shapes,
        compiler_params=_cparams(("parallel",)), name="rec_post",
    )(yf, yb, xbc_act, zs, *od_parts, jnp.repeat(d_skip, SSD_HEAD_DIM).reshape(1, -1), ssd_norm.reshape(1, -1),
      wo[:SSD_INNER], wo[SSD_INNER:], h, mod, gpost.reshape(1, d), gffn.reshape(1, d), router_w.T)


def _ffn_res_kernel(h_ref, f_ref, mod_ref, g_ref, o_ref):
    o_ref[...] = h_ref[...] + mod_ref[5:6, :] * (_rms(f_ref[...], -1) * g_ref[...])


def _ffn_res_call(g, h, f, mod, gain, n_rows):
    d = h.shape[1]
    rb = ROW_BLOCK
    row = pl.BlockSpec((rb, d), lambda i: (i, 0))
    return pl.pallas_call(
        _ffn_res_kernel, grid=(n_rows // rb,),
        in_specs=[row, row, pl.BlockSpec((None, 6, d), lambda i: (g.mod_row(i), 0, 0)), _full((1, d))],
        out_specs=row, out_shape=jax.ShapeDtypeStruct((n_rows, d), F32),
        compiler_params=_cparams(("parallel",)), name="ffn_res",
    )(h, f, mod, gain.reshape(1, d))


def kernel(x, c, ctx, c_ctx, ada_w, ada_b, norm_mix_pre, norm_mix_post, norm_ffn_pre, norm_ffn_post, mix_w_out, att_w_in, mla_q_norm, mla_w_q_up, mla_kv_norm, mla_w_kv_up, gqa_q_norm, gqa_k_norm, rec_w_in, ssd_conv_w, ssd_conv_b, ssd_dt_bias_f, ssd_dt_bias_b, ssd_a_log_f, ssd_a_log_b, ssd_d, ssd_norm, diff_lambda_q1, diff_lambda_k1, diff_lambda_q2, diff_lambda_k2, diff_subln, router_w, router_b, exp_w_gate, exp_w_up, exp_w_down, sh_w_gate, sh_w_up, sh_w_down):
    b, n, d = x.shape
    n_ctx = ctx.shape[1]
    depth = ada_w.shape[0]
    g = _Geom(b, n, n_ctx)
    mod_rows = -(-(b + 1) // 8) * 8
    c_all = jnp.concatenate([c, c_ctx[None, :], jnp.zeros((mod_rows - b - 1, d), F32)], axis=0)
    mods = _ada_call(c_all, ada_w, ada_b).reshape(depth, mod_rows, 6, d)
    h = None
    for i in range(depth):
        last = i == depth - 1
        jdx = i // 2
        mod = mods[i]
        if i % 2 == 0:
            w = _attn_pre_weights(g, att_w_in[jdx], mla_q_norm[jdx], mla_w_q_up[jdx], mla_kv_norm[jdx],
                                  mla_w_kv_up[jdx], gqa_q_norm[jdx], gqa_k_norm[jdx])
            h_parts = (x.reshape(b * n, d), ctx.reshape(b * n_ctx, d)) if i == 0 else (h[:b * n], h[b * n:])
            qm, km, vm, qg, kg, vg = _attn_pre_call(g, h_parts, mod, norm_mix_pre[i].reshape(1, d), w)
            oa = _attn_call(g, qm, km, vm, kv_heads=MLA_HEADS, shared_k=False, name="attn_mla")
            ob = _attn_call(g, qg, kg, vg, kv_heads=GQA_KV_HEADS, shared_k=True, name="attn_gqa")
            h, v, logits_t = _attn_post_call(g, oa, ob, mix_w_out[i], h_parts, mod, norm_mix_post[i], norm_ffn_pre[i],
                                             router_w[i])
        else:
            lambda_init = 0.8 - 0.6 * math.exp(-0.3 * i)
            w = _rec_pre_weights(g, rec_w_in[jdx])
            h, zs, xbc, dt, kd, qd, vd = _rec_pre_call(g, h, f, mods[i - 1], norm_ffn_post[i - 1], mod,
                                                       norm_mix_pre[i].reshape(1, d), w)
            xbc_act = _conv_call(g, xbc, ssd_conv_w[jdx], ssd_conv_b[jdx])
            yf, yb = _ssd_call(g, xbc_act, dt, ssd_dt_bias_f[jdx], ssd_dt_bias_b[jdx], ssd_a_log_f[jdx],
                               ssd_a_log_b[jdx])
            od = _diff_attn_call(g, qd, kd, vd, diff_lambda_q1[jdx], diff_lambda_k1[jdx], diff_lambda_q2[jdx],
                                 diff_lambda_k2[jdx], diff_subln[jdx], lambda_init)
            h, v, logits_t = _rec_post_call(g, yf, yb, xbc_act, zs, od, ssd_d[jdx], ssd_norm[jdx], mix_w_out[i], h,
                                            mod, norm_mix_post[i], norm_ffn_pre[i], router_w[i])
        nsb = -(-(b * n) // MOE_TOKENS) if last else g.t // MOE_TOKENS
        gates, rank, counts = _router_call(g, logits_t, router_b[i], nsb)
        f = _moe_call(g, v, gates, rank, counts, nsb, exp_w_gate[i], exp_w_up[i], exp_w_down[i],
                      sh_w_gate[i], sh_w_up[i], sh_w_down[i])
        if last or i % 2 == 1:
            h = _ffn_res_call(g, h, f, mod, norm_ffn_post[i], b * n if last else g.t)
    return h.reshape(b, n, d)
```

```python
import functools
import math

import numpy as np
import jax
import jax.numpy as jnp
from jax import lax
from jax.experimental import pallas as pl
from jax.experimental.pallas import tpu as pltpu

F32 = jnp.float32
BF16 = jnp.bfloat16
LOG2E = 1.4426950408889634

GRID_W = 64
ROPE_BASE = 10000.0
NORM_EPS = 1e-6

MLA_HEADS, MLA_Q_LORA, MLA_KV_LORA, MLA_NOPE, MLA_ROPE, MLA_V = 8, 256, 128, 64, 32, 64
MLA_SCALE = (MLA_NOPE + MLA_ROPE) ** -0.5
GQA_HEADS, GQA_KV_HEADS, GQA_DIM = 8, 2, 64
GQA_SCALE = GQA_DIM ** -0.5
SSD_HEADS, SSD_HEAD_DIM, SSD_GROUPS, SSD_STATE, SSD_CONV, SSD_CHUNK = 8, 64, 2, 128, 3, 128
SSD_INNER = SSD_HEADS * SSD_HEAD_DIM
SSD_XBC = SSD_INNER + 2 * SSD_GROUPS * SSD_STATE
DIFF_HEADS, DIFF_DIM = 4, 64
DIFF_V = 2 * DIFF_DIM
DIFF_SCALE = DIFF_DIM ** -0.5
N_EXPERTS, TOP_K, N_GROUPS, TOPK_GROUPS, EXPERT_FF, SHARED_FF = 64, 8, 8, 4, 256, 256
ROUTED_SCALE = 2.5

LANES = 128
VMEM_LIMIT_BYTES = 56 * 1024 * 1024

ROW_BLOCK = 512
ATTN_Q_TILE = 1024
DIFF_Q_TILE = 512
ATTN_KV_CHUNK = 512
ATTN_EXP_ROWS = 4096
MOE_TOKENS = 768
MOE_ROWS = 128
MOE_PAIR = 8
MOE_SLOTS = 8


def _cparams(sem):
    return pltpu.CompilerParams(dimension_semantics=sem, vmem_limit_bytes=VMEM_LIMIT_BYTES)


def _rms(x, axis):
    return x * lax.rsqrt(jnp.mean(x * x, axis=axis, keepdims=True) + NORM_EPS)


def _silu(x):
    return x * jax.nn.sigmoid(x)


def _dot(a, b):
    return jnp.dot(a, b, preferred_element_type=F32)


def _dot_nt(a, b):
    return lax.dot_general(a, b, (((1,), (1,)), ((), ())), preferred_element_type=F32)


def _dot_tn(a, b):
    return lax.dot_general(a, b, (((0,), (0,)), ((), ())), preferred_element_type=F32)


def _ada_kernel(c_ref, w_ref, b_ref, o_ref):
    s = _silu(c_ref[...])
    o_ref[...] = jnp.dot(s, w_ref[...], preferred_element_type=F32, precision=lax.Precision.HIGHEST) + b_ref[...]


def _ada_call(c_all, ada_w, ada_b):
    depth, d, six_d = ada_w.shape
    rows = c_all.shape[0]
    cols = six_d // 4
    return pl.pallas_call(
        _ada_kernel,
        grid=(depth, six_d // cols),
        in_specs=[
            pl.BlockSpec((rows, d), lambda i, j: (0, 0)),
            pl.BlockSpec((None, d, cols), lambda i, j: (i, 0, j)),
            pl.BlockSpec((None, 1, cols), lambda i, j: (i, 0, j)),
        ],
        out_specs=pl.BlockSpec((None, rows, cols), lambda i, j: (i, 0, j)),
        out_shape=jax.ShapeDtypeStruct((depth, rows, six_d), F32),
        compiler_params=_cparams(("arbitrary", "arbitrary")),
        name="ada_mod",
    )(c_all, ada_w, ada_b.reshape(depth, 1, six_d))


def _rot_rows(x):
    q = x.shape[0] // 4
    return jnp.concatenate([-x[q:2 * q], x[0:q], -x[3 * q:4 * q], x[2 * q:3 * q]], axis=0)


def _attn_pre_kernel(hl_ref, hc_ref, mod_ref, gpre_ref, wfm_ref, wtm_ref, qn_ref, kvn_col_ref, kvn_row_ref, wq_ref,
                     wk_ref, wv_ref, gq_g_ref, gk_g_ref, gk_gr_ref,
                     cqm_ref, sqm_ref, cqh_ref, sqh_ref, ckm_ref, skm_ref, ckh_ref, skh_ref,
                     qm_ref, km_ref, vm_ref, qg_ref, kg_ref, vg_ref, *, lat_blocks):
    h = _pick_part(lat_blocks, hl_ref, hc_ref)
    u = _rms(h, -1) * gpre_ref[...]
    u = u * (1.0 + mod_ref[1:2, :]) + mod_ref[0:1, :]
    ub = u.astype(BF16)
    zt = _dot_nt(wfm_ref[...], ub)
    zk = _dot(ub, wtm_ref[...])

    qn = (_rms(zt[0:256], 0) * qn_ref[...]).astype(BF16)
    qt = _dot(wq_ref[...], qn)
    cq, sq = cqm_ref[...], sqm_ref[...]
    qs = MLA_SCALE * LOG2E
    zero32 = jnp.zeros((32, h.shape[0]), BF16)
    for hd in range(MLA_HEADS):
        pe_raw = qt[512 + 32 * hd:544 + 32 * hd]
        pe = pe_raw * cq + _rot_rows(pe_raw) * sq
        qm_ref[hd, 0:64, :] = (qt[64 * hd:64 * hd + 64] * qs).astype(BF16)
        qm_ref[hd, 64:96, :] = (pe * qs).astype(BF16)
        qm_ref[hd, 96:128, :] = zero32

    kvn_t = (_rms(zt[256:384], 0) * kvn_col_ref[...]).astype(BF16)
    vt = _dot(wv_ref[...], kvn_t)
    for hd in range(MLA_HEADS):
        vm_ref[hd] = vt[64 * hd:64 * hd + 64].astype(BF16)
    kvn = (_rms(zk[:, 0:128], -1) * kvn_row_ref[...]).astype(BF16)
    kn = _dot(kvn, wk_ref[...])
    kpe = zk[:, 128:256] * ckm_ref[...] + zk[:, 256:384] * skm_ref[...]
    for hd in range(MLA_HEADS):
        km_ref[hd] = (kn[:, 128 * hd:128 * hd + 128] + kpe).astype(BF16)

    cqh, sqh = cqh_ref[...], sqh_ref[...]
    gs = GQA_SCALE * LOG2E
    zero64 = jnp.zeros((64, h.shape[0]), BF16)
    grp = GQA_HEADS // GQA_KV_HEADS
    for hd in range(GQA_HEADS):
        raw = zt[384 + 64 * hd:448 + 64 * hd]
        qn_h = raw * lax.rsqrt(jnp.mean(raw * raw, axis=0, keepdims=True) + NORM_EPS) * gq_g_ref[...]
        q = ((qn_h * cqh + _rot_rows(qn_h) * sqh) * gs).astype(BF16)
        if hd // grp == 0:
            qg_ref[hd, 0:64, :] = q
            qg_ref[hd, 64:128, :] = zero64
        else:
            qg_ref[hd, 0:64, :] = zero64
            qg_ref[hd, 64:128, :] = q
    for kvh in range(GQA_KV_HEADS):
        vg_ref[kvh] = zt[896 + 64 * kvh:960 + 64 * kvh].astype(BF16)

    gk, gkr = zk[:, 384:512], zk[:, 512:640]
    lane = lax.broadcasted_iota(jnp.int32, gk.shape, 1)
    lo = lane < 64
    sq0 = jnp.sum(jnp.where(lo, gk * gk, 0.0), axis=-1, keepdims=True)
    sq1 = jnp.sum(jnp.where(lo, 0.0, gk * gk), axis=-1, keepdims=True)
    r = jnp.where(lo, lax.rsqrt(sq0 / GQA_DIM + NORM_EPS), lax.rsqrt(sq1 / GQA_DIM + NORM_EPS))
    kg_ref[...] = ((gk * r * gk_g_ref[...]) * ckh_ref[...] + (gkr * r * gk_gr_ref[...]) * skh_ref[...]).astype(BF16)


def _rot_map(dim):
    q = dim // 4
    j = np.arange(dim)
    even = (j // q) % 2 == 0
    return np.where(even, j + q, j - q), np.where(even, -1.0, 1.0).astype(np.float32)


def _rope_tables(n_tokens, dim, pad_rows):
    rows = n_tokens // GRID_W
    row = jnp.repeat(jnp.arange(rows, dtype=F32), GRID_W)
    col = jnp.tile(jnp.arange(GRID_W, dtype=F32), rows)
    half = dim // 2
    inv = ROPE_BASE ** (-(jnp.arange(half // 2, dtype=F32) * 2.0 / half))
    ang_r = row[:, None] * inv
    ang_c = col[:, None] * inv
    ang = jnp.concatenate([ang_r, ang_r, ang_c, ang_c], axis=-1)
    cos = jnp.concatenate([jnp.cos(ang), jnp.ones((pad_rows, dim), F32)], axis=0)
    sin = jnp.concatenate([jnp.sin(ang), jnp.zeros((pad_rows, dim), F32)], axis=0)
    return cos, sin


class _Geom:
    def __init__(self, b, n, c):
        assert n % ROW_BLOCK == 0 and (b * c) % ROW_BLOCK == 0 and n % c == 0 and n % GRID_W == 0
        self.b, self.n, self.c = b, n, c
        self.t = b * n + b * c
        self.lat_blocks = b * n // ROW_BLOCK
        self.blocks_per_seq = n // ROW_BLOCK
        self.n_blocks = self.t // ROW_BLOCK
        assert self.t % MOE_TOKENS == 0

    def mod_row(self, i):
        return jnp.where(i < self.lat_blocks, i // self.blocks_per_seq, self.b)

    def pos_block(self, i):
        return jnp.where(i < self.lat_blocks, i % self.blocks_per_seq, self.blocks_per_seq)


def _full(shape):
    nd = len(shape)
    return pl.BlockSpec(shape, lambda *_: (0,) * nd)


def _part_specs(g, block, axis):
    def spec(latent):
        def index(i):
            j = jnp.minimum(i, g.lat_blocks - 1) if latent else jnp.maximum(i - g.lat_blocks, 0)
            return tuple(j if a == axis else 0 for a in range(len(block)))
        return pl.BlockSpec(block, index)
    return [spec(True), spec(False)]


def _pick_part(lat_blocks, lat_ref, ctx_ref):
    return jnp.where(pl.program_id(0) < lat_blocks, lat_ref[...], ctx_ref[...])


def _attn_pre_call(g, h_parts, mod, gpre, w):
    d = h_parts[0].shape[1]
    t = g.t
    rb = ROW_BLOCK
    row = lambda i: (i, 0)
    tm_tab = pl.BlockSpec((rb, LANES), lambda i: (g.pos_block(i), 0))
    fm32 = pl.BlockSpec((32, rb), lambda i: (0, g.pos_block(i)))
    fm64 = pl.BlockSpec((64, rb), lambda i: (0, g.pos_block(i)))
    in_specs = _part_specs(g, (rb, d), 0) + [
        pl.BlockSpec((None, 6, d), lambda i: (g.mod_row(i), 0, 0)),
        _full(gpre.shape), _full(w["wfm"].shape), _full(w["wtm"].shape), _full(w["qn"].shape),
        _full(w["kvn_col"].shape), _full(w["kvn_row"].shape), _full(w["wq"].shape), _full(w["wk"].shape),
        _full(w["wv"].shape), _full(w["gq_g"].shape), _full(w["gk_g"].shape), _full(w["gk_gr"].shape),
        fm32, fm32, fm64, fm64, tm_tab, tm_tab, tm_tab, tm_tab,
    ]
    out_shape = [
        jax.ShapeDtypeStruct((MLA_HEADS, 128, t), BF16),
        jax.ShapeDtypeStruct((MLA_HEADS, t, 128), BF16),
        jax.ShapeDtypeStruct((MLA_HEADS, MLA_V, t), BF16),
        jax.ShapeDtypeStruct((GQA_HEADS, 128, t), BF16),
        jax.ShapeDtypeStruct((t, 128), BF16),
        jax.ShapeDtypeStruct((GQA_KV_HEADS, GQA_DIM, t), BF16),
    ]
    out_specs = [
        pl.BlockSpec((MLA_HEADS, 128, rb), lambda i: (0, 0, i)),
        pl.BlockSpec((MLA_HEADS, rb, 128), lambda i: (0, i, 0)),
        pl.BlockSpec((MLA_HEADS, MLA_V, rb), lambda i: (0, 0, i)),
        pl.BlockSpec((GQA_HEADS, 128, rb), lambda i: (0, 0, i)),
        pl.BlockSpec((rb, 128), row),
        pl.BlockSpec((GQA_KV_HEADS, GQA_DIM, rb), lambda i: (0, 0, i)),
    ]
    return pl.pallas_call(
        functools.partial(_attn_pre_kernel, lat_blocks=g.lat_blocks),
        grid=(g.n_blocks,), in_specs=in_specs, out_specs=out_specs, out_shape=out_shape,
        compiler_params=_cparams(("parallel",)), name="attn_pre",
    )(*h_parts, mod, gpre, w["wfm"], w["wtm"], w["qn"], w["kvn_col"], w["kvn_row"], w["wq"], w["wk"], w["wv"],
      w["gq_g"], w["gk_g"], w["gk_gr"],
      w["cqm"], w["sqm"], w["cqh"], w["sqh"], w["ckm"], w["skm"], w["ckh"], w["skh"])


def _attn_pre_weights(g, att_w_in, q_norm, w_q_up, kv_norm, w_kv_up, gqa_qn, gqa_kn):
    d = att_w_in.shape[0]
    o = np.cumsum([0, MLA_Q_LORA, MLA_KV_LORA, MLA_ROPE, GQA_HEADS * GQA_DIM, GQA_KV_HEADS * GQA_DIM,
                   GQA_KV_HEADS * GQA_DIM])
    w_qlat, w_kvlat, w_kpe, w_gq, w_gk, w_gv = (att_w_in[:, o[i]:o[i + 1]] for i in range(6))
    src32, sgn32 = _rot_map(MLA_ROPE)
    src64, sgn64 = _rot_map(GQA_DIM)

    def rot_heads(wcols, heads, dim, src, sgn):
        wh = wcols.reshape(d, heads, dim)
        return (wh[:, :, src] * sgn).reshape(d, heads * dim)

    wfm = jnp.concatenate([w_qlat, w_kvlat, w_gq, w_gv], axis=1)
    zpad = lambda x, lo, hi: jnp.pad(x, ((0, 0), (lo, hi)))
    wtm = jnp.concatenate([
        w_kvlat, zpad(w_kpe, 64, 32), zpad(w_kpe[:, src32] * sgn32, 64, 32),
        w_gk, rot_heads(w_gk, GQA_KV_HEADS, GQA_DIM, src64, sgn64)], axis=1)
    wq = w_q_up.reshape(MLA_Q_LORA, MLA_HEADS, MLA_NOPE + MLA_ROPE)
    wq_all = jnp.concatenate([wq[:, :, :MLA_NOPE].reshape(MLA_Q_LORA, -1),
                              wq[:, :, MLA_NOPE:].reshape(MLA_Q_LORA, -1)], axis=1)
    wkv = w_kv_up.reshape(MLA_KV_LORA, MLA_HEADS, MLA_NOPE + MLA_V)
    wk = jnp.pad(wkv[:, :, :MLA_NOPE], ((0, 0), (0, 0), (0, 128 - MLA_NOPE))).reshape(MLA_KV_LORA, -1)
    wv = wkv[:, :, MLA_NOPE:].reshape(MLA_KV_LORA, -1)
    cos_m, sin_m = _rope_tables(g.n, MLA_ROPE, ROW_BLOCK)
    cos_h, sin_h = _rope_tables(g.n, GQA_DIM, ROW_BLOCK)
    two = lambda x: jnp.concatenate([x, x], axis=1)
    return dict(
        wfm=wfm.T.astype(BF16), wtm=wtm.astype(BF16),
        qn=q_norm.reshape(-1, 1), kvn_col=kv_norm.reshape(-1, 1), kvn_row=kv_norm.reshape(1, -1),
        wq=wq_all.T.astype(BF16), wk=wk.astype(BF16), wv=wv.T.astype(BF16),
        gq_g=gqa_qn.reshape(-1, 1),
        gk_g=two(gqa_kn.reshape(1, -1)), gk_gr=two(gqa_kn[src64].reshape(1, -1)),
        cqm=cos_m.T, sqm=sin_m.T, cqh=cos_h.T, sqh=sin_h.T,
        ckm=zpad(cos_m, 64, 32), skm=zpad(sin_m, 64, 32), ckh=two(cos_h), skh=two(sin_h),
    )


def _fold8(x, op):
    r, w = x.shape
    return op(x.reshape(r // 8, 8, w), axis=0)


def _attend(qt, segments, s_ref, dv):
    tq = qt.shape[1]
    pieces, off = [], 0
    for k_ref, vt_ref in segments:
        n = k_ref.shape[0]
        for lo in range(0, n, ATTN_KV_CHUNK):
            rows = min(ATTN_KV_CHUNK, n - lo)
            pieces.append((off, lo, rows, k_ref, vt_ref))
            off += rows
    m8 = None
    for so, lo, rows, k_ref, _ in pieces:
        s = _dot(k_ref[lo:lo + rows, :], qt)
        s_ref[so:so + rows, :] = s
        part = _fold8(s, jnp.max)
        m8 = part if m8 is None else jnp.maximum(m8, part)
    m = jnp.max(m8, axis=0, keepdims=True)
    acc = jnp.zeros((dv + 16, tq), F32)
    off = 0
    for _, vt_ref in segments:
        n = vt_ref.shape[1]
        step = min(n, ATTN_EXP_ROWS)

        def body(i, acc, off=off, vt_ref=vt_ref, step=step):
            base = i * step if isinstance(i, int) else pl.multiple_of(i * step, step)
            for lo in range(0, step, ATTN_KV_CHUNK):
                rows = min(ATTN_KV_CHUNK, step - lo)
                p = jnp.exp2((s_ref[pl.ds(off + base + lo, rows), :] - m).astype(BF16))
                vt1 = jnp.concatenate([vt_ref[:, pl.ds(base + lo, rows)], jnp.ones((16, rows), BF16)], axis=0)
                acc = acc + _dot(vt1, p)
            return acc

        acc = body(0, acc) if n == step else lax.fori_loop(0, n // step, body, acc)
        off += n
    return acc[0:dv] / acc[dv:dv + 1]


def _attn_kernel(qt_ref, *refs, dv):
    o_ref, s_ref = refs[-2:]
    kv = refs[:-2]
    half = len(kv) // 2
    segs = [(kv[i], kv[half + i]) for i in range(half)]
    o_ref[...] = _attend(qt_ref[...], segs, s_ref, dv).astype(o_ref.dtype)


def _attn_call(g, qt, k, vt, *, kv_heads, shared_k, name):
    heads = qt.shape[0]
    dv = vt.shape[1]
    grp = heads // kv_heads
    tq = ATTN_Q_TILE
    nq = g.n // tq
    ctx_blk = g.b * g.n // g.c
    if shared_k:
        klat = pl.BlockSpec((g.n, 128), lambda b, h, *_: (b, 0))
        kctx = pl.BlockSpec((g.c, 128), lambda b, h, *_: (ctx_blk + b, 0))
    else:
        klat = pl.BlockSpec((None, g.n, 128), lambda b, h, *_: (h // grp, b, 0))
        kctx = pl.BlockSpec((None, g.c, 128), lambda b, h, *_: (h // grp, ctx_blk + b, 0))
    vlat = pl.BlockSpec((None, dv, g.n), lambda b, h, *_: (h // grp, 0, b))
    vctx = pl.BlockSpec((None, dv, g.c), lambda b, h, *_: (h // grp, 0, ctx_blk + b))
    lat = pl.pallas_call(
        functools.partial(_attn_kernel, dv=dv),
        grid=(g.b, heads, nq),
        scratch_shapes=[pltpu.VMEM((g.n + g.c, tq), F32)],
        in_specs=[pl.BlockSpec((None, 128, tq), lambda b, h, qi: (h, 0, b * nq + qi)), klat, kctx, vlat, vctx],
        out_specs=pl.BlockSpec((dv, tq), lambda b, h, qi: (h, b * nq + qi)),
        out_shape=jax.ShapeDtypeStruct((heads * dv, g.b * g.n), BF16),
        compiler_params=_cparams(("parallel", "parallel", "arbitrary")),
        name=name,
    )(qt, k, k, vt, vt)
    ctx = pl.pallas_call(
        functools.partial(_attn_kernel, dv=dv),
        grid=(g.b, heads),
        scratch_shapes=[pltpu.VMEM((g.c, g.c), F32)],
        in_specs=[pl.BlockSpec((None, 128, g.c), lambda b, h: (h, 0, ctx_blk + b)), kctx, vctx],
        out_specs=pl.BlockSpec((dv, g.c), lambda b, h: (h, b)),
        out_shape=jax.ShapeDtypeStruct((heads * dv, g.b * g.c), BF16),
        compiler_params=_cparams(("parallel", "parallel")),
        name=name + "_ctx",
    )(qt, k, vt)
    return lat, ctx


def _split_bf16(x):
    hi = x.astype(BF16)
    return hi, (x - hi.astype(F32)).astype(BF16)


def _post_tail(o, h, mod_ref, gpost_ref, gffn_ref, rw_ref, hn_ref, v_ref, lg_ref):
    hn = h + mod_ref[2:3, :] * (_rms(o, -1) * gpost_ref[...])
    hn_ref[...] = hn
    v = (_rms(hn, -1) * gffn_ref[...]) * (1.0 + mod_ref[4:5, :]) + mod_ref[3:4, :]
    v_hi, v_lo = _split_bf16(v)
    v_ref[...] = v_hi
    w_hi, w_lo = _split_bf16(rw_ref[...])
    lg_ref[...] = _dot_nt(w_hi, v_hi) + (_dot_nt(w_hi, v_lo) + _dot_nt(w_lo, v_hi))


def _attn_post_kernel(oal_ref, oac_ref, obl_ref, obc_ref, wa_ref, wb_ref, hl_ref, hc_ref, mod_ref, gpost_ref, gffn_ref,
                      rw_ref, hn_ref, v_ref, lg_ref, *, lat_blocks):
    oa = _pick_part(lat_blocks, oal_ref, oac_ref)
    ob = _pick_part(lat_blocks, obl_ref, obc_ref)
    o = _dot_tn(oa, wa_ref[...]) + _dot_tn(ob, wb_ref[...])
    h = _pick_part(lat_blocks, hl_ref, hc_ref)
    _post_tail(o, h, mod_ref, gpost_ref, gffn_ref, rw_ref, hn_ref, v_ref, lg_ref)


def _post_specs(g, d):
    rb = ROW_BLOCK
    ins = [
        pl.BlockSpec((None, 6, d), lambda i: (g.mod_row(i), 0, 0)),
        _full((1, d)), _full((1, d)), _full((N_EXPERTS, d)),
    ]
    outs = [pl.BlockSpec((rb, d), lambda i: (i, 0)), pl.BlockSpec((rb, d), lambda i: (i, 0)),
            pl.BlockSpec((N_EXPERTS, rb), lambda i: (0, i))]
    shapes = [jax.ShapeDtypeStruct((g.t, d), F32), jax.ShapeDtypeStruct((g.t, d), BF16),
              jax.ShapeDtypeStruct((N_EXPERTS, g.t), F32)]
    return ins, outs, shapes


def _attn_post_call(g, oa_parts, ob_parts, w_out, h_parts, mod, gpost, gffn, router_w):
    d = h_parts[0].shape[1]
    rb = ROW_BLOCK
    half = oa_parts[0].shape[0]
    ins, outs, shapes = _post_specs(g, d)
    wo = w_out.astype(BF16)
    o_specs = _part_specs(g, (half, rb), 1)
    return pl.pallas_call(
        functools.partial(_attn_post_kernel, lat_blocks=g.lat_blocks), grid=(g.n_blocks,),
        in_specs=o_specs + o_specs + [_full((half, d)), _full((half, d))] + _part_specs(g, (rb, d), 0) + ins,
        out_specs=outs, out_shape=shapes,
        compiler_params=_cparams(("parallel",)), name="attn_post",
    )(*oa_parts, *ob_parts, wo[:half], wo[half:], *h_parts, mod, gpost.reshape(1, d), gffn.reshape(1, d), router_w.T)


def _router_kernel(lg_ref, bias_ref, tri_ref, gate_ref, rank_ref, cnt_ref):
    tb = lg_ref.shape[1]
    per = N_EXPERTS // N_GROUPS
    shp = (N_GROUPS, per, tb)
    scores = jax.nn.sigmoid(lg_ref[...])
    s3 = scores.reshape(shp)
    sel = (scores + bias_ref[...]).reshape(shp)
    sub = lax.broadcasted_iota(jnp.int32, shp, 1)
    grp = lax.broadcasted_iota(jnp.int32, shp, 0)
    neg = -jnp.inf
    m1 = jnp.max(sel, axis=1, keepdims=True)
    i1 = jnp.min(jnp.where(sel == m1, sub, per), axis=1, keepdims=True)
    m2 = jnp.max(jnp.where(sub == i1, neg, sel), axis=1, keepdims=True)
    cur = jnp.broadcast_to(m1 + m2, shp)
    gmask = jnp.zeros(shp, F32)
    for _ in range(TOPK_GROUPS):
        gm = jnp.max(cur, axis=0, keepdims=True)
        gi = jnp.min(jnp.where(cur == gm, grp, N_GROUPS), axis=0, keepdims=True)
        pick = grp == gi
        gmask = jnp.where(pick, 1.0, gmask)
        cur = jnp.where(pick, neg, cur)
    masked = jnp.where(gmask > 0.0, sel, neg)
    eidx = grp * per + sub
    chosen = jnp.zeros(shp, F32)
    for _ in range(TOP_K):
        mx = jnp.max(jnp.max(masked, axis=0, keepdims=True), axis=1, keepdims=True)
        cand = jnp.where(masked == mx, eidx, N_EXPERTS)
        ei = jnp.min(jnp.min(cand, axis=0, keepdims=True), axis=1, keepdims=True)
        pick = eidx == ei
        chosen = jnp.where(pick, 1.0, chosen)
        masked = jnp.where(pick, neg, masked)
    top_w = jnp.where(chosen > 0.0, s3, 0.0)
    denom = jnp.sum(jnp.sum(top_w, axis=0, keepdims=True), axis=1, keepdims=True)
    gate_ref[...] = (top_w / denom * ROUTED_SCALE).reshape(N_EXPERTS, tb)
    ch2 = chosen.reshape(N_EXPERTS, tb)
    chb = ch2.astype(BF16)
    before = _dot(chb, tri_ref[...])
    rank_ref[...] = jnp.where(ch2 > 0.0, before, -1.0).astype(jnp.int32)
    cnt_ref[...] = _dot(chb, jnp.ones((tb, LANES), BF16))


def _router_call(g, logits_t, router_b, nsb):
    tb = MOE_TOKENS
    tri = (np.arange(tb)[:, None] < np.arange(tb)[None, :]).astype(np.float32)
    blk = pl.BlockSpec((N_EXPERTS, tb), lambda s: (0, s))
    gates, rank, cnt = pl.pallas_call(
        _router_kernel, grid=(nsb,),
        in_specs=[blk, _full((N_EXPERTS, 1)), _full((tb, tb))],
        out_specs=[blk, blk, pl.BlockSpec((None, N_EXPERTS, LANES), lambda s: (s, 0, 0))],
        out_shape=[jax.ShapeDtypeStruct((N_EXPERTS, g.t), F32), jax.ShapeDtypeStruct((N_EXPERTS, g.t), jnp.int32),
                   jax.ShapeDtypeStruct((nsb, N_EXPERTS, LANES), F32)],
        compiler_params=_cparams(("parallel",)), name="router",
    )(logits_t, router_b.reshape(-1, 1), jnp.asarray(tri, BF16))
    return gates, rank, cnt[:, :, 0].astype(jnp.int32).reshape(-1)


def _moe_kernel(cnt_ref, v_ref, gate_ref, rank_ref, wg_ref, wu_ref, wd_ref, sg_ref, su_ref, sd_ref, o_ref,
                psel_ref, ysel_ref, slot_ref):
    s, j = pl.program_id(0), pl.program_id(1)
    tb = v_ref.shape[0]
    rows = MOE_ROWS

    @pl.when(j == 0)
    def _():
        vb = v_ref[...]
        hid = (_silu(_dot(vb, sg_ref[...])) * _dot(vb, su_ref[...])).astype(BF16)
        o_ref[...] = _dot(hid, sd_ref[...])
        psel_ref[...] = jnp.zeros_like(psel_ref)
        ysel_ref[...] = jnp.zeros_like(ysel_ref)
        slot_ref[0] = 0

    row_id = lax.broadcasted_iota(jnp.int32, (rows, tb), 0)

    def flush(n_slots):
        stage_row = lax.broadcasted_iota(jnp.int32, psel_ref.shape, 0)
        sel = jnp.where(stage_row < n_slots * rows, psel_ref[...], jnp.zeros_like(psel_ref))
        o_ref[...] += _dot_tn(sel, ysel_ref[...])

    ends, total = [], 0
    for e in range(MOE_PAIR):
        total = total + (cnt_ref[s * N_EXPERTS + j * MOE_PAIR + e] + rows - 1) // rows
        ends.append(total)

    def select(i):
        e = sum((i >= end).astype(jnp.int32) for end in ends[:-1])
        first = sum(jnp.where(e == k + 1, ends[k], 0) for k in range(MOE_PAIR - 1))
        ex = j * MOE_PAIR + e
        hit = row_id == rank_ref[pl.ds(ex, 1), :] - (i - first) * rows
        onehot = jnp.where(hit, 1.0, 0.0).astype(BF16)
        w_row = jnp.sum(jnp.where(hit, gate_ref[pl.ds(ex, 1), :], 0.0), axis=1, keepdims=True)
        return e, onehot, w_row

    def run_items(i0, count, slot):
        picks = [select(i0 + k) for k in range(count)]
        sel = picks[0][1] if count == 1 else jnp.concatenate([p[1] for p in picks], axis=0)
        at = pl.multiple_of(slot * rows, rows)
        psel_ref[pl.ds(at, count * rows), :] = sel
        xs = _dot(sel, v_ref[...]).astype(BF16)
        hids = [(_silu(_dot(xs[k * rows:(k + 1) * rows], wg_ref[e])) * _dot(xs[k * rows:(k + 1) * rows], wu_ref[e]))
                .astype(BF16) for k, (e, _, _) in enumerate(picks)]
        outs = [(_dot(hid, wd_ref[e]) * w_row).astype(BF16) for hid, (e, _, w_row) in zip(hids, picks)]
        ysel_ref[pl.ds(at, count * rows), :] = outs[0] if count == 1 else jnp.concatenate(outs, axis=0)

        @pl.when(slot + count == MOE_SLOTS)
        def _():
            flush(MOE_SLOTS)

        return jnp.where(slot + count == MOE_SLOTS, 0, slot + count)

    def pair(it, slot):
        no_room = slot > MOE_SLOTS - 2

        @pl.when(no_room)
        def _():
            flush(slot)

        return run_items(2 * it, 2, jnp.where(no_room, 0, slot))

    slot_ref[0] = lax.fori_loop(0, total // 2, pair, slot_ref[0])

    @pl.when(total % 2 == 1)
    def _():
        slot_ref[0] = run_items(total - 1, 1, slot_ref[0])

    @pl.when(jnp.logical_and(j == pl.num_programs(1) - 1, slot_ref[0] > 0))
    def _():
        flush(slot_ref[0])


def _moe_call(g, v, gates, rank, counts, nsb, wg, wu, wd, sg, su, sd):
    t, d = v.shape
    tb = MOE_TOKENS
    ff = wg.shape[2]
    grid_spec = pltpu.PrefetchScalarGridSpec(
        num_scalar_prefetch=1,
        grid=(nsb, N_EXPERTS // MOE_PAIR),
        in_specs=[
            pl.BlockSpec((tb, d), lambda s, j, c: (s, 0)),
            pl.BlockSpec((N_EXPERTS, tb), lambda s, j, c: (0, s)),
            pl.BlockSpec((N_EXPERTS, tb), lambda s, j, c: (0, s)),
            pl.BlockSpec((MOE_PAIR, d, ff), lambda s, j, c: (j, 0, 0)),
            pl.BlockSpec((MOE_PAIR, d, ff), lambda s, j, c: (j, 0, 0)),
            pl.BlockSpec((MOE_PAIR, ff, d), lambda s, j, c: (j, 0, 0)),
            pl.BlockSpec((d, sg.shape[1]), lambda s, j, c: (0, 0)),
            pl.BlockSpec((d, su.shape[1]), lambda s, j, c: (0, 0)),
            pl.BlockSpec((sd.shape[0], d), lambda s, j, c: (0, 0)),
        ],
        out_specs=pl.BlockSpec((tb, d), lambda s, j, c: (s, 0)),
        scratch_shapes=[pltpu.VMEM((MOE_SLOTS * MOE_ROWS, tb), BF16), pltpu.VMEM((MOE_SLOTS * MOE_ROWS, d), BF16),
                        pltpu.SMEM((1,), jnp.int32)],
    )
    return pl.pallas_call(
        _moe_kernel, grid_spec=grid_spec, out_shape=jax.ShapeDtypeStruct((t, d), F32),
        compiler_params=_cparams(("parallel", "arbitrary")), name="moe",
    )(counts, v, gates, rank, wg.astype(BF16), wu.astype(BF16), wd.astype(BF16),
      sg.astype(BF16), su.astype(BF16), sd.astype(BF16))


def _rec_pre_kernel(h_ref, f_ref, pmod_ref, gprev_ref, mod_ref, gpre_ref, wtm_ref, wfm_ref, cqh_ref, sqh_ref, ckh_ref,
                    skh_ref, hn_ref, z_ref, xbc_ref, dt_ref, kd_ref, qd_ref, vd_ref):
    h = h_ref[...] + pmod_ref[5:6, :] * (_rms(f_ref[...], -1) * gprev_ref[...])
    hn_ref[...] = h
    u = _rms(h, -1) * gpre_ref[...]
    u = u * (1.0 + mod_ref[1:2, :]) + mod_ref[0:1, :]
    ub = u.astype(BF16)
    zk = _dot(ub, wtm_ref[...])
    zt = _dot_nt(wfm_ref[...], ub)
    z_ref[...] = zk[:, 0:512].astype(z_ref.dtype)
    xbc_ref[...] = zk[:, 512:1536]
    dt_ref[...] = zk[:, 1536:1664]
    ck, sk = ckh_ref[...], skh_ref[...]
    for j in range(DIFF_HEADS):
        lo = 1664 + 128 * j
        kd_ref[j] = (zk[:, lo:lo + 128] * ck + zk[:, lo + 512:lo + 640] * sk).astype(BF16)
    cq, sq = cqh_ref[...], sqh_ref[...]
    qs = DIFF_SCALE * LOG2E
    zero64 = jnp.zeros((64, h.shape[0]), BF16)
    for hd in range(2 * DIFF_HEADS):
        dq = zt[64 * hd:64 * hd + 64]
        q = ((dq * cq + _rot_rows(dq) * sq) * qs).astype(BF16)
        if hd % 2 == 0:
            qd_ref[hd, 0:64, :] = q
            qd_ref[hd, 64:128, :] = zero64
        else:
            qd_ref[hd, 0:64, :] = zero64
            qd_ref[hd, 64:128, :] = q
    for j in range(DIFF_HEADS):
        vd_ref[j] = zt[512 + 128 * j:640 + 128 * j].astype(BF16)


def _rec_pre_weights(g, rec_w_in):
    d = rec_w_in.shape[0]
    o = np.cumsum([0, SSD_INNER, SSD_XBC, SSD_HEADS, SSD_HEADS, 2 * DIFF_HEADS * DIFF_DIM, 2 * DIFF_HEADS * DIFF_DIM,
                   DIFF_HEADS * DIFF_V])
    w_z, w_xbc, w_dtf, w_dtb, w_dq, w_dk, w_dv = (rec_w_in[:, o[i]:o[i + 1]] for i in range(7))
    src64, sgn64 = _rot_map(DIFF_DIM)

    def rot_heads(wcols):
        wh = wcols.reshape(d, 2 * DIFF_HEADS, DIFF_DIM)
        return (wh[:, :, src64] * sgn64).reshape(d, -1)

    w_dt = jnp.pad(jnp.concatenate([w_dtf, w_dtb], axis=1), ((0, 0), (0, LANES - 2 * SSD_HEADS)))
    wtm = jnp.concatenate([w_z, w_xbc, w_dt, w_dk, rot_heads(w_dk)], axis=1)
    wfm = jnp.concatenate([w_dq, w_dv], axis=1)
    cos_h, sin_h = _rope_tables(g.n, DIFF_DIM, ROW_BLOCK)
    two = lambda x: jnp.concatenate([x, x], axis=1)
    return dict(wtm=wtm.astype(BF16), wfm=wfm.T.astype(BF16), cqh=cos_h.T, sqh=sin_h.T, ckh=two(cos_h), skh=two(sin_h))


def _rec_pre_call(g, h, f, prev_mod, prev_gain, mod, gpre, w):
    t, d = h.shape
    rb = ROW_BLOCK
    row = lambda i: (i, 0)
    modspec = pl.BlockSpec((None, 6, d), lambda i: (g.mod_row(i), 0, 0))
    tm_tab = pl.BlockSpec((rb, LANES), lambda i: (g.pos_block(i), 0))
    fm64 = pl.BlockSpec((64, rb), lambda i: (0, g.pos_block(i)))
    nd = 2 * DIFF_HEADS
    return pl.pallas_call(
        _rec_pre_kernel, grid=(g.n_blocks,),
        in_specs=[pl.BlockSpec((rb, d), row), pl.BlockSpec((rb, d), row), modspec, _full((1, d)), modspec,
                  _full(gpre.shape), _full(w["wtm"].shape), _full(w["wfm"].shape), fm64, fm64, tm_tab, tm_tab],
        out_specs=[
            pl.BlockSpec((rb, d), row), pl.BlockSpec((rb, SSD_INNER), row), pl.BlockSpec((rb, SSD_XBC), row), pl.BlockSpec((rb, LANES), row),
            pl.BlockSpec((DIFF_HEADS, rb, 128), lambda i: (0, i, 0)),
            pl.BlockSpec((nd, 128, rb), lambda i: (0, 0, i)),
            pl.BlockSpec((DIFF_HEADS, DIFF_V, rb), lambda i: (0, 0, i)),
        ],
        out_shape=[
            jax.ShapeDtypeStruct((t, d), F32),
            jax.ShapeDtypeStruct((t, SSD_INNER), BF16), jax.ShapeDtypeStruct((t, SSD_XBC), F32),
            jax.ShapeDtypeStruct((t, LANES), F32), jax.ShapeDtypeStruct((DIFF_HEADS, t, 128), BF16),
            jax.ShapeDtypeStruct((nd, 128, t), BF16), jax.ShapeDtypeStruct((DIFF_HEADS, DIFF_V, t), BF16),
        ],
        compiler_params=_cparams(("parallel",)), name="rec_pre",
    )(h, f, prev_mod, prev_gain.reshape(1, d), mod, gpre, w["wtm"], w["wfm"], w["cqh"], w["sqh"], w["ckh"], w["skh"])


def _conv_kernel(x_ref, prev_ref, next_ref, w_ref, b_ref, o_ref, *, seq_blocks, lat_blocks):
    i = pl.program_id(0)
    rb = x_ref.shape[0]
    pos = i % seq_blocks
    is_ctx = i >= lat_blocks
    first = jnp.logical_or(is_ctx, pos == 0)
    last = jnp.logical_or(is_ctx, pos == seq_blocks - 1)
    x = x_ref[...]
    prev_row = jnp.where(first, 0.0, prev_ref[7:8, :])
    next_row = jnp.where(last, 0.0, next_ref[0:1, :])
    rid = lax.broadcasted_iota(jnp.int32, x.shape, 0)
    x_prev = jnp.where(rid == 0, prev_row, pltpu.roll(x, 1, axis=0))
    x_next = jnp.where(rid == rb - 1, next_row, pltpu.roll(x, rb - 1, axis=0))
    y = w_ref[0:1, :] * x_prev + w_ref[1:2, :] * x + w_ref[2:3, :] * x_next + b_ref[...]
    o_ref[...] = _silu(y)


def _conv_call(g, xbc, conv_w, conv_b):
    t, ch = xbc.shape
    rb = g.c
    halo = 8
    per = rb // halo
    last_halo = t // halo - 1
    return pl.pallas_call(
        functools.partial(_conv_kernel, seq_blocks=g.n // rb, lat_blocks=g.b * g.n // rb),
        grid=(t // rb,),
        in_specs=[
            pl.BlockSpec((rb, ch), lambda i: (i, 0)),
            pl.BlockSpec((halo, ch), lambda i: (jnp.maximum(i * per - 1, 0), 0)),
            pl.BlockSpec((halo, ch), lambda i: (jnp.minimum((i + 1) * per, last_halo), 0)),
            _full((SSD_CONV, ch)), _full((1, ch)),
        ],
        out_specs=pl.BlockSpec((rb, ch), lambda i: (i, 0)),
        out_shape=jax.ShapeDtypeStruct((t, ch), F32),
        compiler_params=_cparams(("parallel",)), name="ssd_conv",
    )(xbc, xbc, xbc, conv_w, conv_b.reshape(1, ch))


def _ssd_kernel(xf_ref, dtf_ref, xb_ref, dtb_ref, bias_ref, alog_ref, tril_ref, triu_ref, eye_ref, yf_ref, yb_ref,
                sf_ref, sb_ref):
    @pl.when(pl.program_id(1) == 0)
    def _():
        sf_ref[...] = jnp.zeros_like(sf_ref)
        sb_ref[...] = jnp.zeros_like(sb_ref)

    _ssd_chunk(xf_ref, dtf_ref, bias_ref, alog_ref, tril_ref, triu_ref, eye_ref, yf_ref, sf_ref, 0, False)
    _ssd_chunk(xb_ref, dtb_ref, bias_ref, alog_ref, triu_ref, tril_ref, eye_ref, yb_ref, sb_ref, SSD_HEADS, True)


def _ssd_chunk(x_ref, dt_ref, bias_ref, alog_ref, tri_ref, trit_ref, eye_ref, y_ref, state_ref, lane_off, reverse):
    ln = SSD_CHUNK
    hi = lax.Precision.HIGHEST
    x = x_ref[:, 0:SSD_INNER]
    raw = dt_ref[...] + bias_ref[...]
    dt = jnp.maximum(raw, 0.0) + jnp.log1p(jnp.exp(-jnp.abs(raw)))
    adt = dt * (-jnp.exp(alog_ref[...]))
    acs = jnp.dot(tri_ref[...], adt, preferred_element_type=F32, precision=hi)
    acs_row = lax.dot_general(adt, trit_ref[...], (((0,), (0,)), ((), ())), preferred_element_type=F32, precision=hi)
    dt_row = lax.dot_general(dt, eye_ref[...], (((0,), (0,)), ((), ())), preferred_element_type=F32, precision=hi)
    tot = acs[0:1, :] if reverse else acs[ln - 1:ln, :]
    w_all = jnp.exp(tot - acs) * dt
    ea_all = jnp.exp(acs)
    etot = jnp.exp(tot)
    li = lax.broadcasted_iota(jnp.int32, (ln, ln), 0)
    si = lax.broadcasted_iota(jnp.int32, (ln, ln), 1)
    keep = (si >= li) if reverse else (si <= li)
    lane = lax.broadcasted_iota(jnp.int32, (ln, LANES), 1)
    left = lane < SSD_HEAD_DIM
    per_group = SSD_HEADS // SSD_GROUPS
    for gi in range(SSD_GROUPS):
        bm = x_ref[:, SSD_INNER + SSD_STATE * gi:SSD_INNER + SSD_STATE * (gi + 1)].astype(BF16)
        cm = x_ref[:, SSD_INNER + SSD_STATE * (SSD_GROUPS + gi):SSD_INNER + SSD_STATE * (SSD_GROUPS + gi + 1)]
        cm = cm.astype(BF16)
        cb = _dot_nt(cm, bm)
        for pr in range(per_group // 2):
            h0 = gi * per_group + 2 * pr
            xp = x[:, SSD_HEAD_DIM * h0:SSD_HEAD_DIM * (h0 + 2)]
            mats = []
            for hd in (h0, h0 + 1):
                c = lane_off + hd
                seg = acs[:, c:c + 1] - acs_row[c:c + 1, :]
                lmat = jnp.exp(jnp.where(keep, seg, -jnp.inf))
                mats.append((cb * lmat * dt_row[c:c + 1, :]).astype(BF16))
            xb = xp.astype(BF16)
            zero = jnp.zeros_like(xb)
            rhs = jnp.concatenate([jnp.where(left, xb, zero), jnp.where(left, zero, xb)], axis=0)
            y_diag = _dot(jnp.concatenate(mats, axis=1), rhs)
            st = state_ref[h0 // 2]
            c0 = lane_off + h0
            ea = jnp.where(left, ea_all[:, c0:c0 + 1], ea_all[:, c0 + 1:c0 + 2])
            y_off = _dot(cm, st.astype(BF16)) * ea
            y_ref[:, SSD_HEAD_DIM * h0:SSD_HEAD_DIM * (h0 + 2)] = y_diag + y_off
            wcol = jnp.where(left, w_all[:, c0:c0 + 1], w_all[:, c0 + 1:c0 + 2])
            cs = _dot_tn(bm, (xp * wcol).astype(BF16))
            dec = jnp.where(left[0:1, :], etot[:, c0:c0 + 1], etot[:, c0 + 1:c0 + 2])
            state_ref[h0 // 2] = st * dec + cs


def _ssd_call(g, xbc_act, dt, dt_bias_f, dt_bias_b, a_log_f, a_log_b):
    t = xbc_act.shape[0]
    ln = SSD_CHUNK
    cc = g.c // ln
    nl = g.n // ln
    ctx0 = g.b * g.n // ln
    fwd = lambda b, k: (jnp.where(k < cc, ctx0 + b * cc + k, b * nl + (k - cc)), 0)
    bwd = lambda b, k: (jnp.where(k < cc, ctx0 + b * cc + (cc - 1 - k), b * nl + (nl - 1 - (k - cc))), 0)
    idx = np.arange(ln)
    lower = (idx[:, None] >= idx[None, :]).astype(np.float32)
    lanes = lambda pf, pb: jnp.pad(jnp.concatenate([pf, pb]).reshape(1, -1), ((0, 0), (0, LANES - 2 * SSD_HEADS)))
    state = pltpu.VMEM((SSD_HEADS // 2, SSD_STATE, 2 * SSD_HEAD_DIM), F32)
    y_shape = jax.ShapeDtypeStruct((t, SSD_INNER), F32)
    return pl.pallas_call(
        _ssd_kernel, grid=(g.b, cc + nl),
        in_specs=[pl.BlockSpec((ln, SSD_XBC), fwd), pl.BlockSpec((ln, LANES), fwd),
                  pl.BlockSpec((ln, SSD_XBC), bwd), pl.BlockSpec((ln, LANES), bwd),
                  _full((1, LANES)), _full((1, LANES)), _full((ln, ln)), _full((ln, ln)), _full((ln, ln))],
        out_specs=[pl.BlockSpec((ln, SSD_INNER), fwd), pl.BlockSpec((ln, SSD_INNER), bwd)],
        out_shape=[y_shape, y_shape],
        scratch_shapes=[state, state],
        compiler_params=_cparams(("parallel", "arbitrary")),
        name="ssd_scan",
    )(xbc_act, dt, xbc_act, dt, lanes(dt_bias_f, dt_bias_b), lanes(a_log_f, a_log_b), jnp.asarray(lower),
      jnp.asarray(lower.T), jnp.eye(ln, dtype=F32))


def _diff_finish(o, tq, lq1_ref, lk1_ref, lq2_ref, lk2_ref, sub_ref, o_ref, lambda_init):
    lam = (jnp.exp(jnp.sum(lq1_ref[...] * lk1_ref[...], axis=1, keepdims=True))
           - jnp.exp(jnp.sum(lq2_ref[...] * lk2_ref[...], axis=1, keepdims=True)) + lambda_init)
    od = o[:, 0:tq] - lam * o[:, tq:2 * tq]
    o_ref[...] = ((_rms(od, 0) * sub_ref[...]) * (1.0 - lambda_init)).astype(o_ref.dtype)


def _diff_attn_kernel(qt_ref, *refs, lambda_init):
    o_ref, s_ref = refs[-2:]
    params = refs[-7:-2]
    kv = refs[:-7]
    half = len(kv) // 2
    segs = [(kv[i], kv[half + i]) for i in range(half)]
    qt = jnp.concatenate([qt_ref[0], qt_ref[1]], axis=1)
    o = _attend(qt, segs, s_ref, DIFF_V)
    _diff_finish(o, qt_ref.shape[2], *params, o_ref, lambda_init)


def _diff_attn_call(g, qd, kd, vd, lq1, lk1, lq2, lk2, subln, lambda_init):
    tq = DIFF_Q_TILE
    nq = g.n // tq
    ctx_blk = g.b * g.n // g.c
    vec = lambda a: a.reshape(1, -1)
    params = [_full((1, DIFF_DIM)), _full((1, DIFF_DIM)), _full((1, DIFF_DIM)), _full((1, DIFF_DIM)),
              _full((DIFF_V, 1))]
    pvals = (vec(lq1), vec(lk1), vec(lq2), vec(lk2), subln.reshape(-1, 1))
    klat = pl.BlockSpec((None, g.n, 128), lambda b, j, *_: (j, b, 0))
    kctx = pl.BlockSpec((None, g.c, 128), lambda b, j, *_: (j, ctx_blk + b, 0))
    vlat = pl.BlockSpec((None, DIFF_V, g.n), lambda b, j, *_: (j, 0, b))
    vctx = pl.BlockSpec((None, DIFF_V, g.c), lambda b, j, *_: (j, 0, ctx_blk + b))
    kern = functools.partial(_diff_attn_kernel, lambda_init=lambda_init)
    lat = pl.pallas_call(
        kern, grid=(g.b, DIFF_HEADS, nq),
        scratch_shapes=[pltpu.VMEM((g.n + g.c, 2 * tq), F32)],
        in_specs=[pl.BlockSpec((2, 128, tq), lambda b, j, qi: (j, 0, b * nq + qi)), klat, kctx, vlat, vctx] + params,
        out_specs=pl.BlockSpec((DIFF_V, tq), lambda b, j, qi: (j, b * nq + qi)),
        out_shape=jax.ShapeDtypeStruct((DIFF_HEADS * DIFF_V, g.b * g.n), BF16),
        compiler_params=_cparams(("parallel", "parallel", "arbitrary")),
        name="attn_diff",
    )(qd, kd, kd, vd, vd, *pvals)
    ctx = pl.pallas_call(
        kern, grid=(g.b, DIFF_HEADS),
        scratch_shapes=[pltpu.VMEM((g.c, 2 * g.c), F32)],
        in_specs=[pl.BlockSpec((2, 128, g.c), lambda b, j: (j, 0, ctx_blk + b)), kctx, vctx] + params,
        out_specs=pl.BlockSpec((DIFF_V, g.c), lambda b, j: (j, b)),
        out_shape=jax.ShapeDtypeStruct((DIFF_HEADS * DIFF_V, g.b * g.c), BF16),
        compiler_params=_cparams(("parallel", "parallel")),
        name="attn_diff_ctx",
    )(qd, kd, vd, *pvals)
    return lat, ctx


def _rec_post_kernel(yf_ref, yb_ref, xs_ref, z_ref, odl_ref, odc_ref, dskip_ref, gn_ref, wa_ref, wb_ref,
                     h_ref, mod_ref, gpost_ref, gffn_ref, rw_ref, hn_ref, v_ref, lg_ref, *, lat_blocks):
    y = (yf_ref[...] + yb_ref[...] + xs_ref[...] * dskip_ref[...]) * _silu(z_ref[...].astype(F32))
    half = SSD_INNER // SSD_GROUPS
    o = _dot_tn(_pick_part(lat_blocks, odl_ref, odc_ref), wb_ref[...])
    for gi in range(SSD_GROUPS):
        lo = half * gi
        yn = (_rms(y[:, lo:lo + half], -1) * gn_ref[:, lo:lo + half]).astype(BF16)
        o = o + _dot(yn, wa_ref[lo:lo + half, :])
    _post_tail(o, h_ref[...], mod_ref, gpost_ref, gffn_ref, rw_ref, hn_ref, v_ref, lg_ref)


def _rec_post_call(g, yf, yb, xbc_act, zs, od_parts, d_skip, ssd_norm, w_out, h, mod, gpost, gffn, router_w):
    d = h.shape[1]
    rb = ROW_BLOCK
    ins, outs, shapes = _post_specs(g, d)
    wo = w_out.astype(BF16)
    row512 = pl.BlockSpec((rb, SSD_INNER), lambda i: (i, 0))
    return pl.pallas_call(
        functools.partial(_rec_post_kernel, lat_blocks=g.lat_blocks), grid=(g.n_blocks,),
        in_specs=[row512, row512, row512, row512] + _part_specs(g, (SSD_INNER, rb), 1)
        + [_full((1, SSD_INNER)), _full((1, SSD_INNER)), _full((SSD_INNER, d)), _full((SSD_INNER, d)),
           pl.BlockSpec((rb, d), lambda i: (i, 0))] + ins,
        out_specs=outs, out_shape=shapes,
        compiler_params=_cparams(("parallel",)), name="rec_post",
    )(yf, yb, xbc_act, zs, *od_parts, jnp.repeat(d_skip, SSD_HEAD_DIM).reshape(1, -1), ssd_norm.reshape(1, -1),
      wo[:SSD_INNER], wo[SSD_INNER:], h, mod, gpost.reshape(1, d), gffn.reshape(1, d), router_w.T)


def _ffn_res_kernel(h_ref, f_ref, mod_ref, g_ref, o_ref):
    o_ref[...] = h_ref[...] + mod_ref[5:6, :] * (_rms(f_ref[...], -1) * g_ref[...])


def _ffn_res_call(g, h, f, mod, gain, n_rows):
    d = h.shape[1]
    rb = ROW_BLOCK
    row = pl.BlockSpec((rb, d), lambda i: (i, 0))
    return pl.pallas_call(
        _ffn_res_kernel, grid=(n_rows // rb,),
        in_specs=[row, row, pl.BlockSpec((None, 6, d), lambda i: (g.mod_row(i), 0, 0)), _full((1, d))],
        out_specs=row, out_shape=jax.ShapeDtypeStruct((n_rows, d), F32),
        compiler_params=_cparams(("parallel",)), name="ffn_res",
    )(h, f, mod, gain.reshape(1, d))


def kernel(x, c, ctx, c_ctx, ada_w, ada_b, norm_mix_pre, norm_mix_post, norm_ffn_pre, norm_ffn_post, mix_w_out, att_w_in, mla_q_norm, mla_w_q_up, mla_kv_norm, mla_w_kv_up, gqa_q_norm, gqa_k_norm, rec_w_in, ssd_conv_w, ssd_conv_b, ssd_dt_bias_f, ssd_dt_bias_b, ssd_a_log_f, ssd_a_log_b, ssd_d, ssd_norm, diff_lambda_q1, diff_lambda_k1, diff_lambda_q2, diff_lambda_k2, diff_subln, router_w, router_b, exp_w_gate, exp_w_up, exp_w_down, sh_w_gate, sh_w_up, sh_w_down):
    b, n, d = x.shape
    n_ctx = ctx.shape[1]
    depth = ada_w.shape[0]
    g = _Geom(b, n, n_ctx)
    mod_rows = -(-(b + 1) // 8) * 8
    c_all = jnp.concatenate([c, c_ctx[None, :], jnp.zeros((mod_rows - b - 1, d), F32)], axis=0)
    mods = _ada_call(c_all, ada_w, ada_b).reshape(depth, mod_rows, 6, d)
    h = None
    for i in range(depth):
        last = i == depth - 1
        jdx = i // 2
        mod = mods[i]
        if i % 2 == 0:
            w = _attn_pre_weights(g, att_w_in[jdx], mla_q_norm[jdx], mla_w_q_up[jdx], mla_kv_norm[jdx],
                                  mla_w_kv_up[jdx], gqa_q_norm[jdx], gqa_k_norm[jdx])
            h_parts = (x.reshape(b * n, d), ctx.reshape(b * n_ctx, d)) if i == 0 else (h[:b * n], h[b * n:])
            qm, km, vm, qg, kg, vg = _attn_pre_call(g, h_parts, mod, norm_mix_pre[i].reshape(1, d), w)
            oa = _attn_call(g, qm, km, vm, kv_heads=MLA_HEADS, shared_k=False, name="attn_mla")
            ob = _attn_call(g, qg, kg, vg, kv_heads=GQA_KV_HEADS, shared_k=True, name="attn_gqa")
            h, v, logits_t = _attn_post_call(g, oa, ob, mix_w_out[i], h_parts, mod, norm_mix_post[i], norm_ffn_pre[i],
                                             router_w[i])
        else:
            lambda_init = 0.8 - 0.6 * math.exp(-0.3 * i)
            w = _rec_pre_weights(g, rec_w_in[jdx])
            h, zs, xbc, dt, kd, qd, vd = _rec_pre_call(g, h, f, mods[i - 1], norm_ffn_post[i - 1], mod,
                                                       norm_mix_pre[i].reshape(1, d), w)
            xbc_act = _conv_call(g, xbc, ssd_conv_w[jdx], ssd_conv_b[jdx])
            yf, yb = _ssd_call(g, xbc_act, dt, ssd_dt_bias_f[jdx], ssd_dt_bias_b[jdx], ssd_a_log_f[jdx],
                               ssd_a_log_b[jdx])
            od = _diff_attn_call(g, qd, kd, vd, diff_lambda_q1[jdx], diff_lambda_k1[jdx], diff_lambda_q2[jdx],
                                 diff_lambda_k2[jdx], diff_subln[jdx], lambda_init)
            h, v, logits_t = _rec_post_call(g, yf, yb, xbc_act, zs, od, ssd_d[jdx], ssd_norm[jdx], mix_w_out[i], h,
                                            mod, norm_mix_post[i], norm_ffn_pre[i], router_w[i])
        nsb = -(-(b * n) // MOE_TOKENS) if last else g.t // MOE_TOKENS
        gates, rank, counts = _router_call(g, logits_t, router_b[i], nsb)
        f = _moe_call(g, v, gates, rank, counts, nsb, exp_w_gate[i], exp_w_up[i], exp_w_down[i],
                      sh_w_gate[i], sh_w_up[i], sh_w_down[i])
        if last or i % 2 == 1:
            h = _ffn_res_call(g, h, f, mod, norm_ffn_post[i], b * n if last else g.t)
    return h.reshape(b, n, d)
```

```python
import functools
import math

import numpy as np
import jax
import jax.numpy as jnp
from jax import lax
from jax.experimental import pallas as pl
from jax.experimental.pallas import tpu as pltpu

F32 = jnp.float32
BF16 = jnp.bfloat16
LOG2E = 1.4426950408889634

GRID_W = 64
ROPE_BASE = 10000.0
NORM_EPS = 1e-6

MLA_HEADS, MLA_Q_LORA, MLA_KV_LORA, MLA_NOPE, MLA_ROPE, MLA_V = 8, 256, 128, 64, 32, 64
MLA_SCALE = (MLA_NOPE + MLA_ROPE) ** -0.5
GQA_HEADS, GQA_KV_HEADS, GQA_DIM = 8, 2, 64
GQA_SCALE = GQA_DIM ** -0.5
SSD_HEADS, SSD_HEAD_DIM, SSD_GROUPS, SSD_STATE, SSD_CONV, SSD_CHUNK = 8, 64, 2, 128, 3, 128
SSD_INNER = SSD_HEADS * SSD_HEAD_DIM
SSD_XBC = SSD_INNER + 2 * SSD_GROUPS * SSD_STATE
DIFF_HEADS, DIFF_DIM = 4, 64
DIFF_V = 2 * DIFF_DIM
DIFF_SCALE = DIFF_DIM ** -0.5
N_EXPERTS, TOP_K, N_GROUPS, TOPK_GROUPS, EXPERT_FF, SHARED_FF = 64, 8, 8, 4, 256, 256
ROUTED_SCALE = 2.5

LANES = 128
VMEM_LIMIT_BYTES = 56 * 1024 * 1024

ROW_BLOCK = 512
ATTN_Q_TILE = 1024
DIFF_Q_TILE = 512
ATTN_KV_CHUNK = 512
ATTN_EXP_ROWS = 4096
MOE_TOKENS = 768
MOE_ROWS = 128
MOE_PAIR = 8
MOE_SLOTS = 8


def _cparams(sem):
    return pltpu.CompilerParams(dimension_semantics=sem, vmem_limit_bytes=VMEM_LIMIT_BYTES)


def _rms(x, axis):
    return x * lax.rsqrt(jnp.mean(x * x, axis=axis, keepdims=True) + NORM_EPS)


def _silu(x):
    return x * jax.nn.sigmoid(x)


def _dot(a, b):
    return jnp.dot(a, b, preferred_element_type=F32)


def _dot_nt(a, b):
    return lax.dot_general(a, b, (((1,), (1,)), ((), ())), preferred_element_type=F32)


def _dot_tn(a, b):
    return lax.dot_general(a, b, (((0,), (0,)), ((), ())), preferred_element_type=F32)


def _ada_kernel(c_ref, w_ref, b_ref, o_ref):
    s = _silu(c_ref[...])
    o_ref[...] = jnp.dot(s, w_ref[...], preferred_element_type=F32, precision=lax.Precision.HIGHEST) + b_ref[...]


def _ada_call(c_all, ada_w, ada_b):
    depth, d, six_d = ada_w.shape
    rows = c_all.shape[0]
    cols = six_d // 4
    return pl.pallas_call(
        _ada_kernel,
        grid=(depth, six_d // cols),
        in_specs=[
            pl.BlockSpec((rows, d), lambda i, j: (0, 0)),
            pl.BlockSpec((None, d, cols), lambda i, j: (i, 0, j)),
            pl.BlockSpec((None, 1, cols), lambda i, j: (i, 0, j)),
        ],
        out_specs=pl.BlockSpec((None, rows, cols), lambda i, j: (i, 0, j)),
        out_shape=jax.ShapeDtypeStruct((depth, rows, six_d), F32),
        compiler_params=_cparams(("arbitrary", "arbitrary")),
        name="ada_mod",
    )(c_all, ada_w, ada_b.reshape(depth, 1, six_d))


def _rot_rows(x):
    q = x.shape[0] // 4
    return jnp.concatenate([-x[q:2 * q], x[0:q], -x[3 * q:4 * q], x[2 * q:3 * q]], axis=0)


def _rot_lanes(x, dim):
    q = dim // 4
    lane = lax.broadcasted_iota(jnp.int32, x.shape, 1)
    first = (lane & q) == 0
    return jnp.where(first, -pltpu.roll(x, x.shape[1] - q, axis=1), pltpu.roll(x, q, axis=1))


def _attn_pre_kernel(hl_ref, hc_ref, mod_ref, gpre_ref, wfm_ref, wtm_ref, qn_ref, kvn_col_ref, kvn_row_ref, wq_ref,
                     wk_ref, wv_ref, gq_g_ref, gk_g_ref, gk_gr_ref,
                     cqm_ref, sqm_ref, cqh_ref, sqh_ref, ckm_ref, skm_ref, ckh_ref, skh_ref,
                     qm_ref, km_ref, vm_ref, qg_ref, kg_ref, vg_ref, *, lat_blocks):
    h = _pick_part(lat_blocks, hl_ref, hc_ref)
    u = _rms(h, -1) * gpre_ref[...]
    u = u * (1.0 + mod_ref[1:2, :]) + mod_ref[0:1, :]
    ub = u.astype(BF16)
    zt = _dot_nt(wfm_ref[...], ub)
    zk = _dot(ub, wtm_ref[...])

    qn = (_rms(zt[0:256], 0) * qn_ref[...]).astype(BF16)
    qt = _dot(wq_ref[...], qn)
    cq, sq = cqm_ref[...], sqm_ref[...]
    qs = MLA_SCALE * LOG2E
    zero32 = jnp.zeros((32, h.shape[0]), BF16)
    for hd in range(MLA_HEADS):
        pe_raw = qt[512 + 32 * hd:544 + 32 * hd]
        pe = pe_raw * cq + _rot_rows(pe_raw) * sq
        qm_ref[hd, 0:64, :] = (qt[64 * hd:64 * hd + 64] * qs).astype(BF16)
        qm_ref[hd, 64:96, :] = (pe * qs).astype(BF16)
        qm_ref[hd, 96:128, :] = zero32

    kvn_t = (_rms(zt[256:384], 0) * kvn_col_ref[...]).astype(BF16)
    vt = _dot(wv_ref[...], kvn_t)
    for hd in range(MLA_HEADS):
        vm_ref[hd] = vt[64 * hd:64 * hd + 64].astype(BF16)
    kvn = (_rms(zk[:, 0:128], -1) * kvn_row_ref[...]).astype(BF16)
    kn = _dot(kvn, wk_ref[...])
    kpe = zk[:, 128:256] * ckm_ref[...] + zk[:, 256:384] * skm_ref[...]
    for hd in range(MLA_HEADS):
        km_ref[hd] = (kn[:, 128 * hd:128 * hd + 128] + kpe).astype(BF16)

    cqh, sqh = cqh_ref[...], sqh_ref[...]
    gs = GQA_SCALE * LOG2E
    zero64 = jnp.zeros((64, h.shape[0]), BF16)
    grp = GQA_HEADS // GQA_KV_HEADS
    for hd in range(GQA_HEADS):
        raw = zt[384 + 64 * hd:448 + 64 * hd]
        qn_h = raw * lax.rsqrt(jnp.mean(raw * raw, axis=0, keepdims=True) + NORM_EPS) * gq_g_ref[...]
        q = ((qn_h * cqh + _rot_rows(qn_h) * sqh) * gs).astype(BF16)
        if hd // grp == 0:
            qg_ref[hd, 0:64, :] = q
            qg_ref[hd, 64:128, :] = zero64
        else:
            qg_ref[hd, 0:64, :] = zero64
            qg_ref[hd, 64:128, :] = q
    for kvh in range(GQA_KV_HEADS):
        vg_ref[kvh] = zt[896 + 64 * kvh:960 + 64 * kvh].astype(BF16)

    gk, gkr = zk[:, 384:512], zk[:, 512:640]
    lane = lax.broadcasted_iota(jnp.int32, gk.shape, 1)
    lo = lane < 64
    sq0 = jnp.sum(jnp.where(lo, gk * gk, 0.0), axis=-1, keepdims=True)
    sq1 = jnp.sum(jnp.where(lo, 0.0, gk * gk), axis=-1, keepdims=True)
    r = jnp.where(lo, lax.rsqrt(sq0 / GQA_DIM + NORM_EPS), lax.rsqrt(sq1 / GQA_DIM + NORM_EPS))
    kg_ref[...] = ((gk * r * gk_g_ref[...]) * ckh_ref[...] + (gkr * r * gk_gr_ref[...]) * skh_ref[...]).astype(BF16)


def _rot_map(dim):
    q = dim // 4
    j = np.arange(dim)
    even = (j // q) % 2 == 0
    return np.where(even, j + q, j - q), np.where(even, -1.0, 1.0).astype(np.float32)


def _rope_tables(n_tokens, dim, pad_rows):
    rows = n_tokens // GRID_W
    row = jnp.repeat(jnp.arange(rows, dtype=F32), GRID_W)
    col = jnp.tile(jnp.arange(GRID_W, dtype=F32), rows)
    half = dim // 2
    inv = ROPE_BASE ** (-(jnp.arange(half // 2, dtype=F32) * 2.0 / half))
    ang_r = row[:, None] * inv
    ang_c = col[:, None] * inv
    ang = jnp.concatenate([ang_r, ang_r, ang_c, ang_c], axis=-1)
    cos = jnp.concatenate([jnp.cos(ang), jnp.ones((pad_rows, dim), F32)], axis=0)
    sin = jnp.concatenate([jnp.sin(ang), jnp.zeros((pad_rows, dim), F32)], axis=0)
    return cos, sin


class _Geom:
    def __init__(self, b, n, c):
        assert n % ROW_BLOCK == 0 and (b * c) % ROW_BLOCK == 0 and n % c == 0 and n % GRID_W == 0
        self.b, self.n, self.c = b, n, c
        self.t = b * n + b * c
        self.lat_blocks = b * n // ROW_BLOCK
        self.blocks_per_seq = n // ROW_BLOCK
        self.n_blocks = self.t // ROW_BLOCK
        assert self.t % MOE_TOKENS == 0

    def mod_row(self, i):
        return jnp.where(i < self.lat_blocks, i // self.blocks_per_seq, self.b)

    def pos_block(self, i):
        return jnp.where(i < self.lat_blocks, i % self.blocks_per_seq, self.blocks_per_seq)


def _full(shape):
    nd = len(shape)
    return pl.BlockSpec(shape, lambda *_: (0,) * nd)


def _part_specs(g, block, axis):
    def spec(latent):
        def index(i):
            j = jnp.minimum(i, g.lat_blocks - 1) if latent else jnp.maximum(i - g.lat_blocks, 0)
            return tuple(j if a == axis else 0 for a in range(len(block)))
        return pl.BlockSpec(block, index)
    return [spec(True), spec(False)]


def _pick_part(lat_blocks, lat_ref, ctx_ref):
    return jnp.where(pl.program_id(0) < lat_blocks, lat_ref[...], ctx_ref[...])


def _attn_pre_call(g, h_parts, mod, gpre, w):
    d = h_parts[0].shape[1]
    t = g.t
    rb = ROW_BLOCK
    row = lambda i: (i, 0)
    tm_tab = pl.BlockSpec((rb, LANES), lambda i: (g.pos_block(i), 0))
    fm32 = pl.BlockSpec((32, rb), lambda i: (0, g.pos_block(i)))
    fm64 = pl.BlockSpec((64, rb), lambda i: (0, g.pos_block(i)))
    in_specs = _part_specs(g, (rb, d), 0) + [
        pl.BlockSpec((None, 6, d), lambda i: (g.mod_row(i), 0, 0)),
        _full(gpre.shape), _full(w["wfm"].shape), _full(w["wtm"].shape), _full(w["qn"].shape),
        _full(w["kvn_col"].shape), _full(w["kvn_row"].shape), _full(w["wq"].shape), _full(w["wk"].shape),
        _full(w["wv"].shape), _full(w["gq_g"].shape), _full(w["gk_g"].shape), _full(w["gk_gr"].shape),
        fm32, fm32, fm64, fm64, tm_tab, tm_tab, tm_tab, tm_tab,
    ]
    out_shape = [
        jax.ShapeDtypeStruct((MLA_HEADS, 128, t), BF16),
        jax.ShapeDtypeStruct((MLA_HEADS, t, 128), BF16),
        jax.ShapeDtypeStruct((MLA_HEADS, MLA_V, t), BF16),
        jax.ShapeDtypeStruct((GQA_HEADS, 128, t), BF16),
        jax.ShapeDtypeStruct((t, 128), BF16),
        jax.ShapeDtypeStruct((GQA_KV_HEADS, GQA_DIM, t), BF16),
    ]
    out_specs = [
        pl.BlockSpec((MLA_HEADS, 128, rb), lambda i: (0, 0, i)),
        pl.BlockSpec((MLA_HEADS, rb, 128), lambda i: (0, i, 0)),
        pl.BlockSpec((MLA_HEADS, MLA_V, rb), lambda i: (0, 0, i)),
        pl.BlockSpec((GQA_HEADS, 128, rb), lambda i: (0, 0, i)),
        pl.BlockSpec((rb, 128), row),
        pl.BlockSpec((GQA_KV_HEADS, GQA_DIM, rb), lambda i: (0, 0, i)),
    ]
    return pl.pallas_call(
        functools.partial(_attn_pre_kernel, lat_blocks=g.lat_blocks),
        grid=(g.n_blocks,), in_specs=in_specs, out_specs=out_specs, out_shape=out_shape,
        compiler_params=_cparams(("parallel",)), name="attn_pre",
    )(*h_parts, mod, gpre, w["wfm"], w["wtm"], w["qn"], w["kvn_col"], w["kvn_row"], w["wq"], w["wk"], w["wv"],
      w["gq_g"], w["gk_g"], w["gk_gr"],
      w["cqm"], w["sqm"], w["cqh"], w["sqh"], w["ckm"], w["skm"], w["ckh"], w["skh"])


def _attn_pre_weights(g, att_w_in, q_norm, w_q_up, kv_norm, w_kv_up, gqa_qn, gqa_kn):
    d = att_w_in.shape[0]
    o = np.cumsum([0, MLA_Q_LORA, MLA_KV_LORA, MLA_ROPE, GQA_HEADS * GQA_DIM, GQA_KV_HEADS * GQA_DIM,
                   GQA_KV_HEADS * GQA_DIM])
    w_qlat, w_kvlat, w_kpe, w_gq, w_gk, w_gv = (att_w_in[:, o[i]:o[i + 1]] for i in range(6))
    src32, sgn32 = _rot_map(MLA_ROPE)
    src64, sgn64 = _rot_map(GQA_DIM)

    def rot_heads(wcols, heads, dim, src, sgn):
        wh = wcols.reshape(d, heads, dim)
        return (wh[:, :, src] * sgn).reshape(d, heads * dim)

    wfm = jnp.concatenate([w_qlat, w_kvlat, w_gq, w_gv], axis=1)
    zpad = lambda x, lo, hi: jnp.pad(x, ((0, 0), (lo, hi)))
    wtm = jnp.concatenate([
        w_kvlat, zpad(w_kpe, 64, 32), zpad(w_kpe[:, src32] * sgn32, 64, 32),
        w_gk, rot_heads(w_gk, GQA_KV_HEADS, GQA_DIM, src64, sgn64)], axis=1)
    wq = w_q_up.reshape(MLA_Q_LORA, MLA_HEADS, MLA_NOPE + MLA_ROPE)
    wq_all = jnp.concatenate([wq[:, :, :MLA_NOPE].reshape(MLA_Q_LORA, -1),
                              wq[:, :, MLA_NOPE:].reshape(MLA_Q_LORA, -1)], axis=1)
    wkv = w_kv_up.reshape(MLA_KV_LORA, MLA_HEADS, MLA_NOPE + MLA_V)
    wk = jnp.pad(wkv[:, :, :MLA_NOPE], ((0, 0), (0, 0), (0, 128 - MLA_NOPE))).reshape(MLA_KV_LORA, -1)
    wv = wkv[:, :, MLA_NOPE:].reshape(MLA_KV_LORA, -1)
    cos_m, sin_m = _rope_tables(g.n, MLA_ROPE, ROW_BLOCK)
    cos_h, sin_h = _rope_tables(g.n, GQA_DIM, ROW_BLOCK)
    two = lambda x: jnp.concatenate([x, x], axis=1)
    return dict(
        wfm=wfm.T.astype(BF16), wtm=wtm.astype(BF16),
        qn=q_norm.reshape(-1, 1), kvn_col=kv_norm.reshape(-1, 1), kvn_row=kv_norm.reshape(1, -1),
        wq=wq_all.T.astype(BF16), wk=wk.astype(BF16), wv=wv.T.astype(BF16),
        gq_g=gqa_qn.reshape(-1, 1),
        gk_g=two(gqa_kn.reshape(1, -1)), gk_gr=two(gqa_kn[src64].reshape(1, -1)),
        cqm=cos_m.T, sqm=sin_m.T, cqh=cos_h.T, sqh=sin_h.T,
        ckm=zpad(cos_m, 64, 32), skm=zpad(sin_m, 64, 32), ckh=two(cos_h), skh=two(sin_h),
    )


def _fold8(x, op):
    r, w = x.shape
    return op(x.reshape(r // 8, 8, w), axis=0)


def _attend(qt, segments, s_ref, dv):
    tq = qt.shape[1]
    pieces, off = [], 0
    for k_ref, vt_ref in segments:
        n = k_ref.shape[0]
        for lo in range(0, n, ATTN_KV_CHUNK):
            rows = min(ATTN_KV_CHUNK, n - lo)
            pieces.append((off, lo, rows, k_ref, vt_ref))
            off += rows
    m8 = None
    for so, lo, rows, k_ref, _ in pieces:
        s = _dot(k_ref[lo:lo + rows, :], qt)
        s_ref[so:so + rows, :] = s
        part = _fold8(s, jnp.max)
        m8 = part if m8 is None else jnp.maximum(m8, part)
    m = jnp.max(m8, axis=0, keepdims=True)
    acc = jnp.zeros((dv + 16, tq), F32)
    off = 0
    for _, vt_ref in segments:
        n = vt_ref.shape[1]
        step = min(n, ATTN_EXP_ROWS)

        def body(i, acc, off=off, vt_ref=vt_ref, step=step):
            base = i * step if isinstance(i, int) else pl.multiple_of(i * step, step)
            for lo in range(0, step, ATTN_KV_CHUNK):
                rows = min(ATTN_KV_CHUNK, step - lo)
                p = jnp.exp2((s_ref[pl.ds(off + base + lo, rows), :] - m).astype(BF16))
                vt1 = jnp.concatenate([vt_ref[:, pl.ds(base + lo, rows)], jnp.ones((16, rows), BF16)], axis=0)
                acc = acc + _dot(vt1, p)
            return acc

        acc = body(0, acc) if n == step else lax.fori_loop(0, n // step, body, acc)
        off += n
    return acc[0:dv] / acc[dv:dv + 1]


def _attn_kernel(qt_ref, *refs, dv):
    o_ref, s_ref = refs[-2:]
    kv = refs[:-2]
    half = len(kv) // 2
    segs = [(kv[i], kv[half + i]) for i in range(half)]
    o_ref[...] = _attend(qt_ref[...], segs, s_ref, dv).astype(o_ref.dtype)


def _attn_call(g, qt, k, vt, *, kv_heads, shared_k, name):
    heads = qt.shape[0]
    dv = vt.shape[1]
    grp = heads // kv_heads
    tq = ATTN_Q_TILE
    nq = g.n // tq
    ctx_blk = g.b * g.n // g.c
    if shared_k:
        klat = pl.BlockSpec((g.n, 128), lambda b, h, *_: (b, 0))
        kctx = pl.BlockSpec((g.c, 128), lambda b, h, *_: (ctx_blk + b, 0))
    else:
        klat = pl.BlockSpec((None, g.n, 128), lambda b, h, *_: (h // grp, b, 0))
        kctx = pl.BlockSpec((None, g.c, 128), lambda b, h, *_: (h // grp, ctx_blk + b, 0))
    vlat = pl.BlockSpec((None, dv, g.n), lambda b, h, *_: (h // grp, 0, b))
    vctx = pl.BlockSpec((None, dv, g.c), lambda b, h, *_: (h // grp, 0, ctx_blk + b))
    lat = pl.pallas_call(
        functools.partial(_attn_kernel, dv=dv),
        grid=(g.b, heads, nq),
        scratch_shapes=[pltpu.VMEM((g.n + g.c, tq), F32)],
        in_specs=[pl.BlockSpec((None, 128, tq), lambda b, h, qi: (h, 0, b * nq + qi)), klat, kctx, vlat, vctx],
        out_specs=pl.BlockSpec((dv, tq), lambda b, h, qi: (h, b * nq + qi)),
        out_shape=jax.ShapeDtypeStruct((heads * dv, g.b * g.n), BF16),
        compiler_params=_cparams(("parallel", "parallel", "arbitrary")),
        name=name,
    )(qt, k, k, vt, vt)
    ctx = pl.pallas_call(
        functools.partial(_attn_kernel, dv=dv),
        grid=(g.b, heads),
        scratch_shapes=[pltpu.VMEM((g.c, g.c), F32)],
        in_specs=[pl.BlockSpec((None, 128, g.c), lambda b, h: (h, 0, ctx_blk + b)), kctx, vctx],
        out_specs=pl.BlockSpec((dv, g.c), lambda b, h: (h, b)),
        out_shape=jax.ShapeDtypeStruct((heads * dv, g.b * g.c), BF16),
        compiler_params=_cparams(("parallel", "parallel")),
        name=name + "_ctx",
    )(qt, k, vt)
    return lat, ctx


def _split_bf16(x):
    hi = x.astype(BF16)
    return hi, (x - hi.astype(F32)).astype(BF16)


def _post_tail(o, h, mod_ref, gpost_ref, gffn_ref, rw_ref, hn_ref, v_ref, lg_ref):
    hn = h + mod_ref[2:3, :] * (_rms(o, -1) * gpost_ref[...])
    hn_ref[...] = hn
    v = (_rms(hn, -1) * gffn_ref[...]) * (1.0 + mod_ref[4:5, :]) + mod_ref[3:4, :]
    v_hi, v_lo = _split_bf16(v)
    v_ref[...] = v_hi
    w_hi, w_lo = _split_bf16(rw_ref[...])
    lg_ref[...] = _dot_nt(w_hi, v_hi) + (_dot_nt(w_hi, v_lo) + _dot_nt(w_lo, v_hi))


def _attn_post_kernel(oal_ref, oac_ref, obl_ref, obc_ref, wa_ref, wb_ref, hl_ref, hc_ref, mod_ref, gpost_ref, gffn_ref,
                      rw_ref, hn_ref, v_ref, lg_ref, *, lat_blocks):
    oa = _pick_part(lat_blocks, oal_ref, oac_ref)
    ob = _pick_part(lat_blocks, obl_ref, obc_ref)
    o = _dot_tn(oa, wa_ref[...]) + _dot_tn(ob, wb_ref[...])
    h = _pick_part(lat_blocks, hl_ref, hc_ref)
    _post_tail(o, h, mod_ref, gpost_ref, gffn_ref, rw_ref, hn_ref, v_ref, lg_ref)


def _post_specs(g, d):
    rb = ROW_BLOCK
    ins = [
        pl.BlockSpec((None, 6, d), lambda i: (g.mod_row(i), 0, 0)),
        _full((1, d)), _full((1, d)), _full((N_EXPERTS, d)),
    ]
    outs = [pl.BlockSpec((rb, d), lambda i: (i, 0)), pl.BlockSpec((rb, d), lambda i: (i, 0)),
            pl.BlockSpec((N_EXPERTS, rb), lambda i: (0, i))]
    shapes = [jax.ShapeDtypeStruct((g.t, d), F32), jax.ShapeDtypeStruct((g.t, d), BF16),
              jax.ShapeDtypeStruct((N_EXPERTS, g.t), F32)]
    return ins, outs, shapes


def _attn_post_call(g, oa_parts, ob_parts, w_out, h_parts, mod, gpost, gffn, router_w):
    d = h_parts[0].shape[1]
    rb = ROW_BLOCK
    half = oa_parts[0].shape[0]
    ins, outs, shapes = _post_specs(g, d)
    wo = w_out.astype(BF16)
    o_specs = _part_specs(g, (half, rb), 1)
    return pl.pallas_call(
        functools.partial(_attn_post_kernel, lat_blocks=g.lat_blocks), grid=(g.n_blocks,),
        in_specs=o_specs + o_specs + [_full((half, d)), _full((half, d))] + _part_specs(g, (rb, d), 0) + ins,
        out_specs=outs, out_shape=shapes,
        compiler_params=_cparams(("parallel",)), name="attn_post",
    )(*oa_parts, *ob_parts, wo[:half], wo[half:], *h_parts, mod, gpost.reshape(1, d), gffn.reshape(1, d), router_w.T)


def _router_kernel(lg_ref, bias_ref, tri_ref, gate_ref, rank_ref, cnt_ref):
    tb = lg_ref.shape[1]
    per = N_EXPERTS // N_GROUPS
    shp = (N_GROUPS, per, tb)
    scores = jax.nn.sigmoid(lg_ref[...])
    s3 = scores.reshape(shp)
    sel = (scores + bias_ref[...]).reshape(shp)
    sub = lax.broadcasted_iota(jnp.int32, shp, 1)
    grp = lax.broadcasted_iota(jnp.int32, shp, 0)
    neg = -jnp.inf
    m1 = jnp.max(sel, axis=1, keepdims=True)
    i1 = jnp.min(jnp.where(sel == m1, sub, per), axis=1, keepdims=True)
    m2 = jnp.max(jnp.where(sub == i1, neg, sel), axis=1, keepdims=True)
    cur = jnp.broadcast_to(m1 + m2, shp)
    gmask = jnp.zeros(shp, F32)
    for _ in range(TOPK_GROUPS):
        gm = jnp.max(cur, axis=0, keepdims=True)
        gi = jnp.min(jnp.where(cur == gm, grp, N_GROUPS), axis=0, keepdims=True)
        pick = grp == gi
        gmask = jnp.where(pick, 1.0, gmask)
        cur = jnp.where(pick, neg, cur)
    masked = jnp.where(gmask > 0.0, sel, neg)
    eidx = grp * per + sub
    chosen = jnp.zeros(shp, F32)
    for _ in range(TOP_K):
        mx = jnp.max(jnp.max(masked, axis=0, keepdims=True), axis=1, keepdims=True)
        cand = jnp.where(masked == mx, eidx, N_EXPERTS)
        ei = jnp.min(jnp.min(cand, axis=0, keepdims=True), axis=1, keepdims=True)
        pick = eidx == ei
        chosen = jnp.where(pick, 1.0, chosen)
        masked = jnp.where(pick, neg, masked)
    top_w = jnp.where(chosen > 0.0, s3, 0.0)
    denom = jnp.sum(jnp.sum(top_w, axis=0, keepdims=True), axis=1, keepdims=True)
    gate_ref[...] = (top_w / denom * ROUTED_SCALE).reshape(N_EXPERTS, tb)
    ch2 = chosen.reshape(N_EXPERTS, tb)
    chb = ch2.astype(BF16)
    before = _dot(chb, tri_ref[...])
    rank_ref[...] = jnp.where(ch2 > 0.0, before, -1.0).astype(jnp.int32)
    cnt_ref[...] = _dot(chb, jnp.ones((tb, LANES), BF16))


def _router_call(g, logits_t, router_b, nsb):
    tb = MOE_TOKENS
    tri = (np.arange(tb)[:, None] < np.arange(tb)[None, :]).astype(np.float32)
    blk = pl.BlockSpec((N_EXPERTS, tb), lambda s: (0, s))
    gates, rank, cnt = pl.pallas_call(
        _router_kernel, grid=(nsb,),
        in_specs=[blk, _full((N_EXPERTS, 1)), _full((tb, tb))],
        out_specs=[blk, blk, pl.BlockSpec((None, N_EXPERTS, LANES), lambda s: (s, 0, 0))],
        out_shape=[jax.ShapeDtypeStruct((N_EXPERTS, g.t), F32), jax.ShapeDtypeStruct((N_EXPERTS, g.t), jnp.int32),
                   jax.ShapeDtypeStruct((nsb, N_EXPERTS, LANES), F32)],
        compiler_params=_cparams(("parallel",)), name="router",
    )(logits_t, router_b.reshape(-1, 1), jnp.asarray(tri, BF16))
    return gates, rank, cnt[:, :, 0].astype(jnp.int32).reshape(-1)


def _moe_kernel(cnt_ref, v_ref, gate_ref, rank_ref, wg_ref, wu_ref, wd_ref, sg_ref, su_ref, sd_ref, o_ref,
                psel_ref, ysel_ref, slot_ref):
    s, j = pl.program_id(0), pl.program_id(1)
    tb = v_ref.shape[0]
    rows = MOE_ROWS

    @pl.when(j == 0)
    def _():
        vb = v_ref[...]
        hid = (_silu(_dot(vb, sg_ref[...])) * _dot(vb, su_ref[...])).astype(BF16)
        o_ref[...] = _dot(hid, sd_ref[...])
        psel_ref[...] = jnp.zeros_like(psel_ref)
        ysel_ref[...] = jnp.zeros_like(ysel_ref)
        slot_ref[0] = 0

    row_id = lax.broadcasted_iota(jnp.int32, (rows, tb), 0)

    def flush(n_slots):
        stage_row = lax.broadcasted_iota(jnp.int32, psel_ref.shape, 0)
        sel = jnp.where(stage_row < n_slots * rows, psel_ref[...], jnp.zeros_like(psel_ref))
        o_ref[...] += _dot_tn(sel, ysel_ref[...])

    ends, total = [], 0
    for e in range(MOE_PAIR):
        total = total + (cnt_ref[s * N_EXPERTS + j * MOE_PAIR + e] + rows - 1) // rows
        ends.append(total)

    def select(i):
        e = sum((i >= end).astype(jnp.int32) for end in ends[:-1])
        first = sum(jnp.where(e == k + 1, ends[k], 0) for k in range(MOE_PAIR - 1))
        ex = j * MOE_PAIR + e
        hit = row_id == rank_ref[pl.ds(ex, 1), :] - (i - first) * rows
        onehot = jnp.where(hit, 1.0, 0.0).astype(BF16)
        w_row = jnp.sum(jnp.where(hit, gate_ref[pl.ds(ex, 1), :], 0.0), axis=1, keepdims=True)
        return e, onehot, w_row

    def run_items(i0, count, slot):
        picks = [select(i0 + k) for k in range(count)]
        sel = picks[0][1] if count == 1 else jnp.concatenate([p[1] for p in picks], axis=0)
        at = pl.multiple_of(slot * rows, rows)
        psel_ref[pl.ds(at, count * rows), :] = sel
        xs = _dot(sel, v_ref[...]).astype(BF16)
        hids = [(_silu(_dot(xs[k * rows:(k + 1) * rows], wg_ref[e])) * _dot(xs[k * rows:(k + 1) * rows], wu_ref[e]))
                .astype(BF16) for k, (e, _, _) in enumerate(picks)]
        outs = [(_dot(hid, wd_ref[e]) * w_row).astype(BF16) for hid, (e, _, w_row) in zip(hids, picks)]
        ysel_ref[pl.ds(at, count * rows), :] = outs[0] if count == 1 else jnp.concatenate(outs, axis=0)

        @pl.when(slot + count == MOE_SLOTS)
        def _():
            flush(MOE_SLOTS)

        return jnp.where(slot + count == MOE_SLOTS, 0, slot + count)

    def pair(it, slot):
        no_room = slot > MOE_SLOTS - 2

        @pl.when(no_room)
        def _():
            flush(slot)

        return run_items(2 * it, 2, jnp.where(no_room, 0, slot))

    slot_ref[0] = lax.fori_loop(0, total // 2, pair, slot_ref[0])

    @pl.when(total % 2 == 1)
    def _():
        slot_ref[0] = run_items(total - 1, 1, slot_ref[0])

    @pl.when(jnp.logical_and(j == pl.num_programs(1) - 1, slot_ref[0] > 0))
    def _():
        flush(slot_ref[0])


def _moe_call(g, v, gates, rank, counts, nsb, wg, wu, wd, sg, su, sd):
    t, d = v.shape
    tb = MOE_TOKENS
    ff = wg.shape[2]
    grid_spec = pltpu.PrefetchScalarGridSpec(
        num_scalar_prefetch=1,
        grid=(nsb, N_EXPERTS // MOE_PAIR),
        in_specs=[
            pl.BlockSpec((tb, d), lambda s, j, c: (s, 0)),
            pl.BlockSpec((N_EXPERTS, tb), lambda s, j, c: (0, s)),
            pl.BlockSpec((N_EXPERTS, tb), lambda s, j, c: (0, s)),
            pl.BlockSpec((MOE_PAIR, d, ff), lambda s, j, c: (j, 0, 0)),
            pl.BlockSpec((MOE_PAIR, d, ff), lambda s, j, c: (j, 0, 0)),
            pl.BlockSpec((MOE_PAIR, ff, d), lambda s, j, c: (j, 0, 0)),
            pl.BlockSpec((d, sg.shape[1]), lambda s, j, c: (0, 0)),
            pl.BlockSpec((d, su.shape[1]), lambda s, j, c: (0, 0)),
            pl.BlockSpec((sd.shape[0], d), lambda s, j, c: (0, 0)),
        ],
        out_specs=pl.BlockSpec((tb, d), lambda s, j, c: (s, 0)),
        scratch_shapes=[pltpu.VMEM((MOE_SLOTS * MOE_ROWS, tb), BF16), pltpu.VMEM((MOE_SLOTS * MOE_ROWS, d), BF16),
                        pltpu.SMEM((1,), jnp.int32)],
    )
    return pl.pallas_call(
        _moe_kernel, grid_spec=grid_spec, out_shape=jax.ShapeDtypeStruct((t, d), F32),
        compiler_params=_cparams(("parallel", "arbitrary")), name="moe",
    )(counts, v, gates, rank, wg.astype(BF16), wu.astype(BF16), wd.astype(BF16),
      sg.astype(BF16), su.astype(BF16), sd.astype(BF16))


def _rec_pre_kernel(h_ref, f_ref, pmod_ref, gprev_ref, mod_ref, gpre_ref, wtm_ref, wfm_ref, cqh_ref, sqh_ref, ckh_ref,
                    skh_ref, hn_ref, z_ref, xbc_ref, dt_ref, kd_ref, qd_ref, vd_ref):
    h = h_ref[...] + pmod_ref[5:6, :] * (_rms(f_ref[...], -1) * gprev_ref[...])
    hn_ref[...] = h
    u = _rms(h, -1) * gpre_ref[...]
    u = u * (1.0 + mod_ref[1:2, :]) + mod_ref[0:1, :]
    ub = u.astype(BF16)
    zk = _dot(ub, wtm_ref[...])
    zt = _dot_nt(wfm_ref[...], ub)
    z_ref[...] = zk[:, 0:512].astype(z_ref.dtype)
    xbc_ref[...] = zk[:, 512:1536]
    dt_ref[...] = zk[:, 1536:1664]
    ck, sk = ckh_ref[...], skh_ref[...]
    dk = zk[:, 1664:2176]
    dk_rot = _rot_lanes(dk, DIFF_DIM)
    for j in range(DIFF_HEADS):
        lo = 128 * j
        kd_ref[j] = (dk[:, lo:lo + 128] * ck + dk_rot[:, lo:lo + 128] * sk).astype(BF16)
    cq, sq = cqh_ref[...], sqh_ref[...]
    qs = DIFF_SCALE * LOG2E
    zero64 = jnp.zeros((64, h.shape[0]), BF16)
    for hd in range(2 * DIFF_HEADS):
        dq = zt[64 * hd:64 * hd + 64]
        q = ((dq * cq + _rot_rows(dq) * sq) * qs).astype(BF16)
        if hd % 2 == 0:
            qd_ref[hd, 0:64, :] = q
            qd_ref[hd, 64:128, :] = zero64
        else:
            qd_ref[hd, 0:64, :] = zero64
            qd_ref[hd, 64:128, :] = q
    for j in range(DIFF_HEADS):
        vd_ref[j] = zt[512 + 128 * j:640 + 128 * j].astype(BF16)


def _rec_pre_weights(g, rec_w_in):
    o = np.cumsum([0, SSD_INNER, SSD_XBC, SSD_HEADS, SSD_HEADS, 2 * DIFF_HEADS * DIFF_DIM, 2 * DIFF_HEADS * DIFF_DIM,
                   DIFF_HEADS * DIFF_V])
    w_z, w_xbc, w_dtf, w_dtb, w_dq, w_dk, w_dv = (rec_w_in[:, o[i]:o[i + 1]] for i in range(7))
    w_dt = jnp.pad(jnp.concatenate([w_dtf, w_dtb], axis=1), ((0, 0), (0, LANES - 2 * SSD_HEADS)))
    wtm = jnp.concatenate([w_z, w_xbc, w_dt, w_dk], axis=1)
    wfm = jnp.concatenate([w_dq, w_dv], axis=1)
    cos_h, sin_h = _rope_tables(g.n, DIFF_DIM, ROW_BLOCK)
    two = lambda x: jnp.concatenate([x, x], axis=1)
    return dict(wtm=wtm.astype(BF16), wfm=wfm.T.astype(BF16), cqh=cos_h.T, sqh=sin_h.T, ckh=two(cos_h), skh=two(sin_h))


def _rec_pre_call(g, h, f, prev_mod, prev_gain, mod, gpre, w):
    t, d = h.shape
    rb = ROW_BLOCK
    row = lambda i: (i, 0)
    modspec = pl.BlockSpec((None, 6, d), lambda i: (g.mod_row(i), 0, 0))
    tm_tab = pl.BlockSpec((rb, LANES), lambda i: (g.pos_block(i), 0))
    fm64 = pl.BlockSpec((64, rb), lambda i: (0, g.pos_block(i)))
    nd = 2 * DIFF_HEADS
    return pl.pallas_call(
        _rec_pre_kernel, grid=(g.n_blocks,),
        in_specs=[pl.BlockSpec((rb, d), row), pl.BlockSpec((rb, d), row), modspec, _full((1, d)), modspec,
                  _full(gpre.shape), _full(w["wtm"].shape), _full(w["wfm"].shape), fm64, fm64, tm_tab, tm_tab],
        out_specs=[
            pl.BlockSpec((rb, d), row), pl.BlockSpec((rb, SSD_INNER), row), pl.BlockSpec((rb, SSD_XBC), row), pl.BlockSpec((rb, LANES), row),
            pl.BlockSpec((DIFF_HEADS, rb, 128), lambda i: (0, i, 0)),
            pl.BlockSpec((nd, 128, rb), lambda i: (0, 0, i)),
            pl.BlockSpec((DIFF_HEADS, DIFF_V, rb), lambda i: (0, 0, i)),
        ],
        out_shape=[
            jax.ShapeDtypeStruct((t, d), F32),
            jax.ShapeDtypeStruct((t, SSD_INNER), BF16), jax.ShapeDtypeStruct((t, SSD_XBC), F32),
            jax.ShapeDtypeStruct((t, LANES), F32), jax.ShapeDtypeStruct((DIFF_HEADS, t, 128), BF16),
            jax.ShapeDtypeStruct((nd, 128, t), BF16), jax.ShapeDtypeStruct((DIFF_HEADS, DIFF_V, t), BF16),
        ],
        compiler_params=_cparams(("parallel",)), name="rec_pre",
    )(h, f, prev_mod, prev_gain.reshape(1, d), mod, gpre, w["wtm"], w["wfm"], w["cqh"], w["sqh"], w["ckh"], w["skh"])


def _conv_kernel(x_ref, prev_ref, next_ref, w_ref, b_ref, o_ref, *, seq_blocks, lat_blocks):
    i = pl.program_id(0)
    rb = x_ref.shape[0]
    pos = i % seq_blocks
    is_ctx = i >= lat_blocks
    first = jnp.logical_or(is_ctx, pos == 0)
    last = jnp.logical_or(is_ctx, pos == seq_blocks - 1)
    x = x_ref[...]
    prev_row = jnp.where(first, 0.0, prev_ref[7:8, :])
    next_row = jnp.where(last, 0.0, next_ref[0:1, :])
    rid = lax.broadcasted_iota(jnp.int32, x.shape, 0)
    x_prev = jnp.where(rid == 0, prev_row, pltpu.roll(x, 1, axis=0))
    x_next = jnp.where(rid == rb - 1, next_row, pltpu.roll(x, rb - 1, axis=0))
    y = w_ref[0:1, :] * x_prev + w_ref[1:2, :] * x + w_ref[2:3, :] * x_next + b_ref[...]
    o_ref[...] = _silu(y)


def _conv_call(g, xbc, conv_w, conv_b):
    t, ch = xbc.shape
    rb = g.c
    halo = 8
    per = rb // halo
    last_halo = t // halo - 1
    return pl.pallas_call(
        functools.partial(_conv_kernel, seq_blocks=g.n // rb, lat_blocks=g.b * g.n // rb),
        grid=(t // rb,),
        in_specs=[
            pl.BlockSpec((rb, ch), lambda i: (i, 0)),
            pl.BlockSpec((halo, ch), lambda i: (jnp.maximum(i * per - 1, 0), 0)),
            pl.BlockSpec((halo, ch), lambda i: (jnp.minimum((i + 1) * per, last_halo), 0)),
            _full((SSD_CONV, ch)), _full((1, ch)),
        ],
        out_specs=pl.BlockSpec((rb, ch), lambda i: (i, 0)),
        out_shape=jax.ShapeDtypeStruct((t, ch), F32),
        compiler_params=_cparams(("parallel",)), name="ssd_conv",
    )(xbc, xbc, xbc, conv_w, conv_b.reshape(1, ch))


def _ssd_kernel(xf_ref, dtf_ref, xb_ref, dtb_ref, bias_ref, alog_ref, tril_ref, triu_ref, eye_ref, yf_ref, yb_ref,
                sf_ref, sb_ref):
    @pl.when(pl.program_id(1) == 0)
    def _():
        sf_ref[...] = jnp.zeros_like(sf_ref)
        sb_ref[...] = jnp.zeros_like(sb_ref)

    _ssd_chunk(xf_ref, dtf_ref, bias_ref, alog_ref, tril_ref, triu_ref, eye_ref, yf_ref, sf_ref, 0, False)
    _ssd_chunk(xb_ref, dtb_ref, bias_ref, alog_ref, triu_ref, tril_ref, eye_ref, yb_ref, sb_ref, SSD_HEADS, True)


def _ssd_chunk(x_ref, dt_ref, bias_ref, alog_ref, tri_ref, trit_ref, eye_ref, y_ref, state_ref, lane_off, reverse):
    ln = SSD_CHUNK
    hi = lax.Precision.HIGHEST
    x = x_ref[:, 0:SSD_INNER]
    raw = dt_ref[...] + bias_ref[...]
    dt = jnp.maximum(raw, 0.0) + jnp.log1p(jnp.exp(-jnp.abs(raw)))
    adt = dt * (-jnp.exp(alog_ref[...]))
    acs = jnp.dot(tri_ref[...], adt, preferred_element_type=F32, precision=hi)
    acs_row = lax.dot_general(adt, trit_ref[...], (((0,), (0,)), ((), ())), preferred_element_type=F32, precision=hi)
    dt_row = lax.dot_general(dt, eye_ref[...], (((0,), (0,)), ((), ())), preferred_element_type=F32, precision=hi)
    tot = acs[0:1, :] if reverse else acs[ln - 1:ln, :]
    w_all = jnp.exp(tot - acs) * dt
    ea_all = jnp.exp(acs)
    etot = jnp.exp(tot)
    li = lax.broadcasted_iota(jnp.int32, (ln, ln), 0)
    si = lax.broadcasted_iota(jnp.int32, (ln, ln), 1)
    keep = (si >= li) if reverse else (si <= li)
    lane = lax.broadcasted_iota(jnp.int32, (ln, LANES), 1)
    left = lane < SSD_HEAD_DIM
    per_group = SSD_HEADS // SSD_GROUPS
    for gi in range(SSD_GROUPS):
        bm = x_ref[:, SSD_INNER + SSD_STATE * gi:SSD_INNER + SSD_STATE * (gi + 1)].astype(BF16)
        cm = x_ref[:, SSD_INNER + SSD_STATE * (SSD_GROUPS + gi):SSD_INNER + SSD_STATE * (SSD_GROUPS + gi + 1)]
        cm = cm.astype(BF16)
        cb = _dot_nt(cm, bm)
        for pr in range(per_group // 2):
            h0 = gi * per_group + 2 * pr
            xp = x[:, SSD_HEAD_DIM * h0:SSD_HEAD_DIM * (h0 + 2)]
            mats = []
            for hd in (h0, h0 + 1):
                c = lane_off + hd
                seg = acs[:, c:c + 1] - acs_row[c:c + 1, :]
                lmat = jnp.exp(jnp.where(keep, seg, -jnp.inf))
                mats.append((cb * lmat * dt_row[c:c + 1, :]).astype(BF16))
            xb = xp.astype(BF16)
            zero = jnp.zeros_like(xb)
            rhs = jnp.concatenate([jnp.where(left, xb, zero), jnp.where(left, zero, xb)], axis=0)
            y_diag = _dot(jnp.concatenate(mats, axis=1), rhs)
            st = state_ref[h0 // 2]
            c0 = lane_off + h0
            ea = jnp.where(left, ea_all[:, c0:c0 + 1], ea_all[:, c0 + 1:c0 + 2])
            y_off = _dot(cm, st.astype(BF16)) * ea
            y_ref[:, SSD_HEAD_DIM * h0:SSD_HEAD_DIM * (h0 + 2)] = y_diag + y_off
            wcol = jnp.where(left, w_all[:, c0:c0 + 1], w_all[:, c0 + 1:c0 + 2])
            cs = _dot_tn(bm, (xp * wcol).astype(BF16))
            dec = jnp.where(left[0:1, :], etot[:, c0:c0 + 1], etot[:, c0 + 1:c0 + 2])
            state_ref[h0 // 2] = st * dec + cs


def _ssd_call(g, xbc_act, dt, dt_bias_f, dt_bias_b, a_log_f, a_log_b):
    t = xbc_act.shape[0]
    ln = SSD_CHUNK
    cc = g.c // ln
    nl = g.n // ln
    ctx0 = g.b * g.n // ln
    fwd = lambda b, k: (jnp.where(k < cc, ctx0 + b * cc + k, b * nl + (k - cc)), 0)
    bwd = lambda b, k: (jnp.where(k < cc, ctx0 + b * cc + (cc - 1 - k), b * nl + (nl - 1 - (k - cc))), 0)
    idx = np.arange(ln)
    lower = (idx[:, None] >= idx[None, :]).astype(np.float32)
    lanes = lambda pf, pb: jnp.pad(jnp.concatenate([pf, pb]).reshape(1, -1), ((0, 0), (0, LANES - 2 * SSD_HEADS)))
    state = pltpu.VMEM((SSD_HEADS // 2, SSD_STATE, 2 * SSD_HEAD_DIM), F32)
    y_shape = jax.ShapeDtypeStruct((t, SSD_INNER), F32)
    return pl.pallas_call(
        _ssd_kernel, grid=(g.b, cc + nl),
        in_specs=[pl.BlockSpec((ln, SSD_XBC), fwd), pl.BlockSpec((ln, LANES), fwd),
                  pl.BlockSpec((ln, SSD_XBC), bwd), pl.BlockSpec((ln, LANES), bwd),
                  _full((1, LANES)), _full((1, LANES)), _full((ln, ln)), _full((ln, ln)), _full((ln, ln))],
        out_specs=[pl.BlockSpec((ln, SSD_INNER), fwd), pl.BlockSpec((ln, SSD_INNER), bwd)],
        out_shape=[y_shape, y_shape],
        scratch_shapes=[state, state],
        compiler_params=_cparams(("parallel", "arbitrary")),
        name="ssd_scan",
    )(xbc_act, dt, xbc_act, dt, lanes(dt_bias_f, dt_bias_b), lanes(a_log_f, a_log_b), jnp.asarray(lower),
      jnp.asarray(lower.T), jnp.eye(ln, dtype=F32))


def _diff_finish(o, tq, lq1_ref, lk1_ref, lq2_ref, lk2_ref, sub_ref, o_ref, lambda_init):
    lam = (jnp.exp(jnp.sum(lq1_ref[...] * lk1_ref[...], axis=1, keepdims=True))
           - jnp.exp(jnp.sum(lq2_ref[...] * lk2_ref[...], axis=1, keepdims=True)) + lambda_init)
    od = o[:, 0:tq] - lam * o[:, tq:2 * tq]
    o_ref[...] = ((_rms(od, 0) * sub_ref[...]) * (1.0 - lambda_init)).astype(o_ref.dtype)


def _diff_attn_kernel(qt_ref, *refs, lambda_init):
    o_ref, s_ref = refs[-2:]
    params = refs[-7:-2]
    kv = refs[:-7]
    half = len(kv) // 2
    segs = [(kv[i], kv[half + i]) for i in range(half)]
    qt = jnp.concatenate([qt_ref[0], qt_ref[1]], axis=1)
    o = _attend(qt, segs, s_ref, DIFF_V)
    _diff_finish(o, qt_ref.shape[2], *params, o_ref, lambda_init)


def _diff_attn_call(g, qd, kd, vd, lq1, lk1, lq2, lk2, subln, lambda_init):
    tq = DIFF_Q_TILE
    nq = g.n // tq
    ctx_blk = g.b * g.n // g.c
    vec = lambda a: a.reshape(1, -1)
    params = [_full((1, DIFF_DIM)), _full((1, DIFF_DIM)), _full((1, DIFF_DIM)), _full((1, DIFF_DIM)),
              _full((DIFF_V, 1))]
    pvals = (vec(lq1), vec(lk1), vec(lq2), vec(lk2), subln.reshape(-1, 1))
    klat = pl.BlockSpec((None, g.n, 128), lambda b, j, *_: (j, b, 0))
    kctx = pl.BlockSpec((None, g.c, 128), lambda b, j, *_: (j, ctx_blk + b, 0))
    vlat = pl.BlockSpec((None, DIFF_V, g.n), lambda b, j, *_: (j, 0, b))
    vctx = pl.BlockSpec((None, DIFF_V, g.c), lambda b, j, *_: (j, 0, ctx_blk + b))
    kern = functools.partial(_diff_attn_kernel, lambda_init=lambda_init)
    lat = pl.pallas_call(
        kern, grid=(g.b, DIFF_HEADS, nq),
        scratch_shapes=[pltpu.VMEM((g.n + g.c, 2 * tq), F32)],
        in_specs=[pl.BlockSpec((2, 128, tq), lambda b, j, qi: (j, 0, b * nq + qi)), klat, kctx, vlat, vctx] + params,
        out_specs=pl.BlockSpec((DIFF_V, tq), lambda b, j, qi: (j, b * nq + qi)),
        out_shape=jax.ShapeDtypeStruct((DIFF_HEADS * DIFF_V, g.b * g.n), BF16),
        compiler_params=_cparams(("parallel", "parallel", "arbitrary")),
        name="attn_diff",
    )(qd, kd, kd, vd, vd, *pvals)
    ctx = pl.pallas_call(
        kern, grid=(g.b, DIFF_HEADS),
        scratch_shapes=[pltpu.VMEM((g.c, 2 * g.c), F32)],
        in_specs=[pl.BlockSpec((2, 128, g.c), lambda b, j: (j, 0, ctx_blk + b)), kctx, vctx] + params,
        out_specs=pl.BlockSpec((DIFF_V, g.c), lambda b, j: (j, b)),
        out_shape=jax.ShapeDtypeStruct((DIFF_HEADS * DIFF_V, g.b * g.c), BF16),
        compiler_params=_cparams(("parallel", "parallel")),
        name="attn_diff_ctx",
    )(qd, kd, vd, *pvals)
    return lat, ctx


def _rec_post_kernel(yf_ref, yb_ref, xs_ref, z_ref, odl_ref, odc_ref, dskip_ref, gn_ref, wa_ref, wb_ref,
                     h_ref, mod_ref, gpost_ref, gffn_ref, rw_ref, hn_ref, v_ref, lg_ref, *, lat_blocks):
    y = (yf_ref[...] + yb_ref[...] + xs_ref[...] * dskip_ref[...]) * _silu(z_ref[...].astype(F32))
    half = SSD_INNER // SSD_GROUPS
    o = _dot_tn(_pick_part(lat_blocks, odl_ref, odc_ref), wb_ref[...])
    for gi in range(SSD_GROUPS):
        lo = half * gi
        yn = (_rms(y[:, lo:lo + half], -1) * gn_ref[:, lo:lo + half]).astype(BF16)
        o = o + _dot(yn, wa_ref[lo:lo + half, :])
    _post_tail(o, h_ref[...], mod_ref, gpost_ref, gffn_ref, rw_ref, hn_ref, v_ref, lg_ref)


def _rec_post_call(g, yf, yb, xbc_act, zs, od_parts, d_skip, ssd_norm, w_out, h, mod, gpost, gffn, router_w):
    d = h.shape[1]
    rb = ROW_BLOCK
    ins, outs, shapes = _post_specs(g, d)
    wo = w_out.astype(BF16)
    row512 = pl.BlockSpec((rb, SSD_INNER), lambda i: (i, 0))
    return pl.pallas_call(
        functools.partial(_rec_post_kernel, lat_blocks=g.lat_blocks), grid=(g.n_blocks,),
        in_specs=[row512, row512, row512, row512] + _part_specs(g, (SSD_INNER, rb), 1)
        + [_full((1, SSD_INNER)), _full((1, SSD_INNER)), _full((SSD_INNER, d)), _full((SSD_INNER, d)),
           pl.BlockSpec((rb, d), lambda i: (i, 0))] + ins,
        out_specs=outs, out_shape=shapes,
        compiler_params=_cparams(("parallel",)), name="rec_post",
    )(yf, yb, xbc_act, zs, *od_parts, jnp.repeat(d_skip, SSD_HEAD_DIM).reshape(1, -1), ssd_norm.reshape(1, -1),
      wo[:SSD_INNER], wo[SSD_INNER:], h, mod, gpost.reshape(1, d), gffn.reshape(1, d), router_w.T)


def _ffn_res_kernel(h_ref, f_ref, mod_ref, g_ref, o_ref):
    o_ref[...] = h_ref[...] + mod_ref[5:6, :] * (_rms(f_ref[...], -1) * g_ref[...])


def _ffn_res_call(g, h, f, mod, gain, n_rows):
    d = h.shape[1]
    rb = ROW_BLOCK
    row = pl.BlockSpec((rb, d), lambda i: (i, 0))
    return pl.pallas_call(
        _ffn_res_kernel, grid=(n_rows // rb,),
        in_specs=[row, row, pl.BlockSpec((None, 6, d), lambda i: (g.mod_row(i), 0, 0)), _full((1, d))],
        out_specs=row, out_shape=jax.ShapeDtypeStruct((n_rows, d), F32),
        compiler_params=_cparams(("parallel",)), name="ffn_res",
    )(h, f, mod, gain.reshape(1, d))


def kernel(x, c, ctx, c_ctx, ada_w, ada_b, norm_mix_pre, norm_mix_post, norm_ffn_pre, norm_ffn_post, mix_w_out, att_w_in, mla_q_norm, mla_w_q_up, mla_kv_norm, mla_w_kv_up, gqa_q_norm, gqa_k_norm, rec_w_in, ssd_conv_w, ssd_conv_b, ssd_dt_bias_f, ssd_dt_bias_b, ssd_a_log_f, ssd_a_log_b, ssd_d, ssd_norm, diff_lambda_q1, diff_lambda_k1, diff_lambda_q2, diff_lambda_k2, diff_subln, router_w, router_b, exp_w_gate, exp_w_up, exp_w_down, sh_w_gate, sh_w_up, sh_w_down):
    b, n, d = x.shape
    n_ctx = ctx.shape[1]
    depth = ada_w.shape[0]
    g = _Geom(b, n, n_ctx)
    mod_rows = -(-(b + 1) // 8) * 8
    c_all = jnp.concatenate([c, c_ctx[None, :], jnp.zeros((mod_rows - b - 1, d), F32)], axis=0)
    mods = _ada_call(c_all, ada_w, ada_b).reshape(depth, mod_rows, 6, d)
    h = None
    for i in range(depth):
        last = i == depth - 1
        jdx = i // 2
        mod = mods[i]
        if i % 2 == 0:
            w = _attn_pre_weights(g, att_w_in[jdx], mla_q_norm[jdx], mla_w_q_up[jdx], mla_kv_norm[jdx],
                                  mla_w_kv_up[jdx], gqa_q_norm[jdx], gqa_k_norm[jdx])
            h_parts = (x.reshape(b * n, d), ctx.reshape(b * n_ctx, d)) if i == 0 else (h[:b * n], h[b * n:])
            qm, km, vm, qg, kg, vg = _attn_pre_call(g, h_parts, mod, norm_mix_pre[i].reshape(1, d), w)
            oa = _attn_call(g, qm, km, vm, kv_heads=MLA_HEADS, shared_k=False, name="attn_mla")
            ob = _attn_call(g, qg, kg, vg, kv_heads=GQA_KV_HEADS, shared_k=True, name="attn_gqa")
            h, v, logits_t = _attn_post_call(g, oa, ob, mix_w_out[i], h_parts, mod, norm_mix_post[i], norm_ffn_pre[i],
                                             router_w[i])
        else:
            lambda_init = 0.8 - 0.6 * math.exp(-0.3 * i)
            w = _rec_pre_weights(g, rec_w_in[jdx])
            h, zs, xbc, dt, kd, qd, vd = _rec_pre_call(g, h, f, mods[i - 1], norm_ffn_post[i - 1], mod,
                                                       norm_mix_pre[i].reshape(1, d), w)
            xbc_act = _conv_call(g, xbc, ssd_conv_w[jdx], ssd_conv_b[jdx])
            yf, yb = _ssd_call(g, xbc_act, dt, ssd_dt_bias_f[jdx], ssd_dt_bias_b[jdx], ssd_a_log_f[jdx],
                               ssd_a_log_b[jdx])
            od = _diff_attn_call(g, qd, kd, vd, diff_lambda_q1[jdx], diff_lambda_k1[jdx], diff_lambda_q2[jdx],
                                 diff_lambda_k2[jdx], diff_subln[jdx], lambda_init)
            h, v, logits_t = _rec_post_call(g, yf, yb, xbc_act, zs, od, ssd_d[jdx], ssd_norm[jdx], mix_w_out[i], h,
                                            mod, norm_mix_post[i], norm_ffn_pre[i], router_w[i])
        nsb = -(-(b * n) // MOE_TOKENS) if last else g.t // MOE_TOKENS
        gates, rank, counts = _router_call(g, logits_t, router_b[i], nsb)
        f = _moe_call(g, v, gates, rank, counts, nsb, exp_w_gate[i], exp_w_up[i], exp_w_down[i],
                      sh_w_gate[i], sh_w_up[i], sh_w_down[i])
        if last or i % 2 == 1:
            h = _ffn_res_call(g, h, f, mod, norm_ffn_post[i], b * n if last else g.t)
    return h.reshape(b, n, d)
```

```python
import functools
import math

import numpy as np
import jax
import jax.numpy as jnp
from jax import lax
from jax.experimental import pallas as pl
from jax.experimental.pallas import tpu as pltpu

F32 = jnp.float32
BF16 = jnp.bfloat16
LOG2E = 1.4426950408889634

GRID_W = 64
ROPE_BASE = 10000.0
NORM_EPS = 1e-6

MLA_HEADS, MLA_Q_LORA, MLA_KV_LORA, MLA_NOPE, MLA_ROPE, MLA_V = 8, 256, 128, 64, 32, 64
MLA_SCALE = (MLA_NOPE + MLA_ROPE) ** -0.5
GQA_HEADS, GQA_KV_HEADS, GQA_DIM = 8, 2, 64
GQA_SCALE = GQA_DIM ** -0.5
SSD_HEADS, SSD_HEAD_DIM, SSD_GROUPS, SSD_STATE, SSD_CONV, SSD_CHUNK = 8, 64, 2, 128, 3, 128
SSD_INNER = SSD_HEADS * SSD_HEAD_DIM
SSD_XBC = SSD_INNER + 2 * SSD_GROUPS * SSD_STATE
DIFF_HEADS, DIFF_DIM = 4, 64
DIFF_V = 2 * DIFF_DIM
DIFF_SCALE = DIFF_DIM ** -0.5
N_EXPERTS, TOP_K, N_GROUPS, TOPK_GROUPS, EXPERT_FF, SHARED_FF = 64, 8, 8, 4, 256, 256
ROUTED_SCALE = 2.5

LANES = 128
VMEM_LIMIT_BYTES = 56 * 1024 * 1024

ROW_BLOCK = 512
ATTN_Q_TILE = 1024
DIFF_Q_TILE = 512
ATTN_KV_CHUNK = 512
ATTN_EXP_ROWS = 4096
MOE_TOKENS = 768
MOE_ROWS = 128
MOE_PAIR = 8
MOE_SLOTS = 8


def _cparams(sem):
    return pltpu.CompilerParams(dimension_semantics=sem, vmem_limit_bytes=VMEM_LIMIT_BYTES)


def _rms(x, axis):
    return x * lax.rsqrt(jnp.mean(x * x, axis=axis, keepdims=True) + NORM_EPS)


def _silu(x):
    return x * jax.nn.sigmoid(x)


def _dot(a, b):
    return jnp.dot(a, b, preferred_element_type=F32)


def _dot_nt(a, b):
    return lax.dot_general(a, b, (((1,), (1,)), ((), ())), preferred_element_type=F32)


def _dot_tn(a, b):
    return lax.dot_general(a, b, (((0,), (0,)), ((), ())), preferred_element_type=F32)


def _ada_kernel(c_ref, w_ref, b_ref, o_ref):
    s = _silu(c_ref[...])
    o_ref[...] = jnp.dot(s, w_ref[...], preferred_element_type=F32, precision=lax.Precision.HIGHEST) + b_ref[...]


def _ada_call(c_all, ada_w, ada_b):
    depth, d, six_d = ada_w.shape
    rows = c_all.shape[0]
    cols = six_d // 4
    return pl.pallas_call(
        _ada_kernel,
        grid=(depth, six_d // cols),
        in_specs=[
            pl.BlockSpec((rows, d), lambda i, j: (0, 0)),
            pl.BlockSpec((None, d, cols), lambda i, j: (i, 0, j)),
            pl.BlockSpec((None, 1, cols), lambda i, j: (i, 0, j)),
        ],
        out_specs=pl.BlockSpec((None, rows, cols), lambda i, j: (i, 0, j)),
        out_shape=jax.ShapeDtypeStruct((depth, rows, six_d), F32),
        compiler_params=_cparams(("arbitrary", "arbitrary")),
        name="ada_mod",
    )(c_all, ada_w, ada_b.reshape(depth, 1, six_d))


def _rot_rows(x):
    q = x.shape[0] // 4
    return jnp.concatenate([-x[q:2 * q], x[0:q], -x[3 * q:4 * q], x[2 * q:3 * q]], axis=0)


def _rot_lanes(x, dim):
    q = dim // 4
    lane = lax.broadcasted_iota(jnp.int32, x.shape, 1)
    first = (lane & q) == 0
    return jnp.where(first, -pltpu.roll(x, x.shape[1] - q, axis=1), pltpu.roll(x, q, axis=1))


def _attn_pre_kernel(hl_ref, hc_ref, mod_ref, gpre_ref, wfm_ref, wtm_ref, qn_ref, kvn_col_ref, kvn_row_ref, wq_ref,
                     wk_ref, wv_ref, gq_g_ref, gk_g_ref, gk_gr_ref,
                     cqm_ref, sqm_ref, cqh_ref, sqh_ref, ckm_ref, skm_ref, ckh_ref, skh_ref,
                     qm_ref, km_ref, vm_ref, qg_ref, kg_ref, vg_ref, *, lat_blocks):
    h = _pick_part(lat_blocks, hl_ref, hc_ref)
    u = _rms(h, -1) * gpre_ref[...]
    u = u * (1.0 + mod_ref[1:2, :]) + mod_ref[0:1, :]
    ub = u.astype(BF16)
    zt = _dot_nt(wfm_ref[...], ub)
    zk = _dot(ub, wtm_ref[...])

    qn = (_rms(zt[0:256], 0) * qn_ref[...]).astype(BF16)
    qt = _dot(wq_ref[...], qn)
    cq, sq = cqm_ref[...], sqm_ref[...]
    qs = MLA_SCALE * LOG2E
    zero32 = jnp.zeros((32, h.shape[0]), BF16)
    for hd in range(MLA_HEADS):
        pe_raw = qt[512 + 32 * hd:544 + 32 * hd]
        pe = pe_raw * cq + _rot_rows(pe_raw) * sq
        qm_ref[hd, 0:64, :] = (qt[64 * hd:64 * hd + 64] * qs).astype(BF16)
        qm_ref[hd, 64:96, :] = (pe * qs).astype(BF16)
        qm_ref[hd, 96:128, :] = zero32

    kvn_t = (_rms(zt[256:384], 0) * kvn_col_ref[...]).astype(BF16)
    vt = _dot(wv_ref[...], kvn_t)
    for hd in range(MLA_HEADS):
        vm_ref[hd] = vt[64 * hd:64 * hd + 64].astype(BF16)
    kvn = (_rms(zk[:, 0:128], -1) * kvn_row_ref[...]).astype(BF16)
    kn = _dot(kvn, wk_ref[...])
    kpe = zk[:, 128:256] * ckm_ref[...] + zk[:, 256:384] * skm_ref[...]
    for hd in range(MLA_HEADS):
        km_ref[hd] = (kn[:, 128 * hd:128 * hd + 128] + kpe).astype(BF16)

    cqh, sqh = cqh_ref[...], sqh_ref[...]
    gs = GQA_SCALE * LOG2E
    zero64 = jnp.zeros((64, h.shape[0]), BF16)
    grp = GQA_HEADS // GQA_KV_HEADS
    for hd in range(GQA_HEADS):
        raw = zt[384 + 64 * hd:448 + 64 * hd]
        qn_h = raw * lax.rsqrt(jnp.mean(raw * raw, axis=0, keepdims=True) + NORM_EPS) * gq_g_ref[...]
        q = ((qn_h * cqh + _rot_rows(qn_h) * sqh) * gs).astype(BF16)
        if hd // grp == 0:
            qg_ref[hd, 0:64, :] = q
            qg_ref[hd, 64:128, :] = zero64
        else:
            qg_ref[hd, 0:64, :] = zero64
            qg_ref[hd, 64:128, :] = q
    for kvh in range(GQA_KV_HEADS):
        vg_ref[kvh] = zt[896 + 64 * kvh:960 + 64 * kvh].astype(BF16)

    gk, gkr = zk[:, 384:512], zk[:, 512:640]
    lane = lax.broadcasted_iota(jnp.int32, gk.shape, 1)
    lo = lane < 64
    sq0 = jnp.sum(jnp.where(lo, gk * gk, 0.0), axis=-1, keepdims=True)
    sq1 = jnp.sum(jnp.where(lo, 0.0, gk * gk), axis=-1, keepdims=True)
    r = jnp.where(lo, lax.rsqrt(sq0 / GQA_DIM + NORM_EPS), lax.rsqrt(sq1 / GQA_DIM + NORM_EPS))
    kg_ref[...] = ((gk * r * gk_g_ref[...]) * ckh_ref[...] + (gkr * r * gk_gr_ref[...]) * skh_ref[...]).astype(BF16)


def _rot_map(dim):
    q = dim // 4
    j = np.arange(dim)
    even = (j // q) % 2 == 0
    return np.where(even, j + q, j - q), np.where(even, -1.0, 1.0).astype(np.float32)


def _rope_tables(n_tokens, dim, pad_rows):
    rows = n_tokens // GRID_W
    row = jnp.repeat(jnp.arange(rows, dtype=F32), GRID_W)
    col = jnp.tile(jnp.arange(GRID_W, dtype=F32), rows)
    half = dim // 2
    inv = ROPE_BASE ** (-(jnp.arange(half // 2, dtype=F32) * 2.0 / half))
    ang_r = row[:, None] * inv
    ang_c = col[:, None] * inv
    ang = jnp.concatenate([ang_r, ang_r, ang_c, ang_c], axis=-1)
    cos = jnp.concatenate([jnp.cos(ang), jnp.ones((pad_rows, dim), F32)], axis=0)
    sin = jnp.concatenate([jnp.sin(ang), jnp.zeros((pad_rows, dim), F32)], axis=0)
    return cos, sin


class _Geom:
    def __init__(self, b, n, c):
        assert n % ROW_BLOCK == 0 and (b * c) % ROW_BLOCK == 0 and n % c == 0 and n % GRID_W == 0
        self.b, self.n, self.c = b, n, c
        self.t = b * n + b * c
        self.lat_blocks = b * n // ROW_BLOCK
        self.blocks_per_seq = n // ROW_BLOCK
        self.n_blocks = self.t // ROW_BLOCK
        assert self.t % MOE_TOKENS == 0

    def mod_row(self, i):
        return jnp.where(i < self.lat_blocks, i // self.blocks_per_seq, self.b)

    def pos_block(self, i):
        return jnp.where(i < self.lat_blocks, i % self.blocks_per_seq, self.blocks_per_seq)


def _full(shape):
    nd = len(shape)
    return pl.BlockSpec(shape, lambda *_: (0,) * nd)


def _part_specs(g, block, axis):
    def spec(latent):
        def index(i):
            j = jnp.minimum(i, g.lat_blocks - 1) if latent else jnp.maximum(i - g.lat_blocks, 0)
            return tuple(j if a == axis else 0 for a in range(len(block)))
        return pl.BlockSpec(block, index)
    return [spec(True), spec(False)]


def _pick_part(lat_blocks, lat_ref, ctx_ref):
    return jnp.where(pl.program_id(0) < lat_blocks, lat_ref[...], ctx_ref[...])


def _attn_pre_call(g, h_parts, mod, gpre, w):
    d = h_parts[0].shape[1]
    t = g.t
    rb = ROW_BLOCK
    row = lambda i: (i, 0)
    tm_tab = pl.BlockSpec((rb, LANES), lambda i: (g.pos_block(i), 0))
    fm32 = pl.BlockSpec((32, rb), lambda i: (0, g.pos_block(i)))
    fm64 = pl.BlockSpec((64, rb), lambda i: (0, g.pos_block(i)))
    in_specs = _part_specs(g, (rb, d), 0) + [
        pl.BlockSpec((None, 6, d), lambda i: (g.mod_row(i), 0, 0)),
        _full(gpre.shape), _full(w["wfm"].shape), _full(w["wtm"].shape), _full(w["qn"].shape),
        _full(w["kvn_col"].shape), _full(w["kvn_row"].shape), _full(w["wq"].shape), _full(w["wk"].shape),
        _full(w["wv"].shape), _full(w["gq_g"].shape), _full(w["gk_g"].shape), _full(w["gk_gr"].shape),
        fm32, fm32, fm64, fm64, tm_tab, tm_tab, tm_tab, tm_tab,
    ]
    out_shape = [
        jax.ShapeDtypeStruct((MLA_HEADS, 128, t), BF16),
        jax.ShapeDtypeStruct((MLA_HEADS, t, 128), BF16),
        jax.ShapeDtypeStruct((MLA_HEADS, MLA_V, t), BF16),
        jax.ShapeDtypeStruct((GQA_HEADS, 128, t), BF16),
        jax.ShapeDtypeStruct((t, 128), BF16),
        jax.ShapeDtypeStruct((GQA_KV_HEADS, GQA_DIM, t), BF16),
    ]
    out_specs = [
        pl.BlockSpec((MLA_HEADS, 128, rb), lambda i: (0, 0, i)),
        pl.BlockSpec((MLA_HEADS, rb, 128), lambda i: (0, i, 0)),
        pl.BlockSpec((MLA_HEADS, MLA_V, rb), lambda i: (0, 0, i)),
        pl.BlockSpec((GQA_HEADS, 128, rb), lambda i: (0, 0, i)),
        pl.BlockSpec((rb, 128), row),
        pl.BlockSpec((GQA_KV_HEADS, GQA_DIM, rb), lambda i: (0, 0, i)),
    ]
    return pl.pallas_call(
        functools.partial(_attn_pre_kernel, lat_blocks=g.lat_blocks),
        grid=(g.n_blocks,), in_specs=in_specs, out_specs=out_specs, out_shape=out_shape,
        compiler_params=_cparams(("parallel",)), name="attn_pre",
    )(*h_parts, mod, gpre, w["wfm"], w["wtm"], w["qn"], w["kvn_col"], w["kvn_row"], w["wq"], w["wk"], w["wv"],
      w["gq_g"], w["gk_g"], w["gk_gr"],
      w["cqm"], w["sqm"], w["cqh"], w["sqh"], w["ckm"], w["skm"], w["ckh"], w["skh"])


def _attn_pre_weights(g, att_w_in, q_norm, w_q_up, kv_norm, w_kv_up, gqa_qn, gqa_kn):
    d = att_w_in.shape[0]
    o = np.cumsum([0, MLA_Q_LORA, MLA_KV_LORA, MLA_ROPE, GQA_HEADS * GQA_DIM, GQA_KV_HEADS * GQA_DIM,
                   GQA_KV_HEADS * GQA_DIM])
    w_qlat, w_kvlat, w_kpe, w_gq, w_gk, w_gv = (att_w_in[:, o[i]:o[i + 1]] for i in range(6))
    src32, sgn32 = _rot_map(MLA_ROPE)
    src64, sgn64 = _rot_map(GQA_DIM)

    def rot_heads(wcols, heads, dim, src, sgn):
        wh = wcols.reshape(d, heads, dim)
        return (wh[:, :, src] * sgn).reshape(d, heads * dim)

    wfm = jnp.concatenate([w_qlat, w_kvlat, w_gq, w_gv], axis=1)
    zpad = lambda x, lo, hi: jnp.pad(x, ((0, 0), (lo, hi)))
    wtm = jnp.concatenate([
        w_kvlat, zpad(w_kpe, 64, 32), zpad(w_kpe[:, src32] * sgn32, 64, 32),
        w_gk, rot_heads(w_gk, GQA_KV_HEADS, GQA_DIM, src64, sgn64)], axis=1)
    wq = w_q_up.reshape(MLA_Q_LORA, MLA_HEADS, MLA_NOPE + MLA_ROPE)
    wq_all = jnp.concatenate([wq[:, :, :MLA_NOPE].reshape(MLA_Q_LORA, -1),
                              wq[:, :, MLA_NOPE:].reshape(MLA_Q_LORA, -1)], axis=1)
    wkv = w_kv_up.reshape(MLA_KV_LORA, MLA_HEADS, MLA_NOPE + MLA_V)
    wk = jnp.pad(wkv[:, :, :MLA_NOPE], ((0, 0), (0, 0), (0, 128 - MLA_NOPE))).reshape(MLA_KV_LORA, -1)
    wv = wkv[:, :, MLA_NOPE:].reshape(MLA_KV_LORA, -1)
    cos_m, sin_m = _rope_tables(g.n, MLA_ROPE, ROW_BLOCK)
    cos_h, sin_h = _rope_tables(g.n, GQA_DIM, ROW_BLOCK)
    two = lambda x: jnp.concatenate([x, x], axis=1)
    return dict(
        wfm=wfm.T.astype(BF16), wtm=wtm.astype(BF16),
        qn=q_norm.reshape(-1, 1), kvn_col=kv_norm.reshape(-1, 1), kvn_row=kv_norm.reshape(1, -1),
        wq=wq_all.T.astype(BF16), wk=wk.astype(BF16), wv=wv.T.astype(BF16),
        gq_g=gqa_qn.reshape(-1, 1),
        gk_g=two(gqa_kn.reshape(1, -1)), gk_gr=two(gqa_kn[src64].reshape(1, -1)),
        cqm=cos_m.T, sqm=sin_m.T, cqh=cos_h.T, sqh=sin_h.T,
        ckm=zpad(cos_m, 64, 32), skm=zpad(sin_m, 64, 32), ckh=two(cos_h), skh=two(sin_h),
    )


def _fold8(x, op):
    r, w = x.shape
    return op(x.reshape(r // 8, 8, w), axis=0)


def _attend(qt, segments, s_ref, dv):
    tq = qt.shape[1]
    pieces, off = [], 0
    for k_ref, vt_ref in segments:
        n = k_ref.shape[0]
        for lo in range(0, n, ATTN_KV_CHUNK):
            rows = min(ATTN_KV_CHUNK, n - lo)
            pieces.append((off, lo, rows, k_ref, vt_ref))
            off += rows
    m8 = None
    for so, lo, rows, k_ref, _ in pieces:
        s = _dot(k_ref[lo:lo + rows, :], qt)
        s_ref[so:so + rows, :] = s
        part = _fold8(s, jnp.max)
        m8 = part if m8 is None else jnp.maximum(m8, part)
    m = jnp.max(m8, axis=0, keepdims=True)
    acc = jnp.zeros((dv + 16, tq), F32)
    off = 0
    for _, vt_ref in segments:
        n = vt_ref.shape[1]
        step = min(n, ATTN_EXP_ROWS)

        def body(i, acc, off=off, vt_ref=vt_ref, step=step):
            base = i * step if isinstance(i, int) else pl.multiple_of(i * step, step)
            for lo in range(0, step, ATTN_KV_CHUNK):
                rows = min(ATTN_KV_CHUNK, step - lo)
                p = jnp.exp2((s_ref[pl.ds(off + base + lo, rows), :] - m).astype(BF16))
                vt1 = jnp.concatenate([vt_ref[:, pl.ds(base + lo, rows)], jnp.ones((16, rows), BF16)], axis=0)
                acc = acc + _dot(vt1, p)
            return acc

        acc = body(0, acc) if n == step else lax.fori_loop(0, n // step, body, acc)
        off += n
    return acc[0:dv] / acc[dv:dv + 1]


def _attn_kernel(qt_ref, *refs, dv):
    o_ref, s_ref = refs[-2:]
    kv = refs[:-2]
    half = len(kv) // 2
    segs = [(kv[i], kv[half + i]) for i in range(half)]
    o_ref[...] = _attend(qt_ref[...], segs, s_ref, dv).astype(o_ref.dtype)


def _attn_call(g, qt, k, vt, *, kv_heads, shared_k, name):
    heads = qt.shape[0]
    dv = vt.shape[1]
    grp = heads // kv_heads
    tq = ATTN_Q_TILE
    nq = g.n // tq
    ctx_blk = g.b * g.n // g.c
    if shared_k:
        klat = pl.BlockSpec((g.n, 128), lambda b, h, *_: (b, 0))
        kctx = pl.BlockSpec((g.c, 128), lambda b, h, *_: (ctx_blk + b, 0))
    else:
        klat = pl.BlockSpec((None, g.n, 128), lambda b, h, *_: (h // grp, b, 0))
        kctx = pl.BlockSpec((None, g.c, 128), lambda b, h, *_: (h // grp, ctx_blk + b, 0))
    vlat = pl.BlockSpec((None, dv, g.n), lambda b, h, *_: (h // grp, 0, b))
    vctx = pl.BlockSpec((None, dv, g.c), lambda b, h, *_: (h // grp, 0, ctx_blk + b))
    lat = pl.pallas_call(
        functools.partial(_attn_kernel, dv=dv),
        grid=(g.b, heads, nq),
        scratch_shapes=[pltpu.VMEM((g.n + g.c, tq), F32)],
        in_specs=[pl.BlockSpec((None, 128, tq), lambda b, h, qi: (h, 0, b * nq + qi)), klat, kctx, vlat, vctx],
        out_specs=pl.BlockSpec((dv, tq), lambda b, h, qi: (h, b * nq + qi)),
        out_shape=jax.ShapeDtypeStruct((heads * dv, g.b * g.n), BF16),
        compiler_params=_cparams(("parallel", "parallel", "arbitrary")),
        name=name,
    )(qt, k, k, vt, vt)
    ctx = pl.pallas_call(
        functools.partial(_attn_kernel, dv=dv),
        grid=(g.b, heads),
        scratch_shapes=[pltpu.VMEM((g.c, g.c), F32)],
        in_specs=[pl.BlockSpec((None, 128, g.c), lambda b, h: (h, 0, ctx_blk + b)), kctx, vctx],
        out_specs=pl.BlockSpec((dv, g.c), lambda b, h: (h, b)),
        out_shape=jax.ShapeDtypeStruct((heads * dv, g.b * g.c), BF16),
        compiler_params=_cparams(("parallel", "parallel")),
        name=name + "_ctx",
    )(qt, k, vt)
    return lat, ctx


def _split_bf16(x):
    hi = x.astype(BF16)
    return hi, (x - hi.astype(F32)).astype(BF16)


def _post_tail(o, h, mod_ref, gpost_ref, gffn_ref, rw_ref, hn_ref, v_ref, lg_ref):
    hn = h + mod_ref[2:3, :] * (_rms(o, -1) * gpost_ref[...])
    hn_ref[...] = hn
    v = (_rms(hn, -1) * gffn_ref[...]) * (1.0 + mod_ref[4:5, :]) + mod_ref[3:4, :]
    v_hi, v_lo = _split_bf16(v)
    v_ref[...] = v_hi
    w_hi, w_lo = _split_bf16(rw_ref[...])
    lg_ref[...] = _dot_nt(w_hi, v_hi) + (_dot_nt(w_hi, v_lo) + _dot_nt(w_lo, v_hi))


def _attn_post_kernel(oal_ref, oac_ref, obl_ref, obc_ref, wa_ref, wb_ref, hl_ref, hc_ref, mod_ref, gpost_ref, gffn_ref,
                      rw_ref, hn_ref, v_ref, lg_ref, *, lat_blocks):
    oa = _pick_part(lat_blocks, oal_ref, oac_ref)
    ob = _pick_part(lat_blocks, obl_ref, obc_ref)
    o = _dot_tn(oa, wa_ref[...]) + _dot_tn(ob, wb_ref[...])
    h = _pick_part(lat_blocks, hl_ref, hc_ref)
    _post_tail(o, h, mod_ref, gpost_ref, gffn_ref, rw_ref, hn_ref, v_ref, lg_ref)


def _post_specs(g, d):
    rb = ROW_BLOCK
    ins = [
        pl.BlockSpec((None, 6, d), lambda i: (g.mod_row(i), 0, 0)),
        _full((1, d)), _full((1, d)), _full((N_EXPERTS, d)),
    ]
    outs = [pl.BlockSpec((rb, d), lambda i: (i, 0)), pl.BlockSpec((rb, d), lambda i: (i, 0)),
            pl.BlockSpec((N_EXPERTS, rb), lambda i: (0, i))]
    shapes = [jax.ShapeDtypeStruct((g.t, d), F32), jax.ShapeDtypeStruct((g.t, d), BF16),
              jax.ShapeDtypeStruct((N_EXPERTS, g.t), F32)]
    return ins, outs, shapes


def _attn_post_call(g, oa_parts, ob_parts, w_out, h_parts, mod, gpost, gffn, router_w):
    d = h_parts[0].shape[1]
    rb = ROW_BLOCK
    half = oa_parts[0].shape[0]
    ins, outs, shapes = _post_specs(g, d)
    wo = w_out.astype(BF16)
    o_specs = _part_specs(g, (half, rb), 1)
    return pl.pallas_call(
        functools.partial(_attn_post_kernel, lat_blocks=g.lat_blocks), grid=(g.n_blocks,),
        in_specs=o_specs + o_specs + [_full((half, d)), _full((half, d))] + _part_specs(g, (rb, d), 0) + ins,
        out_specs=outs, out_shape=shapes,
        compiler_params=_cparams(("parallel",)), name="attn_post",
    )(*oa_parts, *ob_parts, wo[:half], wo[half:], *h_parts, mod, gpost.reshape(1, d), gffn.reshape(1, d), router_w.T)


def _router_kernel(lg_ref, bias_ref, tri_ref, gate_ref, rank_ref, cnt_ref):
    tb = lg_ref.shape[1]
    per = N_EXPERTS // N_GROUPS
    shp = (N_GROUPS, per, tb)
    scores = jax.nn.sigmoid(lg_ref[...])
    s3 = scores.reshape(shp)
    sel = (scores + bias_ref[...]).reshape(shp)
    sub = lax.broadcasted_iota(jnp.int32, shp, 1)
    grp = lax.broadcasted_iota(jnp.int32, shp, 0)
    neg = -jnp.inf
    m1 = jnp.max(sel, axis=1, keepdims=True)
    i1 = jnp.min(jnp.where(sel == m1, sub, per), axis=1, keepdims=True)
    m2 = jnp.max(jnp.where(sub == i1, neg, sel), axis=1, keepdims=True)
    cur = jnp.broadcast_to(m1 + m2, shp)
    gmask = jnp.zeros(shp, F32)
    for _ in range(TOPK_GROUPS):
        gm = jnp.max(cur, axis=0, keepdims=True)
        gi = jnp.min(jnp.where(cur == gm, grp, N_GROUPS), axis=0, keepdims=True)
        pick = grp == gi
        gmask = jnp.where(pick, 1.0, gmask)
        cur = jnp.where(pick, neg, cur)
    masked = jnp.where(gmask > 0.0, sel, neg)
    eidx = grp * per + sub
    chosen = jnp.zeros(shp, F32)
    for _ in range(TOP_K):
        mx = jnp.max(jnp.max(masked, axis=0, keepdims=True), axis=1, keepdims=True)
        cand = jnp.where(masked == mx, eidx, N_EXPERTS)
        ei = jnp.min(jnp.min(cand, axis=0, keepdims=True), axis=1, keepdims=True)
        pick = eidx == ei
        chosen = jnp.where(pick, 1.0, chosen)
        masked = jnp.where(pick, neg, masked)
    top_w = jnp.where(chosen > 0.0, s3, 0.0)
    denom = jnp.sum(jnp.sum(top_w, axis=0, keepdims=True), axis=1, keepdims=True)
    gate_ref[...] = (top_w / denom * ROUTED_SCALE).reshape(N_EXPERTS, tb)
    ch2 = chosen.reshape(N_EXPERTS, tb)
    chb = ch2.astype(BF16)
    before = _dot(chb, tri_ref[...])
    rank_ref[...] = jnp.where(ch2 > 0.0, before, -1.0).astype(jnp.int32)
    cnt_ref[...] = _dot(chb, jnp.ones((tb, LANES), BF16))


def _router_call(g, logits_t, router_b, nsb):
    tb = MOE_TOKENS
    tri = (np.arange(tb)[:, None] < np.arange(tb)[None, :]).astype(np.float32)
    blk = pl.BlockSpec((N_EXPERTS, tb), lambda s: (0, s))
    gates, rank, cnt = pl.pallas_call(
        _router_kernel, grid=(nsb,),
        in_specs=[blk, _full((N_EXPERTS, 1)), _full((tb, tb))],
        out_specs=[blk, blk, pl.BlockSpec((None, N_EXPERTS, LANES), lambda s: (s, 0, 0))],
        out_shape=[jax.ShapeDtypeStruct((N_EXPERTS, g.t), F32), jax.ShapeDtypeStruct((N_EXPERTS, g.t), jnp.int32),
                   jax.ShapeDtypeStruct((nsb, N_EXPERTS, LANES), F32)],
        compiler_params=_cparams(("parallel",)), name="router",
    )(logits_t, router_b.reshape(-1, 1), jnp.asarray(tri, BF16))
    return gates, rank, cnt[:, :, 0].astype(jnp.int32).reshape(-1)


def _moe_kernel(cnt_ref, v_ref, gate_ref, rank_ref, wg_ref, wu_ref, wd_ref, sg_ref, su_ref, sd_ref, o_ref,
                psel_ref, ysel_ref, slot_ref):
    s, j = pl.program_id(0), pl.program_id(1)
    tb = v_ref.shape[0]
    rows = MOE_ROWS

    @pl.when(j == 0)
    def _():
        vb = v_ref[...]
        hid = (_silu(_dot(vb, sg_ref[...])) * _dot(vb, su_ref[...])).astype(BF16)
        o_ref[...] = _dot(hid, sd_ref[...])
        psel_ref[...] = jnp.zeros_like(psel_ref)
        ysel_ref[...] = jnp.zeros_like(ysel_ref)
        slot_ref[0] = 0

    row_id = lax.broadcasted_iota(jnp.int32, (rows, tb), 0)

    def flush(n_slots):
        stage_row = lax.broadcasted_iota(jnp.int32, psel_ref.shape, 0)
        sel = jnp.where(stage_row < n_slots * rows, psel_ref[...], jnp.zeros_like(psel_ref))
        o_ref[...] += _dot_tn(sel, ysel_ref[...])

    ends, total = [], 0
    for e in range(MOE_PAIR):
        total = total + (cnt_ref[s * N_EXPERTS + j * MOE_PAIR + e] + rows - 1) // rows
        ends.append(total)

    def select(i):
        e = sum((i >= end).astype(jnp.int32) for end in ends[:-1])
        first = sum(jnp.where(e == k + 1, ends[k], 0) for k in range(MOE_PAIR - 1))
        ex = j * MOE_PAIR + e
        hit = row_id == rank_ref[pl.ds(ex, 1), :] - (i - first) * rows
        onehot = jnp.where(hit, 1.0, 0.0).astype(BF16)
        w_row = jnp.sum(jnp.where(hit, gate_ref[pl.ds(ex, 1), :], 0.0), axis=1, keepdims=True)
        return e, onehot, w_row

    def run_items(i0, count, slot):
        picks = [select(i0 + k) for k in range(count)]
        sel = picks[0][1] if count == 1 else jnp.concatenate([p[1] for p in picks], axis=0)
        at = pl.multiple_of(slot * rows, rows)
        psel_ref[pl.ds(at, count * rows), :] = sel
        xs = _dot(sel, v_ref[...]).astype(BF16)
        hids = [(_silu(_dot(xs[k * rows:(k + 1) * rows], wg_ref[e])) * _dot(xs[k * rows:(k + 1) * rows], wu_ref[e]))
                .astype(BF16) for k, (e, _, _) in enumerate(picks)]
        outs = [(_dot(hid, wd_ref[e]) * w_row).astype(BF16) for hid, (e, _, w_row) in zip(hids, picks)]
        ysel_ref[pl.ds(at, count * rows), :] = outs[0] if count == 1 else jnp.concatenate(outs, axis=0)

        @pl.when(slot + count == MOE_SLOTS)
        def _():
            flush(MOE_SLOTS)

        return jnp.where(slot + count == MOE_SLOTS, 0, slot + count)

    def pair(it, slot):
        no_room = slot > MOE_SLOTS - 2

        @pl.when(no_room)
        def _():
            flush(slot)

        return run_items(2 * it, 2, jnp.where(no_room, 0, slot))

    slot_ref[0] = lax.fori_loop(0, total // 2, pair, slot_ref[0])

    @pl.when(total % 2 == 1)
    def _():
        slot_ref[0] = run_items(total - 1, 1, slot_ref[0])

    @pl.when(jnp.logical_and(j == pl.num_programs(1) - 1, slot_ref[0] > 0))
    def _():
        flush(slot_ref[0])


def _moe_call(g, v, gates, rank, counts, nsb, wg, wu, wd, sg, su, sd):
    t, d = v.shape
    tb = MOE_TOKENS
    ff = wg.shape[2]
    grid_spec = pltpu.PrefetchScalarGridSpec(
        num_scalar_prefetch=1,
        grid=(nsb, N_EXPERTS // MOE_PAIR),
        in_specs=[
            pl.BlockSpec((tb, d), lambda s, j, c: (s, 0)),
            pl.BlockSpec((N_EXPERTS, tb), lambda s, j, c: (0, s)),
            pl.BlockSpec((N_EXPERTS, tb), lambda s, j, c: (0, s)),
            pl.BlockSpec((MOE_PAIR, d, ff), lambda s, j, c: (j, 0, 0)),
            pl.BlockSpec((MOE_PAIR, d, ff), lambda s, j, c: (j, 0, 0)),
            pl.BlockSpec((MOE_PAIR, ff, d), lambda s, j, c: (j, 0, 0)),
            pl.BlockSpec((d, sg.shape[1]), lambda s, j, c: (0, 0)),
            pl.BlockSpec((d, su.shape[1]), lambda s, j, c: (0, 0)),
            pl.BlockSpec((sd.shape[0], d), lambda s, j, c: (0, 0)),
        ],
        out_specs=pl.BlockSpec((tb, d), lambda s, j, c: (s, 0)),
        scratch_shapes=[pltpu.VMEM((MOE_SLOTS * MOE_ROWS, tb), BF16), pltpu.VMEM((MOE_SLOTS * MOE_ROWS, d), BF16),
                        pltpu.SMEM((1,), jnp.int32)],
    )
    return pl.pallas_call(
        _moe_kernel, grid_spec=grid_spec, out_shape=jax.ShapeDtypeStruct((t, d), F32),
        compiler_params=_cparams(("parallel", "arbitrary")), name="moe",
    )(counts, v, gates, rank, wg.astype(BF16), wu.astype(BF16), wd.astype(BF16),
      sg.astype(BF16), su.astype(BF16), sd.astype(BF16))


def _rec_pre_kernel(h_ref, f_ref, pmod_ref, gprev_ref, mod_ref, gpre_ref, wtm_ref, wfm_ref, cqh_ref, sqh_ref, ckh_ref,
                    skh_ref, hn_ref, z_ref, xbc_ref, dt_ref, kd_ref, qd_ref, vd_ref):
    h = h_ref[...] + pmod_ref[5:6, :] * (_rms(f_ref[...], -1) * gprev_ref[...])
    hn_ref[...] = h
    u = _rms(h, -1) * gpre_ref[...]
    u = u * (1.0 + mod_ref[1:2, :]) + mod_ref[0:1, :]
    ub = u.astype(BF16)
    zk = _dot(ub, wtm_ref[...])
    zt = _dot_nt(wfm_ref[...], ub)
    z_ref[...] = zk[:, 0:512].astype(z_ref.dtype)
    xbc_ref[...] = zk[:, 512:1536]
    dt_ref[...] = zk[:, 1536:1664]
    ck, sk = ckh_ref[...], skh_ref[...]
    dk = zk[:, 1664:2176]
    dk_rot = _rot_lanes(dk, DIFF_DIM)
    for j in range(DIFF_HEADS):
        lo = 128 * j
        kd_ref[j] = (dk[:, lo:lo + 128] * ck + dk_rot[:, lo:lo + 128] * sk).astype(BF16)
    cq, sq = cqh_ref[...], sqh_ref[...]
    qs = DIFF_SCALE * LOG2E
    zero64 = jnp.zeros((64, h.shape[0]), BF16)
    for hd in range(2 * DIFF_HEADS):
        dq = zt[64 * hd:64 * hd + 64]
        q = ((dq * cq + _rot_rows(dq) * sq) * qs).astype(BF16)
        if hd % 2 == 0:
            qd_ref[hd, 0:64, :] = q
            qd_ref[hd, 64:128, :] = zero64
        else:
            qd_ref[hd, 0:64, :] = zero64
            qd_ref[hd, 64:128, :] = q
    for j in range(DIFF_HEADS):
        vd_ref[j] = zt[512 + 128 * j:640 + 128 * j].astype(BF16)


def _rec_pre_weights(g, rec_w_in):
    o = np.cumsum([0, SSD_INNER, SSD_XBC, SSD_HEADS, SSD_HEADS, 2 * DIFF_HEADS * DIFF_DIM, 2 * DIFF_HEADS * DIFF_DIM,
                   DIFF_HEADS * DIFF_V])
    w_z, w_xbc, w_dtf, w_dtb, w_dq, w_dk, w_dv = (rec_w_in[:, o[i]:o[i + 1]] for i in range(7))
    w_dt = jnp.pad(jnp.concatenate([w_dtf, w_dtb], axis=1), ((0, 0), (0, LANES - 2 * SSD_HEADS)))
    wtm = jnp.concatenate([w_z, w_xbc, w_dt, w_dk], axis=1)
    wfm = jnp.concatenate([w_dq, w_dv], axis=1)
    cos_h, sin_h = _rope_tables(g.n, DIFF_DIM, ROW_BLOCK)
    two = lambda x: jnp.concatenate([x, x], axis=1)
    return dict(wtm=wtm.astype(BF16), wfm=wfm.T.astype(BF16), cqh=cos_h.T, sqh=sin_h.T, ckh=two(cos_h), skh=two(sin_h))


def _rec_pre_call(g, h, f, prev_mod, prev_gain, mod, gpre, w):
    t, d = h.shape
    rb = ROW_BLOCK
    row = lambda i: (i, 0)
    modspec = pl.BlockSpec((None, 6, d), lambda i: (g.mod_row(i), 0, 0))
    tm_tab = pl.BlockSpec((rb, LANES), lambda i: (g.pos_block(i), 0))
    fm64 = pl.BlockSpec((64, rb), lambda i: (0, g.pos_block(i)))
    nd = 2 * DIFF_HEADS
    return pl.pallas_call(
        _rec_pre_kernel, grid=(g.n_blocks,),
        in_specs=[pl.BlockSpec((rb, d), row), pl.BlockSpec((rb, d), row), modspec, _full((1, d)), modspec,
                  _full(gpre.shape), _full(w["wtm"].shape), _full(w["wfm"].shape), fm64, fm64, tm_tab, tm_tab],
        out_specs=[
            pl.BlockSpec((rb, d), row), pl.BlockSpec((rb, SSD_INNER), row), pl.BlockSpec((rb, SSD_XBC), row), pl.BlockSpec((rb, LANES), row),
            pl.BlockSpec((DIFF_HEADS, rb, 128), lambda i: (0, i, 0)),
            pl.BlockSpec((nd, 128, rb), lambda i: (0, 0, i)),
            pl.BlockSpec((DIFF_HEADS, DIFF_V, rb), lambda i: (0, 0, i)),
        ],
        out_shape=[
            jax.ShapeDtypeStruct((t, d), F32),
            jax.ShapeDtypeStruct((t, SSD_INNER), BF16), jax.ShapeDtypeStruct((t, SSD_XBC), F32),
            jax.ShapeDtypeStruct((t, LANES), F32), jax.ShapeDtypeStruct((DIFF_HEADS, t, 128), BF16),
            jax.ShapeDtypeStruct((nd, 128, t), BF16), jax.ShapeDtypeStruct((DIFF_HEADS, DIFF_V, t), BF16),
        ],
        compiler_params=_cparams(("parallel",)), name="rec_pre",
    )(h, f, prev_mod, prev_gain.reshape(1, d), mod, gpre, w["wtm"], w["wfm"], w["cqh"], w["sqh"], w["ckh"], w["skh"])


def _conv_kernel(x_ref, prev_ref, next_ref, w_ref, b_ref, o_ref, *, seq_blocks, lat_blocks, ctx_len):
    i = pl.program_id(0)
    rb = x_ref.shape[0]
    pos = i % seq_blocks
    is_ctx = i >= lat_blocks
    x = x_ref[...]
    rid = lax.broadcasted_iota(jnp.int32, x.shape, 0)
    in_seq = rid % ctx_len
    no_prev = jnp.where(is_ctx, in_seq, jnp.where(pos == 0, rid, 1)) == 0
    no_next = jnp.where(is_ctx, in_seq - (ctx_len - 1), jnp.where(pos == seq_blocks - 1, rid - (rb - 1), 1)) == 0
    x_prev = jnp.where(rid == 0, prev_ref[7:8, :], pltpu.roll(x, 1, axis=0))
    x_next = jnp.where(rid == rb - 1, next_ref[0:1, :], pltpu.roll(x, rb - 1, axis=0))
    x_prev = jnp.where(no_prev, 0.0, x_prev)
    x_next = jnp.where(no_next, 0.0, x_next)
    y = w_ref[0:1, :] * x_prev + w_ref[1:2, :] * x + w_ref[2:3, :] * x_next + b_ref[...]
    o_ref[...] = _silu(y)


def _conv_call(g, xbc, conv_w, conv_b):
    t, ch = xbc.shape
    rb = ROW_BLOCK
    assert rb % g.c == 0
    halo = 8
    per = rb // halo
    last_halo = t // halo - 1
    return pl.pallas_call(
        functools.partial(_conv_kernel, seq_blocks=g.n // rb, lat_blocks=g.b * g.n // rb, ctx_len=g.c),
        grid=(t // rb,),
        in_specs=[
            pl.BlockSpec((rb, ch), lambda i: (i, 0)),
            pl.BlockSpec((halo, ch), lambda i: (jnp.maximum(i * per - 1, 0), 0)),
            pl.BlockSpec((halo, ch), lambda i: (jnp.minimum((i + 1) * per, last_halo), 0)),
            _full((SSD_CONV, ch)), _full((1, ch)),
        ],
        out_specs=pl.BlockSpec((rb, ch), lambda i: (i, 0)),
        out_shape=jax.ShapeDtypeStruct((t, ch), F32),
        compiler_params=_cparams(("parallel",)), name="ssd_conv",
    )(xbc, xbc, xbc, conv_w, conv_b.reshape(1, ch))


def _ssd_kernel(xf_ref, dtf_ref, xb_ref, dtb_ref, bias_ref, alog_ref, tril_ref, triu_ref, eye_ref, yf_ref, yb_ref,
                sf_ref, sb_ref):
    @pl.when(pl.program_id(1) == 0)
    def _():
        sf_ref[...] = jnp.zeros_like(sf_ref)
        sb_ref[...] = jnp.zeros_like(sb_ref)

    _ssd_chunk(xf_ref, dtf_ref, bias_ref, alog_ref, tril_ref, triu_ref, eye_ref, yf_ref, sf_ref, 0, False)
    _ssd_chunk(xb_ref, dtb_ref, bias_ref, alog_ref, triu_ref, tril_ref, eye_ref, yb_ref, sb_ref, SSD_HEADS, True)


def _ssd_chunk(x_ref, dt_ref, bias_ref, alog_ref, tri_ref, trit_ref, eye_ref, y_ref, state_ref, lane_off, reverse):
    ln = SSD_CHUNK
    hi = lax.Precision.HIGHEST
    x = x_ref[:, 0:SSD_INNER]
    raw = dt_ref[...] + bias_ref[...]
    dt = jnp.maximum(raw, 0.0) + jnp.log1p(jnp.exp(-jnp.abs(raw)))
    adt = dt * (-jnp.exp(alog_ref[...]))
    acs = jnp.dot(tri_ref[...], adt, preferred_element_type=F32, precision=hi)
    acs_row = lax.dot_general(adt, trit_ref[...], (((0,), (0,)), ((), ())), preferred_element_type=F32, precision=hi)
    dt_row = lax.dot_general(dt, eye_ref[...], (((0,), (0,)), ((), ())), preferred_element_type=F32, precision=hi)
    tot = acs[0:1, :] if reverse else acs[ln - 1:ln, :]
    w_all = jnp.exp(tot - acs) * dt
    ea_all = jnp.exp(acs)
    etot = jnp.exp(tot)
    li = lax.broadcasted_iota(jnp.int32, (ln, ln), 0)
    si = lax.broadcasted_iota(jnp.int32, (ln, ln), 1)
    keep = (si >= li) if reverse else (si <= li)
    lane = lax.broadcasted_iota(jnp.int32, (ln, LANES), 1)
    left = lane < SSD_HEAD_DIM
    per_group = SSD_HEADS // SSD_GROUPS
    for gi in range(SSD_GROUPS):
        bm = x_ref[:, SSD_INNER + SSD_STATE * gi:SSD_INNER + SSD_STATE * (gi + 1)].astype(BF16)
        cm = x_ref[:, SSD_INNER + SSD_STATE * (SSD_GROUPS + gi):SSD_INNER + SSD_STATE * (SSD_GROUPS + gi + 1)]
        cm = cm.astype(BF16)
        cb = _dot_nt(cm, bm)
        for pr in range(per_group // 2):
            h0 = gi * per_group + 2 * pr
            xp = x[:, SSD_HEAD_DIM * h0:SSD_HEAD_DIM * (h0 + 2)]
            mats = []
            for hd in (h0, h0 + 1):
                c = lane_off + hd
                seg = acs[:, c:c + 1] - acs_row[c:c + 1, :]
                lmat = jnp.exp(jnp.where(keep, seg, -jnp.inf))
                mats.append((cb * lmat * dt_row[c:c + 1, :]).astype(BF16))
            xb = xp.astype(BF16)
            zero = jnp.zeros_like(xb)
            rhs = jnp.concatenate([jnp.where(left, xb, zero), jnp.where(left, zero, xb)], axis=0)
            y_diag = _dot(jnp.concatenate(mats, axis=1), rhs)
            st = state_ref[h0 // 2]
            c0 = lane_off + h0
            ea = jnp.where(left, ea_all[:, c0:c0 + 1], ea_all[:, c0 + 1:c0 + 2])
            y_off = _dot(cm, st.astype(BF16)) * ea
            y_ref[:, SSD_HEAD_DIM * h0:SSD_HEAD_DIM * (h0 + 2)] = y_diag + y_off
            wcol = jnp.where(left, w_all[:, c0:c0 + 1], w_all[:, c0 + 1:c0 + 2])
            cs = _dot_tn(bm, (xp * wcol).astype(BF16))
            dec = jnp.where(left[0:1, :], etot[:, c0:c0 + 1], etot[:, c0 + 1:c0 + 2])
            state_ref[h0 // 2] = st * dec + cs


def _ssd_call(g, xbc_act, dt, dt_bias_f, dt_bias_b, a_log_f, a_log_b):
    t = xbc_act.shape[0]
    ln = SSD_CHUNK
    cc = g.c // ln
    nl = g.n // ln
    ctx0 = g.b * g.n // ln
    fwd = lambda b, k: (jnp.where(k < cc, ctx0 + b * cc + k, b * nl + (k - cc)), 0)
    bwd = lambda b, k: (jnp.where(k < cc, ctx0 + b * cc + (cc - 1 - k), b * nl + (nl - 1 - (k - cc))), 0)
    idx = np.arange(ln)
    lower = (idx[:, None] >= idx[None, :]).astype(np.float32)
    lanes = lambda pf, pb: jnp.pad(jnp.concatenate([pf, pb]).reshape(1, -1), ((0, 0), (0, LANES - 2 * SSD_HEADS)))
    state = pltpu.VMEM((SSD_HEADS // 2, SSD_STATE, 2 * SSD_HEAD_DIM), F32)
    y_shape = jax.ShapeDtypeStruct((t, SSD_INNER), F32)
    return pl.pallas_call(
        _ssd_kernel, grid=(g.b, cc + nl),
        in_specs=[pl.BlockSpec((ln, SSD_XBC), fwd), pl.BlockSpec((ln, LANES), fwd),
                  pl.BlockSpec((ln, SSD_XBC), bwd), pl.BlockSpec((ln, LANES), bwd),
                  _full((1, LANES)), _full((1, LANES)), _full((ln, ln)), _full((ln, ln)), _full((ln, ln))],
        out_specs=[pl.BlockSpec((ln, SSD_INNER), fwd), pl.BlockSpec((ln, SSD_INNER), bwd)],
        out_shape=[y_shape, y_shape],
        scratch_shapes=[state, state],
        compiler_params=_cparams(("parallel", "arbitrary")),
        name="ssd_scan",
    )(xbc_act, dt, xbc_act, dt, lanes(dt_bias_f, dt_bias_b), lanes(a_log_f, a_log_b), jnp.asarray(lower),
      jnp.asarray(lower.T), jnp.eye(ln, dtype=F32))


def _diff_finish(o, tq, lq1_ref, lk1_ref, lq2_ref, lk2_ref, sub_ref, o_ref, lambda_init):
    lam = (jnp.exp(jnp.sum(lq1_ref[...] * lk1_ref[...], axis=1, keepdims=True))
           - jnp.exp(jnp.sum(lq2_ref[...] * lk2_ref[...], axis=1, keepdims=True)) + lambda_init)
    od = o[:, 0:tq] - lam * o[:, tq:2 * tq]
    o_ref[...] = ((_rms(od, 0) * sub_ref[...]) * (1.0 - lambda_init)).astype(o_ref.dtype)


def _diff_attn_kernel(qt_ref, *refs, lambda_init):
    o_ref, s_ref = refs[-2:]
    params = refs[-7:-2]
    kv = refs[:-7]
    half = len(kv) // 2
    segs = [(kv[i], kv[half + i]) for i in range(half)]
    qt = jnp.concatenate([qt_ref[0], qt_ref[1]], axis=1)
    o = _attend(qt, segs, s_ref, DIFF_V)
    _diff_finish(o, qt_ref.shape[2], *params, o_ref, lambda_init)


def _diff_attn_call(g, qd, kd, vd, lq1, lk1, lq2, lk2, subln, lambda_init):
    tq = DIFF_Q_TILE
    nq = g.n // tq
    ctx_blk = g.b * g.n // g.c
    vec = lambda a: a.reshape(1, -1)
    params = [_full((1, DIFF_DIM)), _full((1, DIFF_DIM)), _full((1, DIFF_DIM)), _full((1, DIFF_DIM)),
              _full((DIFF_V, 1))]
    pvals = (vec(lq1), vec(lk1), vec(lq2), vec(lk2), subln.reshape(-1, 1))
    klat = pl.BlockSpec((None, g.n, 128), lambda b, j, *_: (j, b, 0))
    kctx = pl.BlockSpec((None, g.c, 128), lambda b, j, *_: (j, ctx_blk + b, 0))
    vlat = pl.BlockSpec((None, DIFF_V, g.n), lambda b, j, *_: (j, 0, b))
    vctx = pl.BlockSpec((None, DIFF_V, g.c), lambda b, j, *_: (j, 0, ctx_blk + b))
    kern = functools.partial(_diff_attn_kernel, lambda_init=lambda_init)
    lat = pl.pallas_call(
        kern, grid=(g.b, DIFF_HEADS, nq),
        scratch_shapes=[pltpu.VMEM((g.n + g.c, 2 * tq), F32)],
        in_specs=[pl.BlockSpec((2, 128, tq), lambda b, j, qi: (j, 0, b * nq + qi)), klat, kctx, vlat, vctx] + params,
        out_specs=pl.BlockSpec((DIFF_V, tq), lambda b, j, qi: (j, b * nq + qi)),
        out_shape=jax.ShapeDtypeStruct((DIFF_HEADS * DIFF_V, g.b * g.n), BF16),
        compiler_params=_cparams(("parallel", "parallel", "arbitrary")),
        name="attn_diff",
    )(qd, kd, kd, vd, vd, *pvals)
    ctx = pl.pallas_call(
        kern, grid=(g.b, DIFF_HEADS),
        scratch_shapes=[pltpu.VMEM((g.c, 2 * g.c), F32)],
        in_specs=[pl.BlockSpec((2, 128, g.c), lambda b, j: (j, 0, ctx_blk + b)), kctx, vctx] + params,
        out_specs=pl.BlockSpec((DIFF_V, g.c), lambda b, j: (j, b)),
        out_shape=jax.ShapeDtypeStruct((DIFF_HEADS * DIFF_V, g.b * g.c), BF16),
        compiler_params=_cparams(("parallel", "parallel")),
        name="attn_diff_ctx",
    )(qd, kd, vd, *pvals)
    return lat, ctx


def _rec_post_kernel(yf_ref, yb_ref, xs_ref, z_ref, odl_ref, odc_ref, dskip_ref, gn_ref, wa_ref, wb_ref,
                     h_ref, mod_ref, gpost_ref, gffn_ref, rw_ref, hn_ref, v_ref, lg_ref, *, lat_blocks):
    y = (yf_ref[...] + yb_ref[...] + xs_ref[...] * dskip_ref[...]) * _silu(z_ref[...].astype(F32))
    half = SSD_INNER // SSD_GROUPS
    o = _dot_tn(_pick_part(lat_blocks, odl_ref, odc_ref), wb_ref[...])
    for gi in range(SSD_GROUPS):
        lo = half * gi
        yn = (_rms(y[:, lo:lo + half], -1) * gn_ref[:, lo:lo + half]).astype(BF16)
        o = o + _dot(yn, wa_ref[lo:lo + half, :])
    _post_tail(o, h_ref[...], mod_ref, gpost_ref, gffn_ref, rw_ref, hn_ref, v_ref, lg_ref)


def _rec_post_call(g, yf, yb, xbc_act, zs, od_parts, d_skip, ssd_norm, w_out, h, mod, gpost, gffn, router_w):
    d = h.shape[1]
    rb = ROW_BLOCK
    ins, outs, shapes = _post_specs(g, d)
    wo = w_out.astype(BF16)
    row512 = pl.BlockSpec((rb, SSD_INNER), lambda i: (i, 0))
    return pl.pallas_call(
        functools.partial(_rec_post_kernel, lat_blocks=g.lat_blocks), grid=(g.n_blocks,),
        in_specs=[row512, row512, row512, row512] + _part_specs(g, (SSD_INNER, rb), 1)
        + [_full((1, SSD_INNER)), _full((1, SSD_INNER)), _full((SSD_INNER, d)), _full((SSD_INNER, d)),
           pl.BlockSpec((rb, d), lambda i: (i, 0))] + ins,
        out_specs=outs, out_shape=shapes,
        compiler_params=_cparams(("parallel",)), name="rec_post",
    )(yf, yb, xbc_act, zs, *od_parts, jnp.repeat(d_skip, SSD_HEAD_DIM).reshape(1, -1), ssd_norm.reshape(1, -1),
      wo[:SSD_INNER], wo[SSD_INNER:], h, mod, gpost.reshape(1, d), gffn.reshape(1, d), router_w.T)


def _ffn_res_kernel(h_ref, f_ref, mod_ref, g_ref, o_ref):
    o_ref[...] = h_ref[...] + mod_ref[5:6, :] * (_rms(f_ref[...], -1) * g_ref[...])


def _ffn_res_call(g, h, f, mod, gain, n_rows):
    d = h.shape[1]
    rb = ROW_BLOCK
    row = pl.BlockSpec((rb, d), lambda i: (i, 0))
    return pl.pallas_call(
        _ffn_res_kernel, grid=(n_rows // rb,),
        in_specs=[row, row, pl.BlockSpec((None, 6, d), lambda i: (g.mod_row(i), 0, 0)), _full((1, d))],
        out_specs=row, out_shape=jax.ShapeDtypeStruct((n_rows, d), F32),
        compiler_params=_cparams(("parallel",)), name="ffn_res",
    )(h, f, mod, gain.reshape(1, d))


def kernel(x, c, ctx, c_ctx, ada_w, ada_b, norm_mix_pre, norm_mix_post, norm_ffn_pre, norm_ffn_post, mix_w_out, att_w_in, mla_q_norm, mla_w_q_up, mla_kv_norm, mla_w_kv_up, gqa_q_norm, gqa_k_norm, rec_w_in, ssd_conv_w, ssd_conv_b, ssd_dt_bias_f, ssd_dt_bias_b, ssd_a_log_f, ssd_a_log_b, ssd_d, ssd_norm, diff_lambda_q1, diff_lambda_k1, diff_lambda_q2, diff_lambda_k2, diff_subln, router_w, router_b, exp_w_gate, exp_w_up, exp_w_down, sh_w_gate, sh_w_up, sh_w_down):
    b, n, d = x.shape
    n_ctx = ctx.shape[1]
    depth = ada_w.shape[0]
    g = _Geom(b, n, n_ctx)
    mod_rows = -(-(b + 1) // 8) * 8
    c_all = jnp.concatenate([c, c_ctx[None, :], jnp.zeros((mod_rows - b - 1, d), F32)], axis=0)
    mods = _ada_call(c_all, ada_w, ada_b).reshape(depth, mod_rows, 6, d)
    h = None
    for i in range(depth):
        last = i == depth - 1
        jdx = i // 2
        mod = mods[i]
        if i % 2 == 0:
            w = _attn_pre_weights(g, att_w_in[jdx], mla_q_norm[jdx], mla_w_q_up[jdx], mla_kv_norm[jdx],
                                  mla_w_kv_up[jdx], gqa_q_norm[jdx], gqa_k_norm[jdx])
            h_parts = (x.reshape(b * n, d), ctx.reshape(b * n_ctx, d)) if i == 0 else (h[:b * n], h[b * n:])
            qm, km, vm, qg, kg, vg = _attn_pre_call(g, h_parts, mod, norm_mix_pre[i].reshape(1, d), w)
            oa = _attn_call(g, qm, km, vm, kv_heads=MLA_HEADS, shared_k=False, name="attn_mla")
            ob = _attn_call(g, qg, kg, vg, kv_heads=GQA_KV_HEADS, shared_k=True, name="attn_gqa")
            h, v, logits_t = _attn_post_call(g, oa, ob, mix_w_out[i], h_parts, mod, norm_mix_post[i], norm_ffn_pre[i],
                                             router_w[i])
        else:
            lambda_init = 0.8 - 0.6 * math.exp(-0.3 * i)
            w = _rec_pre_weights(g, rec_w_in[jdx])
            h, zs, xbc, dt, kd, qd, vd = _rec_pre_call(g, h, f, mods[i - 1], norm_ffn_post[i - 1], mod,
                                                       norm_mix_pre[i].reshape(1, d), w)
            xbc_act = _conv_call(g, xbc, ssd_conv_w[jdx], ssd_conv_b[jdx])
            yf, yb = _ssd_call(g, xbc_act, dt, ssd_dt_bias_f[jdx], ssd_dt_bias_b[jdx], ssd_a_log_f[jdx],
                               ssd_a_log_b[jdx])
            od = _diff_attn_call(g, qd, kd, vd, diff_lambda_q1[jdx], diff_lambda_k1[jdx], diff_lambda_q2[jdx],
                                 diff_lambda_k2[jdx], diff_subln[jdx], lambda_init)
            h, v, logits_t = _rec_post_call(g, yf, yb, xbc_act, zs, od, ssd_d[jdx], ssd_norm[jdx], mix_w_out[i], h,
                                            mod, norm_mix_post[i], norm_ffn_pre[i], router_w[i])
        nsb = -(-(b * n) // MOE_TOKENS) if last else g.t // MOE_TOKENS
        gates, rank, counts = _router_call(g, logits_t, router_b[i], nsb)
        f = _moe_call(g, v, gates, rank, counts, nsb, exp_w_gate[i], exp_w_up[i], exp_w_down[i],
                      sh_w_gate[i], sh_w_up[i], sh_w_down[i])
        if last or i % 2 == 1:
            h = _ffn_res_call(g, h, f, mod, norm_ffn_post[i], b * n if last else g.t)
    return h.reshape(b, n, d)
```
